```python
import jax, jax.numpy as jnp
from jax import lax
import numpy as np

D_MODEL = 2048
BATCH = 8
SEQ = 8192
DEPTH = 4

N_MIXERS = 2
POOL_WINDOWS = (2, 4, 8, 16)
N_POOL_GROUPS = 4
POOL_GROUP_DIM = D_MODEL // N_POOL_GROUPS
HEAD_DIM = 128
N_HEADS = D_MODEL // HEAD_DIM
Q_BLOCK = 128
D_FF = 5504
CONV_WIDTH = 3
LN_EPS = 1e-5
DEEPNORM_ALPHA = (2.0 * DEPTH) ** 0.25
DEEPNORM_BETA = (8.0 * DEPTH) ** -0.25
N_POOL_LAYERS = (DEPTH + 1) // 2
N_ATTN_LAYERS = DEPTH // 2

kernel_name = "hybrid_pool_stickbreak_convffn_deepnorm"


def layer_norm(x, g, b):
    xf = x.astype(jnp.float32)
    mu = jnp.mean(xf, axis=-1, keepdims=True)
    var = jnp.mean(jnp.square(xf - mu), axis=-1, keepdims=True)
    y = (xf - mu) * lax.rsqrt(var + LN_EPS)
    return (y * g.astype(jnp.float32) + b.astype(jnp.float32)).astype(x.dtype)


def pool_mixer(x, w_groups, scale):
    B, S, D = x.shape
    xf = x.astype(jnp.float32)
    cs = jnp.cumsum(xf, axis=1).reshape(B, S, N_POOL_GROUPS, POOL_GROUP_DIM)
    xg = xf.reshape(B, S, N_POOL_GROUPS, POOL_GROUP_DIM)
    t1 = jnp.arange(1, S + 1)
    outs = []
    for g, w in enumerate(POOL_WINDOWS):
        c = cs[:, :, g]
        shifted = jnp.pad(c, ((0, 0), (w, 0), (0, 0)))[:, :S]
        cnt = jnp.minimum(t1, w).astype(jnp.float32)[None, :, None]
        outs.append((c - shifted) / cnt - xg[:, :, g])
    pooled = jnp.stack(outs, axis=2).astype(x.dtype)
    y = jnp.einsum('bsgc,gcd->bsgd', pooled, w_groups).reshape(B, S, D)
    return y * scale


def stick_breaking_attention(x, w_qkv, w_o):
    B, S, D = x.shape
    qkv = x @ w_qkv
    q, k, v = jnp.split(qkv, 3, axis=-1)
    nb = S // Q_BLOCK
    q_blocks = q.reshape(B, nb, Q_BLOCK, N_HEADS, HEAD_DIM).transpose(1, 0, 3, 2, 4)
    k = k.reshape(B, S, N_HEADS, HEAD_DIM).transpose(0, 2, 1, 3)
    v = v.reshape(B, S, N_HEADS, HEAD_DIM).transpose(0, 2, 1, 3)
    k_pos = jnp.arange(S)
    scale = HEAD_DIM ** -0.5

    def block(args):
        qb, bi = args
        z = jnp.einsum('bhqd,bhkd->bhqk', qb, k).astype(jnp.float32) * scale
        q_pos = bi * Q_BLOCK + jnp.arange(Q_BLOCK)
        mask = k_pos[None, :] < q_pos[:, None]
        log_not = jnp.where(mask, jax.nn.log_sigmoid(-z), 0.0)
        rest = lax.cumsum(log_not, axis=3, reverse=True)
        log_a = jnp.where(mask, z + rest, -jnp.inf)
        a = jnp.exp(log_a)
        return jnp.einsum('bhqk,bhkd->bhqd', a.astype(v.dtype), v)

    o = lax.map(block, (q_blocks, jnp.arange(nb)))
    o = o.transpose(1, 0, 3, 2, 4).reshape(B, S, D)
    return o @ w_o


def conv_ffn(x, w_up, conv_w, conv_b, w_down):
    S = x.shape[1]
    h = x @ w_up
    hp = jnp.pad(h, ((0, 0), (CONV_WIDTH - 1, 0), (0, 0)))
    hc = conv_b + conv_w[0] * hp[:, 0:S]
    for kk in range(1, CONV_WIDTH):
        hc = hc + conv_w[kk] * hp[:, kk:kk + S]
    gate, val = jnp.split(hc, 2, axis=-1)
    return (jax.nn.silu(gate) * val) @ w_down


def _fwd_setup_inputs(seed: int = 0) -> dict:
    key = jax.random.key(seed)
    ks = jax.random.split(key, 14)
    f32 = jnp.float32
    nrm = lambda k, shp: jax.random.normal(k, shp, dtype=f32)
    x = nrm(ks[0], (BATCH, SEQ, D_MODEL))
    pool_w = nrm(ks[1], (N_POOL_LAYERS, N_POOL_GROUPS, POOL_GROUP_DIM, POOL_GROUP_DIM)) * (POOL_GROUP_DIM ** -0.5) * DEEPNORM_BETA
    pool_scale = 1.0 + 0.02 * nrm(ks[2], (N_POOL_LAYERS, D_MODEL))
    attn_w_qkv = nrm(ks[3], (N_ATTN_LAYERS, D_MODEL, 3 * D_MODEL)) * (D_MODEL ** -0.5)
    attn_w_o = nrm(ks[4], (N_ATTN_LAYERS, D_MODEL, D_MODEL)) * (D_MODEL ** -0.5) * DEEPNORM_BETA
    ffn_w_up = nrm(ks[5], (DEPTH, D_MODEL, 2 * D_FF)) * (D_MODEL ** -0.5)
    ffn_conv_w = nrm(ks[6], (DEPTH, CONV_WIDTH, 2 * D_FF)) * (CONV_WIDTH ** -0.5)
    ffn_conv_b = 0.02 * nrm(ks[7], (DEPTH, 2 * D_FF))
    ffn_w_down = nrm(ks[8], (DEPTH, D_FF, D_MODEL)) * (D_FF ** -0.5) * DEEPNORM_BETA
    ln_mix_g = 1.0 + 0.02 * nrm(ks[9], (DEPTH, D_MODEL))
    ln_mix_b = 0.02 * nrm(ks[10], (DEPTH, D_MODEL))
    ln_ffn_g = 1.0 + 0.02 * nrm(ks[11], (DEPTH, D_MODEL))
    ln_ffn_b = 0.02 * nrm(ks[12], (DEPTH, D_MODEL))
    return {"x": x, "pool_w": pool_w, "pool_scale": pool_scale,
            "attn_w_qkv": attn_w_qkv, "attn_w_o": attn_w_o,
            "ffn_w_up": ffn_w_up, "ffn_conv_w": ffn_conv_w, "ffn_conv_b": ffn_conv_b,
            "ffn_w_down": ffn_w_down, "ln_mix_g": ln_mix_g, "ln_mix_b": ln_mix_b,
            "ln_ffn_g": ln_ffn_g, "ln_ffn_b": ln_ffn_b}


def _fwd_reference(x, pool_w, pool_scale, attn_w_qkv, attn_w_o, ffn_w_up, ffn_conv_w,
              ffn_conv_b, ffn_w_down, ln_mix_g, ln_mix_b, ln_ffn_g, ln_ffn_b):
    for i in range(DEPTH):
        j = i // N_MIXERS
        if i % N_MIXERS == 0:
            y = pool_mixer(x, pool_w[j], pool_scale[j])
        else:
            y = stick_breaking_attention(x, attn_w_qkv[j], attn_w_o[j])
        x = layer_norm(DEEPNORM_ALPHA * x + y, ln_mix_g[i], ln_mix_b[i])
        f = conv_ffn(x, ffn_w_up[i], ffn_conv_w[i], ffn_conv_b[i], ffn_w_down[i])
        x = layer_norm(DEEPNORM_ALPHA * x + f, ln_ffn_g[i], ln_ffn_b[i])
    return x


import jax as _jax
import jax.numpy as _jnp

TWIN_FORMAT = 'train_step'
FWD_PARAMS = ['x', 'pool_w', 'pool_scale', 'attn_w_qkv', 'attn_w_o', 'ffn_w_up', 'ffn_conv_w', 'ffn_conv_b', 'ffn_w_down', 'ln_mix_g', 'ln_mix_b', 'ln_ffn_g', 'ln_ffn_b']
TWIN_WEIGHTS = ['pool_w', 'pool_scale', 'attn_w_qkv', 'attn_w_o', 'ffn_w_up', 'ffn_conv_w', 'ffn_conv_b', 'ffn_w_down', 'ln_mix_g', 'ln_mix_b', 'ln_ffn_g', 'ln_ffn_b']
TWIN_DIFF_INPUT = 'x'
TWIN_INPUTS = ['x', 'pool_w', 'pool_scale', 'attn_w_qkv', 'attn_w_o', 'ffn_w_up', 'ffn_conv_w', 'ffn_conv_b', 'ffn_w_down', 'ln_mix_g', 'ln_mix_b', 'ln_ffn_g', 'ln_ffn_b', 'loss_target', 'm_pool_w', 'm_pool_scale', 'm_attn_w_qkv', 'm_attn_w_o', 'm_ffn_w_up', 'm_ffn_conv_w', 'm_ffn_conv_b', 'm_ffn_w_down', 'm_ln_mix_g', 'm_ln_mix_b', 'm_ln_ffn_g', 'm_ln_ffn_b', 'v_pool_w', 'v_pool_scale', 'v_attn_w_qkv', 'v_attn_w_o', 'v_ffn_w_up', 'v_ffn_conv_w', 'v_ffn_conv_b', 'v_ffn_w_down', 'v_ln_mix_g', 'v_ln_mix_b', 'v_ln_ffn_g', 'v_ln_ffn_b']
TWIN_OUTPUTS = ['loss', 'grad_x', 'grad_pool_w', 'grad_pool_scale', 'grad_attn_w_qkv', 'grad_attn_w_o', 'grad_ffn_w_up', 'grad_ffn_conv_w', 'grad_ffn_conv_b', 'grad_ffn_w_down', 'grad_ln_mix_g', 'grad_ln_mix_b', 'grad_ln_ffn_g', 'grad_ln_ffn_b', 'delta_pool_w', 'delta_pool_scale', 'delta_attn_w_qkv', 'delta_attn_w_o', 'delta_ffn_w_up', 'delta_ffn_conv_w', 'delta_ffn_conv_b', 'delta_ffn_w_down', 'delta_ln_mix_g', 'delta_ln_mix_b', 'delta_ln_ffn_g', 'delta_ln_ffn_b', 'new_m_pool_w', 'new_m_pool_scale', 'new_m_attn_w_qkv', 'new_m_attn_w_o', 'new_m_ffn_w_up', 'new_m_ffn_conv_w', 'new_m_ffn_conv_b', 'new_m_ffn_w_down', 'new_m_ln_mix_g', 'new_m_ln_mix_b', 'new_m_ln_ffn_g', 'new_m_ln_ffn_b', 'new_v_pool_w', 'new_v_pool_scale', 'new_v_attn_w_qkv', 'new_v_attn_w_o', 'new_v_ffn_w_up', 'new_v_ffn_conv_w', 'new_v_ffn_conv_b', 'new_v_ffn_w_down', 'new_v_ln_mix_g', 'new_v_ln_mix_b', 'new_v_ln_ffn_g', 'new_v_ln_ffn_b']
TWIN_LEAF_KINDS = {'loss': 'loss', 'grad_x': 'grad_x', 'grad_pool_w': 'grad_w', 'grad_pool_scale': 'grad_w', 'grad_attn_w_qkv': 'grad_w', 'grad_attn_w_o': 'grad_w', 'grad_ffn_w_up': 'grad_w', 'grad_ffn_conv_w': 'grad_w', 'grad_ffn_conv_b': 'grad_w', 'grad_ffn_w_down': 'grad_w', 'grad_ln_mix_g': 'grad_w', 'grad_ln_mix_b': 'grad_w', 'grad_ln_ffn_g': 'grad_w', 'grad_ln_ffn_b': 'grad_w', 'delta_pool_w': 'delta_w', 'delta_pool_scale': 'delta_w', 'delta_attn_w_qkv': 'delta_w', 'delta_attn_w_o': 'delta_w', 'delta_ffn_w_up': 'delta_w', 'delta_ffn_conv_w': 'delta_w', 'delta_ffn_conv_b': 'delta_w', 'delta_ffn_w_down': 'delta_w', 'delta_ln_mix_g': 'delta_w', 'delta_ln_mix_b': 'delta_w', 'delta_ln_ffn_g': 'delta_w', 'delta_ln_ffn_b': 'delta_w', 'new_m_pool_w': 'new_m', 'new_m_pool_scale': 'new_m', 'new_m_attn_w_qkv': 'new_m', 'new_m_attn_w_o': 'new_m', 'new_m_ffn_w_up': 'new_m', 'new_m_ffn_conv_w': 'new_m', 'new_m_ffn_conv_b': 'new_m', 'new_m_ffn_w_down': 'new_m', 'new_m_ln_mix_g': 'new_m', 'new_m_ln_mix_b': 'new_m', 'new_m_ln_ffn_g': 'new_m', 'new_m_ln_ffn_b': 'new_m', 'new_v_pool_w': 'new_v', 'new_v_pool_scale': 'new_v', 'new_v_attn_w_qkv': 'new_v', 'new_v_attn_w_o': 'new_v', 'new_v_ffn_w_up': 'new_v', 'new_v_ffn_conv_w': 'new_v', 'new_v_ffn_conv_b': 'new_v', 'new_v_ffn_w_down': 'new_v', 'new_v_ln_mix_g': 'new_v', 'new_v_ln_mix_b': 'new_v', 'new_v_ln_ffn_g': 'new_v', 'new_v_ln_ffn_b': 'new_v'}


def _forward(args):
    return _fwd_reference(*[args[k] for k in FWD_PARAMS])


def _output_shape():
    def fwd():
        inp = _fwd_setup_inputs(0)
        return _fwd_reference(*[inp[k] for k in FWD_PARAMS])
    out = _jax.eval_shape(fwd)
    return out.shape, out.dtype

N_MICROBATCH = 1
ADAM_LR = 0.001
ADAM_B1 = 0.9
ADAM_B2 = 0.999
ADAM_EPS = 1e-08
ADAM_WD = 0.01
ADAM_STEP = 10
PER_EXAMPLE_BATCH_AXIS = {'x': 0, 'loss_target': 0}
SHARED_INPUTS = []
_WEIGHT_DTYPES = {'pool_w': _jnp.float32, 'pool_scale': _jnp.float32, 'attn_w_qkv': _jnp.float32, 'attn_w_o': _jnp.float32, 'ffn_w_up': _jnp.float32, 'ffn_conv_w': _jnp.float32, 'ffn_conv_b': _jnp.float32, 'ffn_w_down': _jnp.float32, 'ln_mix_g': _jnp.float32, 'ln_mix_b': _jnp.float32, 'ln_ffn_g': _jnp.float32, 'ln_ffn_b': _jnp.float32}
MOMENT_SCALE = {'pool_w': 6.793332e-02, 'pool_scale': 4.220827e-02, 'attn_w_qkv': 1.426714e-02, 'attn_w_o': 4.932936e-02, 'ffn_w_up': 1.203648e-02, 'ffn_conv_w': 1.206033e-02, 'ffn_conv_b': 1.462735e-02, 'ffn_w_down': 4.628782e-02, 'ln_mix_g': 9.978434e-01, 'ln_mix_b': 5.158579e-01, 'ln_ffn_g': 1.607209e+01, 'ln_ffn_b': 8.980315e-01}


def _to_microbatches(a, axis):
    t = _jnp.moveaxis(a, axis, 0)
    t = t.reshape((N_MICROBATCH, t.shape[0] // N_MICROBATCH) + t.shape[1:])
    return _jnp.moveaxis(t, 1, axis + 1)


def setup_inputs(seed: int = 0) -> dict:
    inp = _fwd_setup_inputs(seed)
    key = _jax.random.fold_in(_jax.random.key(seed), 7919)
    shape, _ = _output_shape()
    out = dict(inp)
    out["loss_target"] = _jax.random.normal(_jax.random.fold_in(key, 0), shape, _jnp.float32)
    for i, name in enumerate(TWIN_WEIGHTS):
        w = inp[name].astype(_jnp.float32)
        if MOMENT_SCALE is None:
            s = _jnp.sqrt(_jnp.mean(_jnp.square(w)) + 1e-30)
        else:
            s = MOMENT_SCALE[name]
        km, kv = _jax.random.split(_jax.random.fold_in(key, i + 1))
        out[name] = w
        out["m_" + name] = s * _jax.random.normal(km, w.shape, _jnp.float32)
        out["v_" + name] = (s * s) * _jax.random.uniform(kv, w.shape, _jnp.float32, 0.5, 1.5)
    if N_MICROBATCH > 1:
        for name, axis in PER_EXAMPLE_BATCH_AXIS.items():
            out[name] = _to_microbatches(out[name], axis)
    return {'x': out['x'], 'pool_w': out['pool_w'], 'pool_scale': out['pool_scale'], 'attn_w_qkv': out['attn_w_qkv'], 'attn_w_o': out['attn_w_o'], 'ffn_w_up': out['ffn_w_up'], 'ffn_conv_w': out['ffn_conv_w'], 'ffn_conv_b': out['ffn_conv_b'], 'ffn_w_down': out['ffn_w_down'], 'ln_mix_g': out['ln_mix_g'], 'ln_mix_b': out['ln_mix_b'], 'ln_ffn_g': out['ln_ffn_g'], 'ln_ffn_b': out['ln_ffn_b'], 'loss_target': out['loss_target'], 'm_pool_w': out['m_pool_w'], 'm_pool_scale': out['m_pool_scale'], 'm_attn_w_qkv': out['m_attn_w_qkv'], 'm_attn_w_o': out['m_attn_w_o'], 'm_ffn_w_up': out['m_ffn_w_up'], 'm_ffn_conv_w': out['m_ffn_conv_w'], 'm_ffn_conv_b': out['m_ffn_conv_b'], 'm_ffn_w_down': out['m_ffn_w_down'], 'm_ln_mix_g': out['m_ln_mix_g'], 'm_ln_mix_b': out['m_ln_mix_b'], 'm_ln_ffn_g': out['m_ln_ffn_g'], 'm_ln_ffn_b': out['m_ln_ffn_b'], 'v_pool_w': out['v_pool_w'], 'v_pool_scale': out['v_pool_scale'], 'v_attn_w_qkv': out['v_attn_w_qkv'], 'v_attn_w_o': out['v_attn_w_o'], 'v_ffn_w_up': out['v_ffn_w_up'], 'v_ffn_conv_w': out['v_ffn_conv_w'], 'v_ffn_conv_b': out['v_ffn_conv_b'], 'v_ffn_w_down': out['v_ffn_w_down'], 'v_ln_mix_g': out['v_ln_mix_g'], 'v_ln_mix_b': out['v_ln_mix_b'], 'v_ln_ffn_g': out['v_ln_ffn_g'], 'v_ln_ffn_b': out['v_ln_ffn_b']}


def _loss(weights, diff, rest, loss_target):
    with _jax.named_scope("forward"):
        args = {**rest, TWIN_DIFF_INPUT: diff, **{k: w.astype(_WEIGHT_DTYPES[k]) for k, w in weights.items()}}
        y = _forward(args)
    with _jax.named_scope("loss_head"):
        err = _jnp.square(y.astype(_jnp.float32) - loss_target)
        return 0.5 * _jnp.sum(_jnp.mean(err, axis=-1)) if err.ndim else 0.5 * err


def _adamw(w, g, m, v):
    m = ADAM_B1 * m + (1.0 - ADAM_B1) * g
    v = ADAM_B2 * v + (1.0 - ADAM_B2) * _jnp.square(g)
    m_hat = m / (1.0 - ADAM_B1 ** ADAM_STEP)
    v_hat = v / (1.0 - ADAM_B2 ** ADAM_STEP)
    delta = -ADAM_LR * (m_hat / (_jnp.sqrt(v_hat) + ADAM_EPS) + ADAM_WD * w)
    return delta, m, v


def reference(x, pool_w, pool_scale, attn_w_qkv, attn_w_o, ffn_w_up, ffn_conv_w, ffn_conv_b, ffn_w_down, ln_mix_g, ln_mix_b, ln_ffn_g, ln_ffn_b, loss_target, m_pool_w, m_pool_scale, m_attn_w_qkv, m_attn_w_o, m_ffn_w_up, m_ffn_conv_w, m_ffn_conv_b, m_ffn_w_down, m_ln_mix_g, m_ln_mix_b, m_ln_ffn_g, m_ln_ffn_b, v_pool_w, v_pool_scale, v_attn_w_qkv, v_attn_w_o, v_ffn_w_up, v_ffn_conv_w, v_ffn_conv_b, v_ffn_w_down, v_ln_mix_g, v_ln_mix_b, v_ln_ffn_g, v_ln_ffn_b):
    given = dict(x=x, pool_w=pool_w, pool_scale=pool_scale, attn_w_qkv=attn_w_qkv, attn_w_o=attn_w_o, ffn_w_up=ffn_w_up, ffn_conv_w=ffn_conv_w, ffn_conv_b=ffn_conv_b, ffn_w_down=ffn_w_down, ln_mix_g=ln_mix_g, ln_mix_b=ln_mix_b, ln_ffn_g=ln_ffn_g, ln_ffn_b=ln_ffn_b, loss_target=loss_target, m_pool_w=m_pool_w, m_pool_scale=m_pool_scale, m_attn_w_qkv=m_attn_w_qkv, m_attn_w_o=m_attn_w_o, m_ffn_w_up=m_ffn_w_up, m_ffn_conv_w=m_ffn_conv_w, m_ffn_conv_b=m_ffn_conv_b, m_ffn_w_down=m_ffn_w_down, m_ln_mix_g=m_ln_mix_g, m_ln_mix_b=m_ln_mix_b, m_ln_ffn_g=m_ln_ffn_g, m_ln_ffn_b=m_ln_ffn_b, v_pool_w=v_pool_w, v_pool_scale=v_pool_scale, v_attn_w_qkv=v_attn_w_qkv, v_attn_w_o=v_attn_w_o, v_ffn_w_up=v_ffn_w_up, v_ffn_conv_w=v_ffn_conv_w, v_ffn_conv_b=v_ffn_conv_b, v_ffn_w_down=v_ffn_w_down, v_ln_mix_g=v_ln_mix_g, v_ln_mix_b=v_ln_mix_b, v_ln_ffn_g=v_ln_ffn_g, v_ln_ffn_b=v_ln_ffn_b)
    weights = {n: given[n] for n in TWIN_WEIGHTS}
    shared = {n: given[n] for n in SHARED_INPUTS}
    per_example = {n: given[n] for n in ['x']}
    grad_fn = _jax.value_and_grad(_loss, argnums=(0, 1))

    def one_microbatch(ex, loss_target):
        ex = dict(ex)
        diff = ex.pop(TWIN_DIFF_INPUT)
        return grad_fn(weights, diff, {**shared, **ex}, loss_target)

    if N_MICROBATCH == 1:
        loss, (grad_w, grad_x) = one_microbatch(per_example, given["loss_target"])
    else:
        def body(carry, xs):
            loss_sum, grad_sum = carry
            l_k, (gw_k, gx_k) = one_microbatch(xs[0], xs[1])
            with _jax.named_scope("update"):
                return (loss_sum + l_k, _jax.tree.map(_jnp.add, grad_sum, gw_k)), gx_k

        init = (_jnp.zeros((), _jnp.float32), _jax.tree.map(_jnp.zeros_like, weights))
        (loss, grad_w), grad_x = _jax.lax.scan(body, init, (per_example, given["loss_target"]))
    with _jax.named_scope("update"):
        delta_w, new_m, new_v = {}, {}, {}
        for n in TWIN_WEIGHTS:
            delta_w[n], new_m[n], new_v[n] = _adamw(weights[n], grad_w[n], given["m_" + n], given["v_" + n])
    return (loss, grad_x, *[grad_w[n] for n in TWIN_WEIGHTS], *[delta_w[n] for n in TWIN_WEIGHTS],
            *[new_m[n] for n in TWIN_WEIGHTS], *[new_v[n] for n in TWIN_WEIGHTS])
```

```python
import functools

import jax
import jax.numpy as jnp
from jax import lax
from jax.experimental import pallas as pl
from jax.experimental.pallas import tpu as pltpu

F32, BF16 = jnp.float32, jnp.bfloat16
MESH = pl.DeviceIdType.MESH

LANES = 128
HEAD_DIM = 128
ATT_BLOCK = 128
POOL_WINDOWS = (2, 4, 8, 16)
POOL_HALO = 16
CONV_HALO = 8
LN_EPS = 1e-5
DEPTH = 4
ALPHA = (2.0 * DEPTH) ** 0.25
ATT_SCALE = HEAD_DIM ** -0.5
EXP_ZERO = 115.0
ADAM_LR, ADAM_B1, ADAM_B2, ADAM_EPS, ADAM_WD, ADAM_STEP = 0.001, 0.9, 0.999, 1e-08, 0.01, 10

VMEM_LIMIT = 56 << 20
ROW_TILE = 512
LN_ROW_TILE = 256
OPT_ROW_TILE = 128
SUM_ROW_TILE = 512


def _cp(n_axes):
    return pltpu.CompilerParams(dimension_semantics=("arbitrary",) * n_axes, vmem_limit_bytes=VMEM_LIMIT)


def _sds(shape, dtype):
    return jax.ShapeDtypeStruct(tuple(shape), dtype)


def _round_up(n, m):
    return (n + m - 1) // m * m


def _tile(n, cap, mult=8):
    if n <= cap:
        return n
    best = None
    for d in range(mult, cap + 1, mult):
        if n % d == 0:
            best = d
    assert best is not None, (n, cap)
    return best


_NT = (((1,), (1,)), ((), ()))
_TN = (((0,), (0,)), ((), ()))


def _mm_cols(a, b, out_dtype, name, transposed_b=False):
    t, k = a.shape
    g = b.shape[0]
    nb = b.shape[1] if transposed_b else b.shape[2]
    tm = _tile(t, ROW_TILE)

    def body(a_ref, b_ref, o_ref):
        if transposed_b:
            acc = lax.dot_general(a_ref[...], b_ref[...], _NT, preferred_element_type=F32)
        else:
            acc = jnp.dot(a_ref[...], b_ref[...], preferred_element_type=F32)
        o_ref[...] = acc.astype(o_ref.dtype)

    return pl.pallas_call(
        body, grid=(g, t // tm),
        in_specs=[pl.BlockSpec((tm, k), lambda gi, i: (i, 0)),
                  pl.BlockSpec((None,) + b.shape[1:], lambda gi, i: (gi, 0, 0))],
        out_specs=pl.BlockSpec((None, tm, nb), lambda gi, i: (gi, i, 0)),
        out_shape=_sds((g, t, nb), out_dtype), name=name, compiler_params=_cp(2))(a, b)


def _mm_nt_acc(a3, b3, res, kb, name):
    ga, t, ka = a3.shape
    gb, n, kbb = b3.shape
    na, nbk = ka // kb, kbb // kb
    groups = ga * na
    assert groups == gb * nbk
    tm = _tile(t, ROW_TILE)

    def body(a_ref, b_ref, res_ref, o_ref, acc):
        u = pl.program_id(1)

        @pl.when(u == 0)
        def _():
            acc[...] = ALPHA * res_ref[...]

        acc[...] += lax.dot_general(a_ref[...], b_ref[...], _NT, preferred_element_type=F32)

        @pl.when(u == groups - 1)
        def _():
            o_ref[...] = acc[...]

    return pl.pallas_call(
        body, grid=(t // tm, groups),
        in_specs=[pl.BlockSpec((None, tm, kb), lambda i, u: (u // na, i, u % na)),
                  pl.BlockSpec((None, n, kb), lambda i, u: (u // nbk, 0, u % nbk)),
                  pl.BlockSpec((tm, n), lambda i, u: (i, 0))],
        out_specs=pl.BlockSpec((tm, n), lambda i, u: (i, 0)),
        out_shape=_sds((t, n), F32), scratch_shapes=[pltpu.VMEM((tm, n), F32)],
        name=name, compiler_params=_cp(2))(a3, b3, res)


def _mm_tn(x3, dy3, out_shape, bm, bn, groups, x_idx, dy_idx, out_idx, name):
    t = x3.shape[1]
    tm = _tile(t, ROW_TILE)

    def body(x_ref, dy_ref, o_ref):
        @pl.when(pl.program_id(2) == 0)
        def _():
            o_ref[...] = jnp.zeros_like(o_ref)

        o_ref[...] += lax.dot_general(x_ref[...], dy_ref[...], _TN, preferred_element_type=F32)

    return pl.pallas_call(
        body, grid=(groups, x_idx[1], t // tm),
        in_specs=[pl.BlockSpec((None, tm, bm), lambda u, mb, i: (x_idx[0](u), i, x_idx[2](u, mb))),
                  pl.BlockSpec((None, tm, bn), lambda u, mb, i: (dy_idx[0](u), i, dy_idx[1](u)))],
        out_specs=pl.BlockSpec((None, bm, bn), lambda u, mb, i: (out_idx[0](u), out_idx[1](u, mb), out_idx[2](u))),
        out_shape=_sds(out_shape, F32), name=name, compiler_params=_cp(3))(x3, dy3)


def _layer_norm_rows(r, gamma, beta):
    mu = jnp.mean(r, axis=-1, keepdims=True)
    xc = r - mu
    var = jnp.mean(xc * xc, axis=-1, keepdims=True)
    return xc * lax.rsqrt(var + LN_EPS) * gamma + beta


def _mm_res_ln(a3, w3, res, gamma, beta, name):
    g, t, kb = a3.shape
    d = w3.shape[2]
    tm = _tile(t, LN_ROW_TILE)

    def body(a_ref, w_hbm, res_ref, g_ref, b_ref, r_ref, o_ref, ob_ref, w_vmem, sem):
        @pl.when(pl.program_id(0) == 0)
        def _():
            cp = pltpu.make_async_copy(w_hbm, w_vmem, sem)
            cp.start()
            cp.wait()

        acc = ALPHA * res_ref[...]
        for gi in range(g):
            acc = acc + jnp.dot(a_ref[gi], w_vmem[gi], preferred_element_type=F32)
        r_ref[...] = acc
        out = _layer_norm_rows(acc, g_ref[...], b_ref[...])
        o_ref[...] = out
        ob_ref[...] = out.astype(BF16)

    row = pl.BlockSpec((tm, d), lambda i: (i, 0))
    vec = pl.BlockSpec((1, d), lambda i: (0, 0))
    return pl.pallas_call(
        body, grid=(t // tm,),
        in_specs=[pl.BlockSpec((g, tm, kb), lambda i: (0, i, 0)), pl.BlockSpec(memory_space=pl.ANY), row, vec, vec],
        out_specs=[row, row, row],
        out_shape=[_sds((t, d), F32), _sds((t, d), F32), _sds((t, d), BF16)],
        scratch_shapes=[pltpu.VMEM(w3.shape, w3.dtype), pltpu.SemaphoreType.DMA],
        name=name, compiler_params=_cp(1))(a3, w3, res, gamma, beta)


def _ln_bwd(dout, r, gamma, name):
    t, d = r.shape
    tm = _tile(t, ROW_TILE)

    def body(do_ref, r_ref, g_ref, dr_ref, drb_ref, dg_ref, db_ref):
        @pl.when(pl.program_id(0) == 0)
        def _():
            dg_ref[...] = jnp.zeros_like(dg_ref)
            db_ref[...] = jnp.zeros_like(db_ref)

        rr = r_ref[...]
        do = do_ref[...]
        mu = jnp.mean(rr, axis=-1, keepdims=True)
        xc = rr - mu
        rstd = lax.rsqrt(jnp.mean(xc * xc, axis=-1, keepdims=True) + LN_EPS)
        xhat = xc * rstd
        dxh = do * g_ref[...]
        m1 = jnp.mean(dxh, axis=-1, keepdims=True)
        m2 = jnp.mean(dxh * xhat, axis=-1, keepdims=True)
        dr = rstd * (dxh - m1 - xhat * m2)
        dr_ref[...] = dr
        drb_ref[...] = dr.astype(BF16)
        dg_ref[...] += jnp.sum(do * xhat, axis=0, keepdims=True)
        db_ref[...] += jnp.sum(do, axis=0, keepdims=True)

    row = pl.BlockSpec((tm, d), lambda i: (i, 0))
    vec = pl.BlockSpec((1, d), lambda i: (0, 0))
    return pl.pallas_call(
        body, grid=(t // tm,), in_specs=[row, row, vec], out_specs=[row, row, vec, vec],
        out_shape=[_sds((t, d), F32), _sds((t, d), BF16), _sds((1, d), F32), _sds((1, d), F32)],
        name=name, compiler_params=_cp(1))(dout, r, gamma)


def _loss_and_grad(y, target, name):
    t, d = y.shape
    tm = _tile(t, ROW_TILE)
    steps = t // tm

    def body(y_ref, t_ref, loss_ref, dy_ref, acc):
        i = pl.program_id(0)

        @pl.when(i == 0)
        def _():
            acc[...] = jnp.zeros_like(acc)

        diff = y_ref[...] - t_ref[...]
        dy_ref[...] = diff * (1.0 / d)
        acc[...] += jnp.sum(diff * diff, axis=0, keepdims=True)

        @pl.when(i == steps - 1)
        def _():
            total = jnp.sum(acc[...], axis=1, keepdims=True) * (0.5 / d)
            loss_ref[...] = jnp.broadcast_to(total, loss_ref.shape)

    row = pl.BlockSpec((tm, d), lambda i: (i, 0))
    return pl.pallas_call(
        body, grid=(steps,), in_specs=[row, row],
        out_specs=[pl.BlockSpec((1, LANES), lambda i: (0, 0)), row],
        out_shape=[_sds((1, LANES), F32), _sds((t, d), F32)],
        scratch_shapes=[pltpu.VMEM((1, d), F32)], name=name, compiler_params=_cp(1))(y, target)


def _window_sums(ext, window, forward):
    n = ext.shape[0]
    s, span = ext, 1
    while span < window:
        s = s + pltpu.roll(s, (n - span) if forward else span, 0)
        span *= 2
    return s


def _pooled_group(main, halo, gi, row0):
    window = POOL_WINDOWS[gi]
    ext = jnp.concatenate([halo, main], axis=0)
    sums = _window_sums(ext, window, forward=False)[POOL_HALO:, :]
    pos = row0 + lax.broadcasted_iota(jnp.int32, (main.shape[0], 1), 0)
    cnt = jnp.minimum(pos + 1, window).astype(F32)
    return sums / cnt - main


def _pool_specs(t, d, tm):
    per = tm // POOL_HALO
    main = pl.BlockSpec((tm, d), lambda i: (i, 0))
    before = pl.BlockSpec((POOL_HALO, d), lambda i: (jnp.maximum(i * per - 1, 0), 0))
    return main, before


def _pool_fwd(x, w, scale, gamma, beta, name):
    t, d = x.shape
    ng, cg = w.shape[0], w.shape[1]
    tm = _tile(t, LN_ROW_TILE)

    def body(x_ref, h_ref, w_ref, s_ref, g_ref, b_ref, r_ref, o_ref, ob_ref):
        i = pl.program_id(0)
        for gi in range(ng):
            cols = pl.ds(gi * cg, cg)
            main = x_ref[:, cols]
            halo = jnp.where(i > 0, h_ref[:, cols], 0.0)
            pooled = _pooled_group(main, halo, gi, i * tm)
            y = jnp.dot(pooled.astype(BF16), w_ref[gi], preferred_element_type=F32)
            r_ref[:, cols] = ALPHA * main + y * s_ref[:, cols]
        out = _layer_norm_rows(r_ref[...], g_ref[...], b_ref[...])
        o_ref[...] = out
        ob_ref[...] = out.astype(BF16)

    main, before = _pool_specs(t, d, tm)
    vec = pl.BlockSpec((1, d), lambda i: (0, 0))
    return pl.pallas_call(
        body, grid=(t // tm,),
        in_specs=[main, before, pl.BlockSpec(w.shape, lambda i: (0, 0, 0)), vec, vec, vec],
        out_specs=[main, main, main],
        out_shape=[_sds((t, d), F32), _sds((t, d), F32), _sds((t, d), BF16)],
        name=name, compiler_params=_cp(1))(x, x, w, scale, gamma, beta)


def _pool_bwd(x, dy, w, scale, name):
    t, d = x.shape
    ng, cg = w.shape[0], w.shape[1]
    tm = _tile(t, LN_ROW_TILE)

    def body(x_ref, h_ref, dy_ref, w_ref, s_ref, dp_ref, dw_ref, ds_ref):
        i = pl.program_id(0)

        @pl.when(i == 0)
        def _():
            dw_ref[...] = jnp.zeros_like(dw_ref)
            ds_ref[...] = jnp.zeros_like(ds_ref)

        for gi in range(ng):
            cols = pl.ds(gi * cg, cg)
            main = x_ref[:, cols]
            halo = jnp.where(i > 0, h_ref[:, cols], 0.0)
            pooled = _pooled_group(main, halo, gi, i * tm).astype(BF16)
            y = jnp.dot(pooled, w_ref[gi], preferred_element_type=F32)
            dyg = dy_ref[:, cols]
            ds_ref[:, cols] += jnp.sum(dyg * y, axis=0, keepdims=True)
            dyw = (dyg * s_ref[:, cols]).astype(BF16)
            dw_ref[gi] += lax.dot_general(pooled, dyw, _TN, preferred_element_type=F32)
            dp_ref[:, cols] = lax.dot_general(dyw, w_ref[gi], _NT, preferred_element_type=F32)

    main, before = _pool_specs(t, d, tm)
    vec = pl.BlockSpec((1, d), lambda i: (0, 0))
    wspec = pl.BlockSpec(w.shape, lambda i: (0, 0, 0))
    return pl.pallas_call(
        body, grid=(t // tm,), in_specs=[main, before, main, wspec, vec],
        out_specs=[main, wspec, vec],
        out_shape=[_sds((t, d), F32), _sds(w.shape, F32), _sds((1, d), F32)],
        name=name, compiler_params=_cp(1))(x, x, dy, w, scale)


def _pool_adjoint(dp, dres, n_groups, name):
    t, d = dp.shape
    cg = d // n_groups
    tm = _tile(t, ROW_TILE)
    steps = t // tm
    per = tm // POOL_HALO

    def body(dp_ref, after_ref, dres_ref, dx_ref):
        i = pl.program_id(0)
        rows = lax.broadcasted_iota(jnp.int32, (tm, 1), 0)
        rows_after = lax.broadcasted_iota(jnp.int32, (POOL_HALO, 1), 0)
        for gi in range(n_groups):
            window = POOL_WINDOWS[gi]
            cols = pl.ds(gi * cg, cg)
            main = dp_ref[:, cols]
            cnt = jnp.minimum(i * tm + rows + 1, window).astype(F32)
            cnt_after = jnp.minimum((i + 1) * tm + rows_after + 1, window).astype(F32)
            after = jnp.where(i < steps - 1, after_ref[:, cols] / cnt_after, 0.0)
            ext = jnp.concatenate([main / cnt, after], axis=0)
            sums = _window_sums(ext, window, forward=True)[:tm, :]
            dx_ref[:, cols] = ALPHA * dres_ref[:, cols] + sums - main

    main = pl.BlockSpec((tm, d), lambda i: (i, 0))
    after = pl.BlockSpec((POOL_HALO, d), lambda i: (jnp.minimum((i + 1) * per, t // POOL_HALO - 1), 0))
    return pl.pallas_call(
        body, grid=(steps,), in_specs=[main, after, main], out_specs=main,
        out_shape=_sds((t, d), F32), name=name, compiler_params=_cp(1))(dp, dp, dres)


def _split_dot(x, tri):
    hi = x.astype(BF16)
    r1 = x - hi.astype(F32)
    mid = r1.astype(BF16)
    lo = (r1 - mid.astype(F32)).astype(BF16)
    dot = functools.partial(jnp.dot, preferred_element_type=F32)
    return dot(hi, tri) + dot(mid, tri) + dot(lo, tri)


def _att_block(q, k_j, i, j, carry_rest):
    b = ATT_BLOCK
    z = lax.dot_general(q, k_j, _NT, preferred_element_type=F32) * ATT_SCALE
    rows = lax.broadcasted_iota(jnp.int32, (b, b), 0)
    cols = lax.broadcasted_iota(jnp.int32, (b, b), 1)
    mask = (j * b + cols) < (i * b + rows)
    e = jnp.exp(-jnp.abs(z))
    log_not = jnp.where(mask, -(jnp.maximum(z, 0.0) + jnp.log1p(e)), 0.0)
    suffix = (rows >= cols).astype(BF16)
    rest = _split_dot(log_not, suffix) + carry_rest
    a = jnp.where(mask, jnp.exp(z + rest), 0.0)
    return z, mask, e, log_not, a


def _att_specs(qkv3, n_heads):
    t = qkv3.shape[1]
    cpb = qkv3.shape[2] // HEAD_DIM

    def slab(off):
        return pl.BlockSpec((None, t, HEAD_DIM), lambda h, i: ((off + h) // cpb, 0, (off + h) % cpb))

    q = pl.BlockSpec((None, ATT_BLOCK, HEAD_DIM), lambda h, i: (h // cpb, i, h % cpb))
    return q, slab(n_heads), slab(2 * n_heads)


def _key_bound(k_ref, kmax):
    kf = k_ref[...].astype(F32)
    kmax[0] = jnp.sqrt(jnp.max(jnp.sum(kf * kf, axis=1, keepdims=True)))


def _score_bound(q, kmax):
    qf = q.astype(F32)
    return ATT_SCALE * 1.001 * kmax[0] * jnp.sqrt(jnp.sum(qf * qf, axis=1, keepdims=True)) + 1e-3


def _attn_fwd(qkv3, n_heads, name):
    t = qkv3.shape[1]
    b = ATT_BLOCK

    def body(q_ref, k_ref, v_ref, o_ref, kmax):
        i = pl.program_id(1)

        @pl.when(i == 0)
        def _():
            _key_bound(k_ref, kmax)

        q = q_ref[...]
        zb = _score_bound(q, kmax)

        def cond(c):
            return jnp.logical_and(c[0] >= 0, jnp.max(c[1] + zb) > -EXP_ZERO)

        def step(c):
            j, rest, acc = c
            rows = pl.ds(pl.multiple_of(j * b, b), b)
            _, _, _, log_not, a = _att_block(q, k_ref[rows, :], i, j, rest)
            acc = acc + jnp.dot(a.astype(BF16), v_ref[rows, :], preferred_element_type=F32)
            return j - 1, rest + jnp.sum(log_not, axis=1, keepdims=True), acc

        _, _, acc = lax.while_loop(cond, step, (i, jnp.zeros((b, 1), F32), jnp.zeros((b, HEAD_DIM), F32)))
        o_ref[...] = acc.astype(o_ref.dtype)

    qs, ks, vs = _att_specs(qkv3, n_heads)
    return pl.pallas_call(
        body, grid=(n_heads, t // b), in_specs=[qs, ks, vs],
        out_specs=pl.BlockSpec((b, HEAD_DIM), lambda h, i: (i, h)),
        out_shape=_sds((t, n_heads * HEAD_DIM), BF16),
        scratch_shapes=[pltpu.SMEM((1,), F32)], name=name, compiler_params=_cp(2))(qkv3, qkv3, qkv3)


def _attn_bwd(qkv3, do, n_heads, name):
    t = qkv3.shape[1]
    b = ATT_BLOCK
    nq = t // b

    def body(q_ref, k_ref, v_ref, do_ref, dq_ref, dk_ref, dv_ref, kmax, dk_acc, dv_acc):
        i = pl.program_id(1)

        @pl.when(i == 0)
        def _():
            _key_bound(k_ref, kmax)
            dk_acc[...] = jnp.zeros_like(dk_acc)
            dv_acc[...] = jnp.zeros_like(dv_acc)

        q = q_ref[...]
        dout = do_ref[...]
        zb = _score_bound(q, kmax)
        zero_col = jnp.zeros((b, 1), F32)

        def cond(c):
            return jnp.logical_and(c[0] >= 0, jnp.max(c[1] + zb) > -EXP_ZERO)

        def block(j, rest):
            rows = pl.ds(pl.multiple_of(j * b, b), b)
            k_j, v_j = k_ref[rows, :], v_ref[rows, :]
            z, mask, e, log_not, a = _att_block(q, k_j, i, j, rest)
            dla = a * lax.dot_general(dout, v_j, _NT, preferred_element_type=F32)
            return rows, k_j, z, mask, e, log_not, a, dla

        def sweep1(c):
            j, rest, total = c
            rows, _, _, _, _, log_not, a, dla = block(j, rest)
            dv_acc[rows, :] += lax.dot_general(a.astype(BF16), dout, _TN, preferred_element_type=F32)
            return (j - 1, rest + jnp.sum(log_not, axis=1, keepdims=True),
                    total + jnp.sum(dla, axis=1, keepdims=True))

        _, _, total = lax.while_loop(cond, sweep1, (i, zero_col, zero_col))

        def sweep2(c):
            j, rest, later, dq = c
            rows, k_j, z, mask, e, log_not, _, dla = block(j, rest)
            ri = lax.broadcasted_iota(jnp.int32, (b, b), 0)
            ci = lax.broadcasted_iota(jnp.int32, (b, b), 1)
            inside = _split_dot(dla, (ri > ci).astype(BF16))
            dlog_not = total - later - inside
            inv = 1.0 / (1.0 + e)
            sig = jnp.where(z >= 0, inv, e * inv)
            dz = (jnp.where(mask, dla - sig * dlog_not, 0.0) * ATT_SCALE).astype(BF16)
            dq = dq + jnp.dot(dz, k_j, preferred_element_type=F32)
            dk_acc[rows, :] += lax.dot_general(dz, q, _TN, preferred_element_type=F32)
            return (j - 1, rest + jnp.sum(log_not, axis=1, keepdims=True),
                    later + jnp.sum(dla, axis=1, keepdims=True), dq)

        _, _, _, dq = lax.while_loop(cond, sweep2, (i, zero_col, zero_col, jnp.zeros((b, HEAD_DIM), F32)))
        dq_ref[...] = dq.astype(dq_ref.dtype)

        @pl.when(i == nq - 1)
        def _():
            dk_ref[...] = dk_acc[...].astype(dk_ref.dtype)
            dv_ref[...] = dv_acc[...].astype(dv_ref.dtype)

    qs, ks, vs = _att_specs(qkv3, n_heads)
    blk = pl.BlockSpec((b, HEAD_DIM), lambda h, i: (i, h))
    slab = pl.BlockSpec((t, HEAD_DIM), lambda h, i: (0, h))
    d = n_heads * HEAD_DIM
    return pl.pallas_call(
        body, grid=(n_heads, nq),
        in_specs=[qs, ks, vs, pl.BlockSpec((None, b, HEAD_DIM), lambda h, i: (0, i, h))],
        out_specs=[blk, slab, slab],
        out_shape=[_sds((t, d), BF16)] * 3,
        scratch_shapes=[pltpu.SMEM((1,), F32), pltpu.VMEM((t, HEAD_DIM), F32), pltpu.VMEM((t, HEAD_DIM), F32)],
        name=name, compiler_params=_cp(2))(qkv3, qkv3, qkv3, do)


def _conv_rows(main, halo, w_ref, b_ref):
    ext = jnp.concatenate([halo, main], axis=0)
    h1 = pltpu.roll(ext, 1, 0)[CONV_HALO:, :]
    h2 = pltpu.roll(ext, 2, 0)[CONV_HALO:, :]
    hc = b_ref[...] + w_ref[0:1, :] * h2
    hc = hc + w_ref[1:2, :] * h1
    hc = hc + w_ref[2:3, :] * main
    return hc, h1, h2


def _ffn_specs(t, fp, tm, half):
    per = tm // CONV_HALO
    main = lambda off: pl.BlockSpec((None, tm, fp), lambda g, i: (g + off, i, 0))
    before = lambda off: pl.BlockSpec((None, CONV_HALO, fp), lambda g, i: (g + off, jnp.maximum(i * per - 1, 0), 0))
    cw = lambda off: pl.BlockSpec((None, 3, fp), lambda g, i: (g + off, 0, 0))
    cb = lambda off: pl.BlockSpec((None, 1, fp), lambda g, i: (g + off, 0, 0))
    return [main(0), before(0), main(half), before(half), cw(0), cw(half), cb(0), cb(half)]


def _ffn_act(h, cw, cb, name):
    n, t, fp = h.shape
    half = n // 2
    tm = _tile(t, LN_ROW_TILE)

    def body(hg_ref, hgb_ref, hv_ref, hvb_ref, wg_ref, wv_ref, bg_ref, bv_ref, a_ref):
        first = pl.program_id(1) == 0
        gate, _, _ = _conv_rows(hg_ref[...], jnp.where(first, 0.0, hgb_ref[...]), wg_ref, bg_ref)
        val, _, _ = _conv_rows(hv_ref[...], jnp.where(first, 0.0, hvb_ref[...]), wv_ref, bv_ref)
        a_ref[...] = (gate * jax.nn.sigmoid(gate) * val).astype(a_ref.dtype)

    return pl.pallas_call(
        body, grid=(half, t // tm), in_specs=_ffn_specs(t, fp, tm, half),
        out_specs=pl.BlockSpec((None, tm, fp), lambda g, i: (g, i, 0)),
        out_shape=_sds((half, t, fp), BF16), name=name, compiler_params=_cp(2))(h, h, h, h, cw, cw, cb, cb)


def _ffn_act_bwd(h, da, cw, cb, name):
    n, t, fp = h.shape
    half = n // 2
    tm = _tile(t, LN_ROW_TILE)

    def body(hg_ref, hgb_ref, hv_ref, hvb_ref, wg_ref, wv_ref, bg_ref, bv_ref, da_ref, dhc_ref, dw_ref, db_ref):
        first = pl.program_id(1) == 0

        @pl.when(first)
        def _():
            dw_ref[...] = jnp.zeros_like(dw_ref)
            db_ref[...] = jnp.zeros_like(db_ref)

        hg, hv = hg_ref[...], hv_ref[...]
        gate, hg1, hg2 = _conv_rows(hg, jnp.where(first, 0.0, hgb_ref[...]), wg_ref, bg_ref)
        val, hv1, hv2 = _conv_rows(hv, jnp.where(first, 0.0, hvb_ref[...]), wv_ref, bv_ref)
        sig = jax.nn.sigmoid(gate)
        dact = da_ref[...]
        dgate = dact * val * (sig * (1.0 + gate * (1.0 - sig)))
        dval = dact * (gate * sig)
        dhc_ref[0] = dgate
        dhc_ref[1] = dval
        for s, (dd, shifted) in enumerate(((dgate, (hg2, hg1, hg)), (dval, (hv2, hv1, hv)))):
            db_ref[s] += jnp.sum(dd, axis=0, keepdims=True)
            for kk in range(3):
                dw_ref[s, kk:kk + 1, :] += jnp.sum(dd * shifted[kk], axis=0, keepdims=True)

    specs = _ffn_specs(t, fp, tm, half) + [pl.BlockSpec((None, tm, fp), lambda g, i: (g, i, 0))]
    return pl.pallas_call(
        body, grid=(half, t // tm), in_specs=specs,
        out_specs=[pl.BlockSpec((2, None, tm, fp), lambda g, i: (0, g, i, 0)),
                   pl.BlockSpec((2, None, 3, fp), lambda g, i: (0, g, 0, 0)),
                   pl.BlockSpec((2, None, 1, fp), lambda g, i: (0, g, 0, 0))],
        out_shape=[_sds((2, half, t, fp), F32), _sds((2, half, 3, fp), F32), _sds((2, half, 1, fp), F32)],
        name=name, compiler_params=_cp(2))(h, h, h, h, cw, cw, cb, cb, da)


def _conv_adjoint(dhc, cw, name):
    n, t, fp = dhc.shape
    tm = _tile(t, ROW_TILE)
    steps = t // tm
    per = tm // CONV_HALO

    def body(d_ref, after_ref, w_ref, o_ref):
        main = d_ref[...]
        after = jnp.where(pl.program_id(1) < steps - 1, after_ref[...], 0.0)
        ext = jnp.concatenate([main, after], axis=0)
        rows = ext.shape[0]
        d1 = pltpu.roll(ext, rows - 1, 0)[:tm, :]
        d2 = pltpu.roll(ext, rows - 2, 0)[:tm, :]
        o_ref[...] = (w_ref[2:3, :] * main + w_ref[1:2, :] * d1 + w_ref[0:1, :] * d2).astype(o_ref.dtype)

    main = pl.BlockSpec((None, tm, fp), lambda g, i: (g, i, 0))
    after = pl.BlockSpec((None, CONV_HALO, fp), lambda g, i: (g, jnp.minimum((i + 1) * per, t // CONV_HALO - 1), 0))
    return pl.pallas_call(
        body, grid=(n, steps), in_specs=[main, after, pl.BlockSpec((None, 3, fp), lambda g, i: (g, 0, 0))],
        out_specs=main, out_shape=_sds((n, t, fp), BF16), name=name, compiler_params=_cp(2))(dhc, dhc, cw)


def _place():
    x, y, c = lax.axis_index("x"), lax.axis_index("y"), lax.axis_index("c")
    chips = [(1 - x, y), (x, 1 - y), (1 - x, 1 - y)]
    return x, y, c, chips


def _any_specs(n):
    return [pl.BlockSpec(memory_space=pl.ANY)] * n


def _gather_weights(items, name):
    n = len(items)
    kinds = [it[0] for it in items]
    srcs = [it[1] for it in items]
    fulls = []
    for kind, s, _ in items:
        if kind == "lead":
            fulls.append(_sds((4,) + s.shape, s.dtype))
        elif kind == "pool":
            fulls.append(_sds((s.shape[0], 4 * s.shape[1], s.shape[2]), s.dtype))
    zero_bufs = [it[2] for it in items if it[0] == "down"]
    n_zero = len(zero_bufs)

    def body(*refs):
        src = refs[:n]
        down_in = refs[n:n + n_zero]
        outs = list(refs[n + n_zero:n + n_zero + n])
        send_sems, recv_sems, local_sems = refs[n + n_zero + n:]
        del down_in
        x, y, c, chips = _place()
        me, sibling = (x, y, c), (x, y, 1 - c)

        def src_half(m, half):
            s = src[m]
            if kinds[m] == "pool":
                gh = s.shape[0] // 2
                return s.at[pl.ds(half * gh, gh)]
            r2 = s.shape[0] // 2
            return s.at[pl.ds(half * r2, r2)]

        def dst_half(m, chip, half):
            k = 2 * chip[0] + chip[1]
            s, o = src[m], outs[m]
            if kinds[m] == "pool":
                gh, r = s.shape[0] // 2, s.shape[1]
                return o.at[pl.ds(half * gh, gh), pl.ds(k * r, r)]
            r2 = s.shape[0] // 2
            if kinds[m] == "down":
                return o.at[k // 2, pl.ds((k % 2) * 2 * r2 + half * r2, r2)]
            return o.at[k, pl.ds(half * r2, r2)]

        def remote(m, slot, chip, half, to, from_local_shard):
            return pltpu.make_async_remote_copy(
                src_ref=src_half(m, half) if from_local_shard else dst_half(m, chip, half),
                dst_ref=dst_half(m, chip, half), send_sem=send_sems.at[m, slot], recv_sem=recv_sems.at[m, slot],
                device_id=to, device_id_type=MESH)

        local = []
        for m in range(n):
            for half in range(2):
                cp = pltpu.make_async_copy(src_half(m, half), dst_half(m, (x, y), half), local_sems.at[m, half])
                cp.start()
                local.append(cp)
        first = [remote(m, j, (x, y), c, (*chip, c), True) for m in range(n) for j, chip in enumerate(chips)]
        for cp in first:
            cp.start()
        passed = []
        for j, chip in enumerate(chips):
            for m in range(n):
                remote(m, j, chip, c, me, False).wait_recv()
                cp = remote(m, 3 + j, chip, c, sibling, False)
                cp.start()
                passed.append(cp)
        for j, chip in enumerate(chips):
            for m in range(n):
                remote(m, 3 + j, chip, 1 - c, me, False).wait_recv()
        for cp in first + passed:
            cp.wait_send()
        for cp in local:
            cp.wait()

    out_shapes, aliases, zi = [], {}, 0
    li = 0
    for m, kind in enumerate(kinds):
        if kind == "down":
            zb = zero_bufs[zi]
            out_shapes.append(_sds(zb.shape, zb.dtype))
            aliases[n + zi] = m
            zi += 1
        else:
            out_shapes.append(fulls[li])
            li += 1
    return pl.pallas_call(
        body, in_specs=_any_specs(n + n_zero), out_specs=_any_specs(n), out_shape=out_shapes,
        input_output_aliases=aliases,
        scratch_shapes=[pltpu.SemaphoreType.DMA((n, 6)), pltpu.SemaphoreType.DMA((n, 6)),
                        pltpu.SemaphoreType.DMA((n, 2))],
        name=name)(*srcs, *zero_bufs)


def _send_sibling_halves(grads, name):
    n = len(grads)

    def body(*refs):
        g, out = refs[:n], refs[n:2 * n]
        send_sems, recv_sems = refs[2 * n:]
        x, y, c, _ = _place()
        copies = []
        for m in range(n):
            r2 = g[m].shape[1] // 2
            copies.append(pltpu.make_async_remote_copy(
                src_ref=g[m].at[:, pl.ds((1 - c) * r2, r2)], dst_ref=out[m],
                send_sem=send_sems.at[m], recv_sem=recv_sems.at[m], device_id=(x, y, 1 - c), device_id_type=MESH))
        for cp in copies:
            cp.start()
        for cp in copies:
            cp.wait_recv()
        for cp in copies:
            cp.wait_send()

    return pl.pallas_call(
        body, in_specs=_any_specs(n), out_specs=_any_specs(n),
        out_shape=[_sds((4, g.shape[1] // 2, g.shape[2]), g.dtype) for g in grads],
        scratch_shapes=[pltpu.SemaphoreType.DMA((n,)), pltpu.SemaphoreType.DMA((n,))], name=name)(*grads)


def _send_to_owner_chips(parts, name):
    n = len(parts)

    def body(*refs):
        p, out = refs[:n], refs[n:2 * n]
        send_sems, recv_sems = refs[2 * n:]
        _, _, c, chips = _place()
        copies = []
        for m in range(n):
            for j, chip in enumerate(chips):
                copies.append(pltpu.make_async_remote_copy(
                    src_ref=p[m].at[2 * chip[0] + chip[1]], dst_ref=out[m].at[j],
                    send_sem=send_sems.at[m, j], recv_sem=recv_sems.at[m, j],
                    device_id=(*chip, c), device_id_type=MESH))
        for cp in copies:
            cp.start()
        for cp in copies:
            cp.wait_recv()
        for cp in copies:
            cp.wait_send()

    return pl.pallas_call(
        body, in_specs=_any_specs(n), out_specs=_any_specs(n),
        out_shape=[_sds((3,) + p.shape[1:], p.dtype) for p in parts],
        scratch_shapes=[pltpu.SemaphoreType.DMA((n, 3)), pltpu.SemaphoreType.DMA((n, 3))], name=name)(*parts)


def _exchange_finished_halves(halves, name):
    n = len(halves)

    def body(*refs):
        h, out = refs[:n], refs[n:2 * n]
        send_sems, recv_sems, local_sems = refs[2 * n:]
        x, y, c, _ = _place()
        copies, local = [], []
        for m in range(n):
            r2 = h[m].shape[0]
            mine = out[m].at[pl.ds(c * r2, r2)]
            local.append(pltpu.make_async_copy(h[m], mine, local_sems.at[m]))
            copies.append(pltpu.make_async_remote_copy(
                src_ref=h[m], dst_ref=mine, send_sem=send_sems.at[m], recv_sem=recv_sems.at[m],
                device_id=(x, y, 1 - c), device_id_type=MESH))
        for cp in local + copies:
            cp.start()
        for cp in copies:
            cp.wait_recv()
        for cp in copies:
            cp.wait_send()
        for cp in local:
            cp.wait()

    return pl.pallas_call(
        body, in_specs=_any_specs(n), out_specs=_any_specs(n),
        out_shape=[_sds((2 * h.shape[0], h.shape[1]), h.dtype) for h in halves],
        scratch_shapes=[pltpu.SemaphoreType.DMA((n,)), pltpu.SemaphoreType.DMA((n,)), pltpu.SemaphoreType.DMA((n,))],
        name=name)(*halves)


def _all_reduce_small(v, name):
    rows = v.shape[0]

    def body(v_ref, out_ref, buf, send_sems, recv_sems, local_sem):
        x, y, c, chips = _place()
        me, sibling = (x, y, c), (x, y, 1 - c)

        def slot(px, py, pc):
            return buf.at[4 * px + 2 * py + pc]

        def copy(k, block, to, src=None):
            return pltpu.make_async_remote_copy(
                src_ref=slot(*block) if src is None else src, dst_ref=slot(*block),
                send_sem=send_sems.at[k], recv_sem=recv_sems.at[k], device_id=to, device_id_type=MESH)

        mine = pltpu.make_async_copy(v_ref, slot(*me), local_sem)
        mine.start()
        first = [copy(0, me, sibling, src=v_ref)]
        first += [copy(1 + j, me, (*chip, c), src=v_ref) for j, chip in enumerate(chips)]
        for cp in first:
            cp.start()
        passed = [copy(4 + j, (*chip, c), sibling) for j, chip in enumerate(chips)]
        for j, chip in enumerate(chips):
            copy(1 + j, (*chip, c), me).wait_recv()
            passed[j].start()
        copy(0, sibling, me).wait_recv()
        for j, chip in enumerate(chips):
            copy(4 + j, (*chip, 1 - c), me).wait_recv()
        for cp in first + passed:
            cp.wait_send()
        mine.wait()
        total = buf[0]
        for dev in range(1, 8):
            total = total + buf[dev]
        out_ref[...] = total

    vm = pl.BlockSpec(memory_space=pltpu.VMEM)
    return pl.pallas_call(
        body, in_specs=[vm], out_specs=vm, out_shape=_sds(v.shape, F32),
        scratch_shapes=[pltpu.VMEM((8, rows, LANES), F32), pltpu.SemaphoreType.DMA((7,)),
                        pltpu.SemaphoreType.DMA((7,)), pltpu.SemaphoreType.DMA],
        name=name, compiler_params=pltpu.CompilerParams(vmem_limit_bytes=VMEM_LIMIT))(v)


def _chip_partial(grad, from_sibling, core, name):
    _, r, cdim = grad.shape
    r2 = r // 2
    tr = _tile(r2, SUM_ROW_TILE)
    per = r2 // tr

    def body(core_ref, g_ref, s_ref, o_ref, ob_ref):
        del core_ref
        total = g_ref[...] + s_ref[...]
        o_ref[...] = total
        ob_ref[...] = total.astype(BF16)

    blk = pl.BlockSpec((None, tr, cdim), lambda k, i, core_ref: (k, i, 0))
    mine = pl.BlockSpec((None, tr, cdim), lambda k, i, core_ref: (k, core_ref[0] * per + i, 0))
    return pl.pallas_call(
        body,
        grid_spec=pltpu.PrefetchScalarGridSpec(num_scalar_prefetch=1, grid=(4, per), in_specs=[mine, blk],
                                               out_specs=[blk, blk]),
        out_shape=[_sds((4, r2, cdim), F32), _sds((4, r2, cdim), BF16)],
        name=name, compiler_params=_cp(2))(core, grad, from_sibling)


def _owner_sum(partial, from_chips, chip, name):
    _, r2, cdim = partial.shape
    tr = _tile(r2, SUM_ROW_TILE)

    def body(chip_ref, p_ref, f_ref, o_ref):
        del chip_ref
        total = p_ref[...]
        for j in range(3):
            total = total + f_ref[j].astype(F32)
        o_ref[...] = total

    return pl.pallas_call(
        body,
        grid_spec=pltpu.PrefetchScalarGridSpec(
            num_scalar_prefetch=1, grid=(r2 // tr,),
            in_specs=[pl.BlockSpec((None, tr, cdim), lambda i, chip_ref: (chip_ref[0], i, 0)),
                      pl.BlockSpec((3, tr, cdim), lambda i, chip_ref: (0, i, 0))],
            out_specs=pl.BlockSpec((tr, cdim), lambda i, chip_ref: (i, 0))),
        out_shape=_sds((r2, cdim), F32), name=name, compiler_params=_cp(1))(chip, partial, from_chips)


def _adamw(g, w, m, v, name):
    r, cdim = w.shape
    tr = _tile(r, OPT_ROW_TILE)
    c1 = 1.0 / (1.0 - ADAM_B1 ** ADAM_STEP)
    c2 = 1.0 / (1.0 - ADAM_B2 ** ADAM_STEP)

    def body(g_ref, w_ref, m_ref, v_ref, go_ref, d_ref, mo_ref, vo_ref):
        grad = g_ref[:, pl.ds(0, cdim)]
        m_new = ADAM_B1 * m_ref[...] + (1.0 - ADAM_B1) * grad
        v_new = ADAM_B2 * v_ref[...] + (1.0 - ADAM_B2) * (grad * grad)
        go_ref[...] = grad
        mo_ref[...] = m_new
        vo_ref[...] = v_new
        d_ref[...] = -ADAM_LR * ((m_new * c1) / (jnp.sqrt(v_new * c2) + ADAM_EPS) + ADAM_WD * w_ref[...])

    blk = pl.BlockSpec((tr, cdim), lambda i: (i, 0))
    gblk = pl.BlockSpec((tr, g.shape[1]), lambda i: (i, 0))
    return pl.pallas_call(
        body, grid=(r // tr,), in_specs=[gblk, blk, blk, blk], out_specs=[blk] * 4,
        out_shape=[_sds((r, cdim), F32)] * 4, name=name, compiler_params=_cp(1))(g, w, m, v)


def _pack_rows(vectors):
    flat = [v.reshape(-1) for v in vectors]
    sizes = [f.shape[0] for f in flat]
    total = sum(sizes)
    padded = _round_up(total, 8 * LANES)
    buf = jnp.concatenate(flat + [jnp.zeros((padded - total,), F32)])
    return buf.reshape(padded // LANES, LANES), sizes


def _unpack_rows(buf, sizes, shapes):
    flat = buf.reshape(-1)
    out, off = [], 0
    for n, shp in zip(sizes, shapes):
        out.append(flat[off:off + n].reshape(shp))
        off += n
    return out


def kernel(x, pool_w, pool_scale, attn_w_qkv, attn_w_o, ffn_w_up, ffn_conv_w, ffn_conv_b, ffn_w_down, ln_mix_g, ln_mix_b, ln_ffn_g, ln_ffn_b, loss_target, m_pool_w, m_pool_scale, m_attn_w_qkv, m_attn_w_o, m_ffn_w_up, m_ffn_conv_w, m_ffn_conv_b, m_ffn_w_down, m_ln_mix_g, m_ln_mix_b, m_ln_ffn_g, m_ln_ffn_b, v_pool_w, v_pool_scale, v_attn_w_qkv, v_attn_w_o, v_ffn_w_up, v_ffn_conv_w, v_ffn_conv_b, v_ffn_w_down, v_ln_mix_g, v_ln_mix_b, v_ln_ffn_g, v_ln_ffn_b):
    t, d = x.shape[1], x.shape[2]
    n_heads = d // HEAD_DIM
    n_groups = pool_w.shape[1]
    fs = ffn_w_up.shape[2]
    fp = _round_up(fs, LANES)
    rd = ffn_w_down.shape[1]
    assert 2 * rd == fs
    xi, yi, ci = lax.axis_index("x"), lax.axis_index("y"), lax.axis_index("c")
    chip = (2 * xi + yi).astype(jnp.int32)
    chip_arr, core_arr = chip.reshape(1), ci.astype(jnp.int32).reshape(1)

    x2 = x.reshape(t, d)
    target = loss_target.reshape(t, d)
    pad_cols = lambda a: jnp.pad(a, [(0, 0)] * (a.ndim - 1) + [(0, fp - fs)])

    weights = []
    for i in range(DEPTH):
        j = i // 2
        items = []
        if i % 2 == 0:
            items.append(("pool", pool_w[j].astype(BF16), None))
        else:
            items.append(("lead", attn_w_qkv[j].astype(BF16), None))
            items.append(("lead", attn_w_o[j].astype(BF16), None))
        items.append(("lead", pad_cols(ffn_w_up[i]).astype(BF16), None))
        items.append(("down", ffn_w_down[i].astype(BF16), jnp.zeros((2, fp, d), BF16)))
        weights.append(_gather_weights(items, name="gather_pool_layer" if i % 2 == 0 else "gather_attn_layer"))

    conv_b_all = pad_cols(ffn_conv_b.reshape(DEPTH, 4, 1, fs))
    cw_local = pad_cols(ffn_conv_w)
    slot = (jnp.arange(4, dtype=jnp.int32) == chip).astype(F32) * (1.0 - ci.astype(F32))
    cw_placed = slot[None, :, None, None] * cw_local[:, None]
    cw_buf, cw_sizes = _pack_rows([cw_placed])
    conv_w_all = _unpack_rows(_all_reduce_small(cw_buf, name="gather_conv_w"), cw_sizes, [cw_placed.shape])[0]

    gam = lambda a, i: a[i].reshape(1, d)

    saved = []
    cur, cur_b = x2, x2.astype(BF16)
    for i in range(DEPTH):
        j = i // 2
        w = weights[i]
        s = {"x_in": cur, "x_in_b": cur_b}
        if i % 2 == 0:
            w_pool, w_up, w_down = w
            s["scale"] = pool_scale[j].reshape(1, d)
            r1, x1, x1b = _pool_fwd(cur, w_pool, s["scale"], gam(ln_mix_g, i), gam(ln_mix_b, i), name="pool_fwd")
        else:
            w_qkv, w_o, w_up, w_down = w
            w_o3 = w_o.reshape(1, d, d)
            qkv = _mm_cols(cur_b, w_qkv, BF16, name="qkv_proj")
            o = _attn_fwd(qkv, n_heads, name="attn_fwd")
            s["qkv"], s["o"], s["w_o3"] = qkv, o, w_o3
            r1, x1, x1b = _mm_res_ln(o.reshape(1, t, d), w_o3, cur, gam(ln_mix_g, i), gam(ln_mix_b, i),
                                     name="attn_out_ln")
        h = _mm_cols(x1b, w_up, F32, name="ffn_up")
        a = _ffn_act(h, conv_w_all[i], conv_b_all[i], name="ffn_act")
        r2, x2n, x2b = _mm_res_ln(a, w_down, x1, gam(ln_ffn_g, i), gam(ln_ffn_b, i), name="ffn_down_ln")
        s.update(r1=r1, x1b=x1b, h=h, a=a, r2=r2)
        saved.append(s)
        cur, cur_b = x2n, x2b

    loss_row, dcur = _loss_and_grad(cur, target, name="loss")
    loss = lax.psum(loss_row[0, 0], ("x", "y", "c"))

    big_grads = [None] * DEPTH
    small = {}
    for i in reversed(range(DEPTH)):
        j = i // 2
        s, w = saved[i], weights[i]
        w_up, w_down = w[-2], w[-1]
        dr2, dr2b, small["ln_ffn_g", i], small["ln_ffn_b", i] = _ln_bwd(dcur, s["r2"], gam(ln_ffn_g, i), name="ln_bwd")
        da = _mm_cols(dr2b, w_down, F32, name="ffn_down_bwd_act", transposed_b=True)
        dr2b3 = dr2b.reshape(1, t, d)
        nmb = 2
        d_down = _mm_tn(s["a"], dr2b3, (2, fp, d), fp // nmb, d, 2,
                        (lambda u: u, nmb, lambda u, mb: mb), (lambda u: 0, lambda u: 0),
                        (lambda u: u, lambda u, mb: mb, lambda u: 0), name="ffn_down_bwd_w")
        dhc, dcw, dcb = _ffn_act_bwd(s["h"], da, conv_w_all[i], conv_b_all[i], name="ffn_act_bwd")
        small["conv_w", i], small["conv_b", i] = dcw, dcb
        dh = _conv_adjoint(dhc.reshape(4, t, fp), conv_w_all[i], name="ffn_conv_adjoint")
        dx1 = _mm_nt_acc(dh, w_up, dr2, fp, name="ffn_up_bwd_act")
        d_up = _mm_tn(s["x1b"].reshape(1, t, d), dh, (4, d, fp), d // 2, fp, 4,
                      (lambda u: 0, 2, lambda u, mb: mb), (lambda u: u, lambda u: 0),
                      (lambda u: u, lambda u, mb: mb, lambda u: 0), name="ffn_up_bwd_w")
        dr1, dr1b, small["ln_mix_g", i], small["ln_mix_b", i] = _ln_bwd(dx1, s["r1"], gam(ln_mix_g, i), name="ln_bwd")
        d_down4 = d_down[:, :fs].reshape(4, rd, d)
        if i % 2 == 0:
            dp, d_pool, small["pool_scale", j] = _pool_bwd(s["x_in"], dr1, w[0], s["scale"], name="pool_bwd")
            dcur = _pool_adjoint(dp, dr1, n_groups, name="pool_adjoint")
            cg = d // n_groups
            d_pool4 = d_pool.reshape(n_groups, 4, cg // 4, cg).transpose(1, 0, 2, 3).reshape(4, n_groups * (cg // 4), cg)
            big_grads[i] = [d_pool4, d_up, d_down4]
        else:
            w_qkv = w[0]
            do = _mm_cols(dr1b, s["w_o3"], BF16, name="attn_out_bwd_act", transposed_b=True)
            d_wo = _mm_tn(s["o"].reshape(1, t, d), dr1b.reshape(1, t, d), (1, d, d), d // 2, d, 1,
                          (lambda u: 0, 2, lambda u, mb: mb), (lambda u: 0, lambda u: 0),
                          (lambda u: 0, lambda u, mb: mb, lambda u: 0), name="attn_out_bwd_w")
            dq, dk, dv = _attn_bwd(s["qkv"], do, n_heads, name="attn_bwd")
            dqkv = jnp.stack([dq, dk, dv])
            cq = w_qkv.shape[2]
            kb = cq // 3
            na, nbk = d // kb, cq // kb
            dcur = _mm_nt_acc(dqkv, w_qkv, dr1, kb, name="qkv_bwd_act")
            d_qkv = _mm_tn(s["x_in_b"].reshape(1, t, d), dqkv, (4, d, cq), d // 2, kb, 3 * na,
                           (lambda u: 0, 2, lambda u, mb: mb), (lambda u: u // na, lambda u: u % na),
                           (lambda u: u // nbk, lambda u, mb: mb, lambda u: u % nbk), name="qkv_bwd_w")
            big_grads[i] = [d_qkv, d_wo.reshape(4, d // 4, d), d_up, d_down4]
    grad_x = dcur.reshape(1, t, d)

    reduced = []
    for i in range(DEPTH):
        tag = "pool" if i % 2 == 0 else "attn"
        grads = big_grads[i]
        from_sib = _send_sibling_halves(grads, name=f"reduce_{tag}_sibling")
        parts = [_chip_partial(g, fs_, core_arr, name="reduce_chip_partial") for g, fs_ in zip(grads, from_sib)]
        from_chips = _send_to_owner_chips([p[1] for p in parts], name=f"reduce_{tag}_chips")
        halves = [_owner_sum(p[0], fc, chip_arr, name="reduce_owner_sum") for p, fc in zip(parts, from_chips)]
        reduced.append(_exchange_finished_halves(halves, name=f"reduce_{tag}_halves"))

    names = [("pool_scale", j) for j in range(2)]
    for nm in ("ln_mix_g", "ln_mix_b", "ln_ffn_g", "ln_ffn_b", "conv_b", "conv_w"):
        names += [(nm, i) for i in range(DEPTH)]
    vecs = [small[k] for k in names]
    sbuf, ssizes = _pack_rows(vecs)
    summed = dict(zip(names, _unpack_rows(_all_reduce_small(sbuf, name="reduce_small"), ssizes, [v.shape for v in vecs])))

    def stack_layers(nm, count):
        return jnp.stack([summed[nm, i] for i in range(count)])

    g_small = {
        "pool_scale": stack_layers("pool_scale", 2).reshape(2, d),
        "ln_mix_g": stack_layers("ln_mix_g", DEPTH).reshape(DEPTH, d),
        "ln_mix_b": stack_layers("ln_mix_b", DEPTH).reshape(DEPTH, d),
        "ln_ffn_g": stack_layers("ln_ffn_g", DEPTH).reshape(DEPTH, d),
        "ln_ffn_b": stack_layers("ln_ffn_b", DEPTH).reshape(DEPTH, d),
        "conv_b": stack_layers("conv_b", DEPTH).reshape(DEPTH, 4, fp)[:, :, :fs].reshape(DEPTH, 4 * fs),
        "conv_w": lax.dynamic_index_in_dim(stack_layers("conv_w", DEPTH).reshape(DEPTH, 4, 3, fp), chip, axis=1,
                                           keepdims=False)[:, :, :fs],
    }
    w_small = {"pool_scale": (pool_scale, m_pool_scale, v_pool_scale), "ln_mix_g": (ln_mix_g, m_ln_mix_g, v_ln_mix_g),
               "ln_mix_b": (ln_mix_b, m_ln_mix_b, v_ln_mix_b), "ln_ffn_g": (ln_ffn_g, m_ln_ffn_g, v_ln_ffn_g),
               "ln_ffn_b": (ln_ffn_b, m_ln_ffn_b, v_ln_ffn_b), "conv_b": (ffn_conv_b, m_ffn_conv_b, v_ffn_conv_b),
               "conv_w": (ffn_conv_w, m_ffn_conv_w, v_ffn_conv_w)}
    order = list(g_small)
    packs = [_pack_rows([g_small[k] for k in order])[0]]
    for idx in range(3):
        packs.append(_pack_rows([w_small[k][idx] for k in order])[0])
    small_sizes = _pack_rows([g_small[k] for k in order])[1]
    small_out = _adamw(*packs, name="adamw_small")
    shapes = [g_small[k].shape for k in order]
    small_res = {k: [] for k in order}
    for arr in small_out:
        for k, val in zip(order, _unpack_rows(arr, small_sizes, shapes)):
            small_res[k].append(val)

    def opt_layers(per_layer_grads, w_all, m_all, v_all, name):
        outs = [[], [], [], []]
        for li, g in enumerate(per_layer_grads):
            shp = w_all[li].shape
            r = g.shape[0]
            res = _adamw(g, w_all[li].reshape(r, -1), m_all[li].reshape(r, -1), v_all[li].reshape(r, -1), name=name)
            for k in range(4):
                outs[k].append(res[k].reshape(shp))
        return [jnp.stack(o) for o in outs]

    cg = d // n_groups
    pool_g = [reduced[i][0].reshape(n_groups, cg // 4, cg).reshape(n_groups * (cg // 4), cg) for i in (0, 2)]
    big = {
        "pool_w": opt_layers(pool_g, pool_w, m_pool_w, v_pool_w, "adamw_pool"),
        "attn_w_qkv": opt_layers([reduced[i][0] for i in (1, 3)], attn_w_qkv, m_attn_w_qkv, v_attn_w_qkv, "adamw_qkv"),
        "attn_w_o": opt_layers([reduced[i][1] for i in (1, 3)], attn_w_o, m_attn_w_o, v_attn_w_o, "adamw_wo"),
        "ffn_w_up": opt_layers([reduced[i][-2] for i in range(DEPTH)], ffn_w_up, m_ffn_w_up, v_ffn_w_up, "adamw_up"),
        "ffn_w_down": opt_layers([reduced[i][-1] for i in range(DEPTH)], ffn_w_down, m_ffn_w_down, v_ffn_w_down,
                                 "adamw_down"),
    }

    def leaf(k, name):
        if name in big:
            return big[name][k]
        key = {"ffn_conv_w": "conv_w", "ffn_conv_b": "conv_b"}.get(name, name)
        return small_res[key][k]

    weight_names = ["pool_w", "pool_scale", "attn_w_qkv", "attn_w_o", "ffn_w_up", "ffn_conv_w", "ffn_conv_b",
                    "ffn_w_down", "ln_mix_g", "ln_mix_b", "ln_ffn_g", "ln_ffn_b"]
    outs = [loss, grad_x]
    for k in range(4):
        outs += [leaf(k, nm) for nm in weight_names]
    return tuple(outs)
```

```python
import functools

import jax
import jax.numpy as jnp
from jax import lax
from jax.experimental import pallas as pl
from jax.experimental.pallas import tpu as pltpu

F32, BF16 = jnp.float32, jnp.bfloat16
MESH = pl.DeviceIdType.MESH

LANES = 128
HEAD_DIM = 128
ATT_BLOCK = 128
ATT_WINDOW = 3 * ATT_BLOCK
POOL_WINDOWS = (2, 4, 8, 16)
POOL_HALO = 16
CONV_HALO = 8
LN_EPS = 1e-5
DEPTH = 4
ALPHA = (2.0 * DEPTH) ** 0.25
ATT_SCALE = HEAD_DIM ** -0.5
EXP_ZERO = 115.0
ADAM_LR, ADAM_B1, ADAM_B2, ADAM_EPS, ADAM_WD, ADAM_STEP = 0.001, 0.9, 0.999, 1e-08, 0.01, 10

VMEM_LIMIT = 56 << 20
ROW_TILE = 512
LN_ROW_TILE = 256
OPT_ROW_TILE = 128
SUM_ROW_TILE = 512


def _cp(n_axes):
    return pltpu.CompilerParams(dimension_semantics=("arbitrary",) * n_axes, vmem_limit_bytes=VMEM_LIMIT)


def _sds(shape, dtype):
    return jax.ShapeDtypeStruct(tuple(shape), dtype)


def _round_up(n, m):
    return (n + m - 1) // m * m


def _tile(n, cap, mult=8):
    if n <= cap:
        return n
    best = None
    for d in range(mult, cap + 1, mult):
        if n % d == 0:
            best = d
    assert best is not None, (n, cap)
    return best


_NT = (((1,), (1,)), ((), ()))
_TN = (((0,), (0,)), ((), ()))


def _mm_cols(a, b, out_dtype, name, transposed_b=False):
    t, k = a.shape
    g = b.shape[0]
    nb = b.shape[1] if transposed_b else b.shape[2]
    tm = _tile(t, ROW_TILE)

    def body(a_ref, b_ref, o_ref):
        if transposed_b:
            acc = lax.dot_general(a_ref[...], b_ref[...], _NT, preferred_element_type=F32)
        else:
            acc = jnp.dot(a_ref[...], b_ref[...], preferred_element_type=F32)
        o_ref[...] = acc.astype(o_ref.dtype)

    return pl.pallas_call(
        body, grid=(g, t // tm),
        in_specs=[pl.BlockSpec((tm, k), lambda gi, i: (i, 0)),
                  pl.BlockSpec((None,) + b.shape[1:], lambda gi, i: (gi, 0, 0))],
        out_specs=pl.BlockSpec((None, tm, nb), lambda gi, i: (gi, i, 0)),
        out_shape=_sds((g, t, nb), out_dtype), name=name, compiler_params=_cp(2))(a, b)


def _mm_nt_acc(a3, b3, res, kb, name):
    ga, t, ka = a3.shape
    gb, n, kbb = b3.shape
    na, nbk = ka // kb, kbb // kb
    groups = ga * na
    assert groups == gb * nbk
    tm = _tile(t, ROW_TILE)

    def body(a_ref, b_ref, res_ref, o_ref, acc):
        u = pl.program_id(1)

        @pl.when(u == 0)
        def _():
            acc[...] = ALPHA * res_ref[...]

        acc[...] += lax.dot_general(a_ref[...], b_ref[...], _NT, preferred_element_type=F32)

        @pl.when(u == groups - 1)
        def _():
            o_ref[...] = acc[...]

    return pl.pallas_call(
        body, grid=(t // tm, groups),
        in_specs=[pl.BlockSpec((None, tm, kb), lambda i, u: (u // na, i, u % na)),
                  pl.BlockSpec((None, n, kb), lambda i, u: (u // nbk, 0, u % nbk)),
                  pl.BlockSpec((tm, n), lambda i, u: (i, 0))],
        out_specs=pl.BlockSpec((tm, n), lambda i, u: (i, 0)),
        out_shape=_sds((t, n), F32), scratch_shapes=[pltpu.VMEM((tm, n), F32)],
        name=name, compiler_params=_cp(2))(a3, b3, res)


def _mm_tn(x3, dy3, out_shape, bm, bn, groups, x_idx, dy_idx, out_idx, name):
    t = x3.shape[1]
    tm = _tile(t, ROW_TILE)

    def body(x_ref, dy_ref, o_ref):
        @pl.when(pl.program_id(2) == 0)
        def _():
            o_ref[...] = jnp.zeros_like(o_ref)

        o_ref[...] += lax.dot_general(x_ref[...], dy_ref[...], _TN, preferred_element_type=F32)

    return pl.pallas_call(
        body, grid=(groups, x_idx[1], t // tm),
        in_specs=[pl.BlockSpec((None, tm, bm), lambda u, mb, i: (x_idx[0](u), i, x_idx[2](u, mb))),
                  pl.BlockSpec((None, tm, bn), lambda u, mb, i: (dy_idx[0](u), i, dy_idx[1](u)))],
        out_specs=pl.BlockSpec((None, bm, bn), lambda u, mb, i: (out_idx[0](u), out_idx[1](u, mb), out_idx[2](u))),
        out_shape=_sds(out_shape, F32), name=name, compiler_params=_cp(3))(x3, dy3)


def _layer_norm_rows(r, gamma, beta):
    mu = jnp.mean(r, axis=-1, keepdims=True)
    xc = r - mu
    var = jnp.mean(xc * xc, axis=-1, keepdims=True)
    return xc * lax.rsqrt(var + LN_EPS) * gamma + beta


def _mm_res_ln(a3, w3, res, gamma, beta, name):
    g, t, kb = a3.shape
    d = w3.shape[2]
    tm = _tile(t, LN_ROW_TILE)

    def body(a_ref, w_hbm, res_ref, g_ref, b_ref, r_ref, o_ref, ob_ref, w_vmem, sem):
        @pl.when(pl.program_id(0) == 0)
        def _():
            cp = pltpu.make_async_copy(w_hbm, w_vmem, sem)
            cp.start()
            cp.wait()

        acc = ALPHA * res_ref[...]
        for gi in range(g):
            acc = acc + jnp.dot(a_ref[gi], w_vmem[gi], preferred_element_type=F32)
        r_ref[...] = acc
        out = _layer_norm_rows(acc, g_ref[...], b_ref[...])
        o_ref[...] = out
        ob_ref[...] = out.astype(BF16)

    row = pl.BlockSpec((tm, d), lambda i: (i, 0))
    vec = pl.BlockSpec((1, d), lambda i: (0, 0))
    return pl.pallas_call(
        body, grid=(t // tm,),
        in_specs=[pl.BlockSpec((g, tm, kb), lambda i: (0, i, 0)), pl.BlockSpec(memory_space=pl.ANY), row, vec, vec],
        out_specs=[row, row, row],
        out_shape=[_sds((t, d), F32), _sds((t, d), F32), _sds((t, d), BF16)],
        scratch_shapes=[pltpu.VMEM(w3.shape, w3.dtype), pltpu.SemaphoreType.DMA],
        name=name, compiler_params=_cp(1))(a3, w3, res, gamma, beta)


def _ln_bwd(dout, r, gamma, name):
    t, d = r.shape
    tm = _tile(t, ROW_TILE)

    def body(do_ref, r_ref, g_ref, dr_ref, drb_ref, dg_ref, db_ref):
        @pl.when(pl.program_id(0) == 0)
        def _():
            dg_ref[...] = jnp.zeros_like(dg_ref)
            db_ref[...] = jnp.zeros_like(db_ref)

        rr = r_ref[...]
        do = do_ref[...]
        mu = jnp.mean(rr, axis=-1, keepdims=True)
        xc = rr - mu
        rstd = lax.rsqrt(jnp.mean(xc * xc, axis=-1, keepdims=True) + LN_EPS)
        xhat = xc * rstd
        dxh = do * g_ref[...]
        m1 = jnp.mean(dxh, axis=-1, keepdims=True)
        m2 = jnp.mean(dxh * xhat, axis=-1, keepdims=True)
        dr = rstd * (dxh - m1 - xhat * m2)
        dr_ref[...] = dr
        drb_ref[...] = dr.astype(BF16)
        dg_ref[...] += jnp.sum(do * xhat, axis=0, keepdims=True)
        db_ref[...] += jnp.sum(do, axis=0, keepdims=True)

    row = pl.BlockSpec((tm, d), lambda i: (i, 0))
    vec = pl.BlockSpec((1, d), lambda i: (0, 0))
    return pl.pallas_call(
        body, grid=(t // tm,), in_specs=[row, row, vec], out_specs=[row, row, vec, vec],
        out_shape=[_sds((t, d), F32), _sds((t, d), BF16), _sds((1, d), F32), _sds((1, d), F32)],
        name=name, compiler_params=_cp(1))(dout, r, gamma)


def _loss_and_grad(y, target, name):
    t, d = y.shape
    tm = _tile(t, ROW_TILE)
    steps = t // tm

    def body(y_ref, t_ref, loss_ref, dy_ref, acc):
        i = pl.program_id(0)

        @pl.when(i == 0)
        def _():
            acc[...] = jnp.zeros_like(acc)

        diff = y_ref[...] - t_ref[...]
        dy_ref[...] = diff * (1.0 / d)
        acc[...] += jnp.sum(diff * diff, axis=0, keepdims=True)

        @pl.when(i == steps - 1)
        def _():
            total = jnp.sum(acc[...], axis=1, keepdims=True) * (0.5 / d)
            loss_ref[...] = jnp.broadcast_to(total, loss_ref.shape)

    row = pl.BlockSpec((tm, d), lambda i: (i, 0))
    return pl.pallas_call(
        body, grid=(steps,), in_specs=[row, row],
        out_specs=[pl.BlockSpec((1, LANES), lambda i: (0, 0)), row],
        out_shape=[_sds((1, LANES), F32), _sds((t, d), F32)],
        scratch_shapes=[pltpu.VMEM((1, d), F32)], name=name, compiler_params=_cp(1))(y, target)


def _window_sums(ext, window, forward):
    n = ext.shape[0]
    s, span = ext, 1
    while span < window:
        s = s + pltpu.roll(s, (n - span) if forward else span, 0)
        span *= 2
    return s


def _pooled_group(main, halo, gi, row0):
    window = POOL_WINDOWS[gi]
    ext = jnp.concatenate([halo, main], axis=0)
    sums = _window_sums(ext, window, forward=False)[POOL_HALO:, :]
    pos = row0 + lax.broadcasted_iota(jnp.int32, (main.shape[0], 1), 0)
    cnt = jnp.minimum(pos + 1, window).astype(F32)
    return sums / cnt - main


def _pool_specs(t, d, tm):
    per = tm // POOL_HALO
    main = pl.BlockSpec((tm, d), lambda i: (i, 0))
    before = pl.BlockSpec((POOL_HALO, d), lambda i: (jnp.maximum(i * per - 1, 0), 0))
    return main, before


def _pool_fwd(x, w, scale, gamma, beta, name):
    t, d = x.shape
    ng, cg = w.shape[0], w.shape[1]
    tm = _tile(t, LN_ROW_TILE)

    def body(x_ref, h_ref, w_ref, s_ref, g_ref, b_ref, r_ref, o_ref, ob_ref):
        i = pl.program_id(0)
        for gi in range(ng):
            cols = pl.ds(gi * cg, cg)
            main = x_ref[:, cols]
            halo = jnp.where(i > 0, h_ref[:, cols], 0.0)
            pooled = _pooled_group(main, halo, gi, i * tm)
            y = jnp.dot(pooled.astype(BF16), w_ref[gi], preferred_element_type=F32)
            r_ref[:, cols] = ALPHA * main + y * s_ref[:, cols]
        out = _layer_norm_rows(r_ref[...], g_ref[...], b_ref[...])
        o_ref[...] = out
        ob_ref[...] = out.astype(BF16)

    main, before = _pool_specs(t, d, tm)
    vec = pl.BlockSpec((1, d), lambda i: (0, 0))
    return pl.pallas_call(
        body, grid=(t // tm,),
        in_specs=[main, before, pl.BlockSpec(w.shape, lambda i: (0, 0, 0)), vec, vec, vec],
        out_specs=[main, main, main],
        out_shape=[_sds((t, d), F32), _sds((t, d), F32), _sds((t, d), BF16)],
        name=name, compiler_params=_cp(1))(x, x, w, scale, gamma, beta)


def _pool_bwd(x, dy, w, scale, name):
    t, d = x.shape
    ng, cg = w.shape[0], w.shape[1]
    tm = _tile(t, LN_ROW_TILE)

    def body(x_ref, h_ref, dy_ref, w_ref, s_ref, dp_ref, dw_ref, ds_ref):
        i = pl.program_id(0)

        @pl.when(i == 0)
        def _():
            dw_ref[...] = jnp.zeros_like(dw_ref)
            ds_ref[...] = jnp.zeros_like(ds_ref)

        for gi in range(ng):
            cols = pl.ds(gi * cg, cg)
            main = x_ref[:, cols]
            halo = jnp.where(i > 0, h_ref[:, cols], 0.0)
            pooled = _pooled_group(main, halo, gi, i * tm).astype(BF16)
            y = jnp.dot(pooled, w_ref[gi], preferred_element_type=F32)
            dyg = dy_ref[:, cols]
            ds_ref[:, cols] += jnp.sum(dyg * y, axis=0, keepdims=True)
            dyw = (dyg * s_ref[:, cols]).astype(BF16)
            dw_ref[gi] += lax.dot_general(pooled, dyw, _TN, preferred_element_type=F32)
            dp_ref[:, cols] = lax.dot_general(dyw, w_ref[gi], _NT, preferred_element_type=F32)

    main, before = _pool_specs(t, d, tm)
    vec = pl.BlockSpec((1, d), lambda i: (0, 0))
    wspec = pl.BlockSpec(w.shape, lambda i: (0, 0, 0))
    return pl.pallas_call(
        body, grid=(t // tm,), in_specs=[main, before, main, wspec, vec],
        out_specs=[main, wspec, vec],
        out_shape=[_sds((t, d), F32), _sds(w.shape, F32), _sds((1, d), F32)],
        name=name, compiler_params=_cp(1))(x, x, dy, w, scale)


def _pool_adjoint(dp, dres, n_groups, name):
    t, d = dp.shape
    cg = d // n_groups
    tm = _tile(t, ROW_TILE)
    steps = t // tm
    per = tm // POOL_HALO

    def body(dp_ref, after_ref, dres_ref, dx_ref):
        i = pl.program_id(0)
        rows = lax.broadcasted_iota(jnp.int32, (tm, 1), 0)
        rows_after = lax.broadcasted_iota(jnp.int32, (POOL_HALO, 1), 0)
        for gi in range(n_groups):
            window = POOL_WINDOWS[gi]
            cols = pl.ds(gi * cg, cg)
            main = dp_ref[:, cols]
            cnt = jnp.minimum(i * tm + rows + 1, window).astype(F32)
            cnt_after = jnp.minimum((i + 1) * tm + rows_after + 1, window).astype(F32)
            after = jnp.where(i < steps - 1, after_ref[:, cols] / cnt_after, 0.0)
            ext = jnp.concatenate([main / cnt, after], axis=0)
            sums = _window_sums(ext, window, forward=True)[:tm, :]
            dx_ref[:, cols] = ALPHA * dres_ref[:, cols] + sums - main

    main = pl.BlockSpec((tm, d), lambda i: (i, 0))
    after = pl.BlockSpec((POOL_HALO, d), lambda i: (jnp.minimum((i + 1) * per, t // POOL_HALO - 1), 0))
    return pl.pallas_call(
        body, grid=(steps,), in_specs=[main, after, main], out_specs=main,
        out_shape=_sds((t, d), F32), name=name, compiler_params=_cp(1))(dp, dp, dres)


def _split_dot(x, tri):
    hi = x.astype(BF16)
    r1 = x - hi.astype(F32)
    mid = r1.astype(BF16)
    lo = (r1 - mid.astype(F32)).astype(BF16)
    dot = functools.partial(jnp.dot, preferred_element_type=F32)
    return dot(hi, tri) + dot(mid, tri) + dot(lo, tri)


def _att_window(q, k_w, i, start, hi, carry_rest, suffix):
    b, w = ATT_BLOCK, k_w.shape[0]
    z = lax.dot_general(q, k_w, _NT, preferred_element_type=F32) * ATT_SCALE
    qpos = i * b + lax.broadcasted_iota(jnp.int32, (b, w), 0)
    kpos = start + lax.broadcasted_iota(jnp.int32, (b, w), 1)
    mask = kpos < jnp.minimum(qpos, hi)
    e = jnp.exp(-jnp.abs(z))
    log_not = jnp.where(mask, -(jnp.maximum(z, 0.0) + jnp.log1p(e)), 0.0)
    rest = _split_dot(log_not, suffix) + carry_rest
    a = jnp.where(mask, jnp.exp(z + rest), 0.0)
    return z, mask, e, log_not, a


def _tri(w, strict):
    r = lax.broadcasted_iota(jnp.int32, (w, w), 0)
    c = lax.broadcasted_iota(jnp.int32, (w, w), 1)
    return ((r > c) if strict else (r >= c)).astype(BF16)


def _att_specs(qkv3, n_heads):
    t = qkv3.shape[1]
    cpb = qkv3.shape[2] // HEAD_DIM

    def slab(off):
        return pl.BlockSpec((None, t, HEAD_DIM), lambda h, i: ((off + h) // cpb, 0, (off + h) % cpb))

    q = pl.BlockSpec((None, ATT_BLOCK, HEAD_DIM), lambda h, i: (h // cpb, i, h % cpb))
    return q, slab(n_heads), slab(2 * n_heads)


def _key_bound(k_ref, kmax):
    kf = k_ref[...].astype(F32)
    kmax[0] = jnp.sqrt(jnp.max(jnp.sum(kf * kf, axis=1, keepdims=True)))


def _score_bound(q, kmax):
    qf = q.astype(F32)
    return ATT_SCALE * 1.001 * kmax[0] * jnp.sqrt(jnp.sum(qf * qf, axis=1, keepdims=True)) + 1e-3


def _window_rows(hi, w):
    start = jnp.maximum(hi - w, 0)
    return start, pl.ds(pl.multiple_of(start, ATT_BLOCK), w)


def _attn_fwd(qkv3, n_heads, name):
    t = qkv3.shape[1]
    b = ATT_BLOCK
    w = min(ATT_WINDOW, t)

    def body(q_ref, k_ref, v_ref, o_ref, kmax):
        i = pl.program_id(1)

        @pl.when(i == 0)
        def _():
            _key_bound(k_ref, kmax)

        q = q_ref[...]
        zb = _score_bound(q, kmax)
        suffix = _tri(w, strict=False)

        def cond(c):
            return jnp.logical_and(c[0] > 0, jnp.max(c[1] + zb) > -EXP_ZERO)

        def step(c):
            hi, rest, acc = c
            start, rows = _window_rows(hi, w)
            _, _, _, log_not, a = _att_window(q, k_ref[rows, :], i, start, hi, rest, suffix)
            acc = acc + jnp.dot(a.astype(BF16), v_ref[rows, :], preferred_element_type=F32)
            return start, rest + jnp.sum(log_not, axis=1, keepdims=True), acc

        init = ((i + 1) * b, jnp.zeros((b, 1), F32), jnp.zeros((b, HEAD_DIM), F32))
        _, _, acc = lax.while_loop(cond, step, init)
        o_ref[...] = acc.astype(o_ref.dtype)

    qs, ks, vs = _att_specs(qkv3, n_heads)
    return pl.pallas_call(
        body, grid=(n_heads, t // b), in_specs=[qs, ks, vs],
        out_specs=pl.BlockSpec((b, HEAD_DIM), lambda h, i: (i, h)),
        out_shape=_sds((t, n_heads * HEAD_DIM), BF16),
        scratch_shapes=[pltpu.SMEM((1,), F32)], name=name, compiler_params=_cp(2))(qkv3, qkv3, qkv3)


def _attn_bwd(qkv3, do, n_heads, name):
    t = qkv3.shape[1]
    b = ATT_BLOCK
    w = min(ATT_WINDOW, t)
    nq = t // b

    def body(q_ref, k_ref, v_ref, do_ref, dq_ref, dk_ref, dv_ref, kmax, dk_acc, dv_acc):
        i = pl.program_id(1)

        @pl.when(i == 0)
        def _():
            _key_bound(k_ref, kmax)
            dk_acc[...] = jnp.zeros_like(dk_acc)
            dv_acc[...] = jnp.zeros_like(dv_acc)

        q = q_ref[...]
        dout = do_ref[...]
        zb = _score_bound(q, kmax)
        zero_col = jnp.zeros((b, 1), F32)
        suffix = _tri(w, strict=False)
        strict_suffix = _tri(w, strict=True)
        hi0 = (i + 1) * b

        def cond(c):
            return jnp.logical_and(c[0] > 0, jnp.max(c[1] + zb) > -EXP_ZERO)

        def window(hi, rest):
            start, rows = _window_rows(hi, w)
            k_w, v_w = k_ref[rows, :], v_ref[rows, :]
            z, mask, e, log_not, a = _att_window(q, k_w, i, start, hi, rest, suffix)
            dla = a * lax.dot_general(dout, v_w, _NT, preferred_element_type=F32)
            return start, rows, k_w, z, mask, e, log_not, a, dla

        def sweep1(c):
            hi, rest, total = c
            start, rows, _, _, _, _, log_not, a, dla = window(hi, rest)
            dv_acc[rows, :] += lax.dot_general(a.astype(BF16), dout, _TN, preferred_element_type=F32)
            return (start, rest + jnp.sum(log_not, axis=1, keepdims=True),
                    total + jnp.sum(dla, axis=1, keepdims=True))

        _, _, total = lax.while_loop(cond, sweep1, (hi0, zero_col, zero_col))

        def sweep2(c):
            hi, rest, later, dq = c
            start, rows, k_w, z, mask, e, log_not, _, dla = window(hi, rest)
            inside = _split_dot(dla, strict_suffix)
            dlog_not = total - later - inside
            inv = 1.0 / (1.0 + e)
            sig = jnp.where(z >= 0, inv, e * inv)
            dz = (jnp.where(mask, dla - sig * dlog_not, 0.0) * ATT_SCALE).astype(BF16)
            dq = dq + jnp.dot(dz, k_w, preferred_element_type=F32)
            dk_acc[rows, :] += lax.dot_general(dz, q, _TN, preferred_element_type=F32)
            return (start, rest + jnp.sum(log_not, axis=1, keepdims=True),
                    later + jnp.sum(dla, axis=1, keepdims=True), dq)

        _, _, _, dq = lax.while_loop(cond, sweep2, (hi0, zero_col, zero_col, jnp.zeros((b, HEAD_DIM), F32)))
        dq_ref[...] = dq.astype(dq_ref.dtype)

        @pl.when(i == nq - 1)
        def _():
            dk_ref[...] = dk_acc[...].astype(dk_ref.dtype)
            dv_ref[...] = dv_acc[...].astype(dv_ref.dtype)

    qs, ks, vs = _att_specs(qkv3, n_heads)
    blk = pl.BlockSpec((b, HEAD_DIM), lambda h, i: (i, h))
    slab = pl.BlockSpec((t, HEAD_DIM), lambda h, i: (0, h))
    d = n_heads * HEAD_DIM
    return pl.pallas_call(
        body, grid=(n_heads, nq),
        in_specs=[qs, ks, vs, pl.BlockSpec((None, b, HEAD_DIM), lambda h, i: (0, i, h))],
        out_specs=[blk, slab, slab],
        out_shape=[_sds((t, d), BF16)] * 3,
        scratch_shapes=[pltpu.SMEM((1,), F32), pltpu.VMEM((t, HEAD_DIM), F32), pltpu.VMEM((t, HEAD_DIM), F32)],
        name=name, compiler_params=_cp(2))(qkv3, qkv3, qkv3, do)


def _conv_rows(main, halo, w_ref, b_ref):
    ext = jnp.concatenate([halo, main], axis=0)
    h1 = pltpu.roll(ext, 1, 0)[CONV_HALO:, :]
    h2 = pltpu.roll(ext, 2, 0)[CONV_HALO:, :]
    hc = b_ref[...] + w_ref[0:1, :] * h2
    hc = hc + w_ref[1:2, :] * h1
    hc = hc + w_ref[2:3, :] * main
    return hc, h1, h2


def _ffn_specs(t, fp, tm, half):
    per = tm // CONV_HALO
    main = lambda off: pl.BlockSpec((None, tm, fp), lambda g, i: (g + off, i, 0))
    before = lambda off: pl.BlockSpec((None, CONV_HALO, fp), lambda g, i: (g + off, jnp.maximum(i * per - 1, 0), 0))
    cw = lambda off: pl.BlockSpec((None, 3, fp), lambda g, i: (g + off, 0, 0))
    cb = lambda off: pl.BlockSpec((None, 1, fp), lambda g, i: (g + off, 0, 0))
    return [main(0), before(0), main(half), before(half), cw(0), cw(half), cb(0), cb(half)]


def _ffn_act(h, cw, cb, name):
    n, t, fp = h.shape
    half = n // 2
    tm = _tile(t, LN_ROW_TILE)

    def body(hg_ref, hgb_ref, hv_ref, hvb_ref, wg_ref, wv_ref, bg_ref, bv_ref, a_ref):
        first = pl.program_id(1) == 0
        gate, _, _ = _conv_rows(hg_ref[...], jnp.where(first, 0.0, hgb_ref[...]), wg_ref, bg_ref)
        val, _, _ = _conv_rows(hv_ref[...], jnp.where(first, 0.0, hvb_ref[...]), wv_ref, bv_ref)
        a_ref[...] = (gate * jax.nn.sigmoid(gate) * val).astype(a_ref.dtype)

    return pl.pallas_call(
        body, grid=(half, t // tm), in_specs=_ffn_specs(t, fp, tm, half),
        out_specs=pl.BlockSpec((None, tm, fp), lambda g, i: (g, i, 0)),
        out_shape=_sds((half, t, fp), BF16), name=name, compiler_params=_cp(2))(h, h, h, h, cw, cw, cb, cb)


def _ffn_act_bwd(h, da, cw, cb, name):
    n, t, fp = h.shape
    half = n // 2
    tm = _tile(t, LN_ROW_TILE)

    def body(hg_ref, hgb_ref, hv_ref, hvb_ref, wg_ref, wv_ref, bg_ref, bv_ref, da_ref, dhc_ref, dw_ref, db_ref):
        first = pl.program_id(1) == 0

        @pl.when(first)
        def _():
            dw_ref[...] = jnp.zeros_like(dw_ref)
            db_ref[...] = jnp.zeros_like(db_ref)

        hg, hv = hg_ref[...], hv_ref[...]
        gate, hg1, hg2 = _conv_rows(hg, jnp.where(first, 0.0, hgb_ref[...]), wg_ref, bg_ref)
        val, hv1, hv2 = _conv_rows(hv, jnp.where(first, 0.0, hvb_ref[...]), wv_ref, bv_ref)
        sig = jax.nn.sigmoid(gate)
        dact = da_ref[...]
        dgate = dact * val * (sig * (1.0 + gate * (1.0 - sig)))
        dval = dact * (gate * sig)
        dhc_ref[0] = dgate
        dhc_ref[1] = dval
        for s, (dd, shifted) in enumerate(((dgate, (hg2, hg1, hg)), (dval, (hv2, hv1, hv)))):
            db_ref[s] += jnp.sum(dd, axis=0, keepdims=True)
            for kk in range(3):
                dw_ref[s, kk:kk + 1, :] += jnp.sum(dd * shifted[kk], axis=0, keepdims=True)

    specs = _ffn_specs(t, fp, tm, half) + [pl.BlockSpec((None, tm, fp), lambda g, i: (g, i, 0))]
    return pl.pallas_call(
        body, grid=(half, t // tm), in_specs=specs,
        out_specs=[pl.BlockSpec((2, None, tm, fp), lambda g, i: (0, g, i, 0)),
                   pl.BlockSpec((2, None, 3, fp), lambda g, i: (0, g, 0, 0)),
                   pl.BlockSpec((2, None, 1, fp), lambda g, i: (0, g, 0, 0))],
        out_shape=[_sds((2, half, t, fp), F32), _sds((2, half, 3, fp), F32), _sds((2, half, 1, fp), F32)],
        name=name, compiler_params=_cp(2))(h, h, h, h, cw, cw, cb, cb, da)


def _conv_adjoint(dhc, cw, name):
    n, t, fp = dhc.shape
    tm = _tile(t, ROW_TILE)
    steps = t // tm
    per = tm // CONV_HALO

    def body(d_ref, after_ref, w_ref, o_ref):
        main = d_ref[...]
        after = jnp.where(pl.program_id(1) < steps - 1, after_ref[...], 0.0)
        ext = jnp.concatenate([main, after], axis=0)
        rows = ext.shape[0]
        d1 = pltpu.roll(ext, rows - 1, 0)[:tm, :]
        d2 = pltpu.roll(ext, rows - 2, 0)[:tm, :]
        o_ref[...] = (w_ref[2:3, :] * main + w_ref[1:2, :] * d1 + w_ref[0:1, :] * d2).astype(o_ref.dtype)

    main = pl.BlockSpec((None, tm, fp), lambda g, i: (g, i, 0))
    after = pl.BlockSpec((None, CONV_HALO, fp), lambda g, i: (g, jnp.minimum((i + 1) * per, t // CONV_HALO - 1), 0))
    return pl.pallas_call(
        body, grid=(n, steps), in_specs=[main, after, pl.BlockSpec((None, 3, fp), lambda g, i: (g, 0, 0))],
        out_specs=main, out_shape=_sds((n, t, fp), BF16), name=name, compiler_params=_cp(2))(dhc, dhc, cw)


def _place():
    x, y, c = lax.axis_index("x"), lax.axis_index("y"), lax.axis_index("c")
    chips = [(1 - x, y), (x, 1 - y), (1 - x, 1 - y)]
    return x, y, c, chips


def _any_specs(n):
    return [pl.BlockSpec(memory_space=pl.ANY)] * n


def _place_shard(kind, w, chip, name, cols=None, base=None):
    if kind == "pool":
        g, r, cdim = w.shape

        def body(chip_ref, w_ref, o_ref):
            del chip_ref
            o_ref[...] = w_ref[...].astype(BF16)

        return pl.pallas_call(
            body,
            grid_spec=pltpu.PrefetchScalarGridSpec(
                num_scalar_prefetch=1, grid=(1,),
                in_specs=[pl.BlockSpec((g, r, cdim), lambda i, chip_ref: (0, 0, 0))],
                out_specs=pl.BlockSpec((g, r, cdim), lambda i, chip_ref: (0, chip_ref[0], 0))),
            out_shape=_sds((g, 4 * r, cdim), BF16), name=name, compiler_params=_cp(1))(chip, w)

    r, cs = w.shape
    if kind == "lead":
        cols = cols or cs
        tr = _tile(r, ROW_TILE, 16)

        def body(chip_ref, w_ref, o_ref):
            del chip_ref
            if cols > cs:
                o_ref[:, pl.ds(cols - LANES, LANES)] = jnp.zeros((tr, LANES), BF16)
            o_ref[:, pl.ds(0, cs)] = w_ref[...].astype(BF16)

        return pl.pallas_call(
            body,
            grid_spec=pltpu.PrefetchScalarGridSpec(
                num_scalar_prefetch=1, grid=(r // tr,),
                in_specs=[pl.BlockSpec((tr, cs), lambda i, chip_ref: (i, 0))],
                out_specs=pl.BlockSpec((None, tr, cols), lambda i, chip_ref: (chip_ref[0], i, 0))),
            out_shape=_sds((4, r, cols), BF16), name=name, compiler_params=_cp(1))(chip, w)

    assert kind == "down"
    tr = r // 2 if (r // 2) % 16 == 0 else r
    per = r // tr

    def body(chip_ref, w_ref, base_ref, o_ref):
        del chip_ref, base_ref
        o_ref[...] = w_ref[...].astype(BF16)

    return pl.pallas_call(
        body,
        grid_spec=pltpu.PrefetchScalarGridSpec(
            num_scalar_prefetch=1, grid=(per,),
            in_specs=[pl.BlockSpec((tr, cs), lambda i, chip_ref: (i, 0)), pl.BlockSpec(memory_space=pl.ANY)],
            out_specs=pl.BlockSpec((None, tr, cs), lambda i, chip_ref: (chip_ref[0] // 2, (chip_ref[0] % 2) * per + i, 0))),
        out_shape=_sds(base.shape, BF16), input_output_aliases={2: 0},
        name=name, compiler_params=_cp(1))(chip, w, base)


def _gather_weights(items, name):
    n = len(items)
    kinds = [it[0] for it in items]
    bufs = [it[1] for it in items]
    shard_rows = [it[2] for it in items]

    def body(*refs):
        outs = refs[n:2 * n]
        send_sems, recv_sems = refs[2 * n:]
        x, y, c, chips = _place()
        me, sibling = (x, y, c), (x, y, 1 - c)

        def half_of(m, chip, half):
            k = 2 * chip[0] + chip[1]
            o, r = outs[m], shard_rows[m]
            if kinds[m] == "pool":
                gh = o.shape[0] // 2
                return o.at[pl.ds(half * gh, gh), pl.ds(k * r, r)]
            r2 = r // 2
            if kinds[m] == "down":
                return o.at[k // 2, pl.ds((k % 2) * r + half * r2, r2)]
            return o.at[k, pl.ds(half * r2, r2)]

        def remote(m, slot, chip, half, to):
            return pltpu.make_async_remote_copy(
                src_ref=half_of(m, chip, half), dst_ref=half_of(m, chip, half),
                send_sem=send_sems.at[m, slot], recv_sem=recv_sems.at[m, slot], device_id=to, device_id_type=MESH)

        first = [remote(m, j, (x, y), c, (*chip, c)) for m in range(n) for j, chip in enumerate(chips)]
        for cp in first:
            cp.start()
        passed = []
        for j, chip in enumerate(chips):
            for m in range(n):
                remote(m, j, chip, c, me).wait_recv()
                cp = remote(m, 3 + j, chip, c, sibling)
                cp.start()
                passed.append(cp)
        for j, chip in enumerate(chips):
            for m in range(n):
                remote(m, 3 + j, chip, 1 - c, me).wait_recv()
        for cp in first + passed:
            cp.wait_send()

    return pl.pallas_call(
        body, in_specs=_any_specs(n), out_specs=_any_specs(n), out_shape=[_sds(b.shape, b.dtype) for b in bufs],
        input_output_aliases={m: m for m in range(n)},
        scratch_shapes=[pltpu.SemaphoreType.DMA((n, 6)), pltpu.SemaphoreType.DMA((n, 6))],
        name=name)(*bufs)


def _send_sibling_halves(grads, name):
    n = len(grads)

    def body(*refs):
        g, out = refs[:n], refs[n:2 * n]
        send_sems, recv_sems = refs[2 * n:]
        x, y, c, _ = _place()
        copies = []
        for m in range(n):
            r2 = g[m].shape[1] // 2
            copies.append(pltpu.make_async_remote_copy(
                src_ref=g[m].at[:, pl.ds((1 - c) * r2, r2)], dst_ref=out[m],
                send_sem=send_sems.at[m], recv_sem=recv_sems.at[m], device_id=(x, y, 1 - c), device_id_type=MESH))
        for cp in copies:
            cp.start()
        for cp in copies:
            cp.wait_recv()
        for cp in copies:
            cp.wait_send()

    return pl.pallas_call(
        body, in_specs=_any_specs(n), out_specs=_any_specs(n),
        out_shape=[_sds((4, g.shape[1] // 2, g.shape[2]), g.dtype) for g in grads],
        scratch_shapes=[pltpu.SemaphoreType.DMA((n,)), pltpu.SemaphoreType.DMA((n,))], name=name)(*grads)


def _send_to_owner_chips(parts, name):
    n = len(parts)

    def body(*refs):
        p, out = refs[:n], refs[n:2 * n]
        send_sems, recv_sems = refs[2 * n:]
        _, _, c, chips = _place()
        copies = []
        for m in range(n):
            for j, chip in enumerate(chips):
                copies.append(pltpu.make_async_remote_copy(
                    src_ref=p[m].at[2 * chip[0] + chip[1]], dst_ref=out[m].at[j],
                    send_sem=send_sems.at[m, j], recv_sem=recv_sems.at[m, j],
                    device_id=(*chip, c), device_id_type=MESH))
        for cp in copies:
            cp.start()
        for cp in copies:
            cp.wait_recv()
        for cp in copies:
            cp.wait_send()

    return pl.pallas_call(
        body, in_specs=_any_specs(n), out_specs=_any_specs(n),
        out_shape=[_sds((3,) + p.shape[1:], p.dtype) for p in parts],
        scratch_shapes=[pltpu.SemaphoreType.DMA((n, 3)), pltpu.SemaphoreType.DMA((n, 3))], name=name)(*parts)


def _exchange_finished_halves(shards, name):
    n = len(shards)

    def body(*refs):
        out = refs[n:2 * n]
        send_sems, recv_sems = refs[2 * n:]
        x, y, c, _ = _place()
        copies = []
        for m in range(n):
            r2 = out[m].shape[0] // 2
            mine = out[m].at[pl.ds(c * r2, r2)]
            copies.append(pltpu.make_async_remote_copy(
                src_ref=mine, dst_ref=mine, send_sem=send_sems.at[m], recv_sem=recv_sems.at[m],
                device_id=(x, y, 1 - c), device_id_type=MESH))
        for cp in copies:
            cp.start()
        for m in range(n):
            r2 = out[m].shape[0] // 2
            theirs = out[m].at[pl.ds((1 - c) * r2, r2)]
            pltpu.make_async_remote_copy(
                src_ref=theirs, dst_ref=theirs, send_sem=send_sems.at[m], recv_sem=recv_sems.at[m],
                device_id=(x, y, 1 - c), device_id_type=MESH).wait_recv()
        for cp in copies:
            cp.wait_send()

    return pl.pallas_call(
        body, in_specs=_any_specs(n), out_specs=_any_specs(n), out_shape=[_sds(s.shape, s.dtype) for s in shards],
        input_output_aliases={m: m for m in range(n)},
        scratch_shapes=[pltpu.SemaphoreType.DMA((n,)), pltpu.SemaphoreType.DMA((n,))], name=name)(*shards)


def _all_reduce_small(v, name):
    rows = v.shape[0]

    def body(v_ref, out_ref, buf, send_sems, recv_sems, local_sem):
        x, y, c, chips = _place()
        me, sibling = (x, y, c), (x, y, 1 - c)

        def slot(px, py, pc):
            return buf.at[4 * px + 2 * py + pc]

        def copy(k, block, to, src=None):
            return pltpu.make_async_remote_copy(
                src_ref=slot(*block) if src is None else src, dst_ref=slot(*block),
                send_sem=send_sems.at[k], recv_sem=recv_sems.at[k], device_id=to, device_id_type=MESH)

        mine = pltpu.make_async_copy(v_ref, slot(*me), local_sem)
        mine.start()
        first = [copy(0, me, sibling, src=v_ref)]
        first += [copy(1 + j, me, (*chip, c), src=v_ref) for j, chip in enumerate(chips)]
        for cp in first:
            cp.start()
        passed = [copy(4 + j, (*chip, c), sibling) for j, chip in enumerate(chips)]
        for j, chip in enumerate(chips):
            copy(1 + j, (*chip, c), me).wait_recv()
            passed[j].start()
        copy(0, sibling, me).wait_recv()
        for j, chip in enumerate(chips):
            copy(4 + j, (*chip, 1 - c), me).wait_recv()
        for cp in first + passed:
            cp.wait_send()
        mine.wait()
        total = buf[0]
        for dev in range(1, 8):
            total = total + buf[dev]
        out_ref[...] = total

    vm = pl.BlockSpec(memory_space=pltpu.VMEM)
    return pl.pallas_call(
        body, in_specs=[vm], out_specs=vm, out_shape=_sds(v.shape, F32),
        scratch_shapes=[pltpu.VMEM((8, rows, LANES), F32), pltpu.SemaphoreType.DMA((7,)),
                        pltpu.SemaphoreType.DMA((7,)), pltpu.SemaphoreType.DMA],
        name=name, compiler_params=pltpu.CompilerParams(vmem_limit_bytes=VMEM_LIMIT))(v)


def _chip_partial(grad, from_sibling, core, name):
    _, r, cdim = grad.shape
    r2 = r // 2
    tr = _tile(r2, SUM_ROW_TILE)
    per = r2 // tr

    def body(core_ref, g_ref, s_ref, o_ref, ob_ref):
        del core_ref
        total = g_ref[...] + s_ref[...]
        o_ref[...] = total
        ob_ref[...] = total.astype(BF16)

    blk = pl.BlockSpec((None, tr, cdim), lambda k, i, core_ref: (k, i, 0))
    mine = pl.BlockSpec((None, tr, cdim), lambda k, i, core_ref: (k, core_ref[0] * per + i, 0))
    return pl.pallas_call(
        body,
        grid_spec=pltpu.PrefetchScalarGridSpec(num_scalar_prefetch=1, grid=(4, per), in_specs=[mine, blk],
                                               out_specs=[blk, blk]),
        out_shape=[_sds((4, r2, cdim), F32), _sds((4, r2, cdim), BF16)],
        name=name, compiler_params=_cp(2))(core, grad, from_sibling)


def _owner_sum(partial, from_chips, place, name):
    _, r2, cdim = partial.shape
    tr = _tile(r2, SUM_ROW_TILE)
    per = r2 // tr

    def body(place_ref, p_ref, f_ref, o_ref):
        del place_ref
        total = p_ref[...]
        for j in range(3):
            total = total + f_ref[j].astype(F32)
        o_ref[...] = total

    return pl.pallas_call(
        body,
        grid_spec=pltpu.PrefetchScalarGridSpec(
            num_scalar_prefetch=1, grid=(per,),
            in_specs=[pl.BlockSpec((None, tr, cdim), lambda i, place_ref: (place_ref[0], i, 0)),
                      pl.BlockSpec((3, tr, cdim), lambda i, place_ref: (0, i, 0))],
            out_specs=pl.BlockSpec((tr, cdim), lambda i, place_ref: (place_ref[1] * per + i, 0))),
        out_shape=_sds((2 * r2, cdim), F32), name=name, compiler_params=_cp(1))(place, partial, from_chips)


def _adamw(g, w, m, v, name):
    r, cdim = w.shape
    tr = _tile(r, OPT_ROW_TILE)
    c1 = 1.0 / (1.0 - ADAM_B1 ** ADAM_STEP)
    c2 = 1.0 / (1.0 - ADAM_B2 ** ADAM_STEP)

    def body(g_ref, w_ref, m_ref, v_ref, go_ref, d_ref, mo_ref, vo_ref):
        grad = g_ref[:, pl.ds(0, cdim)]
        m_new = ADAM_B1 * m_ref[...] + (1.0 - ADAM_B1) * grad
        v_new = ADAM_B2 * v_ref[...] + (1.0 - ADAM_B2) * (grad * grad)
        go_ref[...] = grad
        mo_ref[...] = m_new
        vo_ref[...] = v_new
        d_ref[...] = -ADAM_LR * ((m_new * c1) / (jnp.sqrt(v_new * c2) + ADAM_EPS) + ADAM_WD * w_ref[...])

    blk = pl.BlockSpec((tr, cdim), lambda i: (i, 0))
    gblk = pl.BlockSpec((tr, g.shape[1]), lambda i: (i, 0))
    return pl.pallas_call(
        body, grid=(r // tr,), in_specs=[gblk, blk, blk, blk], out_specs=[blk] * 4,
        out_shape=[_sds((r, cdim), F32)] * 4, name=name, compiler_params=_cp(1))(g, w, m, v)


def _pack_rows(vectors):
    flat = [v.reshape(-1) for v in vectors]
    sizes = [f.shape[0] for f in flat]
    total = sum(sizes)
    padded = _round_up(total, 8 * LANES)
    buf = jnp.concatenate(flat + [jnp.zeros((padded - total,), F32)])
    return buf.reshape(padded // LANES, LANES), sizes


def _unpack_rows(buf, sizes, shapes):
    flat = buf.reshape(-1)
    out, off = [], 0
    for n, shp in zip(sizes, shapes):
        out.append(flat[off:off + n].reshape(shp))
        off += n
    return out


def kernel(x, pool_w, pool_scale, attn_w_qkv, attn_w_o, ffn_w_up, ffn_conv_w, ffn_conv_b, ffn_w_down, ln_mix_g, ln_mix_b, ln_ffn_g, ln_ffn_b, loss_target, m_pool_w, m_pool_scale, m_attn_w_qkv, m_attn_w_o, m_ffn_w_up, m_ffn_conv_w, m_ffn_conv_b, m_ffn_w_down, m_ln_mix_g, m_ln_mix_b, m_ln_ffn_g, m_ln_ffn_b, v_pool_w, v_pool_scale, v_attn_w_qkv, v_attn_w_o, v_ffn_w_up, v_ffn_conv_w, v_ffn_conv_b, v_ffn_w_down, v_ln_mix_g, v_ln_mix_b, v_ln_ffn_g, v_ln_ffn_b):
    t, d = x.shape[1], x.shape[2]
    n_heads = d // HEAD_DIM
    n_groups = pool_w.shape[1]
    fs = ffn_w_up.shape[2]
    fp = _round_up(fs, LANES)
    rd = ffn_w_down.shape[1]
    assert 2 * rd == fs
    xi, yi, ci = lax.axis_index("x"), lax.axis_index("y"), lax.axis_index("c")
    chip = (2 * xi + yi).astype(jnp.int32)
    chip_arr, core_arr = chip.reshape(1), ci.astype(jnp.int32).reshape(1)
    place_arr = jnp.concatenate([chip_arr, core_arr])

    x2 = x.reshape(t, d)
    target = loss_target.reshape(t, d)
    pad_cols = lambda a: jnp.pad(a, [(0, 0)] * (a.ndim - 1) + [(0, fp - fs)])

    weights = []
    for i in range(DEPTH):
        j = i // 2
        items = []
        if i % 2 == 0:
            items.append(("pool", _place_shard("pool", pool_w[j], chip_arr, name="place_pool"), pool_w.shape[2]))
        else:
            items.append(("lead", _place_shard("lead", attn_w_qkv[j], chip_arr, name="place_qkv"), d))
            items.append(("lead", _place_shard("lead", attn_w_o[j], chip_arr, name="place_wo"), attn_w_o.shape[1]))
        items.append(("lead", _place_shard("lead", ffn_w_up[i], chip_arr, name="place_up", cols=fp), d))
        items.append(("down", _place_shard("down", ffn_w_down[i], chip_arr, name="place_down",
                                           base=jnp.zeros((2, fp, d), BF16)), rd))
        weights.append(_gather_weights(items, name="gather_pool_layer" if i % 2 == 0 else "gather_attn_layer"))

    conv_b_all = pad_cols(ffn_conv_b.reshape(DEPTH, 4, 1, fs))
    cw_local = pad_cols(ffn_conv_w)
    slot = (jnp.arange(4, dtype=jnp.int32) == chip).astype(F32) * (1.0 - ci.astype(F32))
    cw_placed = slot[None, :, None, None] * cw_local[:, None]
    cw_buf, cw_sizes = _pack_rows([cw_placed])
    conv_w_all = _unpack_rows(_all_reduce_small(cw_buf, name="gather_conv_w"), cw_sizes, [cw_placed.shape])[0]

    gam = lambda a, i: a[i].reshape(1, d)

    saved = []
    cur, cur_b = x2, x2.astype(BF16)
    for i in range(DEPTH):
        j = i // 2
        w = weights[i]
        s = {"x_in": cur, "x_in_b": cur_b}
        if i % 2 == 0:
            w_pool, w_up, w_down = w
            s["scale"] = pool_scale[j].reshape(1, d)
            r1, x1, x1b = _pool_fwd(cur, w_pool, s["scale"], gam(ln_mix_g, i), gam(ln_mix_b, i), name="pool_fwd")
        else:
            w_qkv, w_o, w_up, w_down = w
            w_o3 = w_o.reshape(1, d, d)
            qkv = _mm_cols(cur_b, w_qkv, BF16, name="qkv_proj")
            o = _attn_fwd(qkv, n_heads, name="attn_fwd")
            s["qkv"], s["o"], s["w_o3"] = qkv, o, w_o3
            r1, x1, x1b = _mm_res_ln(o.reshape(1, t, d), w_o3, cur, gam(ln_mix_g, i), gam(ln_mix_b, i),
                                     name="attn_out_ln")
        h = _mm_cols(x1b, w_up, F32, name="ffn_up")
        a = _ffn_act(h, conv_w_all[i], conv_b_all[i], name="ffn_act")
        r2, x2n, x2b = _mm_res_ln(a, w_down, x1, gam(ln_ffn_g, i), gam(ln_ffn_b, i), name="ffn_down_ln")
        s.update(r1=r1, x1b=x1b, h=h, a=a, r2=r2)
        saved.append(s)
        cur, cur_b = x2n, x2b

    loss_row, dcur = _loss_and_grad(cur, target, name="loss")
    loss = lax.psum(loss_row[0, 0], ("x", "y", "c"))

    big_grads = [None] * DEPTH
    small = {}
    for i in reversed(range(DEPTH)):
        j = i // 2
        s, w = saved[i], weights[i]
        w_up, w_down = w[-2], w[-1]
        dr2, dr2b, small["ln_ffn_g", i], small["ln_ffn_b", i] = _ln_bwd(dcur, s["r2"], gam(ln_ffn_g, i), name="ln_bwd")
        da = _mm_cols(dr2b, w_down, F32, name="ffn_down_bwd_act", transposed_b=True)
        dr2b3 = dr2b.reshape(1, t, d)
        nmb = 2
        d_down = _mm_tn(s["a"], dr2b3, (2, fp, d), fp // nmb, d, 2,
                        (lambda u: u, nmb, lambda u, mb: mb), (lambda u: 0, lambda u: 0),
                        (lambda u: u, lambda u, mb: mb, lambda u: 0), name="ffn_down_bwd_w")
        dhc, dcw, dcb = _ffn_act_bwd(s["h"], da, conv_w_all[i], conv_b_all[i], name="ffn_act_bwd")
        small["conv_w", i], small["conv_b", i] = dcw, dcb
        dh = _conv_adjoint(dhc.reshape(4, t, fp), conv_w_all[i], name="ffn_conv_adjoint")
        dx1 = _mm_nt_acc(dh, w_up, dr2, fp, name="ffn_up_bwd_act")
        d_up = _mm_tn(s["x1b"].reshape(1, t, d), dh, (4, d, fp), d // 2, fp, 4,
                      (lambda u: 0, 2, lambda u, mb: mb), (lambda u: u, lambda u: 0),
                      (lambda u: u, lambda u, mb: mb, lambda u: 0), name="ffn_up_bwd_w")
        dr1, dr1b, small["ln_mix_g", i], small["ln_mix_b", i] = _ln_bwd(dx1, s["r1"], gam(ln_mix_g, i), name="ln_bwd")
        d_down4 = d_down[:, :fs].reshape(4, rd, d)
        if i % 2 == 0:
            dp, d_pool, small["pool_scale", j] = _pool_bwd(s["x_in"], dr1, w[0], s["scale"], name="pool_bwd")
            dcur = _pool_adjoint(dp, dr1, n_groups, name="pool_adjoint")
            cg = d // n_groups
            d_pool4 = d_pool.reshape(n_groups, 4, cg // 4, cg).transpose(1, 0, 2, 3).reshape(4, n_groups * (cg // 4), cg)
            big_grads[i] = [d_pool4, d_up, d_down4]
        else:
            w_qkv = w[0]
            do = _mm_cols(dr1b, s["w_o3"], BF16, name="attn_out_bwd_act", transposed_b=True)
            d_wo = _mm_tn(s["o"].reshape(1, t, d), dr1b.reshape(1, t, d), (1, d, d), d // 2, d, 1,
                          (lambda u: 0, 2, lambda u, mb: mb), (lambda u: 0, lambda u: 0),
                          (lambda u: 0, lambda u, mb: mb, lambda u: 0), name="attn_out_bwd_w")
            dq, dk, dv = _attn_bwd(s["qkv"], do, n_heads, name="attn_bwd")
            dqkv = jnp.stack([dq, dk, dv])
            cq = w_qkv.shape[2]
            kb = cq // 3
            na, nbk = d // kb, cq // kb
            dcur = _mm_nt_acc(dqkv, w_qkv, dr1, kb, name="qkv_bwd_act")
            d_qkv = _mm_tn(s["x_in_b"].reshape(1, t, d), dqkv, (4, d, cq), d // 2, kb, 3 * na,
                           (lambda u: 0, 2, lambda u, mb: mb), (lambda u: u // na, lambda u: u % na),
                           (lambda u: u // nbk, lambda u, mb: mb, lambda u: u % nbk), name="qkv_bwd_w")
            big_grads[i] = [d_qkv, d_wo.reshape(4, d // 4, d), d_up, d_down4]
    grad_x = dcur.reshape(1, t, d)

    reduced = []
    for i in range(DEPTH):
        tag = "pool" if i % 2 == 0 else "attn"
        grads = big_grads[i]
        from_sib = _send_sibling_halves(grads, name=f"reduce_{tag}_sibling")
        parts = [_chip_partial(g, fs_, core_arr, name="reduce_chip_partial") for g, fs_ in zip(grads, from_sib)]
        from_chips = _send_to_owner_chips([p[1] for p in parts], name=f"reduce_{tag}_chips")
        halves = [_owner_sum(p[0], fc, place_arr, name="reduce_owner_sum") for p, fc in zip(parts, from_chips)]
        reduced.append(_exchange_finished_halves(halves, name=f"reduce_{tag}_halves"))

    names = [("pool_scale", j) for j in range(2)]
    for nm in ("ln_mix_g", "ln_mix_b", "ln_ffn_g", "ln_ffn_b", "conv_b", "conv_w"):
        names += [(nm, i) for i in range(DEPTH)]
    vecs = [small[k] for k in names]
    sbuf, ssizes = _pack_rows(vecs)
    summed = dict(zip(names, _unpack_rows(_all_reduce_small(sbuf, name="reduce_small"), ssizes, [v.shape for v in vecs])))

    def stack_layers(nm, count):
        return jnp.stack([summed[nm, i] for i in range(count)])

    g_small = {
        "pool_scale": stack_layers("pool_scale", 2).reshape(2, d),
        "ln_mix_g": stack_layers("ln_mix_g", DEPTH).reshape(DEPTH, d),
        "ln_mix_b": stack_layers("ln_mix_b", DEPTH).reshape(DEPTH, d),
        "ln_ffn_g": stack_layers("ln_ffn_g", DEPTH).reshape(DEPTH, d),
        "ln_ffn_b": stack_layers("ln_ffn_b", DEPTH).reshape(DEPTH, d),
        "conv_b": stack_layers("conv_b", DEPTH).reshape(DEPTH, 4, fp)[:, :, :fs].reshape(DEPTH, 4 * fs),
        "conv_w": lax.dynamic_index_in_dim(stack_layers("conv_w", DEPTH).reshape(DEPTH, 4, 3, fp), chip, axis=1,
                                           keepdims=False)[:, :, :fs],
    }
    w_small = {"pool_scale": (pool_scale, m_pool_scale, v_pool_scale), "ln_mix_g": (ln_mix_g, m_ln_mix_g, v_ln_mix_g),
               "ln_mix_b": (ln_mix_b, m_ln_mix_b, v_ln_mix_b), "ln_ffn_g": (ln_ffn_g, m_ln_ffn_g, v_ln_ffn_g),
               "ln_ffn_b": (ln_ffn_b, m_ln_ffn_b, v_ln_ffn_b), "conv_b": (ffn_conv_b, m_ffn_conv_b, v_ffn_conv_b),
               "conv_w": (ffn_conv_w, m_ffn_conv_w, v_ffn_conv_w)}
    order = list(g_small)
    packs = [_pack_rows([g_small[k] for k in order])[0]]
    for idx in range(3):
        packs.append(_pack_rows([w_small[k][idx] for k in order])[0])
    small_sizes = _pack_rows([g_small[k] for k in order])[1]
    small_out = _adamw(*packs, name="adamw_small")
    shapes = [g_small[k].shape for k in order]
    small_res = {k: [] for k in order}
    for arr in small_out:
        for k, val in zip(order, _unpack_rows(arr, small_sizes, shapes)):
            small_res[k].append(val)

    def opt_layers(per_layer_grads, w_all, m_all, v_all, name):
        outs = [[], [], [], []]
        for li, g in enumerate(per_layer_grads):
            shp = w_all[li].shape
            r = g.shape[0]
            res = _adamw(g, w_all[li].reshape(r, -1), m_all[li].reshape(r, -1), v_all[li].reshape(r, -1), name=name)
            for k in range(4):
                outs[k].append(res[k].reshape(shp))
        return [jnp.stack(o) for o in outs]

    cg = d // n_groups
    pool_g = [reduced[i][0].reshape(n_groups, cg // 4, cg).reshape(n_groups * (cg // 4), cg) for i in (0, 2)]
    big = {
        "pool_w": opt_layers(pool_g, pool_w, m_pool_w, v_pool_w, "adamw_pool"),
        "attn_w_qkv": opt_layers([reduced[i][0] for i in (1, 3)], attn_w_qkv, m_attn_w_qkv, v_attn_w_qkv, "adamw_qkv"),
        "attn_w_o": opt_layers([reduced[i][1] for i in (1, 3)], attn_w_o, m_attn_w_o, v_attn_w_o, "adamw_wo"),
        "ffn_w_up": opt_layers([reduced[i][-2] for i in range(DEPTH)], ffn_w_up, m_ffn_w_up, v_ffn_w_up, "adamw_up"),
        "ffn_w_down": opt_layers([reduced[i][-1] for i in range(DEPTH)], ffn_w_down, m_ffn_w_down, v_ffn_w_down,
                                 "adamw_down"),
    }

    def leaf(k, name):
        if name in big:
            return big[name][k]
        key = {"ffn_conv_w": "conv_w", "ffn_conv_b": "conv_b"}.get(name, name)
        return small_res[key][k]

    weight_names = ["pool_w", "pool_scale", "attn_w_qkv", "attn_w_o", "ffn_w_up", "ffn_conv_w", "ffn_conv_b",
                    "ffn_w_down", "ln_mix_g", "ln_mix_b", "ln_ffn_g", "ln_ffn_b"]
    outs = [loss, grad_x]
    for k in range(4):
        outs += [leaf(k, nm) for nm in weight_names]
    return tuple(outs)
```

```python
import jax
import jax.numpy as jnp
from jax import lax
from jax.experimental import pallas as pl
from jax.experimental.pallas import tpu as pltpu

F32, BF16 = jnp.float32, jnp.bfloat16
MESH = pl.DeviceIdType.MESH

LANES = 128
HEAD_DIM = 128
ATT_BLOCK = 128
ATT_WINDOW = 3 * ATT_BLOCK
POOL_WINDOWS = (2, 4, 8, 16)
POOL_HALO = 16
CONV_HALO = 8
LN_EPS = 1e-5
DEPTH = 4
ALPHA = (2.0 * DEPTH) ** 0.25
ATT_SCALE = HEAD_DIM ** -0.5
EXP_ZERO = 115.0
MASKED = 1e30
ADAM_LR, ADAM_B1, ADAM_B2, ADAM_EPS, ADAM_WD, ADAM_STEP = 0.001, 0.9, 0.999, 1e-08, 0.01, 10

VMEM_LIMIT = 56 << 20
ROW_TILE = 512
LN_ROW_TILE = 256
OPT_ROW_TILE = 128
SUM_ROW_TILE = 512


def _cp(n_axes):
    return pltpu.CompilerParams(dimension_semantics=("arbitrary",) * n_axes, vmem_limit_bytes=VMEM_LIMIT)


def _sds(shape, dtype):
    return jax.ShapeDtypeStruct(tuple(shape), dtype)


def _round_up(n, m):
    return (n + m - 1) // m * m


def _tile(n, cap, mult=8):
    if n <= cap:
        return n
    best = None
    for d in range(mult, cap + 1, mult):
        if n % d == 0:
            best = d
    assert best is not None, (n, cap)
    return best


_NT = (((1,), (1,)), ((), ()))
_TN = (((0,), (0,)), ((), ()))


def _mm_cols(a, b, out_dtype, name, transposed_b=False):
    t, k = a.shape
    g = b.shape[0]
    nb = b.shape[1] if transposed_b else b.shape[2]
    tm = _tile(t, ROW_TILE)

    def body(a_ref, b_ref, o_ref):
        if transposed_b:
            acc = lax.dot_general(a_ref[...], b_ref[...], _NT, preferred_element_type=F32)
        else:
            acc = jnp.dot(a_ref[...], b_ref[...], preferred_element_type=F32)
        o_ref[...] = acc.astype(o_ref.dtype)

    return pl.pallas_call(
        body, grid=(g, t // tm),
        in_specs=[pl.BlockSpec((tm, k), lambda gi, i: (i, 0)),
                  pl.BlockSpec((None,) + b.shape[1:], lambda gi, i: (gi, 0, 0))],
        out_specs=pl.BlockSpec((None, tm, nb), lambda gi, i: (gi, i, 0)),
        out_shape=_sds((g, t, nb), out_dtype), name=name, compiler_params=_cp(2))(a, b)


def _mm_nt_acc(a3, b3, res, kb, name):
    ga, t, ka = a3.shape
    gb, n, kbb = b3.shape
    na, nbk = ka // kb, kbb // kb
    groups = ga * na
    assert groups == gb * nbk
    tm = _tile(t, ROW_TILE)

    def body(a_ref, b_ref, res_ref, o_ref, acc):
        u = pl.program_id(1)

        @pl.when(u == 0)
        def _():
            acc[...] = ALPHA * res_ref[...]

        acc[...] += lax.dot_general(a_ref[...], b_ref[...], _NT, preferred_element_type=F32)

        @pl.when(u == groups - 1)
        def _():
            o_ref[...] = acc[...]

    return pl.pallas_call(
        body, grid=(t // tm, groups),
        in_specs=[pl.BlockSpec((None, tm, kb), lambda i, u: (u // na, i, u % na)),
                  pl.BlockSpec((None, n, kb), lambda i, u: (u // nbk, 0, u % nbk)),
                  pl.BlockSpec((tm, n), lambda i, u: (i, 0))],
        out_specs=pl.BlockSpec((tm, n), lambda i, u: (i, 0)),
        out_shape=_sds((t, n), F32), scratch_shapes=[pltpu.VMEM((tm, n), F32)],
        name=name, compiler_params=_cp(2))(a3, b3, res)


def _mm_tn(x3, dy3, out_shape, bm, bn, groups, x_idx, dy_idx, out_idx, name):
    t = x3.shape[1]
    tm = _tile(t, 2 * ROW_TILE)

    def body(x_ref, dy_ref, o_ref):
        @pl.when(pl.program_id(2) == 0)
        def _():
            o_ref[...] = jnp.zeros_like(o_ref)

        o_ref[...] += lax.dot_general(x_ref[...], dy_ref[...], _TN, preferred_element_type=F32)

    return pl.pallas_call(
        body, grid=(groups, x_idx[1], t // tm),
        in_specs=[pl.BlockSpec((None, tm, bm), lambda u, mb, i: (x_idx[0](u), i, x_idx[2](u, mb))),
                  pl.BlockSpec((None, tm, bn), lambda u, mb, i: (dy_idx[0](u), i, dy_idx[1](u)))],
        out_specs=pl.BlockSpec((None, bm, bn), lambda u, mb, i: (out_idx[0](u), out_idx[1](u, mb), out_idx[2](u))),
        out_shape=_sds(out_shape, F32), name=name, compiler_params=_cp(3))(x3, dy3)


def _layer_norm_rows(r, gamma, beta):
    mu = jnp.mean(r, axis=-1, keepdims=True)
    xc = r - mu
    var = jnp.mean(xc * xc, axis=-1, keepdims=True)
    return xc * lax.rsqrt(var + LN_EPS) * gamma + beta


def _mm_res_ln(a3, w3, res, gamma, beta, name):
    g, t, kb = a3.shape
    d = w3.shape[2]
    tm = _tile(t, LN_ROW_TILE)

    def body(a_ref, w_hbm, res_ref, g_ref, b_ref, r_ref, o_ref, ob_ref, w_vmem, sem):
        @pl.when(pl.program_id(0) == 0)
        def _():
            cp = pltpu.make_async_copy(w_hbm, w_vmem, sem)
            cp.start()
            cp.wait()

        acc = ALPHA * res_ref[...]
        for gi in range(g):
            acc = acc + jnp.dot(a_ref[gi], w_vmem[gi], preferred_element_type=F32)
        r_ref[...] = acc
        out = _layer_norm_rows(acc, g_ref[...], b_ref[...])
        o_ref[...] = out
        ob_ref[...] = out.astype(BF16)

    row = pl.BlockSpec((tm, d), lambda i: (i, 0))
    vec = pl.BlockSpec((1, d), lambda i: (0, 0))
    return pl.pallas_call(
        body, grid=(t // tm,),
        in_specs=[pl.BlockSpec((g, tm, kb), lambda i: (0, i, 0)), pl.BlockSpec(memory_space=pl.ANY), row, vec, vec],
        out_specs=[row, row, row],
        out_shape=[_sds((t, d), F32), _sds((t, d), F32), _sds((t, d), BF16)],
        scratch_shapes=[pltpu.VMEM(w3.shape, w3.dtype), pltpu.SemaphoreType.DMA],
        name=name, compiler_params=_cp(1))(a3, w3, res, gamma, beta)


def _ln_bwd(dout, r, gamma, name):
    t, d = r.shape
    tm = _tile(t, ROW_TILE)

    def body(do_ref, r_ref, g_ref, dr_ref, drb_ref, dg_ref, db_ref):
        @pl.when(pl.program_id(0) == 0)
        def _():
            dg_ref[...] = jnp.zeros_like(dg_ref)
            db_ref[...] = jnp.zeros_like(db_ref)

        rr = r_ref[...]
        do = do_ref[...]
        mu = jnp.mean(rr, axis=-1, keepdims=True)
        xc = rr - mu
        rstd = lax.rsqrt(jnp.mean(xc * xc, axis=-1, keepdims=True) + LN_EPS)
        xhat = xc * rstd
        dxh = do * g_ref[...]
        m1 = jnp.mean(dxh, axis=-1, keepdims=True)
        m2 = jnp.mean(dxh * xhat, axis=-1, keepdims=True)
        dr = rstd * (dxh - m1 - xhat * m2)
        dr_ref[...] = dr
        drb_ref[...] = dr.astype(BF16)
        dg_ref[...] += jnp.sum(do * xhat, axis=0, keepdims=True)
        db_ref[...] += jnp.sum(do, axis=0, keepdims=True)

    row = pl.BlockSpec((tm, d), lambda i: (i, 0))
    vec = pl.BlockSpec((1, d), lambda i: (0, 0))
    return pl.pallas_call(
        body, grid=(t // tm,), in_specs=[row, row, vec], out_specs=[row, row, vec, vec],
        out_shape=[_sds((t, d), F32), _sds((t, d), BF16), _sds((1, d), F32), _sds((1, d), F32)],
        name=name, compiler_params=_cp(1))(dout, r, gamma)


def _loss_and_grad(y, target, name):
    t, d = y.shape
    tm = _tile(t, ROW_TILE)
    steps = t // tm

    def body(y_ref, t_ref, loss_ref, dy_ref, acc):
        i = pl.program_id(0)

        @pl.when(i == 0)
        def _():
            acc[...] = jnp.zeros_like(acc)

        diff = y_ref[...] - t_ref[...]
        dy_ref[...] = diff * (1.0 / d)
        acc[...] += jnp.sum(diff * diff, axis=0, keepdims=True)

        @pl.when(i == steps - 1)
        def _():
            total = jnp.sum(acc[...], axis=1, keepdims=True) * (0.5 / d)
            loss_ref[...] = jnp.broadcast_to(total, loss_ref.shape)

    row = pl.BlockSpec((tm, d), lambda i: (i, 0))
    return pl.pallas_call(
        body, grid=(steps,), in_specs=[row, row],
        out_specs=[pl.BlockSpec((1, LANES), lambda i: (0, 0)), row],
        out_shape=[_sds((1, LANES), F32), _sds((t, d), F32)],
        scratch_shapes=[pltpu.VMEM((1, d), F32)], name=name, compiler_params=_cp(1))(y, target)


def _window_sums(ext, window, forward):
    n = ext.shape[0]
    s, span = ext, 1
    while span < window:
        s = s + pltpu.roll(s, (n - span) if forward else span, 0)
        span *= 2
    return s


def _pooled_group(main, halo, gi, row0):
    window = POOL_WINDOWS[gi]
    ext = jnp.concatenate([halo, main], axis=0)
    sums = _window_sums(ext, window, forward=False)[POOL_HALO:, :]
    pos = row0 + lax.broadcasted_iota(jnp.int32, (main.shape[0], 1), 0)
    cnt = jnp.minimum(pos + 1, window).astype(F32)
    return sums / cnt - main


def _pool_specs(t, d, tm):
    per = tm // POOL_HALO
    main = pl.BlockSpec((tm, d), lambda i: (i, 0))
    before = pl.BlockSpec((POOL_HALO, d), lambda i: (jnp.maximum(i * per - 1, 0), 0))
    return main, before


def _pool_fwd(x, w, scale, gamma, beta, name):
    t, d = x.shape
    ng, cg = w.shape[0], w.shape[1]
    tm = _tile(t, LN_ROW_TILE)

    def body(x_ref, h_ref, w_ref, s_ref, g_ref, b_ref, r_ref, o_ref, ob_ref):
        i = pl.program_id(0)
        for gi in range(ng):
            cols = pl.ds(gi * cg, cg)
            main = x_ref[:, cols]
            halo = jnp.where(i > 0, h_ref[:, cols], 0.0)
            pooled = _pooled_group(main, halo, gi, i * tm)
            y = jnp.dot(pooled.astype(BF16), w_ref[gi], preferred_element_type=F32)
            r_ref[:, cols] = ALPHA * main + y * s_ref[:, cols]
        out = _layer_norm_rows(r_ref[...], g_ref[...], b_ref[...])
        o_ref[...] = out
        ob_ref[...] = out.astype(BF16)

    main, before = _pool_specs(t, d, tm)
    vec = pl.BlockSpec((1, d), lambda i: (0, 0))
    return pl.pallas_call(
        body, grid=(t // tm,),
        in_specs=[main, before, pl.BlockSpec(w.shape, lambda i: (0, 0, 0)), vec, vec, vec],
        out_specs=[main, main, main],
        out_shape=[_sds((t, d), F32), _sds((t, d), F32), _sds((t, d), BF16)],
        name=name, compiler_params=_cp(1))(x, x, w, scale, gamma, beta)


def _pool_bwd(x, dy, w, scale, name):
    t, d = x.shape
    ng, cg = w.shape[0], w.shape[1]
    tm = _tile(t, LN_ROW_TILE)

    def body(x_ref, h_ref, dy_ref, w_ref, s_ref, dp_ref, dw_ref, ds_ref):
        i = pl.program_id(0)

        @pl.when(i == 0)
        def _():
            dw_ref[...] = jnp.zeros_like(dw_ref)
            ds_ref[...] = jnp.zeros_like(ds_ref)

        for gi in range(ng):
            cols = pl.ds(gi * cg, cg)
            main = x_ref[:, cols]
            halo = jnp.where(i > 0, h_ref[:, cols], 0.0)
            pooled = _pooled_group(main, halo, gi, i * tm).astype(BF16)
            y = jnp.dot(pooled, w_ref[gi], preferred_element_type=F32)
            dyg = dy_ref[:, cols]
            ds_ref[:, cols] += jnp.sum(dyg * y, axis=0, keepdims=True)
            dyw = (dyg * s_ref[:, cols]).astype(BF16)
            dw_ref[gi] += lax.dot_general(pooled, dyw, _TN, preferred_element_type=F32)
            dp_ref[:, cols] = lax.dot_general(dyw, w_ref[gi], _NT, preferred_element_type=F32)

    main, before = _pool_specs(t, d, tm)
    vec = pl.BlockSpec((1, d), lambda i: (0, 0))
    wspec = pl.BlockSpec(w.shape, lambda i: (0, 0, 0))
    return pl.pallas_call(
        body, grid=(t // tm,), in_specs=[main, before, main, wspec, vec],
        out_specs=[main, wspec, vec],
        out_shape=[_sds((t, d), F32), _sds(w.shape, F32), _sds((1, d), F32)],
        name=name, compiler_params=_cp(1))(x, x, dy, w, scale)


def _pool_adjoint(dp, dres, n_groups, name):
    t, d = dp.shape
    cg = d // n_groups
    tm = _tile(t, ROW_TILE)
    steps = t // tm
    per = tm // POOL_HALO

    def body(dp_ref, after_ref, dres_ref, dx_ref):
        i = pl.program_id(0)
        rows = lax.broadcasted_iota(jnp.int32, (tm, 1), 0)
        rows_after = lax.broadcasted_iota(jnp.int32, (POOL_HALO, 1), 0)
        for gi in range(n_groups):
            window = POOL_WINDOWS[gi]
            cols = pl.ds(gi * cg, cg)
            main = dp_ref[:, cols]
            cnt = jnp.minimum(i * tm + rows + 1, window).astype(F32)
            cnt_after = jnp.minimum((i + 1) * tm + rows_after + 1, window).astype(F32)
            after = jnp.where(i < steps - 1, after_ref[:, cols] / cnt_after, 0.0)
            ext = jnp.concatenate([main / cnt, after], axis=0)
            sums = _window_sums(ext, window, forward=True)[:tm, :]
            dx_ref[:, cols] = ALPHA * dres_ref[:, cols] + sums - main

    main = pl.BlockSpec((tm, d), lambda i: (i, 0))
    after = pl.BlockSpec((POOL_HALO, d), lambda i: (jnp.minimum((i + 1) * per, t // POOL_HALO - 1), 0))
    return pl.pallas_call(
        body, grid=(steps,), in_specs=[main, after, main], out_specs=main,
        out_shape=_sds((t, d), F32), name=name, compiler_params=_cp(1))(dp, dp, dres)


def _split_dot(x, tri):
    hi = x.astype(BF16)
    lo = (x - hi.astype(F32)).astype(BF16)
    return jnp.dot(hi, tri, preferred_element_type=F32) + jnp.dot(lo, tri, preferred_element_type=F32)


def _att_window(q, k_w, limit, carry_rest, suffix):
    z = lax.dot_general(q, k_w, _NT, preferred_element_type=F32) * ATT_SCALE
    z = jnp.where(lax.broadcasted_iota(jnp.int32, z.shape, 1) < limit, z, -MASKED)
    e = jnp.exp(-jnp.abs(z))
    log_not = -(jnp.maximum(z, 0.0) + jnp.log(1.0 + e))
    rest = _split_dot(log_not, suffix) + carry_rest
    a = jnp.exp(z + rest)
    return z, e, log_not, a


def _tri(w, strict):
    r = lax.broadcasted_iota(jnp.int32, (w, w), 0)
    c = lax.broadcasted_iota(jnp.int32, (w, w), 1)
    return ((r > c) if strict else (r >= c)).astype(BF16)


def _heads_per_step(qkv3, n_heads):
    cpb = qkv3.shape[2] // HEAD_DIM
    return 2 if (cpb % 2 == 0 and n_heads % 2 == 0) else 1


def _att_specs(qkv3, n_heads, hp):
    t = qkv3.shape[1]
    cpb = qkv3.shape[2] // HEAD_DIM
    wd = hp * HEAD_DIM

    def slab(off):
        return pl.BlockSpec((None, t, wd), lambda g, i: ((off + g * hp) // cpb, 0, ((off + g * hp) % cpb) // hp))

    q = pl.BlockSpec((None, ATT_BLOCK, wd), lambda g, i: ((g * hp) // cpb, i, ((g * hp) % cpb) // hp))
    return q, slab(n_heads), slab(2 * n_heads)


def _head_cols(hh):
    return pl.ds(hh * HEAD_DIM, HEAD_DIM)


def _key_bounds(k_ref, kmax, hp):
    for hh in range(hp):
        kf = k_ref[:, _head_cols(hh)].astype(F32)
        kmax[hh] = jnp.sqrt(jnp.max(jnp.sum(kf * kf, axis=1, keepdims=True)))


def _score_bound(q, key_norm):
    qf = q.astype(F32)
    return ATT_SCALE * 1.001 * key_norm * jnp.sqrt(jnp.sum(qf * qf, axis=1, keepdims=True)) + 1e-3


def _any_alive(rests, bounds):
    alive = jnp.max(rests[0] + bounds[0]) > -EXP_ZERO
    for r, zb in zip(rests[1:], bounds[1:]):
        alive = jnp.logical_or(alive, jnp.max(r + zb) > -EXP_ZERO)
    return alive


def _window_rows(hi, w):
    start = jnp.maximum(hi - w, 0)
    return start, pl.ds(pl.multiple_of(start, ATT_BLOCK), w)


def _attn_fwd(qkv3, n_heads, name):
    t = qkv3.shape[1]
    b = ATT_BLOCK
    w = min(ATT_WINDOW, t)
    hp = _heads_per_step(qkv3, n_heads)
    heads = range(hp)

    def body(q_ref, k_ref, v_ref, o_ref, kmax):
        i = pl.program_id(1)

        @pl.when(i == 0)
        def _():
            _key_bounds(k_ref, kmax, hp)

        qs = [q_ref[:, _head_cols(hh)] for hh in heads]
        bounds = [_score_bound(qs[hh], kmax[hh]) for hh in heads]
        suffix = _tri(w, strict=False)
        qpos = i * b + lax.broadcasted_iota(jnp.int32, (b, 1), 0)

        def cond(c):
            return jnp.logical_and(c[0] > 0, _any_alive(c[1], bounds))

        def step(c):
            hi, rests, accs = c
            start, rows = _window_rows(hi, w)
            limit = jnp.minimum(qpos, hi) - start
            new_rests, new_accs = [], []
            for hh in heads:
                _, _, log_not, a = _att_window(qs[hh], k_ref[rows, _head_cols(hh)], limit, rests[hh], suffix)
                new_accs.append(accs[hh] + jnp.dot(a.astype(BF16), v_ref[rows, _head_cols(hh)],
                                                   preferred_element_type=F32))
                new_rests.append(rests[hh] + jnp.sum(log_not, axis=1, keepdims=True))
            return start, tuple(new_rests), tuple(new_accs)

        init = ((i + 1) * b, tuple(jnp.zeros((b, 1), F32) for _ in heads),
                tuple(jnp.zeros((b, HEAD_DIM), F32) for _ in heads))
        _, _, accs = lax.while_loop(cond, step, init)
        for hh in heads:
            o_ref[:, _head_cols(hh)] = accs[hh].astype(o_ref.dtype)

    qs_, ks_, vs_ = _att_specs(qkv3, n_heads, hp)
    return pl.pallas_call(
        body, grid=(n_heads // hp, t // b), in_specs=[qs_, ks_, vs_],
        out_specs=pl.BlockSpec((b, hp * HEAD_DIM), lambda g, i: (i, g)),
        out_shape=_sds((t, n_heads * HEAD_DIM), BF16),
        scratch_shapes=[pltpu.SMEM((hp,), F32)], name=name, compiler_params=_cp(2))(qkv3, qkv3, qkv3)


def _attn_bwd(qkv3, do, n_heads, name):
    t = qkv3.shape[1]
    b = ATT_BLOCK
    w = min(ATT_WINDOW, t)
    nq = t // b
    hp = _heads_per_step(qkv3, n_heads)
    heads = range(hp)
    wd = hp * HEAD_DIM

    def body(q_ref, k_ref, v_ref, do_ref, dq_ref, dk_ref, dv_ref, kmax, dk_acc, dv_acc):
        i = pl.program_id(1)

        @pl.when(i == 0)
        def _():
            _key_bounds(k_ref, kmax, hp)
            dk_acc[...] = jnp.zeros_like(dk_acc)
            dv_acc[...] = jnp.zeros_like(dv_acc)

        qs = [q_ref[:, _head_cols(hh)] for hh in heads]
        douts = [do_ref[:, _head_cols(hh)] for hh in heads]
        bounds = [_score_bound(qs[hh], kmax[hh]) for hh in heads]
        zero_cols = tuple(jnp.zeros((b, 1), F32) for _ in heads)
        suffix = _tri(w, strict=False)
        strict_suffix = _tri(w, strict=True)
        hi0 = (i + 1) * b
        qpos = i * b + lax.broadcasted_iota(jnp.int32, (b, 1), 0)

        def cond(c):
            return jnp.logical_and(c[0] > 0, _any_alive(c[1], bounds))

        def window(hh, rows, limit, rest):
            k_w = k_ref[rows, _head_cols(hh)]
            z, e, log_not, a = _att_window(qs[hh], k_w, limit, rest, suffix)
            dla = a * lax.dot_general(douts[hh], v_ref[rows, _head_cols(hh)], _NT, preferred_element_type=F32)
            return k_w, z, e, log_not, a, dla

        def sweep1(c):
            hi, rests, totals = c
            start, rows = _window_rows(hi, w)
            limit = jnp.minimum(qpos, hi) - start
            new_rests, new_totals = [], []
            for hh in heads:
                _, _, _, log_not, a, dla = window(hh, rows, limit, rests[hh])
                dv_acc[rows, _head_cols(hh)] += lax.dot_general(a.astype(BF16), douts[hh], _TN,
                                                                preferred_element_type=F32)
                new_rests.append(rests[hh] + jnp.sum(log_not, axis=1, keepdims=True))
                new_totals.append(totals[hh] + jnp.sum(dla, axis=1, keepdims=True))
            return start, tuple(new_rests), tuple(new_totals)

        _, _, totals = lax.while_loop(cond, sweep1, (hi0, zero_cols, zero_cols))

        def sweep2(c):
            hi, rests, laters, dqs = c
            start, rows = _window_rows(hi, w)
            limit = jnp.minimum(qpos, hi) - start
            new_rests, new_laters, new_dqs = [], [], []
            for hh in heads:
                k_w, z, e, log_not, _, dla = window(hh, rows, limit, rests[hh])
                inside = _split_dot(dla, strict_suffix)
                dlog_not = totals[hh] - laters[hh] - inside
                inv = 1.0 / (1.0 + e)
                sig = jnp.where(z >= 0, inv, e * inv)
                dz = ((dla - sig * dlog_not) * ATT_SCALE).astype(BF16)
                new_dqs.append(dqs[hh] + jnp.dot(dz, k_w, preferred_element_type=F32))
                dk_acc[rows, _head_cols(hh)] += lax.dot_general(dz, qs[hh], _TN, preferred_element_type=F32)
                new_rests.append(rests[hh] + jnp.sum(log_not, axis=1, keepdims=True))
                new_laters.append(laters[hh] + jnp.sum(dla, axis=1, keepdims=True))
            return start, tuple(new_rests), tuple(new_laters), tuple(new_dqs)

        init = (hi0, zero_cols, zero_cols, tuple(jnp.zeros((b, HEAD_DIM), F32) for _ in heads))
        _, _, _, dqs = lax.while_loop(cond, sweep2, init)
        for hh in heads:
            dq_ref[:, _head_cols(hh)] = dqs[hh].astype(dq_ref.dtype)

        @pl.when(i == nq - 1)
        def _():
            dk_ref[...] = dk_acc[...].astype(dk_ref.dtype)
            dv_ref[...] = dv_acc[...].astype(dv_ref.dtype)

    qs_, ks_, vs_ = _att_specs(qkv3, n_heads, hp)
    blk = pl.BlockSpec((b, wd), lambda g, i: (i, g))
    slab = pl.BlockSpec((t, wd), lambda g, i: (0, g))
    d = n_heads * HEAD_DIM
    return pl.pallas_call(
        body, grid=(n_heads // hp, nq),
        in_specs=[qs_, ks_, vs_, pl.BlockSpec((None, b, wd), lambda g, i: (0, i, g))],
        out_specs=[blk, slab, slab],
        out_shape=[_sds((t, d), BF16)] * 3,
        scratch_shapes=[pltpu.SMEM((hp,), F32), pltpu.VMEM((t, wd), F32), pltpu.VMEM((t, wd), F32)],
        name=name, compiler_params=_cp(2))(qkv3, qkv3, qkv3, do)


def _conv_rows(main, halo, w_ref, b_ref):
    ext = jnp.concatenate([halo, main], axis=0)
    h1 = pltpu.roll(ext, 1, 0)[CONV_HALO:, :]
    h2 = pltpu.roll(ext, 2, 0)[CONV_HALO:, :]
    hc = b_ref[...] + w_ref[0:1, :] * h2
    hc = hc + w_ref[1:2, :] * h1
    hc = hc + w_ref[2:3, :] * main
    return hc, h1, h2


def _ffn_specs(t, fp, tm, half):
    per = tm // CONV_HALO
    main = lambda off: pl.BlockSpec((None, tm, fp), lambda g, i: (g + off, i, 0))
    before = lambda off: pl.BlockSpec((None, CONV_HALO, fp), lambda g, i: (g + off, jnp.maximum(i * per - 1, 0), 0))
    cw = lambda off: pl.BlockSpec((None, 3, fp), lambda g, i: (g + off, 0, 0))
    cb = lambda off: pl.BlockSpec((None, 1, fp), lambda g, i: (g + off, 0, 0))
    return [main(0), before(0), main(half), before(half), cw(0), cw(half), cb(0), cb(half)]


def _ffn_act(h, cw, cb, name):
    n, t, fp = h.shape
    half = n // 2
    tm = _tile(t, LN_ROW_TILE)

    def body(hg_ref, hgb_ref, hv_ref, hvb_ref, wg_ref, wv_ref, bg_ref, bv_ref, a_ref):
        first = pl.program_id(1) == 0
        gate, _, _ = _conv_rows(hg_ref[...], jnp.where(first, 0.0, hgb_ref[...]), wg_ref, bg_ref)
        val, _, _ = _conv_rows(hv_ref[...], jnp.where(first, 0.0, hvb_ref[...]), wv_ref, bv_ref)
        a_ref[...] = (gate * jax.nn.sigmoid(gate) * val).astype(a_ref.dtype)

    return pl.pallas_call(
        body, grid=(half, t // tm), in_specs=_ffn_specs(t, fp, tm, half),
        out_specs=pl.BlockSpec((None, tm, fp), lambda g, i: (g, i, 0)),
        out_shape=_sds((half, t, fp), BF16), name=name, compiler_params=_cp(2))(h, h, h, h, cw, cw, cb, cb)


def _ffn_act_bwd(h, da, cw, cb, name):
    n, t, fp = h.shape
    half = n // 2
    tm = _tile(t, LN_ROW_TILE)

    def body(hg_ref, hgb_ref, hv_ref, hvb_ref, wg_ref, wv_ref, bg_ref, bv_ref, da_ref, dhc_ref, dw_ref, db_ref):
        first = pl.program_id(1) == 0

        @pl.when(first)
        def _():
            dw_ref[...] = jnp.zeros_like(dw_ref)
            db_ref[...] = jnp.zeros_like(db_ref)

        hg, hv = hg_ref[...], hv_ref[...]
        gate, hg1, hg2 = _conv_rows(hg, jnp.where(first, 0.0, hgb_ref[...]), wg_ref, bg_ref)
        val, hv1, hv2 = _conv_rows(hv, jnp.where(first, 0.0, hvb_ref[...]), wv_ref, bv_ref)
        sig = jax.nn.sigmoid(gate)
        dact = da_ref[...]
        dgate = dact * val * (sig * (1.0 + gate * (1.0 - sig)))
        dval = dact * (gate * sig)
        dhc_ref[0] = dgate
        dhc_ref[1] = dval
        for s, (dd, shifted) in enumerate(((dgate, (hg2, hg1, hg)), (dval, (hv2, hv1, hv)))):
            db_ref[s] += jnp.sum(dd, axis=0, keepdims=True)
            for kk in range(3):
                dw_ref[s, kk:kk + 1, :] += jnp.sum(dd * shifted[kk], axis=0, keepdims=True)

    specs = _ffn_specs(t, fp, tm, half) + [pl.BlockSpec((None, tm, fp), lambda g, i: (g, i, 0))]
    return pl.pallas_call(
        body, grid=(half, t // tm), in_specs=specs,
        out_specs=[pl.BlockSpec((2, None, tm, fp), lambda g, i: (0, g, i, 0)),
                   pl.BlockSpec((2, None, 3, fp), lambda g, i: (0, g, 0, 0)),
                   pl.BlockSpec((2, None, 1, fp), lambda g, i: (0, g, 0, 0))],
        out_shape=[_sds((2, half, t, fp), F32), _sds((2, half, 3, fp), F32), _sds((2, half, 1, fp), F32)],
        name=name, compiler_params=_cp(2))(h, h, h, h, cw, cw, cb, cb, da)


def _conv_adjoint(dhc, cw, name):
    n, t, fp = dhc.shape
    tm = _tile(t, ROW_TILE)
    steps = t // tm
    per = tm // CONV_HALO

    def body(d_ref, after_ref, w_ref, o_ref):
        main = d_ref[...]
        after = jnp.where(pl.program_id(1) < steps - 1, after_ref[...], 0.0)
        ext = jnp.concatenate([main, after], axis=0)
        rows = ext.shape[0]
        d1 = pltpu.roll(ext, rows - 1, 0)[:tm, :]
        d2 = pltpu.roll(ext, rows - 2, 0)[:tm, :]
        o_ref[...] = (w_ref[2:3, :] * main + w_ref[1:2, :] * d1 + w_ref[0:1, :] * d2).astype(o_ref.dtype)

    main = pl.BlockSpec((None, tm, fp), lambda g, i: (g, i, 0))
    after = pl.BlockSpec((None, CONV_HALO, fp), lambda g, i: (g, jnp.minimum((i + 1) * per, t // CONV_HALO - 1), 0))
    return pl.pallas_call(
        body, grid=(n, steps), in_specs=[main, after, pl.BlockSpec((None, 3, fp), lambda g, i: (g, 0, 0))],
        out_specs=main, out_shape=_sds((n, t, fp), BF16), name=name, compiler_params=_cp(2))(dhc, dhc, cw)


def _place():
    x, y, c = lax.axis_index("x"), lax.axis_index("y"), lax.axis_index("c")
    chips = [(1 - x, y), (x, 1 - y), (1 - x, 1 - y)]
    return x, y, c, chips


def _any_specs(n):
    return [pl.BlockSpec(memory_space=pl.ANY)] * n


def _place_shard(kind, w, chip, name, cols=None, base=None):
    if kind == "pool":
        g, r, cdim = w.shape

        def body(chip_ref, w_ref, o_ref):
            del chip_ref
            o_ref[...] = w_ref[...].astype(BF16)

        return pl.pallas_call(
            body,
            grid_spec=pltpu.PrefetchScalarGridSpec(
                num_scalar_prefetch=1, grid=(1,),
                in_specs=[pl.BlockSpec((g, r, cdim), lambda i, chip_ref: (0, 0, 0))],
                out_specs=pl.BlockSpec((g, r, cdim), lambda i, chip_ref: (0, chip_ref[0], 0))),
            out_shape=_sds((g, 4 * r, cdim), BF16), name=name, compiler_params=_cp(1))(chip, w)

    r, cs = w.shape
    if kind == "lead":
        cols = cols or cs
        tr = _tile(r, ROW_TILE, 16)

        def body(chip_ref, w_ref, o_ref):
            del chip_ref
            if cols > cs:
                o_ref[:, pl.ds(cols - LANES, LANES)] = jnp.zeros((tr, LANES), BF16)
            o_ref[:, pl.ds(0, cs)] = w_ref[...].astype(BF16)

        return pl.pallas_call(
            body,
            grid_spec=pltpu.PrefetchScalarGridSpec(
                num_scalar_prefetch=1, grid=(r // tr,),
                in_specs=[pl.BlockSpec((tr, cs), lambda i, chip_ref: (i, 0))],
                out_specs=pl.BlockSpec((None, tr, cols), lambda i, chip_ref: (chip_ref[0], i, 0))),
            out_shape=_sds((4, r, cols), BF16), name=name, compiler_params=_cp(1))(chip, w)

    assert kind == "down"
    tr = r // 2 if (r // 2) % 16 == 0 else r
    per = r // tr

    def body(chip_ref, w_ref, base_ref, o_ref):
        del chip_ref, base_ref
        o_ref[...] = w_ref[...].astype(BF16)

    return pl.pallas_call(
        body,
        grid_spec=pltpu.PrefetchScalarGridSpec(
            num_scalar_prefetch=1, grid=(per,),
            in_specs=[pl.BlockSpec((tr, cs), lambda i, chip_ref: (i, 0)), pl.BlockSpec(memory_space=pl.ANY)],
            out_specs=pl.BlockSpec((None, tr, cs), lambda i, chip_ref: (chip_ref[0] // 2, (chip_ref[0] % 2) * per + i, 0))),
        out_shape=_sds(base.shape, BF16), input_output_aliases={2: 0},
        name=name, compiler_params=_cp(1))(chip, w, base)


def _gather_weights(items, name):
    n = len(items)
    kinds = [it[0] for it in items]
    bufs = [it[1] for it in items]
    shard_rows = [it[2] for it in items]

    def body(*refs):
        outs = refs[n:2 * n]
        send_sems, recv_sems = refs[2 * n:]
        x, y, c, chips = _place()
        me, sibling = (x, y, c), (x, y, 1 - c)

        def half_of(m, chip, half):
            k = 2 * chip[0] + chip[1]
            o, r = outs[m], shard_rows[m]
            if kinds[m] == "pool":
                gh = o.shape[0] // 2
                return o.at[pl.ds(half * gh, gh), pl.ds(k * r, r)]
            r2 = r // 2
            if kinds[m] == "down":
                return o.at[k // 2, pl.ds((k % 2) * r + half * r2, r2)]
            return o.at[k, pl.ds(half * r2, r2)]

        def remote(m, slot, chip, half, to):
            return pltpu.make_async_remote_copy(
                src_ref=half_of(m, chip, half), dst_ref=half_of(m, chip, half),
                send_sem=send_sems.at[m, slot], recv_sem=recv_sems.at[m, slot], device_id=to, device_id_type=MESH)

        first = [remote(m, j, (x, y), c, (*chip, c)) for m in range(n) for j, chip in enumerate(chips)]
        for cp in first:
            cp.start()
        passed = []
        for j, chip in enumerate(chips):
            for m in range(n):
                remote(m, j, chip, c, me).wait_recv()
                cp = remote(m, 3 + j, chip, c, sibling)
                cp.start()
                passed.append(cp)
        for j, chip in enumerate(chips):
            for m in range(n):
                remote(m, 3 + j, chip, 1 - c, me).wait_recv()
        for cp in first + passed:
            cp.wait_send()

    return pl.pallas_call(
        body, in_specs=_any_specs(n), out_specs=_any_specs(n), out_shape=[_sds(b.shape, b.dtype) for b in bufs],
        input_output_aliases={m: m for m in range(n)},
        scratch_shapes=[pltpu.SemaphoreType.DMA((n, 6)), pltpu.SemaphoreType.DMA((n, 6))],
        name=name)(*bufs)


def _send_sibling_halves(grads, name):
    n = len(grads)

    def body(*refs):
        g, out = refs[:n], refs[n:2 * n]
        send_sems, recv_sems = refs[2 * n:]
        x, y, c, _ = _place()
        copies = []
        for m in range(n):
            r2 = g[m].shape[1] // 2
            copies.append(pltpu.make_async_remote_copy(
                src_ref=g[m].at[:, pl.ds((1 - c) * r2, r2)], dst_ref=out[m],
                send_sem=send_sems.at[m], recv_sem=recv_sems.at[m], device_id=(x, y, 1 - c), device_id_type=MESH))
        for cp in copies:
            cp.start()
        for cp in copies:
            cp.wait_recv()
        for cp in copies:
            cp.wait_send()

    return pl.pallas_call(
        body, in_specs=_any_specs(n), out_specs=_any_specs(n),
        out_shape=[_sds((4, g.shape[1] // 2, g.shape[2]), g.dtype) for g in grads],
        scratch_shapes=[pltpu.SemaphoreType.DMA((n,)), pltpu.SemaphoreType.DMA((n,))], name=name)(*grads)


def _send_to_owner_chips(parts, name):
    n = len(parts)

    def body(*refs):
        p, out = refs[:n], refs[n:2 * n]
        send_sems, recv_sems = refs[2 * n:]
        _, _, c, chips = _place()
        copies = []
        for m in range(n):
            for j, chip in enumerate(chips):
                copies.append(pltpu.make_async_remote_copy(
                    src_ref=p[m].at[2 * chip[0] + chip[1]], dst_ref=out[m].at[j],
                    send_sem=send_sems.at[m, j], recv_sem=recv_sems.at[m, j],
                    device_id=(*chip, c), device_id_type=MESH))
        for cp in copies:
            cp.start()
        for cp in copies:
            cp.wait_recv()
        for cp in copies:
            cp.wait_send()

    return pl.pallas_call(
        body, in_specs=_any_specs(n), out_specs=_any_specs(n),
        out_shape=[_sds((3,) + p.shape[1:], p.dtype) for p in parts],
        scratch_shapes=[pltpu.SemaphoreType.DMA((n, 3)), pltpu.SemaphoreType.DMA((n, 3))], name=name)(*parts)


def _exchange_finished_halves(shards, name):
    n = len(shards)

    def body(*refs):
        out = refs[n:2 * n]
        send_sems, recv_sems = refs[2 * n:]
        x, y, c, _ = _place()
        copies = []
        for m in range(n):
            r2 = out[m].shape[0] // 2
            mine = out[m].at[pl.ds(c * r2, r2)]
            copies.append(pltpu.make_async_remote_copy(
                src_ref=mine, dst_ref=mine, send_sem=send_sems.at[m], recv_sem=recv_sems.at[m],
                device_id=(x, y, 1 - c), device_id_type=MESH))
        for cp in copies:
            cp.start()
        for m in range(n):
            r2 = out[m].shape[0] // 2
            theirs = out[m].at[pl.ds((1 - c) * r2, r2)]
            pltpu.make_async_remote_copy(
                src_ref=theirs, dst_ref=theirs, send_sem=send_sems.at[m], recv_sem=recv_sems.at[m],
                device_id=(x, y, 1 - c), device_id_type=MESH).wait_recv()
        for cp in copies:
            cp.wait_send()

    return pl.pallas_call(
        body, in_specs=_any_specs(n), out_specs=_any_specs(n), out_shape=[_sds(s.shape, s.dtype) for s in shards],
        input_output_aliases={m: m for m in range(n)},
        scratch_shapes=[pltpu.SemaphoreType.DMA((n,)), pltpu.SemaphoreType.DMA((n,))], name=name)(*shards)


def _all_reduce_small(v, name):
    rows = v.shape[0]

    def body(v_ref, out_ref, buf, send_sems, recv_sems, local_sem):
        x, y, c, chips = _place()
        me, sibling = (x, y, c), (x, y, 1 - c)

        def slot(px, py, pc):
            return buf.at[4 * px + 2 * py + pc]

        def copy(k, block, to, src=None):
            return pltpu.make_async_remote_copy(
                src_ref=slot(*block) if src is None else src, dst_ref=slot(*block),
                send_sem=send_sems.at[k], recv_sem=recv_sems.at[k], device_id=to, device_id_type=MESH)

        mine = pltpu.make_async_copy(v_ref, slot(*me), local_sem)
        mine.start()
        first = [copy(0, me, sibling, src=v_ref)]
        first += [copy(1 + j, me, (*chip, c), src=v_ref) for j, chip in enumerate(chips)]
        for cp in first:
            cp.start()
        passed = [copy(4 + j, (*chip, c), sibling) for j, chip in enumerate(chips)]
        for j, chip in enumerate(chips):
            copy(1 + j, (*chip, c), me).wait_recv()
            passed[j].start()
        copy(0, sibling, me).wait_recv()
        for j, chip in enumerate(chips):
            copy(4 + j, (*chip, 1 - c), me).wait_recv()
        for cp in first + passed:
            cp.wait_send()
        mine.wait()
        total = buf[0]
        for dev in range(1, 8):
            total = total + buf[dev]
        out_ref[...] = total

    vm = pl.BlockSpec(memory_space=pltpu.VMEM)
    return pl.pallas_call(
        body, in_specs=[vm], out_specs=vm, out_shape=_sds(v.shape, F32),
        scratch_shapes=[pltpu.VMEM((8, rows, LANES), F32), pltpu.SemaphoreType.DMA((7,)),
                        pltpu.SemaphoreType.DMA((7,)), pltpu.SemaphoreType.DMA],
        name=name, compiler_params=pltpu.CompilerParams(vmem_limit_bytes=VMEM_LIMIT))(v)


def _chip_partial(grad, from_sibling, core, name):
    _, r, cdim = grad.shape
    r2 = r // 2
    tr = _tile(r2, SUM_ROW_TILE)
    per = r2 // tr

    def body(core_ref, g_ref, s_ref, o_ref, ob_ref):
        del core_ref
        total = g_ref[...] + s_ref[...]
        o_ref[...] = total
        ob_ref[...] = total.astype(BF16)

    blk = pl.BlockSpec((None, tr, cdim), lambda k, i, core_ref: (k, i, 0))
    mine = pl.BlockSpec((None, tr, cdim), lambda k, i, core_ref: (k, core_ref[0] * per + i, 0))
    return pl.pallas_call(
        body,
        grid_spec=pltpu.PrefetchScalarGridSpec(num_scalar_prefetch=1, grid=(4, per), in_specs=[mine, blk],
                                               out_specs=[blk, blk]),
        out_shape=[_sds((4, r2, cdim), F32), _sds((4, r2, cdim), BF16)],
        name=name, compiler_params=_cp(2))(core, grad, from_sibling)


def _owner_sum(partial, from_chips, place, name):
    _, r2, cdim = partial.shape
    tr = _tile(r2, SUM_ROW_TILE)
    per = r2 // tr

    def body(place_ref, p_ref, f_ref, o_ref):
        del place_ref
        total = p_ref[...]
        for j in range(3):
            total = total + f_ref[j].astype(F32)
        o_ref[...] = total

    return pl.pallas_call(
        body,
        grid_spec=pltpu.PrefetchScalarGridSpec(
            num_scalar_prefetch=1, grid=(per,),
            in_specs=[pl.BlockSpec((None, tr, cdim), lambda i, place_ref: (place_ref[0], i, 0)),
                      pl.BlockSpec((3, tr, cdim), lambda i, place_ref: (0, i, 0))],
            out_specs=pl.BlockSpec((tr, cdim), lambda i, place_ref: (place_ref[1] * per + i, 0))),
        out_shape=_sds((2 * r2, cdim), F32), name=name, compiler_params=_cp(1))(place, partial, from_chips)


def _adamw(g, w, m, v, layer, prev, name):
    _, r, cdim = w.shape
    tr = _tile(r, OPT_ROW_TILE)
    c1 = 1.0 / (1.0 - ADAM_B1 ** ADAM_STEP)
    c2 = 1.0 / (1.0 - ADAM_B2 ** ADAM_STEP)
    n_prev = 0 if prev is None else 4

    def body(g_ref, w_ref, m_ref, v_ref, *rest):
        go_ref, d_ref, mo_ref, vo_ref = rest[n_prev:]
        grad = g_ref[:, pl.ds(0, cdim)]
        m_new = ADAM_B1 * m_ref[...] + (1.0 - ADAM_B1) * grad
        v_new = ADAM_B2 * v_ref[...] + (1.0 - ADAM_B2) * (grad * grad)
        go_ref[...] = grad
        mo_ref[...] = m_new
        vo_ref[...] = v_new
        d_ref[...] = -ADAM_LR * ((m_new * c1) / (jnp.sqrt(v_new * c2) + ADAM_EPS) + ADAM_WD * w_ref[...])

    blk = pl.BlockSpec((None, tr, cdim), lambda i: (layer, i, 0))
    gblk = pl.BlockSpec((tr, g.shape[1]), lambda i: (i, 0))
    return pl.pallas_call(
        body, grid=(r // tr,), in_specs=[gblk, blk, blk, blk] + _any_specs(n_prev), out_specs=[blk] * 4,
        out_shape=[_sds(w.shape, F32)] * 4, input_output_aliases={4 + k: k for k in range(n_prev)},
        name=name, compiler_params=_cp(1))(g, w, m, v, *(prev or ()))


def _pack_rows(vectors):
    flat = [v.reshape(-1) for v in vectors]
    sizes = [f.shape[0] for f in flat]
    total = sum(sizes)
    padded = _round_up(total, 8 * LANES)
    buf = jnp.concatenate(flat + [jnp.zeros((padded - total,), F32)])
    return buf.reshape(padded // LANES, LANES), sizes


def _unpack_rows(buf, sizes, shapes):
    flat = buf.reshape(-1)
    out, off = [], 0
    for n, shp in zip(sizes, shapes):
        out.append(flat[off:off + n].reshape(shp))
        off += n
    return out


def kernel(x, pool_w, pool_scale, attn_w_qkv, attn_w_o, ffn_w_up, ffn_conv_w, ffn_conv_b, ffn_w_down, ln_mix_g, ln_mix_b, ln_ffn_g, ln_ffn_b, loss_target, m_pool_w, m_pool_scale, m_attn_w_qkv, m_attn_w_o, m_ffn_w_up, m_ffn_conv_w, m_ffn_conv_b, m_ffn_w_down, m_ln_mix_g, m_ln_mix_b, m_ln_ffn_g, m_ln_ffn_b, v_pool_w, v_pool_scale, v_attn_w_qkv, v_attn_w_o, v_ffn_w_up, v_ffn_conv_w, v_ffn_conv_b, v_ffn_w_down, v_ln_mix_g, v_ln_mix_b, v_ln_ffn_g, v_ln_ffn_b):
    t, d = x.shape[1], x.shape[2]
    n_heads = d // HEAD_DIM
    n_groups = pool_w.shape[1]
    fs = ffn_w_up.shape[2]
    fp = _round_up(fs, LANES)
    rd = ffn_w_down.shape[1]
    assert 2 * rd == fs
    xi, yi, ci = lax.axis_index("x"), lax.axis_index("y"), lax.axis_index("c")
    chip = (2 * xi + yi).astype(jnp.int32)
    chip_arr, core_arr = chip.reshape(1), ci.astype(jnp.int32).reshape(1)
    place_arr = jnp.concatenate([chip_arr, core_arr])

    x2 = x.reshape(t, d)
    target = loss_target.reshape(t, d)
    pad_cols = lambda a: jnp.pad(a, [(0, 0)] * (a.ndim - 1) + [(0, fp - fs)])

    weights = []
    for i in range(DEPTH):
        j = i // 2
        items = []
        if i % 2 == 0:
            items.append(("pool", _place_shard("pool", pool_w[j], chip_arr, name="place_pool"), pool_w.shape[2]))
        else:
            items.append(("lead", _place_shard("lead", attn_w_qkv[j], chip_arr, name="place_qkv"), d))
            items.append(("lead", _place_shard("lead", attn_w_o[j], chip_arr, name="place_wo"), attn_w_o.shape[1]))
        items.append(("lead", _place_shard("lead", ffn_w_up[i], chip_arr, name="place_up", cols=fp), d))
        items.append(("down", _place_shard("down", ffn_w_down[i], chip_arr, name="place_down",
                                           base=jnp.zeros((2, fp, d), BF16)), rd))
        weights.append(_gather_weights(items, name="gather_pool_layer" if i % 2 == 0 else "gather_attn_layer"))

    conv_b_all = pad_cols(ffn_conv_b.reshape(DEPTH, 4, 1, fs))
    cw_local = pad_cols(ffn_conv_w)
    slot = (jnp.arange(4, dtype=jnp.int32) == chip).astype(F32) * (1.0 - ci.astype(F32))
    cw_placed = slot[None, :, None, None] * cw_local[:, None]
    cw_buf, cw_sizes = _pack_rows([cw_placed])
    conv_w_all = _unpack_rows(_all_reduce_small(cw_buf, name="gather_conv_w"), cw_sizes, [cw_placed.shape])[0]

    gam = lambda a, i: a[i].reshape(1, d)

    saved = []
    cur, cur_b = x2, x2.astype(BF16)
    for i in range(DEPTH):
        j = i // 2
        w = weights[i]
        s = {"x_in": cur, "x_in_b": cur_b}
        if i % 2 == 0:
            w_pool, w_up, w_down = w
            s["scale"] = pool_scale[j].reshape(1, d)
            r1, x1, x1b = _pool_fwd(cur, w_pool, s["scale"], gam(ln_mix_g, i), gam(ln_mix_b, i), name="pool_fwd")
        else:
            w_qkv, w_o, w_up, w_down = w
            w_o3 = w_o.reshape(1, d, d)
            qkv = _mm_cols(cur_b, w_qkv, BF16, name="qkv_proj")
            o = _attn_fwd(qkv, n_heads, name="attn_fwd")
            s["qkv"], s["o"], s["w_o3"] = qkv, o, w_o3
            r1, x1, x1b = _mm_res_ln(o.reshape(1, t, d), w_o3, cur, gam(ln_mix_g, i), gam(ln_mix_b, i),
                                     name="attn_out_ln")
        h = _mm_cols(x1b, w_up, F32, name="ffn_up")
        a = _ffn_act(h, conv_w_all[i], conv_b_all[i], name="ffn_act")
        r2, x2n, x2b = _mm_res_ln(a, w_down, x1, gam(ln_ffn_g, i), gam(ln_ffn_b, i), name="ffn_down_ln")
        s.update(r1=r1, x1b=x1b, h=h, a=a, r2=r2)
        saved.append(s)
        cur, cur_b = x2n, x2b

    loss_row, dcur = _loss_and_grad(cur, target, name="loss")
    loss = lax.psum(loss_row[0, 0], ("x", "y", "c"))

    big_grads = [None] * DEPTH
    small = {}
    for i in reversed(range(DEPTH)):
        j = i // 2
        s, w = saved[i], weights[i]
        w_up, w_down = w[-2], w[-1]
        dr2, dr2b, small["ln_ffn_g", i], small["ln_ffn_b", i] = _ln_bwd(dcur, s["r2"], gam(ln_ffn_g, i), name="ln_bwd")
        da = _mm_cols(dr2b, w_down, F32, name="ffn_down_bwd_act", transposed_b=True)
        dr2b3 = dr2b.reshape(1, t, d)
        nmb = 2
        d_down = _mm_tn(s["a"], dr2b3, (2, fp, d), fp // nmb, d, 2,
                        (lambda u: u, nmb, lambda u, mb: mb), (lambda u: 0, lambda u: 0),
                        (lambda u: u, lambda u, mb: mb, lambda u: 0), name="ffn_down_bwd_w")
        dhc, dcw, dcb = _ffn_act_bwd(s["h"], da, conv_w_all[i], conv_b_all[i], name="ffn_act_bwd")
        small["conv_w", i], small["conv_b", i] = dcw, dcb
        dh = _conv_adjoint(dhc.reshape(4, t, fp), conv_w_all[i], name="ffn_conv_adjoint")
        dx1 = _mm_nt_acc(dh, w_up, dr2, fp, name="ffn_up_bwd_act")
        d_up = _mm_tn(s["x1b"].reshape(1, t, d), dh, (4, d, fp), d // 2, fp, 4,
                      (lambda u: 0, 2, lambda u, mb: mb), (lambda u: u, lambda u: 0),
                      (lambda u: u, lambda u, mb: mb, lambda u: 0), name="ffn_up_bwd_w")
        dr1, dr1b, small["ln_mix_g", i], small["ln_mix_b", i] = _ln_bwd(dx1, s["r1"], gam(ln_mix_g, i), name="ln_bwd")
        d_down4 = d_down[:, :fs].reshape(4, rd, d)
        if i % 2 == 0:
            dp, d_pool, small["pool_scale", j] = _pool_bwd(s["x_in"], dr1, w[0], s["scale"], name="pool_bwd")
            dcur = _pool_adjoint(dp, dr1, n_groups, name="pool_adjoint")
            cg = d // n_groups
            d_pool4 = d_pool.reshape(n_groups, 4, cg // 4, cg).transpose(1, 0, 2, 3).reshape(4, n_groups * (cg // 4), cg)
            big_grads[i] = [d_pool4, d_up, d_down4]
        else:
            w_qkv = w[0]
            do = _mm_cols(dr1b, s["w_o3"], BF16, name="attn_out_bwd_act", transposed_b=True)
            d_wo = _mm_tn(s["o"].reshape(1, t, d), dr1b.reshape(1, t, d), (1, d, d), d // 2, d, 1,
                          (lambda u: 0, 2, lambda u, mb: mb), (lambda u: 0, lambda u: 0),
                          (lambda u: 0, lambda u, mb: mb, lambda u: 0), name="attn_out_bwd_w")
            dq, dk, dv = _attn_bwd(s["qkv"], do, n_heads, name="attn_bwd")
            dqkv = jnp.stack([dq, dk, dv])
            cq = w_qkv.shape[2]
            kb = cq // 3
            na, nbk = d // kb, cq // kb
            dcur = _mm_nt_acc(dqkv, w_qkv, dr1, kb, name="qkv_bwd_act")
            d_qkv = _mm_tn(s["x_in_b"].reshape(1, t, d), dqkv, (4, d, cq), d // 2, kb, 3 * na,
                           (lambda u: 0, 2, lambda u, mb: mb), (lambda u: u // na, lambda u: u % na),
                           (lambda u: u // nbk, lambda u, mb: mb, lambda u: u % nbk), name="qkv_bwd_w")
            big_grads[i] = [d_qkv, d_wo.reshape(4, d // 4, d), d_up, d_down4]
    grad_x = dcur.reshape(1, t, d)

    reduced = []
    for i in range(DEPTH):
        tag = "pool" if i % 2 == 0 else "attn"
        grads = big_grads[i]
        from_sib = _send_sibling_halves(grads, name=f"reduce_{tag}_sibling")
        parts = [_chip_partial(g, fs_, core_arr, name="reduce_chip_partial") for g, fs_ in zip(grads, from_sib)]
        from_chips = _send_to_owner_chips([p[1] for p in parts], name=f"reduce_{tag}_chips")
        halves = [_owner_sum(p[0], fc, place_arr, name="reduce_owner_sum") for p, fc in zip(parts, from_chips)]
        reduced.append(_exchange_finished_halves(halves, name=f"reduce_{tag}_halves"))

    names = [("pool_scale", j) for j in range(2)]
    for nm in ("ln_mix_g", "ln_mix_b", "ln_ffn_g", "ln_ffn_b", "conv_b", "conv_w"):
        names += [(nm, i) for i in range(DEPTH)]
    vecs = [small[k] for k in names]
    sbuf, ssizes = _pack_rows(vecs)
    summed = dict(zip(names, _unpack_rows(_all_reduce_small(sbuf, name="reduce_small"), ssizes, [v.shape for v in vecs])))

    def stack_layers(nm, count):
        return jnp.stack([summed[nm, i] for i in range(count)])

    g_small = {
        "pool_scale": stack_layers("pool_scale", 2).reshape(2, d),
        "ln_mix_g": stack_layers("ln_mix_g", DEPTH).reshape(DEPTH, d),
        "ln_mix_b": stack_layers("ln_mix_b", DEPTH).reshape(DEPTH, d),
        "ln_ffn_g": stack_layers("ln_ffn_g", DEPTH).reshape(DEPTH, d),
        "ln_ffn_b": stack_layers("ln_ffn_b", DEPTH).reshape(DEPTH, d),
        "conv_b": stack_layers("conv_b", DEPTH).reshape(DEPTH, 4, fp)[:, :, :fs].reshape(DEPTH, 4 * fs),
        "conv_w": lax.dynamic_index_in_dim(stack_layers("conv_w", DEPTH).reshape(DEPTH, 4, 3, fp), chip, axis=1,
                                           keepdims=False)[:, :, :fs],
    }
    w_small = {"pool_scale": (pool_scale, m_pool_scale, v_pool_scale), "ln_mix_g": (ln_mix_g, m_ln_mix_g, v_ln_mix_g),
               "ln_mix_b": (ln_mix_b, m_ln_mix_b, v_ln_mix_b), "ln_ffn_g": (ln_ffn_g, m_ln_ffn_g, v_ln_ffn_g),
               "ln_ffn_b": (ln_ffn_b, m_ln_ffn_b, v_ln_ffn_b), "conv_b": (ffn_conv_b, m_ffn_conv_b, v_ffn_conv_b),
               "conv_w": (ffn_conv_w, m_ffn_conv_w, v_ffn_conv_w)}
    order = list(g_small)
    packs = [_pack_rows([g_small[k] for k in order])[0]]
    for idx in range(3):
        packs.append(_pack_rows([w_small[k][idx] for k in order])[0])
    small_sizes = _pack_rows([g_small[k] for k in order])[1]
    small_out = _adamw(packs[0], packs[1][None], packs[2][None], packs[3][None], 0, None, name="adamw_small")
    shapes = [g_small[k].shape for k in order]
    small_res = {k: [] for k in order}
    for arr in small_out:
        for k, val in zip(order, _unpack_rows(arr[0], small_sizes, shapes)):
            small_res[k].append(val)

    def opt_layers(per_layer_grads, w_all, m_all, v_all, name):
        n_layers, r = w_all.shape[0], per_layer_grads[0].shape[0]
        flat = [a.reshape(n_layers, r, -1) for a in (w_all, m_all, v_all)]
        res = None
        for li, g in enumerate(per_layer_grads):
            res = _adamw(g, *flat, li, res, name=name)
        return [o.reshape(w_all.shape) for o in res]

    cg = d // n_groups
    pool_g = [reduced[i][0].reshape(n_groups, cg // 4, cg).reshape(n_groups * (cg // 4), cg) for i in (0, 2)]
    big = {
        "pool_w": opt_layers(pool_g, pool_w, m_pool_w, v_pool_w, "adamw_pool"),
        "attn_w_qkv": opt_layers([reduced[i][0] for i in (1, 3)], attn_w_qkv, m_attn_w_qkv, v_attn_w_qkv, "adamw_qkv"),
        "attn_w_o": opt_layers([reduced[i][1] for i in (1, 3)], attn_w_o, m_attn_w_o, v_attn_w_o, "adamw_wo"),
        "ffn_w_up": opt_layers([reduced[i][-2] for i in range(DEPTH)], ffn_w_up, m_ffn_w_up, v_ffn_w_up, "adamw_up"),
        "ffn_w_down": opt_layers([reduced[i][-1] for i in range(DEPTH)], ffn_w_down, m_ffn_w_down, v_ffn_w_down,
                                 "adamw_down"),
    }

    def leaf(k, name):
        if name in big:
            return big[name][k]
        key = {"ffn_conv_w": "conv_w", "ffn_conv_b": "conv_b"}.get(name, name)
        return small_res[key][k]

    weight_names = ["pool_w", "pool_scale", "attn_w_qkv", "attn_w_o", "ffn_w_up", "ffn_conv_w", "ffn_conv_b",
                    "ffn_w_down", "ln_mix_g", "ln_mix_b", "ln_ffn_g", "ln_ffn_b"]
    outs = [loss, grad_x]
    for k in range(4):
        outs += [leaf(k, nm) for nm in weight_names]
    return tuple(outs)
```

```python
import collections

import jax
import jax.numpy as jnp
from jax import lax
from jax.experimental import pallas as pl
from jax.experimental.pallas import tpu as pltpu

F32, BF16 = jnp.float32, jnp.bfloat16
MESH = pl.DeviceIdType.MESH

LANES = 128
HEAD_DIM = 128
ATT_BLOCK = 128
ATT_WINDOW = 3 * ATT_BLOCK
POOL_WINDOWS = (2, 4, 8, 16)
POOL_HALO = 16
CONV_HALO = 8
LN_EPS = 1e-5
DEPTH = 4
ALPHA = (2.0 * DEPTH) ** 0.25
ATT_SCALE = HEAD_DIM ** -0.5
EXP_ZERO = 115.0
MASKED = 1e30
ADAM_LR, ADAM_B1, ADAM_B2, ADAM_EPS, ADAM_WD, ADAM_STEP = 0.001, 0.9, 0.999, 1e-08, 0.01, 10

VMEM_LIMIT = 56 << 20
ROW_TILE = 512
LN_ROW_TILE = 256
OPT_ROW_TILE = 128
SUM_ROW_TILE = 512


def _cp(n_axes):
    return pltpu.CompilerParams(dimension_semantics=("arbitrary",) * n_axes, vmem_limit_bytes=VMEM_LIMIT)


def _sds(shape, dtype):
    return jax.ShapeDtypeStruct(tuple(shape), dtype)


def _round_up(n, m):
    return (n + m - 1) // m * m


def _tile(n, cap, mult=8):
    if n <= cap:
        return n
    best = None
    for d in range(mult, cap + 1, mult):
        if n % d == 0:
            best = d
    assert best is not None, (n, cap)
    return best


_NT = (((1,), (1,)), ((), ()))
_TN = (((0,), (0,)), ((), ()))

_Side = collections.namedtuple("_Side", "ins outs alias sems start finish")


def _any_specs(n):
    return [pl.BlockSpec(memory_space=pl.ANY)] * n


def _call(body, first, last, side, *, grid, in_specs, out_specs, out_shape, scratch_shapes, name, args):
    n_axes = len(grid)
    if side is None:
        res = pl.pallas_call(body, grid=grid, in_specs=in_specs, out_specs=out_specs, out_shape=out_shape,
                             scratch_shapes=scratch_shapes, name=name, compiler_params=_cp(n_axes))(*args)
        return res, ()
    n_in, n_out, n_scr = len(in_specs), len(out_shape), len(scratch_shapes)
    s_in, s_out = len(side.ins), len(side.outs)

    def carried(*refs):
        ins, refs = refs[:n_in], refs[n_in:]
        side_ins, refs = refs[:s_in], refs[s_in:]
        outs, refs = refs[:n_out], refs[n_out:]
        side_outs, refs = refs[:s_out], refs[s_out:]
        scratch, side_sems = refs[:n_scr], refs[n_scr:]

        @pl.when(first())
        def _():
            side.start(side_ins, side_outs, side_sems)

        body(*ins, *outs, *scratch)

        @pl.when(last())
        def _():
            side.finish(side_ins, side_outs, side_sems)

    res = pl.pallas_call(
        carried, grid=grid, in_specs=list(in_specs) + _any_specs(s_in), out_specs=list(out_specs) + _any_specs(s_out),
        out_shape=list(out_shape) + list(side.outs), scratch_shapes=list(scratch_shapes) + list(side.sems),
        input_output_aliases={n_in + a: n_out + b for a, b in side.alias.items()},
        name=name, compiler_params=_cp(n_axes))(*args, *side.ins)
    return res[:n_out], res[n_out:]


def _mm_cols(a, b, out_dtype, name, transposed_b=False, side=None):
    t, k = a.shape
    g = b.shape[0]
    nb = b.shape[1] if transposed_b else b.shape[2]
    tm = _tile(t, ROW_TILE)
    steps = t // tm

    def body(a_ref, b_ref, o_ref):
        if transposed_b:
            acc = lax.dot_general(a_ref[...], b_ref[...], _NT, preferred_element_type=F32)
        else:
            acc = jnp.dot(a_ref[...], b_ref[...], preferred_element_type=F32)
        o_ref[...] = acc.astype(o_ref.dtype)

    first = lambda: jnp.logical_and(pl.program_id(0) == 0, pl.program_id(1) == 0)
    last = lambda: jnp.logical_and(pl.program_id(0) == g - 1, pl.program_id(1) == steps - 1)
    (out,), side_out = _call(
        body, first, last, side, grid=(g, steps),
        in_specs=[pl.BlockSpec((tm, k), lambda gi, i: (i, 0)),
                  pl.BlockSpec((None,) + b.shape[1:], lambda gi, i: (gi, 0, 0))],
        out_specs=[pl.BlockSpec((None, tm, nb), lambda gi, i: (gi, i, 0))],
        out_shape=[_sds((g, t, nb), out_dtype)], scratch_shapes=[], name=name, args=(a, b))
    return out if side is None else (out, side_out)


def _mm_nt_acc(a3, b3, res, kb, name, side=None):
    ga, t, ka = a3.shape
    gb, n, kbb = b3.shape
    na, nbk = ka // kb, kbb // kb
    groups = ga * na
    assert groups == gb * nbk
    tm = _tile(t, ROW_TILE)
    steps = t // tm

    def body(a_ref, b_ref, res_ref, o_ref, acc):
        u = pl.program_id(1)

        @pl.when(u == 0)
        def _():
            acc[...] = ALPHA * res_ref[...]

        acc[...] += lax.dot_general(a_ref[...], b_ref[...], _NT, preferred_element_type=F32)

        @pl.when(u == groups - 1)
        def _():
            o_ref[...] = acc[...]

    first = lambda: jnp.logical_and(pl.program_id(0) == 0, pl.program_id(1) == 0)
    last = lambda: jnp.logical_and(pl.program_id(0) == steps - 1, pl.program_id(1) == groups - 1)
    (out,), side_out = _call(
        body, first, last, side, grid=(steps, groups),
        in_specs=[pl.BlockSpec((None, tm, kb), lambda i, u: (u // na, i, u % na)),
                  pl.BlockSpec((None, n, kb), lambda i, u: (u // nbk, 0, u % nbk)),
                  pl.BlockSpec((tm, n), lambda i, u: (i, 0))],
        out_specs=[pl.BlockSpec((tm, n), lambda i, u: (i, 0))],
        out_shape=[_sds((t, n), F32)], scratch_shapes=[pltpu.VMEM((tm, n), F32)], name=name, args=(a3, b3, res))
    return out if side is None else (out, side_out)


def _mm_tn(x3, dy3, out_shape, bm, bn, groups, x_idx, dy_idx, out_idx, name):
    t = x3.shape[1]
    tm = _tile(t, 2 * ROW_TILE)

    def body(x_ref, dy_ref, o_ref):
        @pl.when(pl.program_id(2) == 0)
        def _():
            o_ref[...] = jnp.zeros_like(o_ref)

        o_ref[...] += lax.dot_general(x_ref[...], dy_ref[...], _TN, preferred_element_type=F32)

    return pl.pallas_call(
        body, grid=(groups, x_idx[1], t // tm),
        in_specs=[pl.BlockSpec((None, tm, bm), lambda u, mb, i: (x_idx[0](u), i, x_idx[2](u, mb))),
                  pl.BlockSpec((None, tm, bn), lambda u, mb, i: (dy_idx[0](u), i, dy_idx[1](u)))],
        out_specs=pl.BlockSpec((None, bm, bn), lambda u, mb, i: (out_idx[0](u), out_idx[1](u, mb), out_idx[2](u))),
        out_shape=_sds(out_shape, F32), name=name, compiler_params=_cp(3))(x3, dy3)


def _layer_norm_rows(r, gamma, beta):
    mu = jnp.mean(r, axis=-1, keepdims=True)
    xc = r - mu
    var = jnp.mean(xc * xc, axis=-1, keepdims=True)
    return xc * lax.rsqrt(var + LN_EPS) * gamma + beta


def _mm_res_ln(a3, w3, res, gamma, beta, name, side=None):
    g, t, kb = a3.shape
    d = w3.shape[2]
    tm = _tile(t, LN_ROW_TILE)
    steps = t // tm

    def body(a_ref, w_hbm, res_ref, g_ref, b_ref, r_ref, o_ref, ob_ref, w_vmem, sem):
        @pl.when(pl.program_id(0) == 0)
        def _():
            cp = pltpu.make_async_copy(w_hbm, w_vmem, sem)
            cp.start()
            cp.wait()

        acc = ALPHA * res_ref[...]
        for gi in range(g):
            acc = acc + jnp.dot(a_ref[gi], w_vmem[gi], preferred_element_type=F32)
        r_ref[...] = acc
        out = _layer_norm_rows(acc, g_ref[...], b_ref[...])
        o_ref[...] = out
        ob_ref[...] = out.astype(BF16)

    row = pl.BlockSpec((tm, d), lambda i: (i, 0))
    vec = pl.BlockSpec((1, d), lambda i: (0, 0))
    outs, side_out = _call(
        body, lambda: pl.program_id(0) == 0, lambda: pl.program_id(0) == steps - 1, side, grid=(steps,),
        in_specs=[pl.BlockSpec((g, tm, kb), lambda i: (0, i, 0)), pl.BlockSpec(memory_space=pl.ANY), row, vec, vec],
        out_specs=[row, row, row],
        out_shape=[_sds((t, d), F32), _sds((t, d), F32), _sds((t, d), BF16)],
        scratch_shapes=[pltpu.VMEM(w3.shape, w3.dtype), pltpu.SemaphoreType.DMA],
        name=name, args=(a3, w3, res, gamma, beta))
    return outs if side is None else (outs, side_out)


def _ln_bwd(dout, r, gamma, name):
    t, d = r.shape
    tm = _tile(t, ROW_TILE)

    def body(do_ref, r_ref, g_ref, dr_ref, drb_ref, dg_ref, db_ref):
        @pl.when(pl.program_id(0) == 0)
        def _():
            dg_ref[...] = jnp.zeros_like(dg_ref)
            db_ref[...] = jnp.zeros_like(db_ref)

        rr = r_ref[...]
        do = do_ref[...]
        mu = jnp.mean(rr, axis=-1, keepdims=True)
        xc = rr - mu
        rstd = lax.rsqrt(jnp.mean(xc * xc, axis=-1, keepdims=True) + LN_EPS)
        xhat = xc * rstd
        dxh = do * g_ref[...]
        m1 = jnp.mean(dxh, axis=-1, keepdims=True)
        m2 = jnp.mean(dxh * xhat, axis=-1, keepdims=True)
        dr = rstd * (dxh - m1 - xhat * m2)
        dr_ref[...] = dr
        drb_ref[...] = dr.astype(BF16)
        dg_ref[...] += jnp.sum(do * xhat, axis=0, keepdims=True)
        db_ref[...] += jnp.sum(do, axis=0, keepdims=True)

    row = pl.BlockSpec((tm, d), lambda i: (i, 0))
    vec = pl.BlockSpec((1, d), lambda i: (0, 0))
    return pl.pallas_call(
        body, grid=(t // tm,), in_specs=[row, row, vec], out_specs=[row, row, vec, vec],
        out_shape=[_sds((t, d), F32), _sds((t, d), BF16), _sds((1, d), F32), _sds((1, d), F32)],
        name=name, compiler_params=_cp(1))(dout, r, gamma)


def _loss_and_grad(y, target, name):
    t, d = y.shape
    tm = _tile(t, ROW_TILE)
    steps = t // tm

    def body(y_ref, t_ref, loss_ref, dy_ref, acc):
        i = pl.program_id(0)

        @pl.when(i == 0)
        def _():
            acc[...] = jnp.zeros_like(acc)

        diff = y_ref[...] - t_ref[...]
        dy_ref[...] = diff * (1.0 / d)
        acc[...] += jnp.sum(diff * diff, axis=0, keepdims=True)

        @pl.when(i == steps - 1)
        def _():
            total = jnp.sum(acc[...], axis=1, keepdims=True) * (0.5 / d)
            loss_ref[...] = jnp.broadcast_to(total, loss_ref.shape)

    row = pl.BlockSpec((tm, d), lambda i: (i, 0))
    return pl.pallas_call(
        body, grid=(steps,), in_specs=[row, row],
        out_specs=[pl.BlockSpec((1, LANES), lambda i: (0, 0)), row],
        out_shape=[_sds((1, LANES), F32), _sds((t, d), F32)],
        scratch_shapes=[pltpu.VMEM((1, d), F32)], name=name, compiler_params=_cp(1))(y, target)


def _window_sums(ext, window, forward):
    n = ext.shape[0]
    s, span = ext, 1
    while span < window:
        s = s + pltpu.roll(s, (n - span) if forward else span, 0)
        span *= 2
    return s


def _pooled_group(main, halo, gi, row0):
    window = POOL_WINDOWS[gi]
    ext = jnp.concatenate([halo, main], axis=0)
    sums = _window_sums(ext, window, forward=False)[POOL_HALO:, :]
    pos = row0 + lax.broadcasted_iota(jnp.int32, (main.shape[0], 1), 0)
    cnt = jnp.minimum(pos + 1, window).astype(F32)
    return sums / cnt - main


def _pool_specs(t, d, tm):
    per = tm // POOL_HALO
    main = pl.BlockSpec((tm, d), lambda i: (i, 0))
    before = pl.BlockSpec((POOL_HALO, d), lambda i: (jnp.maximum(i * per - 1, 0), 0))
    return main, before


def _pool_fwd(x, w, scale, gamma, beta, name):
    t, d = x.shape
    ng, cg = w.shape[0], w.shape[1]
    tm = _tile(t, LN_ROW_TILE)

    def body(x_ref, h_ref, w_ref, s_ref, g_ref, b_ref, r_ref, o_ref, ob_ref):
        i = pl.program_id(0)
        for gi in range(ng):
            cols = pl.ds(gi * cg, cg)
            main = x_ref[:, cols]
            halo = jnp.where(i > 0, h_ref[:, cols], 0.0)
            pooled = _pooled_group(main, halo, gi, i * tm)
            y = jnp.dot(pooled.astype(BF16), w_ref[gi], preferred_element_type=F32)
            r_ref[:, cols] = ALPHA * main + y * s_ref[:, cols]
        out = _layer_norm_rows(r_ref[...], g_ref[...], b_ref[...])
        o_ref[...] = out
        ob_ref[...] = out.astype(BF16)

    main, before = _pool_specs(t, d, tm)
    vec = pl.BlockSpec((1, d), lambda i: (0, 0))
    return pl.pallas_call(
        body, grid=(t // tm,),
        in_specs=[main, before, pl.BlockSpec(w.shape, lambda i: (0, 0, 0)), vec, vec, vec],
        out_specs=[main, main, main],
        out_shape=[_sds((t, d), F32), _sds((t, d), F32), _sds((t, d), BF16)],
        name=name, compiler_params=_cp(1))(x, x, w, scale, gamma, beta)


def _pool_bwd(x, dy, w, scale, name):
    t, d = x.shape
    ng, cg = w.shape[0], w.shape[1]
    tm = _tile(t, LN_ROW_TILE)

    def body(x_ref, h_ref, dy_ref, w_ref, s_ref, dp_ref, dw_ref, ds_ref):
        i = pl.program_id(0)

        @pl.when(i == 0)
        def _():
            dw_ref[...] = jnp.zeros_like(dw_ref)
            ds_ref[...] = jnp.zeros_like(ds_ref)

        for gi in range(ng):
            cols = pl.ds(gi * cg, cg)
            main = x_ref[:, cols]
            halo = jnp.where(i > 0, h_ref[:, cols], 0.0)
            pooled = _pooled_group(main, halo, gi, i * tm).astype(BF16)
            y = jnp.dot(pooled, w_ref[gi], preferred_element_type=F32)
            dyg = dy_ref[:, cols]
            ds_ref[:, cols] += jnp.sum(dyg * y, axis=0, keepdims=True)
            dyw = (dyg * s_ref[:, cols]).astype(BF16)
            dw_ref[gi] += lax.dot_general(pooled, dyw, _TN, preferred_element_type=F32)
            dp_ref[:, cols] = lax.dot_general(dyw, w_ref[gi], _NT, preferred_element_type=F32)

    main, before = _pool_specs(t, d, tm)
    vec = pl.BlockSpec((1, d), lambda i: (0, 0))
    wspec = pl.BlockSpec(w.shape, lambda i: (0, 0, 0))
    return pl.pallas_call(
        body, grid=(t // tm,), in_specs=[main, before, main, wspec, vec],
        out_specs=[main, wspec, vec],
        out_shape=[_sds((t, d), F32), _sds(w.shape, F32), _sds((1, d), F32)],
        name=name, compiler_params=_cp(1))(x, x, dy, w, scale)


def _pool_adjoint(dp, dres, n_groups, name):
    t, d = dp.shape
    cg = d // n_groups
    tm = _tile(t, ROW_TILE)
    steps = t // tm
    per = tm // POOL_HALO

    def body(dp_ref, after_ref, dres_ref, dx_ref):
        i = pl.program_id(0)
        rows = lax.broadcasted_iota(jnp.int32, (tm, 1), 0)
        rows_after = lax.broadcasted_iota(jnp.int32, (POOL_HALO, 1), 0)
        for gi in range(n_groups):
            window = POOL_WINDOWS[gi]
            cols = pl.ds(gi * cg, cg)
            main = dp_ref[:, cols]
            cnt = jnp.minimum(i * tm + rows + 1, window).astype(F32)
            cnt_after = jnp.minimum((i + 1) * tm + rows_after + 1, window).astype(F32)
            after = jnp.where(i < steps - 1, after_ref[:, cols] / cnt_after, 0.0)
            ext = jnp.concatenate([main / cnt, after], axis=0)
            sums = _window_sums(ext, window, forward=True)[:tm, :]
            dx_ref[:, cols] = ALPHA * dres_ref[:, cols] + sums - main

    main = pl.BlockSpec((tm, d), lambda i: (i, 0))
    after = pl.BlockSpec((POOL_HALO, d), lambda i: (jnp.minimum((i + 1) * per, t // POOL_HALO - 1), 0))
    return pl.pallas_call(
        body, grid=(steps,), in_specs=[main, after, main], out_specs=main,
        out_shape=_sds((t, d), F32), name=name, compiler_params=_cp(1))(dp, dp, dres)


def _split_dot(x, tri):
    hi = x.astype(BF16)
    lo = (x - hi.astype(F32)).astype(BF16)
    return jnp.dot(hi, tri, preferred_element_type=F32) + jnp.dot(lo, tri, preferred_element_type=F32)


def _att_window(q, k_w, limit, carry_rest, suffix):
    z = lax.dot_general(q, k_w, _NT, preferred_element_type=F32) * ATT_SCALE
    z = jnp.where(lax.broadcasted_iota(jnp.int32, z.shape, 1) < limit, z, -MASKED)
    e = jnp.exp(-jnp.abs(z))
    log_not = -(jnp.maximum(z, 0.0) + jnp.log(1.0 + e))
    rest = _split_dot(log_not, suffix) + carry_rest
    a = jnp.exp(z + rest)
    return z, e, log_not, a


def _tri(w, strict):
    r = lax.broadcasted_iota(jnp.int32, (w, w), 0)
    c = lax.broadcasted_iota(jnp.int32, (w, w), 1)
    return ((r > c) if strict else (r >= c)).astype(BF16)


def _heads_per_step(qkv3, n_heads):
    cpb = qkv3.shape[2] // HEAD_DIM
    return 2 if (cpb % 2 == 0 and n_heads % 2 == 0) else 1


def _att_specs(qkv3, n_heads, hp):
    t = qkv3.shape[1]
    cpb = qkv3.shape[2] // HEAD_DIM
    wd = hp * HEAD_DIM

    def slab(off):
        return pl.BlockSpec((None, t, wd), lambda g, i: ((off + g * hp) // cpb, 0, ((off + g * hp) % cpb) // hp))

    q = pl.BlockSpec((None, ATT_BLOCK, wd), lambda g, i: ((g * hp) // cpb, i, ((g * hp) % cpb) // hp))
    return q, slab(n_heads), slab(2 * n_heads)


def _head_cols(hh):
    return pl.ds(hh * HEAD_DIM, HEAD_DIM)


def _key_bounds(k_ref, kmax, hp):
    for hh in range(hp):
        kf = k_ref[:, _head_cols(hh)].astype(F32)
        kmax[hh] = jnp.sqrt(jnp.max(jnp.sum(kf * kf, axis=1, keepdims=True)))


def _score_bound(q, key_norm):
    qf = q.astype(F32)
    return ATT_SCALE * 1.001 * key_norm * jnp.sqrt(jnp.sum(qf * qf, axis=1, keepdims=True)) + 1e-3


def _any_alive(rests, bounds):
    alive = jnp.max(rests[0] + bounds[0]) > -EXP_ZERO
    for r, zb in zip(rests[1:], bounds[1:]):
        alive = jnp.logical_or(alive, jnp.max(r + zb) > -EXP_ZERO)
    return alive


def _window_rows(hi, w):
    start = jnp.maximum(hi - w, 0)
    return start, pl.ds(pl.multiple_of(start, ATT_BLOCK), w)


def _attn_fwd(qkv3, n_heads, name):
    t = qkv3.shape[1]
    b = ATT_BLOCK
    w = min(ATT_WINDOW, t)
    hp = _heads_per_step(qkv3, n_heads)
    heads = range(hp)

    def body(q_ref, k_ref, v_ref, o_ref, kmax):
        i = pl.program_id(1)

        @pl.when(i == 0)
        def _():
            _key_bounds(k_ref, kmax, hp)

        qs = [q_ref[:, _head_cols(hh)] for hh in heads]
        bounds = [_score_bound(qs[hh], kmax[hh]) for hh in heads]
        suffix = _tri(w, strict=False)
        qpos = i * b + lax.broadcasted_iota(jnp.int32, (b, 1), 0)

        def cond(c):
            return jnp.logical_and(c[0] > 0, _any_alive(c[1], bounds))

        def step(c):
            hi, rests, accs = c
            start, rows = _window_rows(hi, w)
            limit = jnp.minimum(qpos, hi) - start
            new_rests, new_accs = [], []
            for hh in heads:
                _, _, log_not, a = _att_window(qs[hh], k_ref[rows, _head_cols(hh)], limit, rests[hh], suffix)
                new_accs.append(accs[hh] + jnp.dot(a.astype(BF16), v_ref[rows, _head_cols(hh)],
                                                   preferred_element_type=F32))
                new_rests.append(rests[hh] + jnp.sum(log_not, axis=1, keepdims=True))
            return start, tuple(new_rests), tuple(new_accs)

        init = ((i + 1) * b, tuple(jnp.zeros((b, 1), F32) for _ in heads),
                tuple(jnp.zeros((b, HEAD_DIM), F32) for _ in heads))
        _, _, accs = lax.while_loop(cond, step, init)
        for hh in heads:
            o_ref[:, _head_cols(hh)] = accs[hh].astype(o_ref.dtype)

    qs_, ks_, vs_ = _att_specs(qkv3, n_heads, hp)
    return pl.pallas_call(
        body, grid=(n_heads // hp, t // b), in_specs=[qs_, ks_, vs_],
        out_specs=pl.BlockSpec((b, hp * HEAD_DIM), lambda g, i: (i, g)),
        out_shape=_sds((t, n_heads * HEAD_DIM), BF16),
        scratch_shapes=[pltpu.SMEM((hp,), F32)], name=name, compiler_params=_cp(2))(qkv3, qkv3, qkv3)


def _attn_bwd(qkv3, do, n_heads, name):
    t = qkv3.shape[1]
    b = ATT_BLOCK
    w = min(ATT_WINDOW, t)
    nq = t // b
    hp = _heads_per_step(qkv3, n_heads)
    heads = range(hp)
    wd = hp * HEAD_DIM

    def body(q_ref, k_ref, v_ref, do_ref, dq_ref, dk_ref, dv_ref, kmax, dk_acc, dv_acc):
        i = pl.program_id(1)

        @pl.when(i == 0)
        def _():
            _key_bounds(k_ref, kmax, hp)
            dk_acc[...] = jnp.zeros_like(dk_acc)
            dv_acc[...] = jnp.zeros_like(dv_acc)

        qs = [q_ref[:, _head_cols(hh)] for hh in heads]
        douts = [do_ref[:, _head_cols(hh)] for hh in heads]
        bounds = [_score_bound(qs[hh], kmax[hh]) for hh in heads]
        zero_cols = tuple(jnp.zeros((b, 1), F32) for _ in heads)
        suffix = _tri(w, strict=False)
        strict_suffix = _tri(w, strict=True)
        hi0 = (i + 1) * b
        qpos = i * b + lax.broadcasted_iota(jnp.int32, (b, 1), 0)

        def cond(c):
            return jnp.logical_and(c[0] > 0, _any_alive(c[1], bounds))

        def window(hh, rows, limit, rest):
            k_w = k_ref[rows, _head_cols(hh)]
            z, e, log_not, a = _att_window(qs[hh], k_w, limit, rest, suffix)
            dla = a * lax.dot_general(douts[hh], v_ref[rows, _head_cols(hh)], _NT, preferred_element_type=F32)
            return k_w, z, e, log_not, a, dla

        def sweep1(c):
            hi, rests, totals = c
            start, rows = _window_rows(hi, w)
            limit = jnp.minimum(qpos, hi) - start
            new_rests, new_totals = [], []
            for hh in heads:
                _, _, _, log_not, a, dla = window(hh, rows, limit, rests[hh])
                dv_acc[rows, _head_cols(hh)] += lax.dot_general(a.astype(BF16), douts[hh], _TN,
                                                                preferred_element_type=F32)
                new_rests.append(rests[hh] + jnp.sum(log_not, axis=1, keepdims=True))
                new_totals.append(totals[hh] + jnp.sum(dla, axis=1, keepdims=True))
            return start, tuple(new_rests), tuple(new_totals)

        _, _, totals = lax.while_loop(cond, sweep1, (hi0, zero_cols, zero_cols))

        def sweep2(c):
            hi, rests, laters, dqs = c
            start, rows = _window_rows(hi, w)
            limit = jnp.minimum(qpos, hi) - start
            new_rests, new_laters, new_dqs = [], [], []
            for hh in heads:
                k_w, z, e, log_not, _, dla = window(hh, rows, limit, rests[hh])
                inside = _split_dot(dla, strict_suffix)
                dlog_not = totals[hh] - laters[hh] - inside
                inv = 1.0 / (1.0 + e)
                sig = jnp.where(z >= 0, inv, e * inv)
                dz = ((dla - sig * dlog_not) * ATT_SCALE).astype(BF16)
                new_dqs.append(dqs[hh] + jnp.dot(dz, k_w, preferred_element_type=F32))
                dk_acc[rows, _head_cols(hh)] += lax.dot_general(dz, qs[hh], _TN, preferred_element_type=F32)
                new_rests.append(rests[hh] + jnp.sum(log_not, axis=1, keepdims=True))
                new_laters.append(laters[hh] + jnp.sum(dla, axis=1, keepdims=True))
            return start, tuple(new_rests), tuple(new_laters), tuple(new_dqs)

        init = (hi0, zero_cols, zero_cols, tuple(jnp.zeros((b, HEAD_DIM), F32) for _ in heads))
        _, _, _, dqs = lax.while_loop(cond, sweep2, init)
        for hh in heads:
            dq_ref[:, _head_cols(hh)] = dqs[hh].astype(dq_ref.dtype)

        @pl.when(i == nq - 1)
        def _():
            dk_ref[...] = dk_acc[...].astype(dk_ref.dtype)
            dv_ref[...] = dv_acc[...].astype(dv_ref.dtype)

    qs_, ks_, vs_ = _att_specs(qkv3, n_heads, hp)
    blk = pl.BlockSpec((b, wd), lambda g, i: (i, g))
    slab = pl.BlockSpec((t, wd), lambda g, i: (0, g))
    d = n_heads * HEAD_DIM
    return pl.pallas_call(
        body, grid=(n_heads // hp, nq),
        in_specs=[qs_, ks_, vs_, pl.BlockSpec((None, b, wd), lambda g, i: (0, i, g))],
        out_specs=[blk, slab, slab],
        out_shape=[_sds((t, d), BF16)] * 3,
        scratch_shapes=[pltpu.SMEM((hp,), F32), pltpu.VMEM((t, wd), F32), pltpu.VMEM((t, wd), F32)],
        name=name, compiler_params=_cp(2))(qkv3, qkv3, qkv3, do)


def _conv_rows(main, halo, w_ref, b_ref):
    ext = jnp.concatenate([halo, main], axis=0)
    h1 = pltpu.roll(ext, 1, 0)[CONV_HALO:, :]
    h2 = pltpu.roll(ext, 2, 0)[CONV_HALO:, :]
    hc = b_ref[...] + w_ref[0:1, :] * h2
    hc = hc + w_ref[1:2, :] * h1
    hc = hc + w_ref[2:3, :] * main
    return hc, h1, h2


def _ffn_specs(t, fp, tm, half):
    per = tm // CONV_HALO
    main = lambda off: pl.BlockSpec((None, tm, fp), lambda g, i: (g + off, i, 0))
    before = lambda off: pl.BlockSpec((None, CONV_HALO, fp), lambda g, i: (g + off, jnp.maximum(i * per - 1, 0), 0))
    cw = lambda off: pl.BlockSpec((None, 3, fp), lambda g, i: (g + off, 0, 0))
    cb = lambda off: pl.BlockSpec((None, 1, fp), lambda g, i: (g + off, 0, 0))
    return [main(0), before(0), main(half), before(half), cw(0), cw(half), cb(0), cb(half)]


def _ffn_act(h, cw, cb, name):
    n, t, fp = h.shape
    half = n // 2
    tm = _tile(t, LN_ROW_TILE)

    def body(hg_ref, hgb_ref, hv_ref, hvb_ref, wg_ref, wv_ref, bg_ref, bv_ref, a_ref):
        first = pl.program_id(1) == 0
        gate, _, _ = _conv_rows(hg_ref[...], jnp.where(first, 0.0, hgb_ref[...]), wg_ref, bg_ref)
        val, _, _ = _conv_rows(hv_ref[...], jnp.where(first, 0.0, hvb_ref[...]), wv_ref, bv_ref)
        a_ref[...] = (gate * jax.nn.sigmoid(gate) * val).astype(a_ref.dtype)

    return pl.pallas_call(
        body, grid=(half, t // tm), in_specs=_ffn_specs(t, fp, tm, half),
        out_specs=pl.BlockSpec((None, tm, fp), lambda g, i: (g, i, 0)),
        out_shape=_sds((half, t, fp), BF16), name=name, compiler_params=_cp(2))(h, h, h, h, cw, cw, cb, cb)


def _ffn_act_bwd(h, da, cw, cb, name):
    n, t, fp = h.shape
    half = n // 2
    tm = _tile(t, LN_ROW_TILE)

    def body(hg_ref, hgb_ref, hv_ref, hvb_ref, wg_ref, wv_ref, bg_ref, bv_ref, da_ref, dhc_ref, dw_ref, db_ref):
        first = pl.program_id(1) == 0

        @pl.when(first)
        def _():
            dw_ref[...] = jnp.zeros_like(dw_ref)
            db_ref[...] = jnp.zeros_like(db_ref)

        hg, hv = hg_ref[...], hv_ref[...]
        gate, hg1, hg2 = _conv_rows(hg, jnp.where(first, 0.0, hgb_ref[...]), wg_ref, bg_ref)
        val, hv1, hv2 = _conv_rows(hv, jnp.where(first, 0.0, hvb_ref[...]), wv_ref, bv_ref)
        sig = jax.nn.sigmoid(gate)
        dact = da_ref[...]
        dgate = dact * val * (sig * (1.0 + gate * (1.0 - sig)))
        dval = dact * (gate * sig)
        dhc_ref[0] = dgate
        dhc_ref[1] = dval
        for s, (dd, shifted) in enumerate(((dgate, (hg2, hg1, hg)), (dval, (hv2, hv1, hv)))):
            db_ref[s] += jnp.sum(dd, axis=0, keepdims=True)
            for kk in range(3):
                dw_ref[s, kk:kk + 1, :] += jnp.sum(dd * shifted[kk], axis=0, keepdims=True)

    specs = _ffn_specs(t, fp, tm, half) + [pl.BlockSpec((None, tm, fp), lambda g, i: (g, i, 0))]
    return pl.pallas_call(
        body, grid=(half, t // tm), in_specs=specs,
        out_specs=[pl.BlockSpec((2, None, tm, fp), lambda g, i: (0, g, i, 0)),
                   pl.BlockSpec((2, None, 3, fp), lambda g, i: (0, g, 0, 0)),
                   pl.BlockSpec((2, None, 1, fp), lambda g, i: (0, g, 0, 0))],
        out_shape=[_sds((2, half, t, fp), F32), _sds((2, half, 3, fp), F32), _sds((2, half, 1, fp), F32)],
        name=name, compiler_params=_cp(2))(h, h, h, h, cw, cw, cb, cb, da)


def _conv_adjoint(dhc, cw, name):
    n, t, fp = dhc.shape
    tm = _tile(t, ROW_TILE)
    steps = t // tm
    per = tm // CONV_HALO

    def body(d_ref, after_ref, w_ref, o_ref):
        main = d_ref[...]
        after = jnp.where(pl.program_id(1) < steps - 1, after_ref[...], 0.0)
        ext = jnp.concatenate([main, after], axis=0)
        rows = ext.shape[0]
        d1 = pltpu.roll(ext, rows - 1, 0)[:tm, :]
        d2 = pltpu.roll(ext, rows - 2, 0)[:tm, :]
        o_ref[...] = (w_ref[2:3, :] * main + w_ref[1:2, :] * d1 + w_ref[0:1, :] * d2).astype(o_ref.dtype)

    main = pl.BlockSpec((None, tm, fp), lambda g, i: (g, i, 0))
    after = pl.BlockSpec((None, CONV_HALO, fp), lambda g, i: (g, jnp.minimum((i + 1) * per, t // CONV_HALO - 1), 0))
    return pl.pallas_call(
        body, grid=(n, steps), in_specs=[main, after, pl.BlockSpec((None, 3, fp), lambda g, i: (g, 0, 0))],
        out_specs=main, out_shape=_sds((n, t, fp), BF16), name=name, compiler_params=_cp(2))(dhc, dhc, cw)


def _place():
    x, y, c = lax.axis_index("x"), lax.axis_index("y"), lax.axis_index("c")
    chips = [(1 - x, y), (x, 1 - y), (1 - x, 1 - y)]
    return x, y, c, chips


def _run_sides(sides, name):
    n_in = [len(s.ins) for s in sides]
    n_out = [len(s.outs) for s in sides]
    n_sem = [len(s.sems) for s in sides]

    def body(*refs):
        ins, outs, sems = refs[:sum(n_in)], refs[sum(n_in):sum(n_in) + sum(n_out)], refs[sum(n_in) + sum(n_out):]
        oi = oo = os_ = 0
        for k, s in enumerate(sides):
            mine = (ins[oi:oi + n_in[k]], outs[oo:oo + n_out[k]], sems[os_:os_ + n_sem[k]])
            s.start(*mine)
            s.finish(*mine)
            oi, oo, os_ = oi + n_in[k], oo + n_out[k], os_ + n_sem[k]

    aliases, oi, oo = {}, 0, 0
    for k, s in enumerate(sides):
        aliases.update({oi + a: oo + b for a, b in s.alias.items()})
        oi, oo = oi + n_in[k], oo + n_out[k]
    return pl.pallas_call(
        body, in_specs=_any_specs(sum(n_in)), out_specs=_any_specs(sum(n_out)),
        out_shape=[o for s in sides for o in s.outs], input_output_aliases=aliases,
        scratch_shapes=[q for s in sides for q in s.sems], name=name)(*[a for s in sides for a in s.ins])


def _place_shard(kind, w, chip, name, cols=None, base=None):
    if kind == "pool":
        g, r, cdim = w.shape

        def body(chip_ref, w_ref, o_ref):
            del chip_ref
            o_ref[...] = w_ref[...].astype(BF16)

        return pl.pallas_call(
            body,
            grid_spec=pltpu.PrefetchScalarGridSpec(
                num_scalar_prefetch=1, grid=(1,),
                in_specs=[pl.BlockSpec((g, r, cdim), lambda i, chip_ref: (0, 0, 0))],
                out_specs=pl.BlockSpec((g, r, cdim), lambda i, chip_ref: (0, chip_ref[0], 0))),
            out_shape=_sds((g, 4 * r, cdim), BF16), name=name, compiler_params=_cp(1))(chip, w)

    r, cs = w.shape
    if kind == "lead":
        cols = cols or cs
        tr = _tile(r, ROW_TILE, 16)

        def body(chip_ref, w_ref, o_ref):
            del chip_ref
            if cols > cs:
                o_ref[:, pl.ds(cols - LANES, LANES)] = jnp.zeros((tr, LANES), BF16)
            o_ref[:, pl.ds(0, cs)] = w_ref[...].astype(BF16)

        return pl.pallas_call(
            body,
            grid_spec=pltpu.PrefetchScalarGridSpec(
                num_scalar_prefetch=1, grid=(r // tr,),
                in_specs=[pl.BlockSpec((tr, cs), lambda i, chip_ref: (i, 0))],
                out_specs=pl.BlockSpec((None, tr, cols), lambda i, chip_ref: (chip_ref[0], i, 0))),
            out_shape=_sds((4, r, cols), BF16), name=name, compiler_params=_cp(1))(chip, w)

    assert kind == "down"
    tr = r // 2 if (r // 2) % 16 == 0 else r
    per = r // tr

    def body(chip_ref, w_ref, base_ref, o_ref):
        del chip_ref, base_ref
        o_ref[...] = w_ref[...].astype(BF16)

    return pl.pallas_call(
        body,
        grid_spec=pltpu.PrefetchScalarGridSpec(
            num_scalar_prefetch=1, grid=(per,),
            in_specs=[pl.BlockSpec((tr, cs), lambda i, chip_ref: (i, 0)), pl.BlockSpec(memory_space=pl.ANY)],
            out_specs=pl.BlockSpec((None, tr, cs), lambda i, chip_ref: (chip_ref[0] // 2, (chip_ref[0] % 2) * per + i, 0))),
        out_shape=_sds(base.shape, BF16), input_output_aliases={2: 0},
        name=name, compiler_params=_cp(1))(chip, w, base)


def _gather_sides(items, bufs=None):
    n = len(items)
    kinds = [it[0] for it in items]
    shard_rows = [it[2] for it in items]
    bufs = [it[1] for it in items] if bufs is None else list(bufs)

    def half_of(outs, m, chip, half):
        k = 2 * chip[0] + chip[1]
        o, r = outs[m], shard_rows[m]
        if kinds[m] == "pool":
            gh = o.shape[0] // 2
            return o.at[pl.ds(half * gh, gh), pl.ds(k * r, r)]
        r2 = r // 2
        if kinds[m] == "down":
            return o.at[k // 2, pl.ds((k % 2) * r + half * r2, r2)]
        return o.at[k, pl.ds(half * r2, r2)]

    def remote(outs, sems, m, j, chip, half, to):
        ref = half_of(outs, m, chip, half)
        return pltpu.make_async_remote_copy(src_ref=ref, dst_ref=ref, send_sem=sems[0].at[m, j],
                                            recv_sem=sems[1].at[m, j], device_id=to, device_id_type=MESH)

    def ici_copies(outs, sems, sending):
        x, y, c, chips = _place()
        if sending:
            return [remote(outs, sems, m, j, (x, y), c, (*chip, c)) for m in range(n) for j, chip in enumerate(chips)]
        return [remote(outs, sems, m, j, chip, c, (x, y, c)) for m in range(n) for j, chip in enumerate(chips)]

    def d2d_copies(outs, sems, sending):
        x, y, c, chips = _place()
        if sending:
            return [remote(outs, sems, m, j, chip, c, (x, y, 1 - c)) for m in range(n) for j, chip in enumerate(chips)]
        return [remote(outs, sems, m, j, chip, 1 - c, (x, y, c)) for m in range(n) for j, chip in enumerate(chips)]

    def phase(copies):
        def start(ins, outs, sems):
            for cp in copies(outs, sems, True):
                cp.start()

        def finish(ins, outs, sems):
            for cp in copies(outs, sems, False):
                cp.wait_recv()
            for cp in copies(outs, sems, True):
                cp.wait_send()

        return start, finish

    ici, d2d = phase(ici_copies), phase(d2d_copies)

    def both_finish(ins, outs, sems):
        ici[1](ins, outs, sems[:2])
        d2d[0](ins, outs, sems[2:])
        d2d[1](ins, outs, sems[2:])

    pair = [pltpu.SemaphoreType.DMA((n, 3)), pltpu.SemaphoreType.DMA((n, 3))]
    shapes = [_sds(b.shape, b.dtype) for b in bufs]
    alias = {m: m for m in range(n)}

    def side(which):
        if which == "both":
            return _Side(bufs, shapes, alias, pair + pair, lambda i, o, s: ici[0](i, o, s[:2]), both_finish)
        start, finish = ici if which == "ici" else d2d
        return _Side(bufs, shapes, alias, pair, start, finish)

    return side


def _sibling_side(grads):
    n = len(grads)

    def copies(ins, outs, sems):
        x, y, c, _ = _place()
        res = []
        for m in range(n):
            r2 = ins[m].shape[1] // 2
            res.append(pltpu.make_async_remote_copy(
                src_ref=ins[m].at[:, pl.ds((1 - c) * r2, r2)], dst_ref=outs[m],
                send_sem=sems[0].at[m], recv_sem=sems[1].at[m], device_id=(x, y, 1 - c), device_id_type=MESH))
        return res

    def start(ins, outs, sems):
        for cp in copies(ins, outs, sems):
            cp.start()

    def finish(ins, outs, sems):
        for cp in copies(ins, outs, sems):
            cp.wait_recv()
        for cp in copies(ins, outs, sems):
            cp.wait_send()

    return _Side(list(grads), [_sds((4, g.shape[1] // 2, g.shape[2]), g.dtype) for g in grads], {},
                 [pltpu.SemaphoreType.DMA((n,)), pltpu.SemaphoreType.DMA((n,))], start, finish)


def _owner_chips_side(parts):
    n = len(parts)

    def copies(ins, outs, sems):
        _, _, c, chips = _place()
        return [pltpu.make_async_remote_copy(
            src_ref=ins[m].at[2 * chip[0] + chip[1]], dst_ref=outs[m].at[j], send_sem=sems[0].at[m, j],
            recv_sem=sems[1].at[m, j], device_id=(*chip, c), device_id_type=MESH)
            for m in range(n) for j, chip in enumerate(chips)]

    def start(ins, outs, sems):
        for cp in copies(ins, outs, sems):
            cp.start()

    def finish(ins, outs, sems):
        for cp in copies(ins, outs, sems):
            cp.wait_recv()
        for cp in copies(ins, outs, sems):
            cp.wait_send()

    return _Side(list(parts), [_sds((3,) + p.shape[1:], p.dtype) for p in parts], {},
                 [pltpu.SemaphoreType.DMA((n, 3)), pltpu.SemaphoreType.DMA((n, 3))], start, finish)


def _exchange_finished_halves(shards, name):
    n = len(shards)

    def body(*refs):
        out = refs[n:2 * n]
        send_sems, recv_sems = refs[2 * n:]
        x, y, c, _ = _place()
        copies = []
        for m in range(n):
            r2 = out[m].shape[0] // 2
            mine = out[m].at[pl.ds(c * r2, r2)]
            copies.append(pltpu.make_async_remote_copy(
                src_ref=mine, dst_ref=mine, send_sem=send_sems.at[m], recv_sem=recv_sems.at[m],
                device_id=(x, y, 1 - c), device_id_type=MESH))
        for cp in copies:
            cp.start()
        for m in range(n):
            r2 = out[m].shape[0] // 2
            theirs = out[m].at[pl.ds((1 - c) * r2, r2)]
            pltpu.make_async_remote_copy(
                src_ref=theirs, dst_ref=theirs, send_sem=send_sems.at[m], recv_sem=recv_sems.at[m],
                device_id=(x, y, 1 - c), device_id_type=MESH).wait_recv()
        for cp in copies:
            cp.wait_send()

    return pl.pallas_call(
        body, in_specs=_any_specs(n), out_specs=_any_specs(n), out_shape=[_sds(s.shape, s.dtype) for s in shards],
        input_output_aliases={m: m for m in range(n)},
        scratch_shapes=[pltpu.SemaphoreType.DMA((n,)), pltpu.SemaphoreType.DMA((n,))], name=name)(*shards)


def _all_reduce_small(v, name):
    rows = v.shape[0]

    def body(v_ref, out_ref, buf, send_sems, recv_sems, local_sem):
        x, y, c, chips = _place()
        me, sibling = (x, y, c), (x, y, 1 - c)

        def slot(px, py, pc):
            return buf.at[4 * px + 2 * py + pc]

        def copy(k, block, to, src=None):
            return pltpu.make_async_remote_copy(
                src_ref=slot(*block) if src is None else src, dst_ref=slot(*block),
                send_sem=send_sems.at[k], recv_sem=recv_sems.at[k], device_id=to, device_id_type=MESH)

        mine = pltpu.make_async_copy(v_ref, slot(*me), local_sem)
        mine.start()
        first = [copy(0, me, sibling, src=v_ref)]
        first += [copy(1 + j, me, (*chip, c), src=v_ref) for j, chip in enumerate(chips)]
        for cp in first:
            cp.start()
        passed = [copy(4 + j, (*chip, c), sibling) for j, chip in enumerate(chips)]
        for j, chip in enumerate(chips):
            copy(1 + j, (*chip, c), me).wait_recv()
            passed[j].start()
        copy(0, sibling, me).wait_recv()
        for j, chip in enumerate(chips):
            copy(4 + j, (*chip, 1 - c), me).wait_recv()
        for cp in first + passed:
            cp.wait_send()
        mine.wait()
        total = buf[0]
        for dev in range(1, 8):
            total = total + buf[dev]
        out_ref[...] = total

    vm = pl.BlockSpec(memory_space=pltpu.VMEM)
    return pl.pallas_call(
        body, in_specs=[vm], out_specs=vm, out_shape=_sds(v.shape, F32),
        scratch_shapes=[pltpu.VMEM((8, rows, LANES), F32), pltpu.SemaphoreType.DMA((7,)),
                        pltpu.SemaphoreType.DMA((7,)), pltpu.SemaphoreType.DMA],
        name=name, compiler_params=pltpu.CompilerParams(vmem_limit_bytes=VMEM_LIMIT))(v)


def _chip_partial(grad, from_sibling, core, name):
    _, r, cdim = grad.shape
    r2 = r // 2
    tr = _tile(r2, SUM_ROW_TILE)
    per = r2 // tr

    def body(core_ref, g_ref, s_ref, o_ref, ob_ref):
        del core_ref
        total = g_ref[...] + s_ref[...]
        o_ref[...] = total
        ob_ref[...] = total.astype(BF16)

    blk = pl.BlockSpec((None, tr, cdim), lambda k, i, core_ref: (k, i, 0))
    mine = pl.BlockSpec((None, tr, cdim), lambda k, i, core_ref: (k, core_ref[0] * per + i, 0))
    return pl.pallas_call(
        body,
        grid_spec=pltpu.PrefetchScalarGridSpec(num_scalar_prefetch=1, grid=(4, per), in_specs=[mine, blk],
                                               out_specs=[blk, blk]),
        out_shape=[_sds((4, r2, cdim), F32), _sds((4, r2, cdim), BF16)],
        name=name, compiler_params=_cp(2))(core, grad, from_sibling)


def _owner_sum(partial, from_chips, place, name):
    _, r2, cdim = partial.shape
    tr = _tile(r2, SUM_ROW_TILE)
    per = r2 // tr

    def body(place_ref, p_ref, f_ref, o_ref):
        del place_ref
        total = p_ref[...]
        for j in range(3):
            total = total + f_ref[j].astype(F32)
        o_ref[...] = total

    return pl.pallas_call(
        body,
        grid_spec=pltpu.PrefetchScalarGridSpec(
            num_scalar_prefetch=1, grid=(per,),
            in_specs=[pl.BlockSpec((None, tr, cdim), lambda i, place_ref: (place_ref[0], i, 0)),
                      pl.BlockSpec((3, tr, cdim), lambda i, place_ref: (0, i, 0))],
            out_specs=pl.BlockSpec((tr, cdim), lambda i, place_ref: (place_ref[1] * per + i, 0))),
        out_shape=_sds((2 * r2, cdim), F32), name=name, compiler_params=_cp(1))(place, partial, from_chips)


def _adamw(g, w, m, v, layer, prev, name):
    _, r, cdim = w.shape
    tr = _tile(r, OPT_ROW_TILE)
    c1 = 1.0 / (1.0 - ADAM_B1 ** ADAM_STEP)
    c2 = 1.0 / (1.0 - ADAM_B2 ** ADAM_STEP)
    n_prev = 0 if prev is None else 4

    def body(g_ref, w_ref, m_ref, v_ref, *rest):
        go_ref, d_ref, mo_ref, vo_ref = rest[n_prev:]
        grad = g_ref[:, pl.ds(0, cdim)]
        m_new = ADAM_B1 * m_ref[...] + (1.0 - ADAM_B1) * grad
        v_new = ADAM_B2 * v_ref[...] + (1.0 - ADAM_B2) * (grad * grad)
        go_ref[...] = grad
        mo_ref[...] = m_new
        vo_ref[...] = v_new
        d_ref[...] = -ADAM_LR * ((m_new * c1) / (jnp.sqrt(v_new * c2) + ADAM_EPS) + ADAM_WD * w_ref[...])

    blk = pl.BlockSpec((None, tr, cdim), lambda i: (layer, i, 0))
    gblk = pl.BlockSpec((tr, g.shape[1]), lambda i: (i, 0))
    return pl.pallas_call(
        body, grid=(r // tr,), in_specs=[gblk, blk, blk, blk] + _any_specs(n_prev), out_specs=[blk] * 4,
        out_shape=[_sds(w.shape, F32)] * 4, input_output_aliases={4 + k: k for k in range(n_prev)},
        name=name, compiler_params=_cp(1))(g, w, m, v, *(prev or ()))


def _pack_rows(vectors):
    flat = [v.reshape(-1) for v in vectors]
    sizes = [f.shape[0] for f in flat]
    total = sum(sizes)
    padded = _round_up(total, 8 * LANES)
    buf = jnp.concatenate(flat + [jnp.zeros((padded - total,), F32)])
    return buf.reshape(padded // LANES, LANES), sizes


def _unpack_rows(buf, sizes, shapes):
    flat = buf.reshape(-1)
    out, off = [], 0
    for n, shp in zip(sizes, shapes):
        out.append(flat[off:off + n].reshape(shp))
        off += n
    return out


def kernel(x, pool_w, pool_scale, attn_w_qkv, attn_w_o, ffn_w_up, ffn_conv_w, ffn_conv_b, ffn_w_down, ln_mix_g, ln_mix_b, ln_ffn_g, ln_ffn_b, loss_target, m_pool_w, m_pool_scale, m_attn_w_qkv, m_attn_w_o, m_ffn_w_up, m_ffn_conv_w, m_ffn_conv_b, m_ffn_w_down, m_ln_mix_g, m_ln_mix_b, m_ln_ffn_g, m_ln_ffn_b, v_pool_w, v_pool_scale, v_attn_w_qkv, v_attn_w_o, v_ffn_w_up, v_ffn_conv_w, v_ffn_conv_b, v_ffn_w_down, v_ln_mix_g, v_ln_mix_b, v_ln_ffn_g, v_ln_ffn_b):
    t, d = x.shape[1], x.shape[2]
    n_heads = d // HEAD_DIM
    n_groups = pool_w.shape[1]
    fs = ffn_w_up.shape[2]
    fp = _round_up(fs, LANES)
    rd = ffn_w_down.shape[1]
    assert 2 * rd == fs
    xi, yi, ci = lax.axis_index("x"), lax.axis_index("y"), lax.axis_index("c")
    chip = (2 * xi + yi).astype(jnp.int32)
    chip_arr, core_arr = chip.reshape(1), ci.astype(jnp.int32).reshape(1)
    place_arr = jnp.concatenate([chip_arr, core_arr])

    x2 = x.reshape(t, d)
    target = loss_target.reshape(t, d)
    pad_cols = lambda a: jnp.pad(a, [(0, 0)] * (a.ndim - 1) + [(0, fp - fs)])

    gather_items = []
    for i in range(DEPTH):
        j = i // 2
        items = []
        if i % 2 == 0:
            items.append(("pool", _place_shard("pool", pool_w[j], chip_arr, name="place_pool"), pool_w.shape[2]))
        else:
            items.append(("lead", _place_shard("lead", attn_w_qkv[j], chip_arr, name="place_qkv"), d))
            items.append(("lead", _place_shard("lead", attn_w_o[j], chip_arr, name="place_wo"), attn_w_o.shape[1]))
        items.append(("lead", _place_shard("lead", ffn_w_up[i], chip_arr, name="place_up", cols=fp), d))
        items.append(("down", _place_shard("down", ffn_w_down[i], chip_arr, name="place_down",
                                           base=jnp.zeros((2, fp, d), BF16)), rd))
        gather_items.append(items)
    weights = [None] * DEPTH
    weights[0] = _run_sides([_gather_sides(gather_items[0])("both")], name="gather_layer0")

    conv_b_all = pad_cols(ffn_conv_b.reshape(DEPTH, 4, 1, fs))
    cw_local = pad_cols(ffn_conv_w)
    slot = (jnp.arange(4, dtype=jnp.int32) == chip).astype(F32) * (1.0 - ci.astype(F32))
    cw_placed = slot[None, :, None, None] * cw_local[:, None]
    cw_buf, cw_sizes = _pack_rows([cw_placed])
    conv_w_all = _unpack_rows(_all_reduce_small(cw_buf, name="gather_conv_w"), cw_sizes, [cw_placed.shape])[0]

    gam = lambda a, i: a[i].reshape(1, d)

    saved = []
    cur, cur_b = x2, x2.astype(BF16)
    for i in range(DEPTH):
        j = i // 2
        w = weights[i]
        s = {"x_in": cur, "x_in_b": cur_b}
        if i % 2 == 0:
            w_pool, w_up, w_down = w
            s["scale"] = pool_scale[j].reshape(1, d)
            r1, x1, x1b = _pool_fwd(cur, w_pool, s["scale"], gam(ln_mix_g, i), gam(ln_mix_b, i), name="pool_fwd")
        else:
            w_qkv, w_o, w_up, w_down = w
            w_o3 = w_o.reshape(1, d, d)
            qkv = _mm_cols(cur_b, w_qkv, BF16, name="qkv_proj")
            o = _attn_fwd(qkv, n_heads, name="attn_fwd")
            s["qkv"], s["o"], s["w_o3"] = qkv, o, w_o3
            r1, x1, x1b = _mm_res_ln(o.reshape(1, t, d), w_o3, cur, gam(ln_mix_g, i), gam(ln_mix_b, i),
                                     name="attn_out_ln")
        if i + 1 < DEPTH:
            nxt = gather_items[i + 1]
            h, landed = _mm_cols(x1b, w_up, F32, name="ffn_up", side=_gather_sides(nxt)("ici"))
            a = _ffn_act(h, conv_w_all[i], conv_b_all[i], name="ffn_act")
            (r2, x2n, x2b), gathered = _mm_res_ln(a, w_down, x1, gam(ln_ffn_g, i), gam(ln_ffn_b, i),
                                                  name="ffn_down_ln", side=_gather_sides(nxt, landed)("d2d"))
            weights[i + 1] = list(gathered)
        else:
            h = _mm_cols(x1b, w_up, F32, name="ffn_up")
            a = _ffn_act(h, conv_w_all[i], conv_b_all[i], name="ffn_act")
            r2, x2n, x2b = _mm_res_ln(a, w_down, x1, gam(ln_ffn_g, i), gam(ln_ffn_b, i), name="ffn_down_ln")
        s.update(r1=r1, x1b=x1b, h=h, a=a, r2=r2)
        saved.append(s)
        cur, cur_b = x2n, x2b

    loss_row, dcur = _loss_and_grad(cur, target, name="loss")
    loss = lax.psum(loss_row[0, 0], ("x", "y", "c"))

    big_grads = [None] * DEPTH
    reduced = [None] * DEPTH
    small = {}

    def finish_reduce(parts, from_chips, layer):
        halves = [_owner_sum(p[0], fc, place_arr, name="reduce_owner_sum") for p, fc in zip(parts, from_chips)]
        return _exchange_finished_halves(halves, name="reduce_halves_pool" if layer % 2 == 0 else "reduce_halves_attn")

    for i in reversed(range(DEPTH)):
        j = i // 2
        s, w = saved[i], weights[i]
        w_up, w_down = w[-2], w[-1]
        dr2, dr2b, small["ln_ffn_g", i], small["ln_ffn_b", i] = _ln_bwd(dcur, s["r2"], gam(ln_ffn_g, i), name="ln_bwd")
        pending = big_grads[i + 1] if i + 1 < DEPTH else None
        if pending is not None:
            da, from_sib = _mm_cols(dr2b, w_down, F32, name="ffn_down_bwd_act", transposed_b=True,
                                    side=_sibling_side(pending))
            parts = [_chip_partial(g, fs_, core_arr, name="reduce_chip_partial") for g, fs_ in zip(pending, from_sib)]
        else:
            da = _mm_cols(dr2b, w_down, F32, name="ffn_down_bwd_act", transposed_b=True)
        dr2b3 = dr2b.reshape(1, t, d)
        nmb = 2
        d_down = _mm_tn(s["a"], dr2b3, (2, fp, d), fp // nmb, d, 2,
                        (lambda u: u, nmb, lambda u, mb: mb), (lambda u: 0, lambda u: 0),
                        (lambda u: u, lambda u, mb: mb, lambda u: 0), name="ffn_down_bwd_w")
        dhc, dcw, dcb = _ffn_act_bwd(s["h"], da, conv_w_all[i], conv_b_all[i], name="ffn_act_bwd")
        small["conv_w", i], small["conv_b", i] = dcw, dcb
        dh = _conv_adjoint(dhc.reshape(4, t, fp), conv_w_all[i], name="ffn_conv_adjoint")
        if pending is not None:
            dx1, from_chips = _mm_nt_acc(dh, w_up, dr2, fp, name="ffn_up_bwd_act",
                                         side=_owner_chips_side([p[1] for p in parts]))
            reduced[i + 1] = finish_reduce(parts, from_chips, i + 1)
        else:
            dx1 = _mm_nt_acc(dh, w_up, dr2, fp, name="ffn_up_bwd_act")
        d_up = _mm_tn(s["x1b"].reshape(1, t, d), dh, (4, d, fp), d // 2, fp, 4,
                      (lambda u: 0, 2, lambda u, mb: mb), (lambda u: u, lambda u: 0),
                      (lambda u: u, lambda u, mb: mb, lambda u: 0), name="ffn_up_bwd_w")
        dr1, dr1b, small["ln_mix_g", i], small["ln_mix_b", i] = _ln_bwd(dx1, s["r1"], gam(ln_mix_g, i), name="ln_bwd")
        d_down4 = d_down[:, :fs].reshape(4, rd, d)
        if i % 2 == 0:
            dp, d_pool, small["pool_scale", j] = _pool_bwd(s["x_in"], dr1, w[0], s["scale"], name="pool_bwd")
            dcur = _pool_adjoint(dp, dr1, n_groups, name="pool_adjoint")
            cg = d // n_groups
            d_pool4 = d_pool.reshape(n_groups, 4, cg // 4, cg).transpose(1, 0, 2, 3).reshape(4, n_groups * (cg // 4), cg)
            big_grads[i] = [d_pool4, d_up, d_down4]
        else:
            w_qkv = w[0]
            do = _mm_cols(dr1b, s["w_o3"], BF16, name="attn_out_bwd_act", transposed_b=True)
            d_wo = _mm_tn(s["o"].reshape(1, t, d), dr1b.reshape(1, t, d), (1, d, d), d // 2, d, 1,
                          (lambda u: 0, 2, lambda u, mb: mb), (lambda u: 0, lambda u: 0),
                          (lambda u: 0, lambda u, mb: mb, lambda u: 0), name="attn_out_bwd_w")
            dq, dk, dv = _attn_bwd(s["qkv"], do, n_heads, name="attn_bwd")
            dqkv = jnp.stack([dq, dk, dv])
            cq = w_qkv.shape[2]
            kb = cq // 3
            na, nbk = d // kb, cq // kb
            dcur = _mm_nt_acc(dqkv, w_qkv, dr1, kb, name="qkv_bwd_act")
            d_qkv = _mm_tn(s["x_in_b"].reshape(1, t, d), dqkv, (4, d, cq), d // 2, kb, 3 * na,
                           (lambda u: 0, 2, lambda u, mb: mb), (lambda u: u // na, lambda u: u % na),
                           (lambda u: u // nbk, lambda u, mb: mb, lambda u: u % nbk), name="qkv_bwd_w")
            big_grads[i] = [d_qkv, d_wo.reshape(4, d // 4, d), d_up, d_down4]
    grad_x = dcur.reshape(1, t, d)

    from_sib = _run_sides([_sibling_side(big_grads[0])], name="reduce_layer0_sibling")
    parts = [_chip_partial(g, fs_, core_arr, name="reduce_chip_partial") for g, fs_ in zip(big_grads[0], from_sib)]
    from_chips = _run_sides([_owner_chips_side([p[1] for p in parts])], name="reduce_layer0_chips")
    reduced[0] = finish_reduce(parts, from_chips, 0)

    names = [("pool_scale", j) for j in range(2)]
    for nm in ("ln_mix_g", "ln_mix_b", "ln_ffn_g", "ln_ffn_b", "conv_b", "conv_w"):
        names += [(nm, i) for i in range(DEPTH)]
    vecs = [small[k] for k in names]
    sbuf, ssizes = _pack_rows(vecs)
    summed = dict(zip(names, _unpack_rows(_all_reduce_small(sbuf, name="reduce_small"), ssizes, [v.shape for v in vecs])))

    def stack_layers(nm, count):
        return jnp.stack([summed[nm, i] for i in range(count)])

    g_small = {
        "pool_scale": stack_layers("pool_scale", 2).reshape(2, d),
        "ln_mix_g": stack_layers("ln_mix_g", DEPTH).reshape(DEPTH, d),
        "ln_mix_b": stack_layers("ln_mix_b", DEPTH).reshape(DEPTH, d),
        "ln_ffn_g": stack_layers("ln_ffn_g", DEPTH).reshape(DEPTH, d),
        "ln_ffn_b": stack_layers("ln_ffn_b", DEPTH).reshape(DEPTH, d),
        "conv_b": stack_layers("conv_b", DEPTH).reshape(DEPTH, 4, fp)[:, :, :fs].reshape(DEPTH, 4 * fs),
        "conv_w": lax.dynamic_index_in_dim(stack_layers("conv_w", DEPTH).reshape(DEPTH, 4, 3, fp), chip, axis=1,
                                           keepdims=False)[:, :, :fs],
    }
    w_small = {"pool_scale": (pool_scale, m_pool_scale, v_pool_scale), "ln_mix_g": (ln_mix_g, m_ln_mix_g, v_ln_mix_g),
               "ln_mix_b": (ln_mix_b, m_ln_mix_b, v_ln_mix_b), "ln_ffn_g": (ln_ffn_g, m_ln_ffn_g, v_ln_ffn_g),
               "ln_ffn_b": (ln_ffn_b, m_ln_ffn_b, v_ln_ffn_b), "conv_b": (ffn_conv_b, m_ffn_conv_b, v_ffn_conv_b),
               "conv_w": (ffn_conv_w, m_ffn_conv_w, v_ffn_conv_w)}
    order = list(g_small)
    packs = [_pack_rows([g_small[k] for k in order])[0]]
    for idx in range(3):
        packs.append(_pack_rows([w_small[k][idx] for k in order])[0])
    small_sizes = _pack_rows([g_small[k] for k in order])[1]
    small_out = _adamw(packs[0], packs[1][None], packs[2][None], packs[3][None], 0, None, name="adamw_small")
    shapes = [g_small[k].shape for k in order]
    small_res = {k: [] for k in order}
    for arr in small_out:
        for k, val in zip(order, _unpack_rows(arr[0], small_sizes, shapes)):
            small_res[k].append(val)

    def opt_layers(per_layer_grads, w_all, m_all, v_all, name):
        n_layers, r = w_all.shape[0], per_layer_grads[0].shape[0]
        flat = [a.reshape(n_layers, r, -1) for a in (w_all, m_all, v_all)]
        res = None
        for li, g in enumerate(per_layer_grads):
            res = _adamw(g, *flat, li, res, name=name)
        return [o.reshape(w_all.shape) for o in res]

    cg = d // n_groups
    pool_g = [reduced[i][0].reshape(n_groups, cg // 4, cg).reshape(n_groups * (cg // 4), cg) for i in (0, 2)]
    big = {
        "pool_w": opt_layers(pool_g, pool_w, m_pool_w, v_pool_w, "adamw_pool"),
        "attn_w_qkv": opt_layers([reduced[i][0] for i in (1, 3)], attn_w_qkv, m_attn_w_qkv, v_attn_w_qkv, "adamw_qkv"),
        "attn_w_o": opt_layers([reduced[i][1] for i in (1, 3)], attn_w_o, m_attn_w_o, v_attn_w_o, "adamw_wo"),
        "ffn_w_up": opt_layers([reduced[i][-2] for i in range(DEPTH)], ffn_w_up, m_ffn_w_up, v_ffn_w_up, "adamw_up"),
        "ffn_w_down": opt_layers([reduced[i][-1] for i in range(DEPTH)], ffn_w_down, m_ffn_w_down, v_ffn_w_down,
                                 "adamw_down"),
    }

    def leaf(k, name):
        if name in big:
            return big[name][k]
        key = {"ffn_conv_w": "conv_w", "ffn_conv_b": "conv_b"}.get(name, name)
        return small_res[key][k]

    weight_names = ["pool_w", "pool_scale", "attn_w_qkv", "attn_w_o", "ffn_w_up", "ffn_conv_w", "ffn_conv_b",
                    "ffn_w_down", "ln_mix_g", "ln_mix_b", "ln_ffn_g", "ln_ffn_b"]
    outs = [loss, grad_x]
    for k in range(4):
        outs += [leaf(k, nm) for nm in weight_names]
    return tuple(outs)
```

```python
import collections

import jax
import jax.numpy as jnp
from jax import lax
from jax.experimental import pallas as pl
from jax.experimental.pallas import tpu as pltpu

F32, BF16 = jnp.float32, jnp.bfloat16
MESH = pl.DeviceIdType.MESH

LANES = 128
HEAD_DIM = 128
ATT_BLOCK = 128
ATT_WINDOW = 3 * ATT_BLOCK
POOL_WINDOWS = (2, 4, 8, 16)
POOL_HALO = 16
CONV_HALO = 8
LN_EPS = 1e-5
DEPTH = 4
ALPHA = (2.0 * DEPTH) ** 0.25
ATT_SCALE = HEAD_DIM ** -0.5
EXP_ZERO = 115.0
MASKED = 1e30
ADAM_LR, ADAM_B1, ADAM_B2, ADAM_EPS, ADAM_WD, ADAM_STEP = 0.001, 0.9, 0.999, 1e-08, 0.01, 10

VMEM_LIMIT = 56 << 20
ROW_TILE = 512
LN_ROW_TILE = 256
OPT_ROW_TILE = 128
SUM_ROW_TILE = 512


def _cp(n_axes):
    return pltpu.CompilerParams(dimension_semantics=("arbitrary",) * n_axes, vmem_limit_bytes=VMEM_LIMIT)


def _sds(shape, dtype):
    return jax.ShapeDtypeStruct(tuple(shape), dtype)


def _round_up(n, m):
    return (n + m - 1) // m * m


def _tile(n, cap, mult=8):
    if n <= cap:
        return n
    best = None
    for d in range(mult, cap + 1, mult):
        if n % d == 0:
            best = d
    assert best is not None, (n, cap)
    return best


_NT = (((1,), (1,)), ((), ()))
_TN = (((0,), (0,)), ((), ()))

_Side = collections.namedtuple("_Side", "ins outs alias sems start finish")


def _any_specs(n):
    return [pl.BlockSpec(memory_space=pl.ANY)] * n


def _call(body, first, last, side, *, grid, in_specs, out_specs, out_shape, scratch_shapes, name, args):
    n_axes = len(grid)
    if side is None:
        res = pl.pallas_call(body, grid=grid, in_specs=in_specs, out_specs=out_specs, out_shape=out_shape,
                             scratch_shapes=scratch_shapes, name=name, compiler_params=_cp(n_axes))(*args)
        return res, ()
    n_in, n_out, n_scr = len(in_specs), len(out_shape), len(scratch_shapes)
    s_in, s_out = len(side.ins), len(side.outs)

    def carried(*refs):
        ins, refs = refs[:n_in], refs[n_in:]
        side_ins, refs = refs[:s_in], refs[s_in:]
        outs, refs = refs[:n_out], refs[n_out:]
        side_outs, refs = refs[:s_out], refs[s_out:]
        scratch, side_sems = refs[:n_scr], refs[n_scr:]

        @pl.when(first())
        def _():
            side.start(side_ins, side_outs, side_sems)

        body(*ins, *outs, *scratch)

        @pl.when(last())
        def _():
            side.finish(side_ins, side_outs, side_sems)

    res = pl.pallas_call(
        carried, grid=grid, in_specs=list(in_specs) + _any_specs(s_in), out_specs=list(out_specs) + _any_specs(s_out),
        out_shape=list(out_shape) + list(side.outs), scratch_shapes=list(scratch_shapes) + list(side.sems),
        input_output_aliases={n_in + a: n_out + b for a, b in side.alias.items()},
        name=name, compiler_params=_cp(n_axes))(*args, *side.ins)
    return res[:n_out], res[n_out:]


def _mm_cols(a, b, out_dtype, name, transposed_b=False, side=None):
    t, k = a.shape
    g = b.shape[0]
    nb = b.shape[1] if transposed_b else b.shape[2]
    tm = _tile(t, ROW_TILE)
    steps = t // tm

    def body(a_ref, b_ref, o_ref):
        if transposed_b:
            acc = lax.dot_general(a_ref[...], b_ref[...], _NT, preferred_element_type=F32)
        else:
            acc = jnp.dot(a_ref[...], b_ref[...], preferred_element_type=F32)
        o_ref[...] = acc.astype(o_ref.dtype)

    first = lambda: jnp.logical_and(pl.program_id(0) == 0, pl.program_id(1) == 0)
    last = lambda: jnp.logical_and(pl.program_id(0) == g - 1, pl.program_id(1) == steps - 1)
    (out,), side_out = _call(
        body, first, last, side, grid=(g, steps),
        in_specs=[pl.BlockSpec((tm, k), lambda gi, i: (i, 0)),
                  pl.BlockSpec((None,) + b.shape[1:], lambda gi, i: (gi, 0, 0))],
        out_specs=[pl.BlockSpec((None, tm, nb), lambda gi, i: (gi, i, 0))],
        out_shape=[_sds((g, t, nb), out_dtype)], scratch_shapes=[], name=name, args=(a, b))
    return out if side is None else (out, side_out)


def _mm_nt_acc(a3, b3, res, kb, name, side=None, b_is_kn=False):
    ga, t, ka = a3.shape
    gb, n, kbb = (b3.shape[0], b3.shape[2], b3.shape[1]) if b_is_kn else b3.shape
    na, nbk = ka // kb, kbb // kb
    groups = ga * na
    assert groups == gb * nbk
    tm = _tile(t, ROW_TILE)
    steps = t // tm

    def body(a_ref, b_ref, res_ref, o_ref, acc):
        u = pl.program_id(1)

        @pl.when(u == 0)
        def _():
            acc[...] = ALPHA * res_ref[...]

        if b_is_kn:
            acc[...] += jnp.dot(a_ref[...], b_ref[...], preferred_element_type=F32)
        else:
            acc[...] += lax.dot_general(a_ref[...], b_ref[...], _NT, preferred_element_type=F32)

        @pl.when(u == groups - 1)
        def _():
            o_ref[...] = acc[...]

    first = lambda: jnp.logical_and(pl.program_id(0) == 0, pl.program_id(1) == 0)
    last = lambda: jnp.logical_and(pl.program_id(0) == steps - 1, pl.program_id(1) == groups - 1)
    if b_is_kn:
        b_spec = pl.BlockSpec((None, kb, n), lambda i, u: (u // nbk, u % nbk, 0))
    else:
        b_spec = pl.BlockSpec((None, n, kb), lambda i, u: (u // nbk, 0, u % nbk))
    (out,), side_out = _call(
        body, first, last, side, grid=(steps, groups),
        in_specs=[pl.BlockSpec((None, tm, kb), lambda i, u: (u // na, i, u % na)), b_spec,
                  pl.BlockSpec((tm, n), lambda i, u: (i, 0))],
        out_specs=[pl.BlockSpec((tm, n), lambda i, u: (i, 0))],
        out_shape=[_sds((t, n), F32)], scratch_shapes=[pltpu.VMEM((tm, n), F32)], name=name, args=(a3, b3, res))
    return out if side is None else (out, side_out)


def _mm_tn(x3, dy3, out_shape, bm, bn, groups, x_idx, dy_idx, out_idx, name):
    t = x3.shape[1]
    tm = _tile(t, 2 * ROW_TILE)

    def body(x_ref, dy_ref, o_ref):
        @pl.when(pl.program_id(2) == 0)
        def _():
            o_ref[...] = jnp.zeros_like(o_ref)

        o_ref[...] += lax.dot_general(x_ref[...], dy_ref[...], _TN, preferred_element_type=F32)

    return pl.pallas_call(
        body, grid=(groups, x_idx[1], t // tm),
        in_specs=[pl.BlockSpec((None, tm, bm), lambda u, mb, i: (x_idx[0](u), i, x_idx[2](u, mb))),
                  pl.BlockSpec((None, tm, bn), lambda u, mb, i: (dy_idx[0](u), i, dy_idx[1](u)))],
        out_specs=pl.BlockSpec((None, bm, bn), lambda u, mb, i: (out_idx[0](u), out_idx[1](u, mb), out_idx[2](u))),
        out_shape=_sds(out_shape, F32), name=name, compiler_params=_cp(3))(x3, dy3)


def _layer_norm_rows(r, gamma, beta):
    mu = jnp.mean(r, axis=-1, keepdims=True)
    xc = r - mu
    var = jnp.mean(xc * xc, axis=-1, keepdims=True)
    return xc * lax.rsqrt(var + LN_EPS) * gamma + beta


def _mm_res_ln(a3, w3, res, gamma, beta, name, side=None):
    g, t, kb = a3.shape
    d = w3.shape[2]
    tm = _tile(t, LN_ROW_TILE)
    steps = t // tm

    def body(a_ref, w_hbm, res_ref, g_ref, b_ref, r_ref, o_ref, ob_ref, w_vmem, sem):
        @pl.when(pl.program_id(0) == 0)
        def _():
            cp = pltpu.make_async_copy(w_hbm, w_vmem, sem)
            cp.start()
            cp.wait()

        acc = ALPHA * res_ref[...]
        for gi in range(g):
            acc = acc + jnp.dot(a_ref[gi], w_vmem[gi], preferred_element_type=F32)
        r_ref[...] = acc
        out = _layer_norm_rows(acc, g_ref[...], b_ref[...])
        o_ref[...] = out
        ob_ref[...] = out.astype(BF16)

    row = pl.BlockSpec((tm, d), lambda i: (i, 0))
    vec = pl.BlockSpec((1, d), lambda i: (0, 0))
    outs, side_out = _call(
        body, lambda: pl.program_id(0) == 0, lambda: pl.program_id(0) == steps - 1, side, grid=(steps,),
        in_specs=[pl.BlockSpec((g, tm, kb), lambda i: (0, i, 0)), pl.BlockSpec(memory_space=pl.ANY), row, vec, vec],
        out_specs=[row, row, row],
        out_shape=[_sds((t, d), F32), _sds((t, d), F32), _sds((t, d), BF16)],
        scratch_shapes=[pltpu.VMEM(w3.shape, w3.dtype), pltpu.SemaphoreType.DMA],
        name=name, args=(a3, w3, res, gamma, beta))
    return outs if side is None else (outs, side_out)


def _ln_bwd(dout, r, gamma, name):
    t, d = r.shape
    tm = _tile(t, ROW_TILE)

    def body(do_ref, r_ref, g_ref, dr_ref, drb_ref, dg_ref, db_ref):
        @pl.when(pl.program_id(0) == 0)
        def _():
            dg_ref[...] = jnp.zeros_like(dg_ref)
            db_ref[...] = jnp.zeros_like(db_ref)

        rr = r_ref[...]
        do = do_ref[...]
        mu = jnp.mean(rr, axis=-1, keepdims=True)
        xc = rr - mu
        rstd = lax.rsqrt(jnp.mean(xc * xc, axis=-1, keepdims=True) + LN_EPS)
        xhat = xc * rstd
        dxh = do * g_ref[...]
        m1 = jnp.mean(dxh, axis=-1, keepdims=True)
        m2 = jnp.mean(dxh * xhat, axis=-1, keepdims=True)
        dr = rstd * (dxh - m1 - xhat * m2)
        dr_ref[...] = dr
        drb_ref[...] = dr.astype(BF16)
        dg_ref[...] += jnp.sum(do * xhat, axis=0, keepdims=True)
        db_ref[...] += jnp.sum(do, axis=0, keepdims=True)

    row = pl.BlockSpec((tm, d), lambda i: (i, 0))
    vec = pl.BlockSpec((1, d), lambda i: (0, 0))
    return pl.pallas_call(
        body, grid=(t // tm,), in_specs=[row, row, vec], out_specs=[row, row, vec, vec],
        out_shape=[_sds((t, d), F32), _sds((t, d), BF16), _sds((1, d), F32), _sds((1, d), F32)],
        name=name, compiler_params=_cp(1))(dout, r, gamma)


def _loss_and_grad(y, target, name):
    t, d = y.shape
    tm = _tile(t, ROW_TILE)
    steps = t // tm

    def body(y_ref, t_ref, loss_ref, dy_ref, acc):
        i = pl.program_id(0)

        @pl.when(i == 0)
        def _():
            acc[...] = jnp.zeros_like(acc)

        diff = y_ref[...] - t_ref[...]
        dy_ref[...] = diff * (1.0 / d)
        acc[...] += jnp.sum(diff * diff, axis=0, keepdims=True)

        @pl.when(i == steps - 1)
        def _():
            total = jnp.sum(acc[...], axis=1, keepdims=True) * (0.5 / d)
            loss_ref[...] = jnp.broadcast_to(total, loss_ref.shape)

    row = pl.BlockSpec((tm, d), lambda i: (i, 0))
    return pl.pallas_call(
        body, grid=(steps,), in_specs=[row, row],
        out_specs=[pl.BlockSpec((1, LANES), lambda i: (0, 0)), row],
        out_shape=[_sds((1, LANES), F32), _sds((t, d), F32)],
        scratch_shapes=[pltpu.VMEM((1, d), F32)], name=name, compiler_params=_cp(1))(y, target)


def _window_sums(ext, window, forward):
    n = ext.shape[0]
    s, span = ext, 1
    while span < window:
        s = s + pltpu.roll(s, (n - span) if forward else span, 0)
        span *= 2
    return s


def _pooled_group(main, halo, gi, row0):
    window = POOL_WINDOWS[gi]
    ext = jnp.concatenate([halo, main], axis=0)
    sums = _window_sums(ext, window, forward=False)[POOL_HALO:, :]
    pos = row0 + lax.broadcasted_iota(jnp.int32, (main.shape[0], 1), 0)
    cnt = jnp.minimum(pos + 1, window).astype(F32)
    return sums / cnt - main


def _pool_specs(t, d, tm):
    per = tm // POOL_HALO
    main = pl.BlockSpec((tm, d), lambda i: (i, 0))
    before = pl.BlockSpec((POOL_HALO, d), lambda i: (jnp.maximum(i * per - 1, 0), 0))
    return main, before


def _pool_fwd(x, w, scale, gamma, beta, name):
    t, d = x.shape
    ng, cg = w.shape[0], w.shape[1]
    tm = _tile(t, LN_ROW_TILE)

    def body(x_ref, h_ref, w_ref, s_ref, g_ref, b_ref, r_ref, o_ref, ob_ref):
        i = pl.program_id(0)
        for gi in range(ng):
            cols = pl.ds(gi * cg, cg)
            main = x_ref[:, cols]
            halo = jnp.where(i > 0, h_ref[:, cols], 0.0)
            pooled = _pooled_group(main, halo, gi, i * tm)
            y = jnp.dot(pooled.astype(BF16), w_ref[gi], preferred_element_type=F32)
            r_ref[:, cols] = ALPHA * main + y * s_ref[:, cols]
        out = _layer_norm_rows(r_ref[...], g_ref[...], b_ref[...])
        o_ref[...] = out
        ob_ref[...] = out.astype(BF16)

    main, before = _pool_specs(t, d, tm)
    vec = pl.BlockSpec((1, d), lambda i: (0, 0))
    return pl.pallas_call(
        body, grid=(t // tm,),
        in_specs=[main, before, pl.BlockSpec(w.shape, lambda i: (0, 0, 0)), vec, vec, vec],
        out_specs=[main, main, main],
        out_shape=[_sds((t, d), F32), _sds((t, d), F32), _sds((t, d), BF16)],
        name=name, compiler_params=_cp(1))(x, x, w, scale, gamma, beta)


def _pool_bwd(x, dy, w, scale, name):
    t, d = x.shape
    ng, cg = w.shape[0], w.shape[1]
    tm = _tile(t, LN_ROW_TILE)

    def body(x_ref, h_ref, dy_ref, w_ref, s_ref, dp_ref, dw_ref, ds_ref):
        i = pl.program_id(0)

        @pl.when(i == 0)
        def _():
            dw_ref[...] = jnp.zeros_like(dw_ref)
            ds_ref[...] = jnp.zeros_like(ds_ref)

        for gi in range(ng):
            cols = pl.ds(gi * cg, cg)
            main = x_ref[:, cols]
            halo = jnp.where(i > 0, h_ref[:, cols], 0.0)
            pooled = _pooled_group(main, halo, gi, i * tm).astype(BF16)
            y = jnp.dot(pooled, w_ref[gi], preferred_element_type=F32)
            dyg = dy_ref[:, cols]
            ds_ref[:, cols] += jnp.sum(dyg * y, axis=0, keepdims=True)
            dyw = (dyg * s_ref[:, cols]).astype(BF16)
            dw_ref[gi] += lax.dot_general(pooled, dyw, _TN, preferred_element_type=F32)
            dp_ref[:, cols] = lax.dot_general(dyw, w_ref[gi], _NT, preferred_element_type=F32)

    main, before = _pool_specs(t, d, tm)
    vec = pl.BlockSpec((1, d), lambda i: (0, 0))
    wspec = pl.BlockSpec(w.shape, lambda i: (0, 0, 0))
    return pl.pallas_call(
        body, grid=(t // tm,), in_specs=[main, before, main, wspec, vec],
        out_specs=[main, wspec, vec],
        out_shape=[_sds((t, d), F32), _sds(w.shape, F32), _sds((1, d), F32)],
        name=name, compiler_params=_cp(1))(x, x, dy, w, scale)


def _pool_adjoint(dp, dres, n_groups, name):
    t, d = dp.shape
    cg = d // n_groups
    tm = _tile(t, ROW_TILE)
    steps = t // tm
    per = tm // POOL_HALO

    def body(dp_ref, after_ref, dres_ref, dx_ref):
        i = pl.program_id(0)
        rows = lax.broadcasted_iota(jnp.int32, (tm, 1), 0)
        rows_after = lax.broadcasted_iota(jnp.int32, (POOL_HALO, 1), 0)
        for gi in range(n_groups):
            window = POOL_WINDOWS[gi]
            cols = pl.ds(gi * cg, cg)
            main = dp_ref[:, cols]
            cnt = jnp.minimum(i * tm + rows + 1, window).astype(F32)
            cnt_after = jnp.minimum((i + 1) * tm + rows_after + 1, window).astype(F32)
            after = jnp.where(i < steps - 1, after_ref[:, cols] / cnt_after, 0.0)
            ext = jnp.concatenate([main / cnt, after], axis=0)
            sums = _window_sums(ext, window, forward=True)[:tm, :]
            dx_ref[:, cols] = ALPHA * dres_ref[:, cols] + sums - main

    main = pl.BlockSpec((tm, d), lambda i: (i, 0))
    after = pl.BlockSpec((POOL_HALO, d), lambda i: (jnp.minimum((i + 1) * per, t // POOL_HALO - 1), 0))
    return pl.pallas_call(
        body, grid=(steps,), in_specs=[main, after, main], out_specs=main,
        out_shape=_sds((t, d), F32), name=name, compiler_params=_cp(1))(dp, dp, dres)


def _split_dot(x, tri):
    hi = x.astype(BF16)
    lo = (x - hi.astype(F32)).astype(BF16)
    return jnp.dot(hi, tri, preferred_element_type=F32) + jnp.dot(lo, tri, preferred_element_type=F32)


def _att_window(q, k_w, limit, carry_rest, suffix):
    z = lax.dot_general(q, k_w, _NT, preferred_element_type=F32) * ATT_SCALE
    z = jnp.where(lax.broadcasted_iota(jnp.int32, z.shape, 1) < limit, z, -MASKED)
    e = jnp.exp(-jnp.abs(z))
    log_not = -(jnp.maximum(z, 0.0) + jnp.log(1.0 + e))
    rest = _split_dot(log_not, suffix[...]) + carry_rest
    a = jnp.exp(z + rest)
    return z, e, log_not, a


def _tri(w, strict):
    r = lax.broadcasted_iota(jnp.int32, (w, w), 0)
    c = lax.broadcasted_iota(jnp.int32, (w, w), 1)
    return ((r > c) if strict else (r >= c)).astype(BF16)


def _heads_per_step(qkv3, n_heads):
    cpb = qkv3.shape[2] // HEAD_DIM
    return 2 if (cpb % 2 == 0 and n_heads % 2 == 0) else 1


def _att_specs(qkv3, n_heads, hp):
    t = qkv3.shape[1]
    cpb = qkv3.shape[2] // HEAD_DIM
    wd = hp * HEAD_DIM

    def slab(off):
        return pl.BlockSpec((None, t, wd), lambda g, i: ((off + g * hp) // cpb, 0, ((off + g * hp) % cpb) // hp))

    q = pl.BlockSpec((None, ATT_BLOCK, wd), lambda g, i: ((g * hp) // cpb, i, ((g * hp) % cpb) // hp))
    return q, slab(n_heads), slab(2 * n_heads)


def _head_cols(hh):
    return pl.ds(hh * HEAD_DIM, HEAD_DIM)


def _key_bounds(k_ref, kmax, hp):
    for hh in range(hp):
        kf = k_ref[:, _head_cols(hh)].astype(F32)
        kmax[hh] = jnp.sqrt(jnp.max(jnp.sum(kf * kf, axis=1, keepdims=True)))


def _score_bound(q, key_norm):
    qf = q.astype(F32)
    return ATT_SCALE * 1.001 * key_norm * jnp.sqrt(jnp.sum(qf * qf, axis=1, keepdims=True)) + 1e-3


def _any_alive(rests, bounds):
    alive = jnp.max(rests[0] + bounds[0]) > -EXP_ZERO
    for r, zb in zip(rests[1:], bounds[1:]):
        alive = jnp.logical_or(alive, jnp.max(r + zb) > -EXP_ZERO)
    return alive


def _window_rows(hi, w):
    start = jnp.maximum(hi - w, 0)
    return start, pl.ds(pl.multiple_of(start, ATT_BLOCK), w)


def _attn_fwd(qkv3, n_heads, name):
    t = qkv3.shape[1]
    b = ATT_BLOCK
    w = min(ATT_WINDOW, t)
    hp = _heads_per_step(qkv3, n_heads)
    heads = range(hp)

    def body(q_ref, k_ref, v_ref, o_ref, kmax, suffix):
        i = pl.program_id(1)

        @pl.when(i == 0)
        def _():
            _key_bounds(k_ref, kmax, hp)
            suffix[...] = _tri(w, strict=False)

        qs = [q_ref[:, _head_cols(hh)] for hh in heads]
        bounds = [_score_bound(qs[hh], kmax[hh]) for hh in heads]
        qpos = i * b + lax.broadcasted_iota(jnp.int32, (b, 1), 0)

        def cond(c):
            return jnp.logical_and(c[0] > 0, _any_alive(c[1], bounds))

        def step(c):
            hi, rests, accs = c
            start, rows = _window_rows(hi, w)
            limit = jnp.minimum(qpos, hi) - start
            new_rests, new_accs = [], []
            for hh in heads:
                _, _, log_not, a = _att_window(qs[hh], k_ref[rows, _head_cols(hh)], limit, rests[hh], suffix)
                new_accs.append(accs[hh] + jnp.dot(a.astype(BF16), v_ref[rows, _head_cols(hh)],
                                                   preferred_element_type=F32))
                new_rests.append(rests[hh] + jnp.sum(log_not, axis=1, keepdims=True))
            return start, tuple(new_rests), tuple(new_accs)

        init = ((i + 1) * b, tuple(jnp.zeros((b, 1), F32) for _ in heads),
                tuple(jnp.zeros((b, HEAD_DIM), F32) for _ in heads))
        _, _, accs = lax.while_loop(cond, step, init)
        for hh in heads:
            o_ref[:, _head_cols(hh)] = accs[hh].astype(o_ref.dtype)

    qs_, ks_, vs_ = _att_specs(qkv3, n_heads, hp)
    return pl.pallas_call(
        body, grid=(n_heads // hp, t // b), in_specs=[qs_, ks_, vs_],
        out_specs=pl.BlockSpec((b, hp * HEAD_DIM), lambda g, i: (i, g)),
        out_shape=_sds((t, n_heads * HEAD_DIM), BF16),
        scratch_shapes=[pltpu.SMEM((hp,), F32), pltpu.VMEM((w, w), BF16)],
        name=name, compiler_params=_cp(2))(qkv3, qkv3, qkv3)


def _attn_bwd(qkv3, do, n_heads, name):
    t = qkv3.shape[1]
    b = ATT_BLOCK
    w = min(ATT_WINDOW, t)
    nq = t // b
    hp = _heads_per_step(qkv3, n_heads)
    heads = range(hp)
    wd = hp * HEAD_DIM

    def body(q_ref, k_ref, v_ref, do_ref, dq_ref, dk_ref, dv_ref, kmax, dk_acc, dv_acc, suffix, strict_suffix):
        i = pl.program_id(1)

        @pl.when(i == 0)
        def _():
            _key_bounds(k_ref, kmax, hp)
            dk_acc[...] = jnp.zeros_like(dk_acc)
            dv_acc[...] = jnp.zeros_like(dv_acc)
            suffix[...] = _tri(w, strict=False)
            strict_suffix[...] = _tri(w, strict=True)

        qs = [q_ref[:, _head_cols(hh)] for hh in heads]
        douts = [do_ref[:, _head_cols(hh)] for hh in heads]
        bounds = [_score_bound(qs[hh], kmax[hh]) for hh in heads]
        zero_cols = tuple(jnp.zeros((b, 1), F32) for _ in heads)
        hi0 = (i + 1) * b
        qpos = i * b + lax.broadcasted_iota(jnp.int32, (b, 1), 0)

        def cond(c):
            return jnp.logical_and(c[0] > 0, _any_alive(c[1], bounds))

        def window(hh, rows, limit, rest):
            k_w = k_ref[rows, _head_cols(hh)]
            z, e, log_not, a = _att_window(qs[hh], k_w, limit, rest, suffix)
            dla = a * lax.dot_general(douts[hh], v_ref[rows, _head_cols(hh)], _NT, preferred_element_type=F32)
            return k_w, z, e, log_not, a, dla

        def sweep1(c):
            hi, rests, totals = c
            start, rows = _window_rows(hi, w)
            limit = jnp.minimum(qpos, hi) - start
            new_rests, new_totals = [], []
            for hh in heads:
                _, _, _, log_not, a, dla = window(hh, rows, limit, rests[hh])
                dv_acc[rows, _head_cols(hh)] += lax.dot_general(a.astype(BF16), douts[hh], _TN,
                                                                preferred_element_type=F32)
                new_rests.append(rests[hh] + jnp.sum(log_not, axis=1, keepdims=True))
                new_totals.append(totals[hh] + jnp.sum(dla, axis=1, keepdims=True))
            return start, tuple(new_rests), tuple(new_totals)

        _, _, totals = lax.while_loop(cond, sweep1, (hi0, zero_cols, zero_cols))

        def sweep2(c):
            hi, rests, laters, dqs = c
            start, rows = _window_rows(hi, w)
            limit = jnp.minimum(qpos, hi) - start
            new_rests, new_laters, new_dqs = [], [], []
            for hh in heads:
                k_w, z, e, log_not, _, dla = window(hh, rows, limit, rests[hh])
                inside = _split_dot(dla, strict_suffix[...])
                dlog_not = totals[hh] - laters[hh] - inside
                inv = 1.0 / (1.0 + e)
                sig = jnp.where(z >= 0, inv, e * inv)
                dz = ((dla - sig * dlog_not) * ATT_SCALE).astype(BF16)
                new_dqs.append(dqs[hh] + jnp.dot(dz, k_w, preferred_element_type=F32))
                dk_acc[rows, _head_cols(hh)] += lax.dot_general(dz, qs[hh], _TN, preferred_element_type=F32)
                new_rests.append(rests[hh] + jnp.sum(log_not, axis=1, keepdims=True))
                new_laters.append(laters[hh] + jnp.sum(dla, axis=1, keepdims=True))
            return start, tuple(new_rests), tuple(new_laters), tuple(new_dqs)

        init = (hi0, zero_cols, zero_cols, tuple(jnp.zeros((b, HEAD_DIM), F32) for _ in heads))
        _, _, _, dqs = lax.while_loop(cond, sweep2, init)
        for hh in heads:
            dq_ref[:, _head_cols(hh)] = dqs[hh].astype(dq_ref.dtype)

        @pl.when(i == nq - 1)
        def _():
            dk_ref[...] = dk_acc[...].astype(dk_ref.dtype)
            dv_ref[...] = dv_acc[...].astype(dv_ref.dtype)

    qs_, ks_, vs_ = _att_specs(qkv3, n_heads, hp)
    blk = pl.BlockSpec((b, wd), lambda g, i: (i, g))
    slab = pl.BlockSpec((t, wd), lambda g, i: (0, g))
    d = n_heads * HEAD_DIM
    return pl.pallas_call(
        body, grid=(n_heads // hp, nq),
        in_specs=[qs_, ks_, vs_, pl.BlockSpec((None, b, wd), lambda g, i: (0, i, g))],
        out_specs=[blk, slab, slab],
        out_shape=[_sds((t, d), BF16)] * 3,
        scratch_shapes=[pltpu.SMEM((hp,), F32), pltpu.VMEM((t, wd), F32), pltpu.VMEM((t, wd), F32),
                        pltpu.VMEM((w, w), BF16), pltpu.VMEM((w, w), BF16)],
        name=name, compiler_params=_cp(2))(qkv3, qkv3, qkv3, do)


def _conv_rows(main, halo, w_ref, b_ref):
    ext = jnp.concatenate([halo, main], axis=0)
    h1 = pltpu.roll(ext, 1, 0)[CONV_HALO:, :]
    h2 = pltpu.roll(ext, 2, 0)[CONV_HALO:, :]
    hc = b_ref[...] + w_ref[0:1, :] * h2
    hc = hc + w_ref[1:2, :] * h1
    hc = hc + w_ref[2:3, :] * main
    return hc, h1, h2


def _ffn_specs(t, fp, tm, half):
    per = tm // CONV_HALO
    main = lambda off: pl.BlockSpec((None, tm, fp), lambda g, i: (g + off, i, 0))
    before = lambda off: pl.BlockSpec((None, CONV_HALO, fp), lambda g, i: (g + off, jnp.maximum(i * per - 1, 0), 0))
    cw = lambda off: pl.BlockSpec((None, 3, fp), lambda g, i: (g + off, 0, 0))
    cb = lambda off: pl.BlockSpec((None, 1, fp), lambda g, i: (g + off, 0, 0))
    return [main(0), before(0), main(half), before(half), cw(0), cw(half), cb(0), cb(half)]


def _ffn_act(h, cw, cb, name):
    n, t, fp = h.shape
    half = n // 2
    tm = _tile(t, LN_ROW_TILE)

    def body(hg_ref, hgb_ref, hv_ref, hvb_ref, wg_ref, wv_ref, bg_ref, bv_ref, a_ref):
        first = pl.program_id(1) == 0
        gate, _, _ = _conv_rows(hg_ref[...], jnp.where(first, 0.0, hgb_ref[...]), wg_ref, bg_ref)
        val, _, _ = _conv_rows(hv_ref[...], jnp.where(first, 0.0, hvb_ref[...]), wv_ref, bv_ref)
        a_ref[...] = (gate * jax.nn.sigmoid(gate) * val).astype(a_ref.dtype)

    return pl.pallas_call(
        body, grid=(half, t // tm), in_specs=_ffn_specs(t, fp, tm, half),
        out_specs=pl.BlockSpec((None, tm, fp), lambda g, i: (g, i, 0)),
        out_shape=_sds((half, t, fp), BF16), name=name, compiler_params=_cp(2))(h, h, h, h, cw, cw, cb, cb)


def _act_grads(dact, gate, val):
    sig = jax.nn.sigmoid(gate)
    return dact * val * (sig * (1.0 + gate * (1.0 - sig))), dact * (gate * sig)


def _ffn_act_bwd(h, da, cw, cb, name):
    n, t, fp = h.shape
    half = n // 2
    tm = _tile(t, LN_ROW_TILE)
    steps = t // tm
    per = tm // CONV_HALO

    def body(hg_ref, hgb_ref, hv_ref, hvb_ref, wg_ref, wv_ref, bg_ref, bv_ref, da_ref,
             hga_ref, hva_ref, daa_ref, dh_ref, dw_ref, db_ref):
        i = pl.program_id(1)
        first = i == 0

        @pl.when(first)
        def _():
            dw_ref[...] = jnp.zeros_like(dw_ref)
            db_ref[...] = jnp.zeros_like(db_ref)

        hg, hv = hg_ref[...], hv_ref[...]
        gate, hg1, hg2 = _conv_rows(hg, jnp.where(first, 0.0, hgb_ref[...]), wg_ref, bg_ref)
        val, hv1, hv2 = _conv_rows(hv, jnp.where(first, 0.0, hvb_ref[...]), wv_ref, bv_ref)
        dgate, dval = _act_grads(da_ref[...], gate, val)
        gate_a, _, _ = _conv_rows(hga_ref[...], hg[tm - CONV_HALO:, :], wg_ref, bg_ref)
        val_a, _, _ = _conv_rows(hva_ref[...], hv[tm - CONV_HALO:, :], wv_ref, bv_ref)
        dgate_a, dval_a = _act_grads(jnp.where(i < steps - 1, daa_ref[...], 0.0), gate_a, val_a)
        for s, (dd, dd_a, w_ref, shifted) in enumerate(((dgate, dgate_a, wg_ref, (hg2, hg1, hg)),
                                                         (dval, dval_a, wv_ref, (hv2, hv1, hv)))):
            db_ref[s] += jnp.sum(dd, axis=0, keepdims=True)
            for kk in range(3):
                dw_ref[s, kk:kk + 1, :] += jnp.sum(dd * shifted[kk], axis=0, keepdims=True)
            ext = jnp.concatenate([dd, dd_a], axis=0)
            rows = ext.shape[0]
            d1 = pltpu.roll(ext, rows - 1, 0)[:tm, :]
            d2 = pltpu.roll(ext, rows - 2, 0)[:tm, :]
            dh_ref[s] = (w_ref[2:3, :] * dd + w_ref[1:2, :] * d1 + w_ref[0:1, :] * d2).astype(dh_ref.dtype)

    blk = pl.BlockSpec((None, tm, fp), lambda g, i: (g, i, 0))
    after = lambda off: pl.BlockSpec(
        (None, CONV_HALO, fp), lambda g, i: (g + off, jnp.minimum((i + 1) * per, t // CONV_HALO - 1), 0))
    specs = _ffn_specs(t, fp, tm, half) + [blk, after(0), after(half), after(0)]
    return pl.pallas_call(
        body, grid=(half, steps), in_specs=specs,
        out_specs=[pl.BlockSpec((2, None, tm, fp), lambda g, i: (0, g, i, 0)),
                   pl.BlockSpec((2, None, 3, fp), lambda g, i: (0, g, 0, 0)),
                   pl.BlockSpec((2, None, 1, fp), lambda g, i: (0, g, 0, 0))],
        out_shape=[_sds((2, half, t, fp), BF16), _sds((2, half, 3, fp), F32), _sds((2, half, 1, fp), F32)],
        name=name, compiler_params=_cp(2))(h, h, h, h, cw, cw, cb, cb, da, h, h, da)


def _place():
    x, y, c = lax.axis_index("x"), lax.axis_index("y"), lax.axis_index("c")
    chips = [(1 - x, y), (x, 1 - y), (1 - x, 1 - y)]
    return x, y, c, chips


def _run_sides(sides, name):
    n_in = [len(s.ins) for s in sides]
    n_out = [len(s.outs) for s in sides]
    n_sem = [len(s.sems) for s in sides]

    def body(*refs):
        ins, outs, sems = refs[:sum(n_in)], refs[sum(n_in):sum(n_in) + sum(n_out)], refs[sum(n_in) + sum(n_out):]
        oi = oo = os_ = 0
        for k, s in enumerate(sides):
            mine = (ins[oi:oi + n_in[k]], outs[oo:oo + n_out[k]], sems[os_:os_ + n_sem[k]])
            s.start(*mine)
            s.finish(*mine)
            oi, oo, os_ = oi + n_in[k], oo + n_out[k], os_ + n_sem[k]

    aliases, oi, oo = {}, 0, 0
    for k, s in enumerate(sides):
        aliases.update({oi + a: oo + b for a, b in s.alias.items()})
        oi, oo = oi + n_in[k], oo + n_out[k]
    return pl.pallas_call(
        body, in_specs=_any_specs(sum(n_in)), out_specs=_any_specs(sum(n_out)),
        out_shape=[o for s in sides for o in s.outs], input_output_aliases=aliases,
        scratch_shapes=[q for s in sides for q in s.sems], name=name)(*[a for s in sides for a in s.ins])


def _place_shard(kind, w, chip, name, rows=None, base=None):
    if kind == "pool":
        g, r, cdim = w.shape

        def body(chip_ref, w_ref, o_ref):
            del chip_ref
            o_ref[...] = w_ref[...].astype(BF16)

        return pl.pallas_call(
            body,
            grid_spec=pltpu.PrefetchScalarGridSpec(
                num_scalar_prefetch=1, grid=(1,),
                in_specs=[pl.BlockSpec((g, r, cdim), lambda i, chip_ref: (0, 0, 0))],
                out_specs=pl.BlockSpec((g, r, cdim), lambda i, chip_ref: (0, chip_ref[0], 0))),
            out_shape=_sds((g, 4 * r, cdim), BF16), name=name, compiler_params=_cp(1))(chip, w)

    r, cs = w.shape
    if kind == "lead":
        rows = rows or r
        tr = _tile(r, ROW_TILE, 16) if rows == r else rows - r
        assert r % tr == 0 and tr % 16 == 0
        n_src = r // tr

        def body(chip_ref, w_ref, o_ref):
            del chip_ref
            o_ref[...] = jnp.where(pl.program_id(0) < n_src, w_ref[...], 0.0).astype(BF16)

        return pl.pallas_call(
            body,
            grid_spec=pltpu.PrefetchScalarGridSpec(
                num_scalar_prefetch=1, grid=(rows // tr,),
                in_specs=[pl.BlockSpec((tr, cs), lambda i, chip_ref: (jnp.minimum(i, n_src - 1), 0))],
                out_specs=pl.BlockSpec((None, tr, cs), lambda i, chip_ref: (chip_ref[0], i, 0))),
            out_shape=_sds((4, rows, cs), BF16), name=name, compiler_params=_cp(1))(chip, w)

    assert kind == "down"
    tr = r // 2 if (r // 2) % 16 == 0 else r
    per = r // tr

    def body(chip_ref, w_ref, base_ref, o_ref):
        del chip_ref, base_ref
        o_ref[...] = w_ref[...].astype(BF16)

    return pl.pallas_call(
        body,
        grid_spec=pltpu.PrefetchScalarGridSpec(
            num_scalar_prefetch=1, grid=(per,),
            in_specs=[pl.BlockSpec((tr, cs), lambda i, chip_ref: (i, 0)), pl.BlockSpec(memory_space=pl.ANY)],
            out_specs=pl.BlockSpec((None, tr, cs), lambda i, chip_ref: (chip_ref[0] // 2, (chip_ref[0] % 2) * per + i, 0))),
        out_shape=_sds(base.shape, BF16), input_output_aliases={2: 0},
        name=name, compiler_params=_cp(1))(chip, w, base)


def _gather_sides(items, bufs=None):
    n = len(items)
    kinds = [it[0] for it in items]
    shard_rows = [it[2] for it in items]
    bufs = [it[1] for it in items] if bufs is None else list(bufs)

    def half_of(outs, m, chip, half):
        k = 2 * chip[0] + chip[1]
        o, r = outs[m], shard_rows[m]
        if kinds[m] == "pool":
            gh = o.shape[0] // 2
            return o.at[pl.ds(half * gh, gh), pl.ds(k * r, r)]
        r2 = r // 2
        if kinds[m] == "down":
            return o.at[k // 2, pl.ds((k % 2) * r + half * r2, r2)]
        return o.at[k, pl.ds(half * r2, r2)]

    def remote(outs, sems, m, j, chip, half, to):
        ref = half_of(outs, m, chip, half)
        return pltpu.make_async_remote_copy(src_ref=ref, dst_ref=ref, send_sem=sems[0].at[m, j],
                                            recv_sem=sems[1].at[m, j], device_id=to, device_id_type=MESH)

    def ici_copies(outs, sems, sending):
        x, y, c, chips = _place()
        if sending:
            return [remote(outs, sems, m, j, (x, y), c, (*chip, c)) for m in range(n) for j, chip in enumerate(chips)]
        return [remote(outs, sems, m, j, chip, c, (x, y, c)) for m in range(n) for j, chip in enumerate(chips)]

    def d2d_copies(outs, sems, sending):
        x, y, c, chips = _place()
        if sending:
            return [remote(outs, sems, m, j, chip, c, (x, y, 1 - c)) for m in range(n) for j, chip in enumerate(chips)]
        return [remote(outs, sems, m, j, chip, 1 - c, (x, y, c)) for m in range(n) for j, chip in enumerate(chips)]

    def phase(copies):
        def start(ins, outs, sems):
            for cp in copies(outs, sems, True):
                cp.start()

        def finish(ins, outs, sems):
            for cp in copies(outs, sems, False):
                cp.wait_recv()
            for cp in copies(outs, sems, True):
                cp.wait_send()

        return start, finish

    ici, d2d = phase(ici_copies), phase(d2d_copies)

    def both_finish(ins, outs, sems):
        ici[1](ins, outs, sems[:2])
        d2d[0](ins, outs, sems[2:])
        d2d[1](ins, outs, sems[2:])

    pair = [pltpu.SemaphoreType.DMA((n, 3)), pltpu.SemaphoreType.DMA((n, 3))]
    shapes = [_sds(b.shape, b.dtype) for b in bufs]
    alias = {m: m for m in range(n)}

    def side(which):
        if which == "both":
            return _Side(bufs, shapes, alias, pair + pair, lambda i, o, s: ici[0](i, o, s[:2]), both_finish)
        start, finish = ici if which == "ici" else d2d
        return _Side(bufs, shapes, alias, pair, start, finish)

    return side


def _sibling_side(grads):
    n = len(grads)

    def copies(ins, outs, sems):
        x, y, c, _ = _place()
        res = []
        for m in range(n):
            r2 = ins[m].shape[1] // 2
            res.append(pltpu.make_async_remote_copy(
                src_ref=ins[m].at[:, pl.ds((1 - c) * r2, r2)], dst_ref=outs[m],
                send_sem=sems[0].at[m], recv_sem=sems[1].at[m], device_id=(x, y, 1 - c), device_id_type=MESH))
        return res

    def start(ins, outs, sems):
        for cp in copies(ins, outs, sems):
            cp.start()

    def finish(ins, outs, sems):
        for cp in copies(ins, outs, sems):
            cp.wait_recv()
        for cp in copies(ins, outs, sems):
            cp.wait_send()

    return _Side(list(grads), [_sds((4, g.shape[1] // 2, g.shape[2]), g.dtype) for g in grads], {},
                 [pltpu.SemaphoreType.DMA((n,)), pltpu.SemaphoreType.DMA((n,))], start, finish)


def _owner_chips_side(parts):
    n = len(parts)

    def copies(ins, outs, sems):
        _, _, c, chips = _place()
        return [pltpu.make_async_remote_copy(
            src_ref=ins[m].at[2 * chip[0] + chip[1]], dst_ref=outs[m].at[j], send_sem=sems[0].at[m, j],
            recv_sem=sems[1].at[m, j], device_id=(*chip, c), device_id_type=MESH)
            for m in range(n) for j, chip in enumerate(chips)]

    def start(ins, outs, sems):
        for cp in copies(ins, outs, sems):
            cp.start()

    def finish(ins, outs, sems):
        for cp in copies(ins, outs, sems):
            cp.wait_recv()
        for cp in copies(ins, outs, sems):
            cp.wait_send()

    return _Side(list(parts), [_sds((3,) + p.shape[1:], p.dtype) for p in parts], {},
                 [pltpu.SemaphoreType.DMA((n, 3)), pltpu.SemaphoreType.DMA((n, 3))], start, finish)


def _exchange_finished_halves(shards, name):
    n = len(shards)

    def body(*refs):
        out = refs[n:2 * n]
        send_sems, recv_sems = refs[2 * n:]
        x, y, c, _ = _place()
        copies = []
        for m in range(n):
            r2 = out[m].shape[0] // 2
            mine = out[m].at[pl.ds(c * r2, r2)]
            copies.append(pltpu.make_async_remote_copy(
                src_ref=mine, dst_ref=mine, send_sem=send_sems.at[m], recv_sem=recv_sems.at[m],
                device_id=(x, y, 1 - c), device_id_type=MESH))
        for cp in copies:
            cp.start()
        for m in range(n):
            r2 = out[m].shape[0] // 2
            theirs = out[m].at[pl.ds((1 - c) * r2, r2)]
            pltpu.make_async_remote_copy(
                src_ref=theirs, dst_ref=theirs, send_sem=send_sems.at[m], recv_sem=recv_sems.at[m],
                device_id=(x, y, 1 - c), device_id_type=MESH).wait_recv()
        for cp in copies:
            cp.wait_send()

    return pl.pallas_call(
        body, in_specs=_any_specs(n), out_specs=_any_specs(n), out_shape=[_sds(s.shape, s.dtype) for s in shards],
        input_output_aliases={m: m for m in range(n)},
        scratch_shapes=[pltpu.SemaphoreType.DMA((n,)), pltpu.SemaphoreType.DMA((n,))], name=name)(*shards)


def _all_reduce_small(v, name):
    rows = v.shape[0]

    def body(v_ref, out_ref, buf, send_sems, recv_sems, local_sem):
        x, y, c, chips = _place()
        me, sibling = (x, y, c), (x, y, 1 - c)

        def slot(px, py, pc):
            return buf.at[4 * px + 2 * py + pc]

        def copy(k, block, to, src=None):
            return pltpu.make_async_remote_copy(
                src_ref=slot(*block) if src is None else src, dst_ref=slot(*block),
                send_sem=send_sems.at[k], recv_sem=recv_sems.at[k], device_id=to, device_id_type=MESH)

        mine = pltpu.make_async_copy(v_ref, slot(*me), local_sem)
        mine.start()
        first = [copy(0, me, sibling, src=v_ref)]
        first += [copy(1 + j, me, (*chip, c), src=v_ref) for j, chip in enumerate(chips)]
        for cp in first:
            cp.start()
        passed = [copy(4 + j, (*chip, c), sibling) for j, chip in enumerate(chips)]
        for j, chip in enumerate(chips):
            copy(1 + j, (*chip, c), me).wait_recv()
            passed[j].start()
        copy(0, sibling, me).wait_recv()
        for j, chip in enumerate(chips):
            copy(4 + j, (*chip, 1 - c), me).wait_recv()
        for cp in first + passed:
            cp.wait_send()
        mine.wait()
        total = buf[0]
        for dev in range(1, 8):
            total = total + buf[dev]
        out_ref[...] = total

    vm = pl.BlockSpec(memory_space=pltpu.VMEM)
    return pl.pallas_call(
        body, in_specs=[vm], out_specs=vm, out_shape=_sds(v.shape, F32),
        scratch_shapes=[pltpu.VMEM((8, rows, LANES), F32), pltpu.SemaphoreType.DMA((7,)),
                        pltpu.SemaphoreType.DMA((7,)), pltpu.SemaphoreType.DMA],
        name=name, compiler_params=pltpu.CompilerParams(vmem_limit_bytes=VMEM_LIMIT))(v)


def _chip_partial(grad, from_sibling, core, name):
    _, r, cdim = grad.shape
    r2 = r // 2
    tr = _tile(r2, SUM_ROW_TILE)
    per = r2 // tr

    def body(core_ref, g_ref, s_ref, o_ref, ob_ref):
        del core_ref
        total = g_ref[...] + s_ref[...]
        o_ref[...] = total
        ob_ref[...] = total.astype(BF16)

    blk = pl.BlockSpec((None, tr, cdim), lambda k, i, core_ref: (k, i, 0))
    mine = pl.BlockSpec((None, tr, cdim), lambda k, i, core_ref: (k, core_ref[0] * per + i, 0))
    return pl.pallas_call(
        body,
        grid_spec=pltpu.PrefetchScalarGridSpec(num_scalar_prefetch=1, grid=(4, per), in_specs=[mine, blk],
                                               out_specs=[blk, blk]),
        out_shape=[_sds((4, r2, cdim), F32), _sds((4, r2, cdim), BF16)],
        name=name, compiler_params=_cp(2))(core, grad, from_sibling)


def _owner_sum(partial, from_chips, place, name):
    _, r2, cdim = partial.shape
    tr = _tile(r2, SUM_ROW_TILE)
    per = r2 // tr

    def body(place_ref, p_ref, f_ref, o_ref):
        del place_ref
        total = p_ref[...]
        for j in range(3):
            total = total + f_ref[j].astype(F32)
        o_ref[...] = total

    return pl.pallas_call(
        body,
        grid_spec=pltpu.PrefetchScalarGridSpec(
            num_scalar_prefetch=1, grid=(per,),
            in_specs=[pl.BlockSpec((None, tr, cdim), lambda i, place_ref: (place_ref[0], i, 0)),
                      pl.BlockSpec((3, tr, cdim), lambda i, place_ref: (0, i, 0))],
            out_specs=pl.BlockSpec((tr, cdim), lambda i, place_ref: (place_ref[1] * per + i, 0))),
        out_shape=_sds((2 * r2, cdim), F32), name=name, compiler_params=_cp(1))(place, partial, from_chips)


def _adamw(g, w, m, v, layer, prev, name):
    _, r, cdim = w.shape
    tr = _tile(r, OPT_ROW_TILE)
    c1 = 1.0 / (1.0 - ADAM_B1 ** ADAM_STEP)
    c2 = 1.0 / (1.0 - ADAM_B2 ** ADAM_STEP)
    n_prev = 0 if prev is None else 4

    def body(g_ref, w_ref, m_ref, v_ref, *rest):
        go_ref, d_ref, mo_ref, vo_ref = rest[n_prev:]
        grad = g_ref[:, pl.ds(0, cdim)]
        m_new = ADAM_B1 * m_ref[...] + (1.0 - ADAM_B1) * grad
        v_new = ADAM_B2 * v_ref[...] + (1.0 - ADAM_B2) * (grad * grad)
        go_ref[...] = grad
        mo_ref[...] = m_new
        vo_ref[...] = v_new
        d_ref[...] = -ADAM_LR * ((m_new * c1) / (jnp.sqrt(v_new * c2) + ADAM_EPS) + ADAM_WD * w_ref[...])

    blk = pl.BlockSpec((None, tr, cdim), lambda i: (layer, i, 0))
    gblk = pl.BlockSpec((tr, g.shape[1]), lambda i: (i, 0))
    return pl.pallas_call(
        body, grid=(r // tr,), in_specs=[gblk, blk, blk, blk] + _any_specs(n_prev), out_specs=[blk] * 4,
        out_shape=[_sds(w.shape, F32)] * 4, input_output_aliases={4 + k: k for k in range(n_prev)},
        name=name, compiler_params=_cp(1))(g, w, m, v, *(prev or ()))


def _pack_rows(vectors):
    flat = [v.reshape(-1) for v in vectors]
    sizes = [f.shape[0] for f in flat]
    total = sum(sizes)
    padded = _round_up(total, 8 * LANES)
    buf = jnp.concatenate(flat + [jnp.zeros((padded - total,), F32)])
    return buf.reshape(padded // LANES, LANES), sizes


def _unpack_rows(buf, sizes, shapes):
    flat = buf.reshape(-1)
    out, off = [], 0
    for n, shp in zip(sizes, shapes):
        out.append(flat[off:off + n].reshape(shp))
        off += n
    return out


def kernel(x, pool_w, pool_scale, attn_w_qkv, attn_w_o, ffn_w_up, ffn_conv_w, ffn_conv_b, ffn_w_down, ln_mix_g, ln_mix_b, ln_ffn_g, ln_ffn_b, loss_target, m_pool_w, m_pool_scale, m_attn_w_qkv, m_attn_w_o, m_ffn_w_up, m_ffn_conv_w, m_ffn_conv_b, m_ffn_w_down, m_ln_mix_g, m_ln_mix_b, m_ln_ffn_g, m_ln_ffn_b, v_pool_w, v_pool_scale, v_attn_w_qkv, v_attn_w_o, v_ffn_w_up, v_ffn_conv_w, v_ffn_conv_b, v_ffn_w_down, v_ln_mix_g, v_ln_mix_b, v_ln_ffn_g, v_ln_ffn_b):
    t, d = x.shape[1], x.shape[2]
    n_heads = d // HEAD_DIM
    n_groups = pool_w.shape[1]
    fs = ffn_w_up.shape[2]
    fp = _round_up(fs, LANES)
    rd = ffn_w_down.shape[1]
    assert 2 * rd == fs
    xi, yi, ci = lax.axis_index("x"), lax.axis_index("y"), lax.axis_index("c")
    chip = (2 * xi + yi).astype(jnp.int32)
    chip_arr, core_arr = chip.reshape(1), ci.astype(jnp.int32).reshape(1)
    place_arr = jnp.concatenate([chip_arr, core_arr])

    x2 = x.reshape(t, d)
    target = loss_target.reshape(t, d)
    pad_cols = lambda a: jnp.pad(a, [(0, 0)] * (a.ndim - 1) + [(0, fp - fs)])
    up_t = [jnp.transpose(a, (0, 2, 1)) for a in (ffn_w_up, m_ffn_w_up, v_ffn_w_up)]

    gather_items = []
    for i in range(DEPTH):
        j = i // 2
        items = []
        if i % 2 == 0:
            items.append(("pool", _place_shard("pool", pool_w[j], chip_arr, name="place_pool"), pool_w.shape[2]))
        else:
            items.append(("lead", _place_shard("lead", attn_w_qkv[j], chip_arr, name="place_qkv"), d))
            items.append(("lead", _place_shard("lead", attn_w_o[j], chip_arr, name="place_wo"), attn_w_o.shape[1]))
        items.append(("lead", _place_shard("lead", up_t[0][i], chip_arr, name="place_up", rows=fp), fp))
        items.append(("down", _place_shard("down", ffn_w_down[i], chip_arr, name="place_down",
                                           base=jnp.zeros((2, fp, d), BF16)), rd))
        gather_items.append(items)
    weights = [None] * DEPTH
    weights[0] = _run_sides([_gather_sides(gather_items[0])("both")], name="gather_layer0")

    conv_b_all = pad_cols(ffn_conv_b.reshape(DEPTH, 4, 1, fs))
    cw_local = pad_cols(ffn_conv_w)
    slot = (jnp.arange(4, dtype=jnp.int32) == chip).astype(F32) * (1.0 - ci.astype(F32))
    cw_placed = slot[None, :, None, None] * cw_local[:, None]
    cw_buf, cw_sizes = _pack_rows([cw_placed])
    conv_w_all = _unpack_rows(_all_reduce_small(cw_buf, name="gather_conv_w"), cw_sizes, [cw_placed.shape])[0]

    gam = lambda a, i: a[i].reshape(1, d)

    saved = []
    cur, cur_b = x2, x2.astype(BF16)
    for i in range(DEPTH):
        j = i // 2
        w = weights[i]
        s = {"x_in": cur, "x_in_b": cur_b}
        if i % 2 == 0:
            w_pool, w_up, w_down = w
            s["scale"] = pool_scale[j].reshape(1, d)
            r1, x1, x1b = _pool_fwd(cur, w_pool, s["scale"], gam(ln_mix_g, i), gam(ln_mix_b, i), name="pool_fwd")
        else:
            w_qkv, w_o, w_up, w_down = w
            w_o3 = w_o.reshape(1, d, d)
            qkv = _mm_cols(cur_b, w_qkv, BF16, name="qkv_proj")
            o = _attn_fwd(qkv, n_heads, name="attn_fwd")
            s["qkv"], s["o"], s["w_o3"] = qkv, o, w_o3
            r1, x1, x1b = _mm_res_ln(o.reshape(1, t, d), w_o3, cur, gam(ln_mix_g, i), gam(ln_mix_b, i),
                                     name="attn_out_ln")
        if i + 1 < DEPTH:
            nxt = gather_items[i + 1]
            h, landed = _mm_cols(x1b, w_up, F32, name="ffn_up", transposed_b=True, side=_gather_sides(nxt)("ici"))
            a = _ffn_act(h, conv_w_all[i], conv_b_all[i], name="ffn_act")
            (r2, x2n, x2b), gathered = _mm_res_ln(a, w_down, x1, gam(ln_ffn_g, i), gam(ln_ffn_b, i),
                                                  name="ffn_down_ln", side=_gather_sides(nxt, landed)("d2d"))
            weights[i + 1] = list(gathered)
        else:
            h = _mm_cols(x1b, w_up, F32, name="ffn_up", transposed_b=True)
            a = _ffn_act(h, conv_w_all[i], conv_b_all[i], name="ffn_act")
            r2, x2n, x2b = _mm_res_ln(a, w_down, x1, gam(ln_ffn_g, i), gam(ln_ffn_b, i), name="ffn_down_ln")
        s.update(r1=r1, x1b=x1b, h=h, a=a, r2=r2)
        saved.append(s)
        cur, cur_b = x2n, x2b

    loss_row, dcur = _loss_and_grad(cur, target, name="loss")
    loss = lax.psum(loss_row[0, 0], ("x", "y", "c"))

    big_grads = [None] * DEPTH
    reduced = [None] * DEPTH
    small = {}

    def finish_reduce(parts, from_chips, layer):
        halves = [_owner_sum(p[0], fc, place_arr, name="reduce_owner_sum") for p, fc in zip(parts, from_chips)]
        return _exchange_finished_halves(halves, name="reduce_halves_pool" if layer % 2 == 0 else "reduce_halves_attn")

    for i in reversed(range(DEPTH)):
        j = i // 2
        s, w = saved[i], weights[i]
        w_up, w_down = w[-2], w[-1]
        dr2, dr2b, small["ln_ffn_g", i], small["ln_ffn_b", i] = _ln_bwd(dcur, s["r2"], gam(ln_ffn_g, i), name="ln_bwd")
        pending = big_grads[i + 1] if i + 1 < DEPTH else None
        if pending is not None:
            da, from_sib = _mm_cols(dr2b, w_down, F32, name="ffn_down_bwd_act", transposed_b=True,
                                    side=_sibling_side(pending))
            parts = [_chip_partial(g, fs_, core_arr, name="reduce_chip_partial") for g, fs_ in zip(pending, from_sib)]
        else:
            da = _mm_cols(dr2b, w_down, F32, name="ffn_down_bwd_act", transposed_b=True)
        dr2b3 = dr2b.reshape(1, t, d)
        nmb = 2
        d_down = _mm_tn(s["a"], dr2b3, (2, fp, d), fp // nmb, d, 2,
                        (lambda u: u, nmb, lambda u, mb: mb), (lambda u: 0, lambda u: 0),
                        (lambda u: u, lambda u, mb: mb, lambda u: 0), name="ffn_down_bwd_w")
        dh4, dcw, dcb = _ffn_act_bwd(s["h"], da, conv_w_all[i], conv_b_all[i], name="ffn_act_bwd")
        small["conv_w", i], small["conv_b", i] = dcw, dcb
        dh = dh4.reshape(4, t, fp)
        if pending is not None:
            dx1, from_chips = _mm_nt_acc(dh, w_up, dr2, fp, name="ffn_up_bwd_act", b_is_kn=True,
                                         side=_owner_chips_side([p[1] for p in parts]))
            reduced[i + 1] = finish_reduce(parts, from_chips, i + 1)
        else:
            dx1 = _mm_nt_acc(dh, w_up, dr2, fp, name="ffn_up_bwd_act", b_is_kn=True)
        d_up = _mm_tn(dh, s["x1b"].reshape(1, t, d), (4, fp, d), fp // 2, d, 4,
                      (lambda u: u, 2, lambda u, mb: mb), (lambda u: 0, lambda u: 0),
                      (lambda u: u, lambda u, mb: mb, lambda u: 0), name="ffn_up_bwd_w")
        dr1, dr1b, small["ln_mix_g", i], small["ln_mix_b", i] = _ln_bwd(dx1, s["r1"], gam(ln_mix_g, i), name="ln_bwd")
        d_down4 = d_down[:, :fs].reshape(4, rd, d)
        if i % 2 == 0:
            dp, d_pool, small["pool_scale", j] = _pool_bwd(s["x_in"], dr1, w[0], s["scale"], name="pool_bwd")
            dcur = _pool_adjoint(dp, dr1, n_groups, name="pool_adjoint")
            cg = d // n_groups
            d_pool4 = d_pool.reshape(n_groups, 4, cg // 4, cg).transpose(1, 0, 2, 3).reshape(4, n_groups * (cg // 4), cg)
            big_grads[i] = [d_pool4, d_up, d_down4]
        else:
            w_qkv = w[0]
            do = _mm_cols(dr1b, s["w_o3"], BF16, name="attn_out_bwd_act", transposed_b=True)
            d_wo = _mm_tn(s["o"].reshape(1, t, d), dr1b.reshape(1, t, d), (1, d, d), d // 2, d, 1,
                          (lambda u: 0, 2, lambda u, mb: mb), (lambda u: 0, lambda u: 0),
                          (lambda u: 0, lambda u, mb: mb, lambda u: 0), name="attn_out_bwd_w")
            dq, dk, dv = _attn_bwd(s["qkv"], do, n_heads, name="attn_bwd")
            dqkv = jnp.stack([dq, dk, dv])
            cq = w_qkv.shape[2]
            kb = cq // 3
            na, nbk = d // kb, cq // kb
            dcur = _mm_nt_acc(dqkv, w_qkv, dr1, kb, name="qkv_bwd_act")
            d_qkv = _mm_tn(s["x_in_b"].reshape(1, t, d), dqkv, (4, d, cq), d // 2, kb, 3 * na,
                           (lambda u: 0, 2, lambda u, mb: mb), (lambda u: u // na, lambda u: u % na),
                           (lambda u: u // nbk, lambda u, mb: mb, lambda u: u % nbk), name="qkv_bwd_w")
            big_grads[i] = [d_qkv, d_wo.reshape(4, d // 4, d), d_up, d_down4]
    grad_x = dcur.reshape(1, t, d)

    from_sib = _run_sides([_sibling_side(big_grads[0])], name="reduce_layer0_sibling")
    parts = [_chip_partial(g, fs_, core_arr, name="reduce_chip_partial") for g, fs_ in zip(big_grads[0], from_sib)]
    from_chips = _run_sides([_owner_chips_side([p[1] for p in parts])], name="reduce_layer0_chips")
    reduced[0] = finish_reduce(parts, from_chips, 0)

    names = [("pool_scale", j) for j in range(2)]
    for nm in ("ln_mix_g", "ln_mix_b", "ln_ffn_g", "ln_ffn_b", "conv_b", "conv_w"):
        names += [(nm, i) for i in range(DEPTH)]
    vecs = [small[k] for k in names]
    sbuf, ssizes = _pack_rows(vecs)
    summed = dict(zip(names, _unpack_rows(_all_reduce_small(sbuf, name="reduce_small"), ssizes, [v.shape for v in vecs])))

    def stack_layers(nm, count):
        return jnp.stack([summed[nm, i] for i in range(count)])

    g_small = {
        "pool_scale": stack_layers("pool_scale", 2).reshape(2, d),
        "ln_mix_g": stack_layers("ln_mix_g", DEPTH).reshape(DEPTH, d),
        "ln_mix_b": stack_layers("ln_mix_b", DEPTH).reshape(DEPTH, d),
        "ln_ffn_g": stack_layers("ln_ffn_g", DEPTH).reshape(DEPTH, d),
        "ln_ffn_b": stack_layers("ln_ffn_b", DEPTH).reshape(DEPTH, d),
        "conv_b": stack_layers("conv_b", DEPTH).reshape(DEPTH, 4, fp)[:, :, :fs].reshape(DEPTH, 4 * fs),
        "conv_w": lax.dynamic_index_in_dim(stack_layers("conv_w", DEPTH).reshape(DEPTH, 4, 3, fp), chip, axis=1,
                                           keepdims=False)[:, :, :fs],
    }
    w_small = {"pool_scale": (pool_scale, m_pool_scale, v_pool_scale), "ln_mix_g": (ln_mix_g, m_ln_mix_g, v_ln_mix_g),
               "ln_mix_b": (ln_mix_b, m_ln_mix_b, v_ln_mix_b), "ln_ffn_g": (ln_ffn_g, m_ln_ffn_g, v_ln_ffn_g),
               "ln_ffn_b": (ln_ffn_b, m_ln_ffn_b, v_ln_ffn_b), "conv_b": (ffn_conv_b, m_ffn_conv_b, v_ffn_conv_b),
               "conv_w": (ffn_conv_w, m_ffn_conv_w, v_ffn_conv_w)}
    order = list(g_small)
    packs = [_pack_rows([g_small[k] for k in order])[0]]
    for idx in range(3):
        packs.append(_pack_rows([w_small[k][idx] for k in order])[0])
    small_sizes = _pack_rows([g_small[k] for k in order])[1]
    small_out = _adamw(packs[0], packs[1][None], packs[2][None], packs[3][None], 0, None, name="adamw_small")
    shapes = [g_small[k].shape for k in order]
    small_res = {k: [] for k in order}
    for arr in small_out:
        for k, val in zip(order, _unpack_rows(arr[0], small_sizes, shapes)):
            small_res[k].append(val)

    def opt_layers(per_layer_grads, w_all, m_all, v_all, name, rows=None):
        n_layers = w_all.shape[0]
        flat = [a.reshape(n_layers, rows or a.shape[1], -1) for a in (w_all, m_all, v_all)]
        res = None
        for li, g in enumerate(per_layer_grads):
            res = _adamw(g, *flat, li, res, name=name)
        return [o.reshape(w_all.shape) for o in res]

    cg = d // n_groups
    big = {
        "pool_w": opt_layers([reduced[i][0] for i in (0, 2)], pool_w, m_pool_w, v_pool_w, "adamw_pool",
                             rows=n_groups * (cg // 4)),
        "attn_w_qkv": opt_layers([reduced[i][0] for i in (1, 3)], attn_w_qkv, m_attn_w_qkv, v_attn_w_qkv, "adamw_qkv"),
        "attn_w_o": opt_layers([reduced[i][1] for i in (1, 3)], attn_w_o, m_attn_w_o, v_attn_w_o, "adamw_wo"),
        "ffn_w_up": [jnp.transpose(o, (0, 2, 1))
                     for o in opt_layers([reduced[i][-2] for i in range(DEPTH)], *up_t, "adamw_up")],
        "ffn_w_down": opt_layers([reduced[i][-1] for i in range(DEPTH)], ffn_w_down, m_ffn_w_down, v_ffn_w_down,
                                 "adamw_down"),
    }

    def leaf(k, name):
        if name in big:
            return big[name][k]
        key = {"ffn_conv_w": "conv_w", "ffn_conv_b": "conv_b"}.get(name, name)
        return small_res[key][k]

    weight_names = ["pool_w", "pool_scale", "attn_w_qkv", "attn_w_o", "ffn_w_up", "ffn_conv_w", "ffn_conv_b",
                    "ffn_w_down", "ln_mix_g", "ln_mix_b", "ln_ffn_g", "ln_ffn_b"]
    outs = [loss, grad_x]
    for k in range(4):
        outs += [leaf(k, nm) for nm in weight_names]
    return tuple(outs)
```

```python
import collections

import jax
import jax.numpy as jnp
from jax import lax
from jax.experimental import pallas as pl
from jax.experimental.pallas import tpu as pltpu

F32, BF16 = jnp.float32, jnp.bfloat16
MESH = pl.DeviceIdType.MESH

LANES = 128
HEAD_DIM = 128
ATT_BLOCK = 128
ATT_WINDOW = 3 * ATT_BLOCK
ATT_FWD_HEADS = 4
ATT_BWD_HEADS = 2
POOL_WINDOWS = (2, 4, 8, 16)
POOL_HALO = 16
CONV_HALO = 8
LN_EPS = 1e-5
DEPTH = 4
ALPHA = (2.0 * DEPTH) ** 0.25
ATT_SCALE = HEAD_DIM ** -0.5
EXP_ZERO = 115.0
MASKED = 1e30
ADAM_LR, ADAM_B1, ADAM_B2, ADAM_EPS, ADAM_WD, ADAM_STEP = 0.001, 0.9, 0.999, 1e-08, 0.01, 10

VMEM_LIMIT = 56 << 20
ROW_TILE = 512
LN_ROW_TILE = 256
OPT_ROW_TILE = 128
SUM_ROW_TILE = 512


def _cp(n_axes):
    return pltpu.CompilerParams(dimension_semantics=("arbitrary",) * n_axes, vmem_limit_bytes=VMEM_LIMIT)


def _sds(shape, dtype):
    return jax.ShapeDtypeStruct(tuple(shape), dtype)


def _round_up(n, m):
    return (n + m - 1) // m * m


def _tile(n, cap, mult=8):
    if n <= cap:
        return n
    best = None
    for d in range(mult, cap + 1, mult):
        if n % d == 0:
            best = d
    assert best is not None, (n, cap)
    return best


_NT = (((1,), (1,)), ((), ()))
_TN = (((0,), (0,)), ((), ()))

_Side = collections.namedtuple("_Side", "ins outs alias sems start finish")


def _any_specs(n):
    return [pl.BlockSpec(memory_space=pl.ANY)] * n


def _call(body, first, last, side, *, grid, in_specs, out_specs, out_shape, scratch_shapes, name, args):
    n_axes = len(grid)
    if side is None:
        res = pl.pallas_call(body, grid=grid, in_specs=in_specs, out_specs=out_specs, out_shape=out_shape,
                             scratch_shapes=scratch_shapes, name=name, compiler_params=_cp(n_axes))(*args)
        return res, ()
    n_in, n_out, n_scr = len(in_specs), len(out_shape), len(scratch_shapes)
    s_in, s_out = len(side.ins), len(side.outs)

    def carried(*refs):
        ins, refs = refs[:n_in], refs[n_in:]
        side_ins, refs = refs[:s_in], refs[s_in:]
        outs, refs = refs[:n_out], refs[n_out:]
        side_outs, refs = refs[:s_out], refs[s_out:]
        scratch, side_sems = refs[:n_scr], refs[n_scr:]

        @pl.when(first())
        def _():
            side.start(side_ins, side_outs, side_sems)

        body(*ins, *outs, *scratch)

        @pl.when(last())
        def _():
            side.finish(side_ins, side_outs, side_sems)

    res = pl.pallas_call(
        carried, grid=grid, in_specs=list(in_specs) + _any_specs(s_in), out_specs=list(out_specs) + _any_specs(s_out),
        out_shape=list(out_shape) + list(side.outs), scratch_shapes=list(scratch_shapes) + list(side.sems),
        input_output_aliases={n_in + a: n_out + b for a, b in side.alias.items()},
        name=name, compiler_params=_cp(n_axes))(*args, *side.ins)
    return res[:n_out], res[n_out:]


def _mm_cols(a, b, out_dtype, name, transposed_b=False, side=None):
    t, k = a.shape
    g = b.shape[0]
    nb = b.shape[1] if transposed_b else b.shape[2]
    tm = _tile(t, ROW_TILE)
    steps = t // tm

    def body(a_ref, b_ref, o_ref):
        if transposed_b:
            acc = lax.dot_general(a_ref[...], b_ref[...], _NT, preferred_element_type=F32)
        else:
            acc = jnp.dot(a_ref[...], b_ref[...], preferred_element_type=F32)
        o_ref[...] = acc.astype(o_ref.dtype)

    first = lambda: jnp.logical_and(pl.program_id(0) == 0, pl.program_id(1) == 0)
    last = lambda: jnp.logical_and(pl.program_id(0) == g - 1, pl.program_id(1) == steps - 1)
    (out,), side_out = _call(
        body, first, last, side, grid=(g, steps),
        in_specs=[pl.BlockSpec((tm, k), lambda gi, i: (i, 0)),
                  pl.BlockSpec((None,) + b.shape[1:], lambda gi, i: (gi, 0, 0))],
        out_specs=[pl.BlockSpec((None, tm, nb), lambda gi, i: (gi, i, 0))],
        out_shape=[_sds((g, t, nb), out_dtype)], scratch_shapes=[], name=name, args=(a, b))
    return out if side is None else (out, side_out)


def _mm_nt_acc(a3, b3, res, kb, name, side=None, b_is_kn=False):
    ga, t, ka = a3.shape
    gb, n, kbb = (b3.shape[0], b3.shape[2], b3.shape[1]) if b_is_kn else b3.shape
    na, nbk = ka // kb, kbb // kb
    groups = ga * na
    assert groups == gb * nbk
    tm = _tile(t, ROW_TILE)
    steps = t // tm

    def body(a_ref, b_ref, res_ref, o_ref, acc):
        u = pl.program_id(1)

        @pl.when(u == 0)
        def _():
            acc[...] = ALPHA * res_ref[...]

        if b_is_kn:
            acc[...] += jnp.dot(a_ref[...], b_ref[...], preferred_element_type=F32)
        else:
            acc[...] += lax.dot_general(a_ref[...], b_ref[...], _NT, preferred_element_type=F32)

        @pl.when(u == groups - 1)
        def _():
            o_ref[...] = acc[...]

    first = lambda: jnp.logical_and(pl.program_id(0) == 0, pl.program_id(1) == 0)
    last = lambda: jnp.logical_and(pl.program_id(0) == steps - 1, pl.program_id(1) == groups - 1)
    if b_is_kn:
        b_spec = pl.BlockSpec((None, kb, n), lambda i, u: (u // nbk, u % nbk, 0))
    else:
        b_spec = pl.BlockSpec((None, n, kb), lambda i, u: (u // nbk, 0, u % nbk))
    (out,), side_out = _call(
        body, first, last, side, grid=(steps, groups),
        in_specs=[pl.BlockSpec((None, tm, kb), lambda i, u: (u // na, i, u % na)), b_spec,
                  pl.BlockSpec((tm, n), lambda i, u: (i, 0))],
        out_specs=[pl.BlockSpec((tm, n), lambda i, u: (i, 0))],
        out_shape=[_sds((t, n), F32)], scratch_shapes=[pltpu.VMEM((tm, n), F32)], name=name, args=(a3, b3, res))
    return out if side is None else (out, side_out)


def _mm_tn(x3, dy3, out_shape, bm, bn, groups, x_idx, dy_idx, out_idx, name):
    t = x3.shape[1]
    tm = _tile(t, 2 * ROW_TILE)

    def body(x_ref, dy_ref, o_ref):
        @pl.when(pl.program_id(2) == 0)
        def _():
            o_ref[...] = jnp.zeros_like(o_ref)

        o_ref[...] += lax.dot_general(x_ref[...], dy_ref[...], _TN, preferred_element_type=F32)

    return pl.pallas_call(
        body, grid=(groups, x_idx[1], t // tm),
        in_specs=[pl.BlockSpec((None, tm, bm), lambda u, mb, i: (x_idx[0](u), i, x_idx[2](u, mb))),
                  pl.BlockSpec((None, tm, bn), lambda u, mb, i: (dy_idx[0](u), i, dy_idx[1](u)))],
        out_specs=pl.BlockSpec((None, bm, bn), lambda u, mb, i: (out_idx[0](u), out_idx[1](u, mb), out_idx[2](u))),
        out_shape=_sds(out_shape, F32), name=name, compiler_params=_cp(3))(x3, dy3)


def _layer_norm_rows(r, gamma, beta):
    mu = jnp.mean(r, axis=-1, keepdims=True)
    xc = r - mu
    var = jnp.mean(xc * xc, axis=-1, keepdims=True)
    return xc * lax.rsqrt(var + LN_EPS) * gamma + beta


def _mm_res_ln(a3, w3, res, gamma, beta, name, side=None):
    g, t, kb = a3.shape
    d = w3.shape[2]
    tm = _tile(t, LN_ROW_TILE)
    steps = t // tm

    def body(a_ref, w_hbm, res_ref, g_ref, b_ref, r_ref, o_ref, ob_ref, w_vmem, sem):
        @pl.when(pl.program_id(0) == 0)
        def _():
            cp = pltpu.make_async_copy(w_hbm, w_vmem, sem)
            cp.start()
            cp.wait()

        acc = ALPHA * res_ref[...]
        for gi in range(g):
            acc = acc + jnp.dot(a_ref[gi], w_vmem[gi], preferred_element_type=F32)
        r_ref[...] = acc
        out = _layer_norm_rows(acc, g_ref[...], b_ref[...])
        o_ref[...] = out
        ob_ref[...] = out.astype(BF16)

    row = pl.BlockSpec((tm, d), lambda i: (i, 0))
    vec = pl.BlockSpec((1, d), lambda i: (0, 0))
    outs, side_out = _call(
        body, lambda: pl.program_id(0) == 0, lambda: pl.program_id(0) == steps - 1, side, grid=(steps,),
        in_specs=[pl.BlockSpec((g, tm, kb), lambda i: (0, i, 0)), pl.BlockSpec(memory_space=pl.ANY), row, vec, vec],
        out_specs=[row, row, row],
        out_shape=[_sds((t, d), F32), _sds((t, d), F32), _sds((t, d), BF16)],
        scratch_shapes=[pltpu.VMEM(w3.shape, w3.dtype), pltpu.SemaphoreType.DMA],
        name=name, args=(a3, w3, res, gamma, beta))
    return outs if side is None else (outs, side_out)


def _ln_bwd(dout, r, gamma, name):
    t, d = r.shape
    tm = _tile(t, ROW_TILE)

    def body(do_ref, r_ref, g_ref, dr_ref, drb_ref, dg_ref, db_ref):
        @pl.when(pl.program_id(0) == 0)
        def _():
            dg_ref[...] = jnp.zeros_like(dg_ref)
            db_ref[...] = jnp.zeros_like(db_ref)

        rr = r_ref[...]
        do = do_ref[...]
        mu = jnp.mean(rr, axis=-1, keepdims=True)
        xc = rr - mu
        rstd = lax.rsqrt(jnp.mean(xc * xc, axis=-1, keepdims=True) + LN_EPS)
        xhat = xc * rstd
        dxh = do * g_ref[...]
        m1 = jnp.mean(dxh, axis=-1, keepdims=True)
        m2 = jnp.mean(dxh * xhat, axis=-1, keepdims=True)
        dr = rstd * (dxh - m1 - xhat * m2)
        dr_ref[...] = dr
        drb_ref[...] = dr.astype(BF16)
        dg_ref[...] += jnp.sum(do * xhat, axis=0, keepdims=True)
        db_ref[...] += jnp.sum(do, axis=0, keepdims=True)

    row = pl.BlockSpec((tm, d), lambda i: (i, 0))
    vec = pl.BlockSpec((1, d), lambda i: (0, 0))
    return pl.pallas_call(
        body, grid=(t // tm,), in_specs=[row, row, vec], out_specs=[row, row, vec, vec],
        out_shape=[_sds((t, d), F32), _sds((t, d), BF16), _sds((1, d), F32), _sds((1, d), F32)],
        name=name, compiler_params=_cp(1))(dout, r, gamma)


def _loss_and_grad(y, target, name):
    t, d = y.shape
    tm = _tile(t, ROW_TILE)
    steps = t // tm

    def body(y_ref, t_ref, loss_ref, dy_ref, acc):
        i = pl.program_id(0)

        @pl.when(i == 0)
        def _():
            acc[...] = jnp.zeros_like(acc)

        diff = y_ref[...] - t_ref[...]
        dy_ref[...] = diff * (1.0 / d)
        acc[...] += jnp.sum(diff * diff, axis=0, keepdims=True)

        @pl.when(i == steps - 1)
        def _():
            total = jnp.sum(acc[...], axis=1, keepdims=True) * (0.5 / d)
            loss_ref[...] = jnp.broadcast_to(total, loss_ref.shape)

    row = pl.BlockSpec((tm, d), lambda i: (i, 0))
    return pl.pallas_call(
        body, grid=(steps,), in_specs=[row, row],
        out_specs=[pl.BlockSpec((1, LANES), lambda i: (0, 0)), row],
        out_shape=[_sds((1, LANES), F32), _sds((t, d), F32)],
        scratch_shapes=[pltpu.VMEM((1, d), F32)], name=name, compiler_params=_cp(1))(y, target)


def _window_sums(ext, window, forward):
    n = ext.shape[0]
    s, span = ext, 1
    while span < window:
        s = s + pltpu.roll(s, (n - span) if forward else span, 0)
        span *= 2
    return s


def _pooled_group(main, halo, gi, row0):
    window = POOL_WINDOWS[gi]
    ext = jnp.concatenate([halo, main], axis=0)
    sums = _window_sums(ext, window, forward=False)[POOL_HALO:, :]
    pos = row0 + lax.broadcasted_iota(jnp.int32, (main.shape[0], 1), 0)
    cnt = jnp.minimum(pos + 1, window).astype(F32)
    return sums / cnt - main


def _pool_specs(t, d, tm):
    per = tm // POOL_HALO
    main = pl.BlockSpec((tm, d), lambda i: (i, 0))
    before = pl.BlockSpec((POOL_HALO, d), lambda i: (jnp.maximum(i * per - 1, 0), 0))
    return main, before


def _pool_fwd(x, w, scale, gamma, beta, name):
    t, d = x.shape
    ng, cg = w.shape[0], w.shape[1]
    tm = _tile(t, LN_ROW_TILE)

    def body(x_ref, h_ref, w_ref, s_ref, g_ref, b_ref, r_ref, o_ref, ob_ref):
        i = pl.program_id(0)
        for gi in range(ng):
            cols = pl.ds(gi * cg, cg)
            main = x_ref[:, cols]
            halo = jnp.where(i > 0, h_ref[:, cols], 0.0)
            pooled = _pooled_group(main, halo, gi, i * tm)
            y = jnp.dot(pooled.astype(BF16), w_ref[gi], preferred_element_type=F32)
            r_ref[:, cols] = ALPHA * main + y * s_ref[:, cols]
        out = _layer_norm_rows(r_ref[...], g_ref[...], b_ref[...])
        o_ref[...] = out
        ob_ref[...] = out.astype(BF16)

    main, before = _pool_specs(t, d, tm)
    vec = pl.BlockSpec((1, d), lambda i: (0, 0))
    return pl.pallas_call(
        body, grid=(t // tm,),
        in_specs=[main, before, pl.BlockSpec(w.shape, lambda i: (0, 0, 0)), vec, vec, vec],
        out_specs=[main, main, main],
        out_shape=[_sds((t, d), F32), _sds((t, d), F32), _sds((t, d), BF16)],
        name=name, compiler_params=_cp(1))(x, x, w, scale, gamma, beta)


def _pool_bwd(x, dy, w, scale, name):
    t, d = x.shape
    ng, cg = w.shape[0], w.shape[1]
    tm = _tile(t, LN_ROW_TILE)

    def body(x_ref, h_ref, dy_ref, w_ref, s_ref, dp_ref, dw_ref, ds_ref):
        i = pl.program_id(0)

        @pl.when(i == 0)
        def _():
            dw_ref[...] = jnp.zeros_like(dw_ref)
            ds_ref[...] = jnp.zeros_like(ds_ref)

        for gi in range(ng):
            cols = pl.ds(gi * cg, cg)
            main = x_ref[:, cols]
            halo = jnp.where(i > 0, h_ref[:, cols], 0.0)
            pooled = _pooled_group(main, halo, gi, i * tm).astype(BF16)
            y = jnp.dot(pooled, w_ref[gi], preferred_element_type=F32)
            dyg = dy_ref[:, cols]
            ds_ref[:, cols] += jnp.sum(dyg * y, axis=0, keepdims=True)
            dyw = (dyg * s_ref[:, cols]).astype(BF16)
            dw_ref[gi] += lax.dot_general(pooled, dyw, _TN, preferred_element_type=F32)
            dp_ref[:, cols] = lax.dot_general(dyw, w_ref[gi], _NT, preferred_element_type=F32)

    main, before = _pool_specs(t, d, tm)
    vec = pl.BlockSpec((1, d), lambda i: (0, 0))
    wspec = pl.BlockSpec(w.shape, lambda i: (0, 0, 0))
    return pl.pallas_call(
        body, grid=(t // tm,), in_specs=[main, before, main, wspec, vec],
        out_specs=[main, wspec, vec],
        out_shape=[_sds((t, d), F32), _sds(w.shape, F32), _sds((1, d), F32)],
        name=name, compiler_params=_cp(1))(x, x, dy, w, scale)


def _pool_adjoint(dp, dres, n_groups, name):
    t, d = dp.shape
    cg = d // n_groups
    tm = _tile(t, ROW_TILE)
    steps = t // tm
    per = tm // POOL_HALO

    def body(dp_ref, after_ref, dres_ref, dx_ref):
        i = pl.program_id(0)
        rows = lax.broadcasted_iota(jnp.int32, (tm, 1), 0)
        rows_after = lax.broadcasted_iota(jnp.int32, (POOL_HALO, 1), 0)
        for gi in range(n_groups):
            window = POOL_WINDOWS[gi]
            cols = pl.ds(gi * cg, cg)
            main = dp_ref[:, cols]
            cnt = jnp.minimum(i * tm + rows + 1, window).astype(F32)
            cnt_after = jnp.minimum((i + 1) * tm + rows_after + 1, window).astype(F32)
            after = jnp.where(i < steps - 1, after_ref[:, cols] / cnt_after, 0.0)
            ext = jnp.concatenate([main / cnt, after], axis=0)
            sums = _window_sums(ext, window, forward=True)[:tm, :]
            dx_ref[:, cols] = ALPHA * dres_ref[:, cols] + sums - main

    main = pl.BlockSpec((tm, d), lambda i: (i, 0))
    after = pl.BlockSpec((POOL_HALO, d), lambda i: (jnp.minimum((i + 1) * per, t // POOL_HALO - 1), 0))
    return pl.pallas_call(
        body, grid=(steps,), in_specs=[main, after, main], out_specs=main,
        out_shape=_sds((t, d), F32), name=name, compiler_params=_cp(1))(dp, dp, dres)


def _split_dot(x, tri):
    hi = x.astype(BF16)
    lo = (x - hi.astype(F32)).astype(BF16)
    return jnp.dot(hi, tri, preferred_element_type=F32) + jnp.dot(lo, tri, preferred_element_type=F32)


def _att_windows(qs, k_ws, limit, carry_rests, suffix):
    heads = range(len(qs))
    zs = [lax.dot_general(qs[hh], k_ws[hh], _NT, preferred_element_type=F32) * ATT_SCALE for hh in heads]
    visible = lax.broadcasted_iota(jnp.int32, zs[0].shape, 1) < limit
    zs = [jnp.where(visible, z, -MASKED) for z in zs]
    es = [jnp.exp(-jnp.abs(z)) for z in zs]
    log_nots = [-(jnp.maximum(z, 0.0) + jnp.log(1.0 + e)) for z, e in zip(zs, es)]
    rests = [_split_dot(ln, suffix[...]) + carry for ln, carry in zip(log_nots, carry_rests)]
    weights = [jnp.exp(z + r) for z, r in zip(zs, rests)]
    return zs, es, log_nots, weights


def _tri(w, strict):
    r = lax.broadcasted_iota(jnp.int32, (w, w), 0)
    c = lax.broadcasted_iota(jnp.int32, (w, w), 1)
    return ((r > c) if strict else (r >= c)).astype(BF16)


def _heads_per_step(qkv3, n_heads, most):
    cpb = qkv3.shape[2] // HEAD_DIM
    hp = most
    while cpb % hp or n_heads % hp:
        hp //= 2
    return hp


def _att_specs(qkv3, n_heads, hp):
    t = qkv3.shape[1]
    cpb = qkv3.shape[2] // HEAD_DIM
    wd = hp * HEAD_DIM

    def slab(off):
        return pl.BlockSpec((None, t, wd), lambda g, i: ((off + g * hp) // cpb, 0, ((off + g * hp) % cpb) // hp))

    q = pl.BlockSpec((None, ATT_BLOCK, wd), lambda g, i: ((g * hp) // cpb, i, ((g * hp) % cpb) // hp))
    return q, slab(n_heads), slab(2 * n_heads)


def _head_cols(hh):
    return pl.ds(hh * HEAD_DIM, HEAD_DIM)


def _key_bounds(k_ref, kmax, hp):
    for hh in range(hp):
        kf = k_ref[:, _head_cols(hh)].astype(F32)
        kmax[hh] = jnp.sqrt(jnp.max(jnp.sum(kf * kf, axis=1, keepdims=True)))


def _score_bound(q, key_norm):
    qf = q.astype(F32)
    return ATT_SCALE * 1.001 * key_norm * jnp.sqrt(jnp.sum(qf * qf, axis=1, keepdims=True)) + 1e-3


def _any_alive(rests, bounds):
    alive = jnp.max(rests[0] + bounds[0]) > -EXP_ZERO
    for r, zb in zip(rests[1:], bounds[1:]):
        alive = jnp.logical_or(alive, jnp.max(r + zb) > -EXP_ZERO)
    return alive


def _window_rows(hi, w):
    start = jnp.maximum(hi - w, 0)
    return start, pl.ds(pl.multiple_of(start, ATT_BLOCK), w)


def _attn_fwd(qkv3, n_heads, name):
    t = qkv3.shape[1]
    b = ATT_BLOCK
    w = min(ATT_WINDOW, t)
    hp = _heads_per_step(qkv3, n_heads, ATT_FWD_HEADS)
    heads = range(hp)

    def body(q_ref, k_ref, v_ref, o_ref, kmax, suffix):
        i = pl.program_id(1)

        @pl.when(i == 0)
        def _():
            _key_bounds(k_ref, kmax, hp)
            suffix[...] = _tri(w, strict=False)

        qs = [q_ref[:, _head_cols(hh)] for hh in heads]
        bounds = [_score_bound(qs[hh], kmax[hh]) for hh in heads]
        qpos = i * b + lax.broadcasted_iota(jnp.int32, (b, 1), 0)

        def cond(c):
            return jnp.logical_and(c[0] > 0, _any_alive(c[1], bounds))

        def step(c):
            hi, rests, accs = c
            start, rows = _window_rows(hi, w)
            limit = jnp.minimum(qpos, hi) - start
            k_ws = [k_ref[rows, _head_cols(hh)] for hh in heads]
            _, _, log_nots, weights = _att_windows(qs, k_ws, limit, rests, suffix)
            new_accs = tuple(accs[hh] + jnp.dot(weights[hh].astype(BF16), v_ref[rows, _head_cols(hh)],
                                                preferred_element_type=F32) for hh in heads)
            new_rests = tuple(rests[hh] + jnp.sum(log_nots[hh], axis=1, keepdims=True) for hh in heads)
            return start, new_rests, new_accs

        init = ((i + 1) * b, tuple(jnp.zeros((b, 1), F32) for _ in heads),
                tuple(jnp.zeros((b, HEAD_DIM), F32) for _ in heads))
        _, _, accs = lax.while_loop(cond, step, init)
        for hh in heads:
            o_ref[:, _head_cols(hh)] = accs[hh].astype(o_ref.dtype)

    qs_, ks_, vs_ = _att_specs(qkv3, n_heads, hp)
    return pl.pallas_call(
        body, grid=(n_heads // hp, t // b), in_specs=[qs_, ks_, vs_],
        out_specs=pl.BlockSpec((b, hp * HEAD_DIM), lambda g, i: (i, g)),
        out_shape=_sds((t, n_heads * HEAD_DIM), BF16),
        scratch_shapes=[pltpu.SMEM((hp,), F32), pltpu.VMEM((w, w), BF16)],
        name=name, compiler_params=_cp(2))(qkv3, qkv3, qkv3)


def _attn_bwd(qkv3, do, n_heads, name):
    t = qkv3.shape[1]
    b = ATT_BLOCK
    w = min(ATT_WINDOW, t)
    nq = t // b
    hp = _heads_per_step(qkv3, n_heads, ATT_BWD_HEADS)
    heads = range(hp)
    wd = hp * HEAD_DIM

    def body(q_ref, k_ref, v_ref, do_ref, dq_ref, dk_ref, dv_ref, kmax, dk_acc, dv_acc, suffix, strict_suffix):
        i = pl.program_id(1)

        @pl.when(i == 0)
        def _():
            _key_bounds(k_ref, kmax, hp)
            dk_acc[...] = jnp.zeros_like(dk_acc)
            dv_acc[...] = jnp.zeros_like(dv_acc)
            suffix[...] = _tri(w, strict=False)
            strict_suffix[...] = _tri(w, strict=True)

        qs = [q_ref[:, _head_cols(hh)] for hh in heads]
        douts = [do_ref[:, _head_cols(hh)] for hh in heads]
        bounds = [_score_bound(qs[hh], kmax[hh]) for hh in heads]
        zero_cols = tuple(jnp.zeros((b, 1), F32) for _ in heads)
        hi0 = (i + 1) * b
        qpos = i * b + lax.broadcasted_iota(jnp.int32, (b, 1), 0)

        def cond(c):
            return jnp.logical_and(c[0] > 0, _any_alive(c[1], bounds))

        def windows(rows, limit, rests):
            k_ws = [k_ref[rows, _head_cols(hh)] for hh in heads]
            dps = [lax.dot_general(douts[hh], v_ref[rows, _head_cols(hh)], _NT, preferred_element_type=F32)
                   for hh in heads]
            zs, es, log_nots, weights = _att_windows(qs, k_ws, limit, rests, suffix)
            dlas = [a * dp for a, dp in zip(weights, dps)]
            return k_ws, zs, es, log_nots, weights, dlas

        def row_sums(carries, tiles):
            return tuple(c + jnp.sum(x, axis=1, keepdims=True) for c, x in zip(carries, tiles))

        def sweep1(c):
            hi, rests, totals = c
            start, rows = _window_rows(hi, w)
            _, _, _, log_nots, weights, dlas = windows(rows, jnp.minimum(qpos, hi) - start, rests)
            for hh in heads:
                dv_acc[rows, _head_cols(hh)] += lax.dot_general(weights[hh].astype(BF16), douts[hh], _TN,
                                                                preferred_element_type=F32)
            return start, row_sums(rests, log_nots), row_sums(totals, dlas)

        _, _, totals = lax.while_loop(cond, sweep1, (hi0, zero_cols, zero_cols))

        def sweep2(c):
            hi, rests, laters, dqs = c
            start, rows = _window_rows(hi, w)
            k_ws, zs, es, log_nots, _, dlas = windows(rows, jnp.minimum(qpos, hi) - start, rests)
            insides = [_split_dot(dla, strict_suffix[...]) for dla in dlas]
            dzs = []
            for hh in heads:
                dlog_not = totals[hh] - laters[hh] - insides[hh]
                inv = 1.0 / (1.0 + es[hh])
                sig = jnp.where(zs[hh] >= 0, inv, es[hh] * inv)
                dzs.append(((dlas[hh] - sig * dlog_not) * ATT_SCALE).astype(BF16))
            new_dqs = tuple(dqs[hh] + jnp.dot(dzs[hh], k_ws[hh], preferred_element_type=F32) for hh in heads)
            for hh in heads:
                dk_acc[rows, _head_cols(hh)] += lax.dot_general(dzs[hh], qs[hh], _TN, preferred_element_type=F32)
            return start, row_sums(rests, log_nots), row_sums(laters, dlas), new_dqs

        init = (hi0, zero_cols, zero_cols, tuple(jnp.zeros((b, HEAD_DIM), F32) for _ in heads))
        _, _, _, dqs = lax.while_loop(cond, sweep2, init)
        for hh in heads:
            dq_ref[:, _head_cols(hh)] = dqs[hh].astype(dq_ref.dtype)

        @pl.when(i == nq - 1)
        def _():
            dk_ref[...] = dk_acc[...].astype(dk_ref.dtype)
            dv_ref[...] = dv_acc[...].astype(dv_ref.dtype)

    qs_, ks_, vs_ = _att_specs(qkv3, n_heads, hp)
    blk = pl.BlockSpec((b, wd), lambda g, i: (i, g))
    slab = pl.BlockSpec((t, wd), lambda g, i: (0, g))
    d = n_heads * HEAD_DIM
    return pl.pallas_call(
        body, grid=(n_heads // hp, nq),
        in_specs=[qs_, ks_, vs_, pl.BlockSpec((None, b, wd), lambda g, i: (0, i, g))],
        out_specs=[blk, slab, slab],
        out_shape=[_sds((t, d), BF16)] * 3,
        scratch_shapes=[pltpu.SMEM((hp,), F32), pltpu.VMEM((t, wd), F32), pltpu.VMEM((t, wd), F32),
                        pltpu.VMEM((w, w), BF16), pltpu.VMEM((w, w), BF16)],
        name=name, compiler_params=_cp(2))(qkv3, qkv3, qkv3, do)


def _conv_rows(main, halo, w_ref, b_ref):
    ext = jnp.concatenate([halo, main], axis=0)
    h1 = pltpu.roll(ext, 1, 0)[CONV_HALO:, :]
    h2 = pltpu.roll(ext, 2, 0)[CONV_HALO:, :]
    hc = b_ref[...] + w_ref[0:1, :] * h2
    hc = hc + w_ref[1:2, :] * h1
    hc = hc + w_ref[2:3, :] * main
    return hc, h1, h2


def _ffn_specs(t, fp, tm, half):
    per = tm // CONV_HALO
    main = lambda off: pl.BlockSpec((None, tm, fp), lambda g, i: (g + off, i, 0))
    before = lambda off: pl.BlockSpec((None, CONV_HALO, fp), lambda g, i: (g + off, jnp.maximum(i * per - 1, 0), 0))
    cw = lambda off: pl.BlockSpec((None, 3, fp), lambda g, i: (g + off, 0, 0))
    cb = lambda off: pl.BlockSpec((None, 1, fp), lambda g, i: (g + off, 0, 0))
    return [main(0), before(0), main(half), before(half), cw(0), cw(half), cb(0), cb(half)]


def _ffn_act(h, cw, cb, name):
    n, t, fp = h.shape
    half = n // 2
    tm = _tile(t, LN_ROW_TILE)

    def body(hg_ref, hgb_ref, hv_ref, hvb_ref, wg_ref, wv_ref, bg_ref, bv_ref, a_ref):
        first = pl.program_id(1) == 0
        gate, _, _ = _conv_rows(hg_ref[...], jnp.where(first, 0.0, hgb_ref[...]), wg_ref, bg_ref)
        val, _, _ = _conv_rows(hv_ref[...], jnp.where(first, 0.0, hvb_ref[...]), wv_ref, bv_ref)
        a_ref[...] = (gate * jax.nn.sigmoid(gate) * val).astype(a_ref.dtype)

    return pl.pallas_call(
        body, grid=(half, t // tm), in_specs=_ffn_specs(t, fp, tm, half),
        out_specs=pl.BlockSpec((None, tm, fp), lambda g, i: (g, i, 0)),
        out_shape=_sds((half, t, fp), BF16), name=name, compiler_params=_cp(2))(h, h, h, h, cw, cw, cb, cb)


def _act_grads(dact, gate, val):
    sig = jax.nn.sigmoid(gate)
    return dact * val * (sig * (1.0 + gate * (1.0 - sig))), dact * (gate * sig)


def _ffn_act_bwd(h, da, cw, cb, name):
    n, t, fp = h.shape
    half = n // 2
    tm = _tile(t, LN_ROW_TILE)

    def body(hg_ref, hgb_ref, hv_ref, hvb_ref, wg_ref, wv_ref, bg_ref, bv_ref, da_ref, dhc_ref, dw_ref, db_ref):
        first = pl.program_id(1) == 0

        @pl.when(first)
        def _():
            dw_ref[...] = jnp.zeros_like(dw_ref)
            db_ref[...] = jnp.zeros_like(db_ref)

        hg, hv = hg_ref[...], hv_ref[...]
        gate, hg1, hg2 = _conv_rows(hg, jnp.where(first, 0.0, hgb_ref[...]), wg_ref, bg_ref)
        val, hv1, hv2 = _conv_rows(hv, jnp.where(first, 0.0, hvb_ref[...]), wv_ref, bv_ref)
        dgate, dval = _act_grads(da_ref[...], gate, val)
        dhc_ref[0] = dgate
        dhc_ref[1] = dval
        for s, (dd, shifted) in enumerate(((dgate, (hg2, hg1, hg)), (dval, (hv2, hv1, hv)))):
            db_ref[s] += jnp.sum(dd, axis=0, keepdims=True)
            for kk in range(3):
                dw_ref[s, kk:kk + 1, :] += jnp.sum(dd * shifted[kk], axis=0, keepdims=True)

    specs = _ffn_specs(t, fp, tm, half) + [pl.BlockSpec((None, tm, fp), lambda g, i: (g, i, 0))]
    return pl.pallas_call(
        body, grid=(half, t // tm), in_specs=specs,
        out_specs=[pl.BlockSpec((2, None, tm, fp), lambda g, i: (0, g, i, 0)),
                   pl.BlockSpec((2, None, 3, fp), lambda g, i: (0, g, 0, 0)),
                   pl.BlockSpec((2, None, 1, fp), lambda g, i: (0, g, 0, 0))],
        out_shape=[_sds((2, half, t, fp), F32), _sds((2, half, 3, fp), F32), _sds((2, half, 1, fp), F32)],
        name=name, compiler_params=_cp(2))(h, h, h, h, cw, cw, cb, cb, da)


def _conv_adjoint(dhc, cw, name):
    n, t, fp = dhc.shape
    tm = _tile(t, ROW_TILE)
    steps = t // tm
    per = tm // CONV_HALO

    def body(d_ref, after_ref, w_ref, o_ref):
        main = d_ref[...]
        after = jnp.where(pl.program_id(1) < steps - 1, after_ref[...], 0.0)
        ext = jnp.concatenate([main, after], axis=0)
        rows = ext.shape[0]
        d1 = pltpu.roll(ext, rows - 1, 0)[:tm, :]
        d2 = pltpu.roll(ext, rows - 2, 0)[:tm, :]
        o_ref[...] = (w_ref[2:3, :] * main + w_ref[1:2, :] * d1 + w_ref[0:1, :] * d2).astype(o_ref.dtype)

    main = pl.BlockSpec((None, tm, fp), lambda g, i: (g, i, 0))
    after = pl.BlockSpec((None, CONV_HALO, fp), lambda g, i: (g, jnp.minimum((i + 1) * per, t // CONV_HALO - 1), 0))
    return pl.pallas_call(
        body, grid=(n, steps), in_specs=[main, after, pl.BlockSpec((None, 3, fp), lambda g, i: (g, 0, 0))],
        out_specs=main, out_shape=_sds((n, t, fp), BF16), name=name, compiler_params=_cp(2))(dhc, dhc, cw)


def _place():
    x, y, c = lax.axis_index("x"), lax.axis_index("y"), lax.axis_index("c")
    chips = [(1 - x, y), (x, 1 - y), (1 - x, 1 - y)]
    return x, y, c, chips


def _run_sides(sides, name):
    n_in = [len(s.ins) for s in sides]
    n_out = [len(s.outs) for s in sides]
    n_sem = [len(s.sems) for s in sides]

    def body(*refs):
        ins, outs, sems = refs[:sum(n_in)], refs[sum(n_in):sum(n_in) + sum(n_out)], refs[sum(n_in) + sum(n_out):]
        oi = oo = os_ = 0
        for k, s in enumerate(sides):
            mine = (ins[oi:oi + n_in[k]], outs[oo:oo + n_out[k]], sems[os_:os_ + n_sem[k]])
            s.start(*mine)
            s.finish(*mine)
            oi, oo, os_ = oi + n_in[k], oo + n_out[k], os_ + n_sem[k]

    aliases, oi, oo = {}, 0, 0
    for k, s in enumerate(sides):
        aliases.update({oi + a: oo + b for a, b in s.alias.items()})
        oi, oo = oi + n_in[k], oo + n_out[k]
    return pl.pallas_call(
        body, in_specs=_any_specs(sum(n_in)), out_specs=_any_specs(sum(n_out)),
        out_shape=[o for s in sides for o in s.outs], input_output_aliases=aliases,
        scratch_shapes=[q for s in sides for q in s.sems], name=name)(*[a for s in sides for a in s.ins])


def _place_shard(kind, w, chip, name, rows=None, base=None):
    if kind == "pool":
        g, r, cdim = w.shape

        def body(chip_ref, w_ref, o_ref):
            del chip_ref
            o_ref[...] = w_ref[...].astype(BF16)

        return pl.pallas_call(
            body,
            grid_spec=pltpu.PrefetchScalarGridSpec(
                num_scalar_prefetch=1, grid=(1,),
                in_specs=[pl.BlockSpec((g, r, cdim), lambda i, chip_ref: (0, 0, 0))],
                out_specs=pl.BlockSpec((g, r, cdim), lambda i, chip_ref: (0, chip_ref[0], 0))),
            out_shape=_sds((g, 4 * r, cdim), BF16), name=name, compiler_params=_cp(1))(chip, w)

    r, cs = w.shape
    if kind == "lead":
        rows = rows or r
        tr = _tile(r, ROW_TILE, 16) if rows == r else rows - r
        assert r % tr == 0 and tr % 16 == 0
        n_src = r // tr

        def body(chip_ref, w_ref, o_ref):
            del chip_ref
            o_ref[...] = jnp.where(pl.program_id(0) < n_src, w_ref[...], 0.0).astype(BF16)

        return pl.pallas_call(
            body,
            grid_spec=pltpu.PrefetchScalarGridSpec(
                num_scalar_prefetch=1, grid=(rows // tr,),
                in_specs=[pl.BlockSpec((tr, cs), lambda i, chip_ref: (jnp.minimum(i, n_src - 1), 0))],
                out_specs=pl.BlockSpec((None, tr, cs), lambda i, chip_ref: (chip_ref[0], i, 0))),
            out_shape=_sds((4, rows, cs), BF16), name=name, compiler_params=_cp(1))(chip, w)

    assert kind == "down"
    tr = r // 2 if (r // 2) % 16 == 0 else r
    per = r // tr

    def body(chip_ref, w_ref, base_ref, o_ref):
        del chip_ref, base_ref
        o_ref[...] = w_ref[...].astype(BF16)

    return pl.pallas_call(
        body,
        grid_spec=pltpu.PrefetchScalarGridSpec(
            num_scalar_prefetch=1, grid=(per,),
            in_specs=[pl.BlockSpec((tr, cs), lambda i, chip_ref: (i, 0)), pl.BlockSpec(memory_space=pl.ANY)],
            out_specs=pl.BlockSpec((None, tr, cs), lambda i, chip_ref: (chip_ref[0] // 2, (chip_ref[0] % 2) * per + i, 0))),
        out_shape=_sds(base.shape, BF16), input_output_aliases={2: 0},
        name=name, compiler_params=_cp(1))(chip, w, base)


def _gather_sides(items, bufs=None):
    n = len(items)
    kinds = [it[0] for it in items]
    shard_rows = [it[2] for it in items]
    bufs = [it[1] for it in items] if bufs is None else list(bufs)

    def half_of(outs, m, chip, half):
        k = 2 * chip[0] + chip[1]
        o, r = outs[m], shard_rows[m]
        if kinds[m] == "pool":
            gh = o.shape[0] // 2
            return o.at[pl.ds(half * gh, gh), pl.ds(k * r, r)]
        r2 = r // 2
        if kinds[m] == "down":
            return o.at[k // 2, pl.ds((k % 2) * r + half * r2, r2)]
        return o.at[k, pl.ds(half * r2, r2)]

    def remote(outs, sems, m, j, chip, half, to):
        ref = half_of(outs, m, chip, half)
        return pltpu.make_async_remote_copy(src_ref=ref, dst_ref=ref, send_sem=sems[0].at[m, j],
                                            recv_sem=sems[1].at[m, j], device_id=to, device_id_type=MESH)

    def ici_copies(outs, sems, sending):
        x, y, c, chips = _place()
        if sending:
            return [remote(outs, sems, m, j, (x, y), c, (*chip, c)) for m in range(n) for j, chip in enumerate(chips)]
        return [remote(outs, sems, m, j, chip, c, (x, y, c)) for m in range(n) for j, chip in enumerate(chips)]

    def d2d_copies(outs, sems, sending):
        x, y, c, chips = _place()
        if sending:
            return [remote(outs, sems, m, j, chip, c, (x, y, 1 - c)) for m in range(n) for j, chip in enumerate(chips)]
        return [remote(outs, sems, m, j, chip, 1 - c, (x, y, c)) for m in range(n) for j, chip in enumerate(chips)]

    def phase(copies):
        def start(ins, outs, sems):
            for cp in copies(outs, sems, True):
                cp.start()

        def finish(ins, outs, sems):
            for cp in copies(outs, sems, False):
                cp.wait_recv()
            for cp in copies(outs, sems, True):
                cp.wait_send()

        return start, finish

    ici, d2d = phase(ici_copies), phase(d2d_copies)

    def both_finish(ins, outs, sems):
        ici[1](ins, outs, sems[:2])
        d2d[0](ins, outs, sems[2:])
        d2d[1](ins, outs, sems[2:])

    pair = [pltpu.SemaphoreType.DMA((n, 3)), pltpu.SemaphoreType.DMA((n, 3))]
    shapes = [_sds(b.shape, b.dtype) for b in bufs]
    alias = {m: m for m in range(n)}

    def side(which):
        if which == "both":
            return _Side(bufs, shapes, alias, pair + pair, lambda i, o, s: ici[0](i, o, s[:2]), both_finish)
        start, finish = ici if which == "ici" else d2d
        return _Side(bufs, shapes, alias, pair, start, finish)

    return side


def _sibling_side(grads):
    n = len(grads)

    def copies(ins, outs, sems):
        x, y, c, _ = _place()
        res = []
        for m in range(n):
            r2 = ins[m].shape[1] // 2
            res.append(pltpu.make_async_remote_copy(
                src_ref=ins[m].at[:, pl.ds((1 - c) * r2, r2)], dst_ref=outs[m],
                send_sem=sems[0].at[m], recv_sem=sems[1].at[m], device_id=(x, y, 1 - c), device_id_type=MESH))
        return res

    def start(ins, outs, sems):
        for cp in copies(ins, outs, sems):
            cp.start()

    def finish(ins, outs, sems):
        for cp in copies(ins, outs, sems):
            cp.wait_recv()
        for cp in copies(ins, outs, sems):
            cp.wait_send()

    return _Side(list(grads), [_sds((4, g.shape[1] // 2, g.shape[2]), g.dtype) for g in grads], {},
                 [pltpu.SemaphoreType.DMA((n,)), pltpu.SemaphoreType.DMA((n,))], start, finish)


def _owner_chips_side(parts):
    n = len(parts)

    def copies(ins, outs, sems):
        _, _, c, chips = _place()
        return [pltpu.make_async_remote_copy(
            src_ref=ins[m].at[2 * chip[0] + chip[1]], dst_ref=outs[m].at[j], send_sem=sems[0].at[m, j],
            recv_sem=sems[1].at[m, j], device_id=(*chip, c), device_id_type=MESH)
            for m in range(n) for j, chip in enumerate(chips)]

    def start(ins, outs, sems):
        for cp in copies(ins, outs, sems):
            cp.start()

    def finish(ins, outs, sems):
        for cp in copies(ins, outs, sems):
            cp.wait_recv()
        for cp in copies(ins, outs, sems):
            cp.wait_send()

    return _Side(list(parts), [_sds((3,) + p.shape[1:], p.dtype) for p in parts], {},
                 [pltpu.SemaphoreType.DMA((n, 3)), pltpu.SemaphoreType.DMA((n, 3))], start, finish)


def _exchange_finished_halves(shards, name):
    n = len(shards)

    def body(*refs):
        out = refs[n:2 * n]
        send_sems, recv_sems = refs[2 * n:]
        x, y, c, _ = _place()
        copies = []
        for m in range(n):
            r2 = out[m].shape[0] // 2
            mine = out[m].at[pl.ds(c * r2, r2)]
            copies.append(pltpu.make_async_remote_copy(
                src_ref=mine, dst_ref=mine, send_sem=send_sems.at[m], recv_sem=recv_sems.at[m],
                device_id=(x, y, 1 - c), device_id_type=MESH))
        for cp in copies:
            cp.start()
        for m in range(n):
            r2 = out[m].shape[0] // 2
            theirs = out[m].at[pl.ds((1 - c) * r2, r2)]
            pltpu.make_async_remote_copy(
                src_ref=theirs, dst_ref=theirs, send_sem=send_sems.at[m], recv_sem=recv_sems.at[m],
                device_id=(x, y, 1 - c), device_id_type=MESH).wait_recv()
        for cp in copies:
            cp.wait_send()

    return pl.pallas_call(
        body, in_specs=_any_specs(n), out_specs=_any_specs(n), out_shape=[_sds(s.shape, s.dtype) for s in shards],
        input_output_aliases={m: m for m in range(n)},
        scratch_shapes=[pltpu.SemaphoreType.DMA((n,)), pltpu.SemaphoreType.DMA((n,))], name=name)(*shards)


def _all_reduce_small(v, name):
    rows = v.shape[0]

    def body(v_ref, out_ref, buf, send_sems, recv_sems, local_sem):
        x, y, c, chips = _place()
        me, sibling = (x, y, c), (x, y, 1 - c)

        def slot(px, py, pc):
            return buf.at[4 * px + 2 * py + pc]

        def copy(k, block, to, src=None):
            return pltpu.make_async_remote_copy(
                src_ref=slot(*block) if src is None else src, dst_ref=slot(*block),
                send_sem=send_sems.at[k], recv_sem=recv_sems.at[k], device_id=to, device_id_type=MESH)

        mine = pltpu.make_async_copy(v_ref, slot(*me), local_sem)
        mine.start()
        first = [copy(0, me, sibling, src=v_ref)]
        first += [copy(1 + j, me, (*chip, c), src=v_ref) for j, chip in enumerate(chips)]
        for cp in first:
            cp.start()
        passed = [copy(4 + j, (*chip, c), sibling) for j, chip in enumerate(chips)]
        for j, chip in enumerate(chips):
            copy(1 + j, (*chip, c), me).wait_recv()
            passed[j].start()
        copy(0, sibling, me).wait_recv()
        for j, chip in enumerate(chips):
            copy(4 + j, (*chip, 1 - c), me).wait_recv()
        for cp in first + passed:
            cp.wait_send()
        mine.wait()
        total = buf[0]
        for dev in range(1, 8):
            total = total + buf[dev]
        out_ref[...] = total

    vm = pl.BlockSpec(memory_space=pltpu.VMEM)
    return pl.pallas_call(
        body, in_specs=[vm], out_specs=vm, out_shape=_sds(v.shape, F32),
        scratch_shapes=[pltpu.VMEM((8, rows, LANES), F32), pltpu.SemaphoreType.DMA((7,)),
                        pltpu.SemaphoreType.DMA((7,)), pltpu.SemaphoreType.DMA],
        name=name, compiler_params=pltpu.CompilerParams(vmem_limit_bytes=VMEM_LIMIT))(v)


def _chip_partial(grad, from_sibling, core, name):
    _, r, cdim = grad.shape
    r2 = r // 2
    tr = _tile(r2, SUM_ROW_TILE)
    per = r2 // tr

    def body(core_ref, g_ref, s_ref, o_ref, ob_ref):
        del core_ref
        total = g_ref[...] + s_ref[...]
        o_ref[...] = total
        ob_ref[...] = total.astype(BF16)

    blk = pl.BlockSpec((None, tr, cdim), lambda k, i, core_ref: (k, i, 0))
    mine = pl.BlockSpec((None, tr, cdim), lambda k, i, core_ref: (k, core_ref[0] * per + i, 0))
    return pl.pallas_call(
        body,
        grid_spec=pltpu.PrefetchScalarGridSpec(num_scalar_prefetch=1, grid=(4, per), in_specs=[mine, blk],
                                               out_specs=[blk, blk]),
        out_shape=[_sds((4, r2, cdim), F32), _sds((4, r2, cdim), BF16)],
        name=name, compiler_params=_cp(2))(core, grad, from_sibling)


def _owner_sum(partial, from_chips, place, name):
    _, r2, cdim = partial.shape
    tr = _tile(r2, SUM_ROW_TILE)
    per = r2 // tr

    def body(place_ref, p_ref, f_ref, o_ref):
        del place_ref
        total = p_ref[...]
        for j in range(3):
            total = total + f_ref[j].astype(F32)
        o_ref[...] = total

    return pl.pallas_call(
        body,
        grid_spec=pltpu.PrefetchScalarGridSpec(
            num_scalar_prefetch=1, grid=(per,),
            in_specs=[pl.BlockSpec((None, tr, cdim), lambda i, place_ref: (place_ref[0], i, 0)),
                      pl.BlockSpec((3, tr, cdim), lambda i, place_ref: (0, i, 0))],
            out_specs=pl.BlockSpec((tr, cdim), lambda i, place_ref: (place_ref[1] * per + i, 0))),
        out_shape=_sds((2 * r2, cdim), F32), name=name, compiler_params=_cp(1))(place, partial, from_chips)


def _adamw(g, w, m, v, layer, prev, name):
    _, r, cdim = w.shape
    tr = _tile(r, OPT_ROW_TILE)
    c1 = 1.0 / (1.0 - ADAM_B1 ** ADAM_STEP)
    c2 = 1.0 / (1.0 - ADAM_B2 ** ADAM_STEP)
    n_prev = 0 if prev is None else 4

    def body(g_ref, w_ref, m_ref, v_ref, *rest):
        go_ref, d_ref, mo_ref, vo_ref = rest[n_prev:]
        grad = g_ref[:, pl.ds(0, cdim)]
        m_new = ADAM_B1 * m_ref[...] + (1.0 - ADAM_B1) * grad
        v_new = ADAM_B2 * v_ref[...] + (1.0 - ADAM_B2) * (grad * grad)
        go_ref[...] = grad
        mo_ref[...] = m_new
        vo_ref[...] = v_new
        d_ref[...] = -ADAM_LR * ((m_new * c1) / (jnp.sqrt(v_new * c2) + ADAM_EPS) + ADAM_WD * w_ref[...])

    blk = pl.BlockSpec((None, tr, cdim), lambda i: (layer, i, 0))
    gblk = pl.BlockSpec((tr, g.shape[1]), lambda i: (i, 0))
    return pl.pallas_call(
        body, grid=(r // tr,), in_specs=[gblk, blk, blk, blk] + _any_specs(n_prev), out_specs=[blk] * 4,
        out_shape=[_sds(w.shape, F32)] * 4, input_output_aliases={4 + k: k for k in range(n_prev)},
        name=name, compiler_params=_cp(1))(g, w, m, v, *(prev or ()))


def _pack_rows(vectors):
    flat = [v.reshape(-1) for v in vectors]
    sizes = [f.shape[0] for f in flat]
    total = sum(sizes)
    padded = _round_up(total, 8 * LANES)
    buf = jnp.concatenate(flat + [jnp.zeros((padded - total,), F32)])
    return buf.reshape(padded // LANES, LANES), sizes


def _unpack_rows(buf, sizes, shapes):
    flat = buf.reshape(-1)
    out, off = [], 0
    for n, shp in zip(sizes, shapes):
        out.append(flat[off:off + n].reshape(shp))
        off += n
    return out


def kernel(x, pool_w, pool_scale, attn_w_qkv, attn_w_o, ffn_w_up, ffn_conv_w, ffn_conv_b, ffn_w_down, ln_mix_g, ln_mix_b, ln_ffn_g, ln_ffn_b, loss_target, m_pool_w, m_pool_scale, m_attn_w_qkv, m_attn_w_o, m_ffn_w_up, m_ffn_conv_w, m_ffn_conv_b, m_ffn_w_down, m_ln_mix_g, m_ln_mix_b, m_ln_ffn_g, m_ln_ffn_b, v_pool_w, v_pool_scale, v_attn_w_qkv, v_attn_w_o, v_ffn_w_up, v_ffn_conv_w, v_ffn_conv_b, v_ffn_w_down, v_ln_mix_g, v_ln_mix_b, v_ln_ffn_g, v_ln_ffn_b):
    t, d = x.shape[1], x.shape[2]
    n_heads = d // HEAD_DIM
    n_groups = pool_w.shape[1]
    fs = ffn_w_up.shape[2]
    fp = _round_up(fs, LANES)
    rd = ffn_w_down.shape[1]
    assert 2 * rd == fs
    xi, yi, ci = lax.axis_index("x"), lax.axis_index("y"), lax.axis_index("c")
    chip = (2 * xi + yi).astype(jnp.int32)
    chip_arr, core_arr = chip.reshape(1), ci.astype(jnp.int32).reshape(1)
    place_arr = jnp.concatenate([chip_arr, core_arr])

    x2 = x.reshape(t, d)
    target = loss_target.reshape(t, d)
    pad_cols = lambda a: jnp.pad(a, [(0, 0)] * (a.ndim - 1) + [(0, fp - fs)])
    up_t = [jnp.transpose(a, (0, 2, 1)) for a in (ffn_w_up, m_ffn_w_up, v_ffn_w_up)]

    gather_items = []
    for i in range(DEPTH):
        j = i // 2
        items = []
        if i % 2 == 0:
            items.append(("pool", _place_shard("pool", pool_w[j], chip_arr, name="place_pool"), pool_w.shape[2]))
        else:
            items.append(("lead", _place_shard("lead", attn_w_qkv[j], chip_arr, name="place_qkv"), d))
            items.append(("lead", _place_shard("lead", attn_w_o[j], chip_arr, name="place_wo"), attn_w_o.shape[1]))
        items.append(("lead", _place_shard("lead", up_t[0][i], chip_arr, name="place_up", rows=fp), fp))
        items.append(("down", _place_shard("down", ffn_w_down[i], chip_arr, name="place_down",
                                           base=jnp.zeros((2, fp, d), BF16)), rd))
        gather_items.append(items)
    weights = [None] * DEPTH
    weights[0] = _run_sides([_gather_sides(gather_items[0])("both")], name="gather_layer0")

    conv_b_all = pad_cols(ffn_conv_b.reshape(DEPTH, 4, 1, fs))
    cw_local = pad_cols(ffn_conv_w)
    slot = (jnp.arange(4, dtype=jnp.int32) == chip).astype(F32) * (1.0 - ci.astype(F32))
    cw_placed = slot[None, :, None, None] * cw_local[:, None]
    cw_buf, cw_sizes = _pack_rows([cw_placed])
    conv_w_all = _unpack_rows(_all_reduce_small(cw_buf, name="gather_conv_w"), cw_sizes, [cw_placed.shape])[0]

    gam = lambda a, i: a[i].reshape(1, d)

    saved = []
    cur, cur_b = x2, x2.astype(BF16)
    for i in range(DEPTH):
        j = i // 2
        w = weights[i]
        s = {"x_in": cur, "x_in_b": cur_b}
        if i % 2 == 0:
            w_pool, w_up, w_down = w
            s["scale"] = pool_scale[j].reshape(1, d)
            r1, x1, x1b = _pool_fwd(cur, w_pool, s["scale"], gam(ln_mix_g, i), gam(ln_mix_b, i), name="pool_fwd")
        else:
            w_qkv, w_o, w_up, w_down = w
            w_o3 = w_o.reshape(1, d, d)
            qkv = _mm_cols(cur_b, w_qkv, BF16, name="qkv_proj")
            o = _attn_fwd(qkv, n_heads, name="attn_fwd")
            s["qkv"], s["o"], s["w_o3"] = qkv, o, w_o3
            r1, x1, x1b = _mm_res_ln(o.reshape(1, t, d), w_o3, cur, gam(ln_mix_g, i), gam(ln_mix_b, i),
                                     name="attn_out_ln")
        if i + 1 < DEPTH:
            nxt = gather_items[i + 1]
            h, landed = _mm_cols(x1b, w_up, F32, name="ffn_up", transposed_b=True, side=_gather_sides(nxt)("ici"))
            a = _ffn_act(h, conv_w_all[i], conv_b_all[i], name="ffn_act")
            (r2, x2n, x2b), gathered = _mm_res_ln(a, w_down, x1, gam(ln_ffn_g, i), gam(ln_ffn_b, i),
                                                  name="ffn_down_ln", side=_gather_sides(nxt, landed)("d2d"))
            weights[i + 1] = list(gathered)
        else:
            h = _mm_cols(x1b, w_up, F32, name="ffn_up", transposed_b=True)
            a = _ffn_act(h, conv_w_all[i], conv_b_all[i], name="ffn_act")
            r2, x2n, x2b = _mm_res_ln(a, w_down, x1, gam(ln_ffn_g, i), gam(ln_ffn_b, i), name="ffn_down_ln")
        s.update(r1=r1, x1b=x1b, h=h, a=a, r2=r2)
        saved.append(s)
        cur, cur_b = x2n, x2b

    loss_row, dcur = _loss_and_grad(cur, target, name="loss")
    loss = lax.psum(loss_row[0, 0], ("x", "y", "c"))

    big_grads = [None] * DEPTH
    reduced = [None] * DEPTH
    small = {}

    def finish_reduce(parts, from_chips, layer):
        halves = [_owner_sum(p[0], fc, place_arr, name="reduce_owner_sum") for p, fc in zip(parts, from_chips)]
        return _exchange_finished_halves(halves, name="reduce_halves_pool" if layer % 2 == 0 else "reduce_halves_attn")

    for i in reversed(range(DEPTH)):
        j = i // 2
        s, w = saved[i], weights[i]
        w_up, w_down = w[-2], w[-1]
        dr2, dr2b, small["ln_ffn_g", i], small["ln_ffn_b", i] = _ln_bwd(dcur, s["r2"], gam(ln_ffn_g, i), name="ln_bwd")
        pending = big_grads[i + 1] if i + 1 < DEPTH else None
        if pending is not None:
            da, from_sib = _mm_cols(dr2b, w_down, F32, name="ffn_down_bwd_act", transposed_b=True,
                                    side=_sibling_side(pending))
            parts = [_chip_partial(g, fs_, core_arr, name="reduce_chip_partial") for g, fs_ in zip(pending, from_sib)]
        else:
            da = _mm_cols(dr2b, w_down, F32, name="ffn_down_bwd_act", transposed_b=True)
        dr2b3 = dr2b.reshape(1, t, d)
        nmb = 2
        d_down = _mm_tn(s["a"], dr2b3, (2, fp, d), fp // nmb, d, 2,
                        (lambda u: u, nmb, lambda u, mb: mb), (lambda u: 0, lambda u: 0),
                        (lambda u: u, lambda u, mb: mb, lambda u: 0), name="ffn_down_bwd_w")
        dhc, dcw, dcb = _ffn_act_bwd(s["h"], da, conv_w_all[i], conv_b_all[i], name="ffn_act_bwd")
        small["conv_w", i], small["conv_b", i] = dcw, dcb
        dh = _conv_adjoint(dhc.reshape(4, t, fp), conv_w_all[i], name="ffn_conv_adjoint")
        if pending is not None:
            dx1, from_chips = _mm_nt_acc(dh, w_up, dr2, fp, name="ffn_up_bwd_act", b_is_kn=True,
                                         side=_owner_chips_side([p[1] for p in parts]))
            reduced[i + 1] = finish_reduce(parts, from_chips, i + 1)
        else:
            dx1 = _mm_nt_acc(dh, w_up, dr2, fp, name="ffn_up_bwd_act", b_is_kn=True)
        d_up = _mm_tn(dh, s["x1b"].reshape(1, t, d), (4, fp, d), fp // 2, d, 4,
                      (lambda u: u, 2, lambda u, mb: mb), (lambda u: 0, lambda u: 0),
                      (lambda u: u, lambda u, mb: mb, lambda u: 0), name="ffn_up_bwd_w")
        dr1, dr1b, small["ln_mix_g", i], small["ln_mix_b", i] = _ln_bwd(dx1, s["r1"], gam(ln_mix_g, i), name="ln_bwd")
        d_down4 = d_down[:, :fs].reshape(4, rd, d)
        if i % 2 == 0:
            dp, d_pool, small["pool_scale", j] = _pool_bwd(s["x_in"], dr1, w[0], s["scale"], name="pool_bwd")
            dcur = _pool_adjoint(dp, dr1, n_groups, name="pool_adjoint")
            cg = d // n_groups
            d_pool4 = d_pool.reshape(n_groups, 4, cg // 4, cg).transpose(1, 0, 2, 3).reshape(4, n_groups * (cg // 4), cg)
            big_grads[i] = [d_pool4, d_up, d_down4]
        else:
            w_qkv = w[0]
            do = _mm_cols(dr1b, s["w_o3"], BF16, name="attn_out_bwd_act", transposed_b=True)
            d_wo = _mm_tn(s["o"].reshape(1, t, d), dr1b.reshape(1, t, d), (1, d, d), d // 2, d, 1,
                          (lambda u: 0, 2, lambda u, mb: mb), (lambda u: 0, lambda u: 0),
                          (lambda u: 0, lambda u, mb: mb, lambda u: 0), name="attn_out_bwd_w")
            dq, dk, dv = _attn_bwd(s["qkv"], do, n_heads, name="attn_bwd")
            dqkv = jnp.stack([dq, dk, dv])
            cq = w_qkv.shape[2]
            kb = cq // 3
            na, nbk = d // kb, cq // kb
            dcur = _mm_nt_acc(dqkv, w_qkv, dr1, kb, name="qkv_bwd_act")
            d_qkv = _mm_tn(s["x_in_b"].reshape(1, t, d), dqkv, (4, d, cq), d // 2, kb, 3 * na,
                           (lambda u: 0, 2, lambda u, mb: mb), (lambda u: u // na, lambda u: u % na),
                           (lambda u: u // nbk, lambda u, mb: mb, lambda u: u % nbk), name="qkv_bwd_w")
            big_grads[i] = [d_qkv, d_wo.reshape(4, d // 4, d), d_up, d_down4]
    grad_x = dcur.reshape(1, t, d)

    from_sib = _run_sides([_sibling_side(big_grads[0])], name="reduce_layer0_sibling")
    parts = [_chip_partial(g, fs_, core_arr, name="reduce_chip_partial") for g, fs_ in zip(big_grads[0], from_sib)]
    from_chips = _run_sides([_owner_chips_side([p[1] for p in parts])], name="reduce_layer0_chips")
    reduced[0] = finish_reduce(parts, from_chips, 0)

    names = [("pool_scale", j) for j in range(2)]
    for nm in ("ln_mix_g", "ln_mix_b", "ln_ffn_g", "ln_ffn_b", "conv_b", "conv_w"):
        names += [(nm, i) for i in range(DEPTH)]
    vecs = [small[k] for k in names]
    sbuf, ssizes = _pack_rows(vecs)
    summed = dict(zip(names, _unpack_rows(_all_reduce_small(sbuf, name="reduce_small"), ssizes, [v.shape for v in vecs])))

    def stack_layers(nm, count):
        return jnp.stack([summed[nm, i] for i in range(count)])

    g_small = {
        "pool_scale": stack_layers("pool_scale", 2).reshape(2, d),
        "ln_mix_g": stack_layers("ln_mix_g", DEPTH).reshape(DEPTH, d),
        "ln_mix_b": stack_layers("ln_mix_b", DEPTH).reshape(DEPTH, d),
        "ln_ffn_g": stack_layers("ln_ffn_g", DEPTH).reshape(DEPTH, d),
        "ln_ffn_b": stack_layers("ln_ffn_b", DEPTH).reshape(DEPTH, d),
        "conv_b": stack_layers("conv_b", DEPTH).reshape(DEPTH, 4, fp)[:, :, :fs].reshape(DEPTH, 4 * fs),
        "conv_w": lax.dynamic_index_in_dim(stack_layers("conv_w", DEPTH).reshape(DEPTH, 4, 3, fp), chip, axis=1,
                                           keepdims=False)[:, :, :fs],
    }
    w_small = {"pool_scale": (pool_scale, m_pool_scale, v_pool_scale), "ln_mix_g": (ln_mix_g, m_ln_mix_g, v_ln_mix_g),
               "ln_mix_b": (ln_mix_b, m_ln_mix_b, v_ln_mix_b), "ln_ffn_g": (ln_ffn_g, m_ln_ffn_g, v_ln_ffn_g),
               "ln_ffn_b": (ln_ffn_b, m_ln_ffn_b, v_ln_ffn_b), "conv_b": (ffn_conv_b, m_ffn_conv_b, v_ffn_conv_b),
               "conv_w": (ffn_conv_w, m_ffn_conv_w, v_ffn_conv_w)}
    order = list(g_small)
    packs = [_pack_rows([g_small[k] for k in order])[0]]
    for idx in range(3):
        packs.append(_pack_rows([w_small[k][idx] for k in order])[0])
    small_sizes = _pack_rows([g_small[k] for k in order])[1]
    small_out = _adamw(packs[0], packs[1][None], packs[2][None], packs[3][None], 0, None, name="adamw_small")
    shapes = [g_small[k].shape for k in order]
    small_res = {k: [] for k in order}
    for arr in small_out:
        for k, val in zip(order, _unpack_rows(arr[0], small_sizes, shapes)):
            small_res[k].append(val)

    def opt_layers(per_layer_grads, w_all, m_all, v_all, name, rows=None):
        n_layers = w_all.shape[0]
        flat = [a.reshape(n_layers, rows or a.shape[1], -1) for a in (w_all, m_all, v_all)]
        res = None
        for li, g in enumerate(per_layer_grads):
            res = _adamw(g, *flat, li, res, name=name)
        return [o.reshape(w_all.shape) for o in res]

    cg = d // n_groups
    big = {
        "pool_w": opt_layers([reduced[i][0] for i in (0, 2)], pool_w, m_pool_w, v_pool_w, "adamw_pool",
                             rows=n_groups * (cg // 4)),
        "attn_w_qkv": opt_layers([reduced[i][0] for i in (1, 3)], attn_w_qkv, m_attn_w_qkv, v_attn_w_qkv, "adamw_qkv"),
        "attn_w_o": opt_layers([reduced[i][1] for i in (1, 3)], attn_w_o, m_attn_w_o, v_attn_w_o, "adamw_wo"),
        "ffn_w_up": [jnp.transpose(o, (0, 2, 1))
                     for o in opt_layers([reduced[i][-2] for i in range(DEPTH)], *up_t, "adamw_up")],
        "ffn_w_down": opt_layers([reduced[i][-1] for i in range(DEPTH)], ffn_w_down, m_ffn_w_down, v_ffn_w_down,
                                 "adamw_down"),
    }

    def leaf(k, name):
        if name in big:
            return big[name][k]
        key = {"ffn_conv_w": "conv_w", "ffn_conv_b": "conv_b"}.get(name, name)
        return small_res[key][k]

    weight_names = ["pool_w", "pool_scale", "attn_w_qkv", "attn_w_o", "ffn_w_up", "ffn_conv_w", "ffn_conv_b",
                    "ffn_w_down", "ln_mix_g", "ln_mix_b", "ln_ffn_g", "ln_ffn_b"]
    outs = [loss, grad_x]
    for k in range(4):
        outs += [leaf(k, nm) for nm in weight_names]
    return tuple(outs)
```

```python
import collections

import jax
import jax.numpy as jnp
from jax import lax
from jax.experimental import pallas as pl
from jax.experimental.pallas import tpu as pltpu

F32, BF16 = jnp.float32, jnp.bfloat16
MESH = pl.DeviceIdType.MESH

LANES = 128
HEAD_DIM = 128
ATT_BLOCK = 128
ATT_WINDOW = 3 * ATT_BLOCK
ATT_FWD_HEADS = 4
ATT_BWD_HEADS = 2
POOL_WINDOWS = (2, 4, 8, 16)
POOL_HALO = 16
CONV_HALO = 8
LN_EPS = 1e-5
DEPTH = 4
ALPHA = (2.0 * DEPTH) ** 0.25
ATT_SCALE = HEAD_DIM ** -0.5
EXP_ZERO = 115.0
MASKED = 1e30
ADAM_LR, ADAM_B1, ADAM_B2, ADAM_EPS, ADAM_WD, ADAM_STEP = 0.001, 0.9, 0.999, 1e-08, 0.01, 10

VMEM_LIMIT = 56 << 20
ROW_TILE = 512
LN_ROW_TILE = 256
OPT_ROW_TILE = 128
SUM_ROW_TILE = 512


def _cp(n_axes):
    return pltpu.CompilerParams(dimension_semantics=("arbitrary",) * n_axes, vmem_limit_bytes=VMEM_LIMIT)


def _sds(shape, dtype):
    return jax.ShapeDtypeStruct(tuple(shape), dtype)


def _round_up(n, m):
    return (n + m - 1) // m * m


def _tile(n, cap, mult=8):
    if n <= cap:
        return n
    best = None
    for d in range(mult, cap + 1, mult):
        if n % d == 0:
            best = d
    assert best is not None, (n, cap)
    return best


_NT = (((1,), (1,)), ((), ()))
_TN = (((0,), (0,)), ((), ()))

_Side = collections.namedtuple("_Side", "ins outs alias sems start finish")


def _any_specs(n):
    return [pl.BlockSpec(memory_space=pl.ANY)] * n


def _call(body, first, last, side, *, grid, in_specs, out_specs, out_shape, scratch_shapes, name, args):
    n_axes = len(grid)
    if side is None:
        res = pl.pallas_call(body, grid=grid, in_specs=in_specs, out_specs=out_specs, out_shape=out_shape,
                             scratch_shapes=scratch_shapes, name=name, compiler_params=_cp(n_axes))(*args)
        return res, ()
    n_in, n_out, n_scr = len(in_specs), len(out_shape), len(scratch_shapes)
    s_in, s_out = len(side.ins), len(side.outs)

    def carried(*refs):
        ins, refs = refs[:n_in], refs[n_in:]
        side_ins, refs = refs[:s_in], refs[s_in:]
        outs, refs = refs[:n_out], refs[n_out:]
        side_outs, refs = refs[:s_out], refs[s_out:]
        scratch, side_sems = refs[:n_scr], refs[n_scr:]

        @pl.when(first())
        def _():
            side.start(side_ins, side_outs, side_sems)

        body(*ins, *outs, *scratch)

        @pl.when(last())
        def _():
            side.finish(side_ins, side_outs, side_sems)

    res = pl.pallas_call(
        carried, grid=grid, in_specs=list(in_specs) + _any_specs(s_in), out_specs=list(out_specs) + _any_specs(s_out),
        out_shape=list(out_shape) + list(side.outs), scratch_shapes=list(scratch_shapes) + list(side.sems),
        input_output_aliases={n_in + a: n_out + b for a, b in side.alias.items()},
        name=name, compiler_params=_cp(n_axes))(*args, *side.ins)
    return res[:n_out], res[n_out:]


def _mm_cols(a, b, out_dtype, name, transposed_b=False, side=None):
    t, k = a.shape
    g = b.shape[0]
    nb = b.shape[1] if transposed_b else b.shape[2]
    tm = _tile(t, ROW_TILE)
    steps = t // tm

    def body(a_ref, b_ref, o_ref):
        if transposed_b:
            acc = lax.dot_general(a_ref[...], b_ref[...], _NT, preferred_element_type=F32)
        else:
            acc = jnp.dot(a_ref[...], b_ref[...], preferred_element_type=F32)
        o_ref[...] = acc.astype(o_ref.dtype)

    first = lambda: jnp.logical_and(pl.program_id(0) == 0, pl.program_id(1) == 0)
    last = lambda: jnp.logical_and(pl.program_id(0) == g - 1, pl.program_id(1) == steps - 1)
    (out,), side_out = _call(
        body, first, last, side, grid=(g, steps),
        in_specs=[pl.BlockSpec((tm, k), lambda gi, i: (i, 0)),
                  pl.BlockSpec((None,) + b.shape[1:], lambda gi, i: (gi, 0, 0))],
        out_specs=[pl.BlockSpec((None, tm, nb), lambda gi, i: (gi, i, 0))],
        out_shape=[_sds((g, t, nb), out_dtype)], scratch_shapes=[], name=name, args=(a, b))
    return out if side is None else (out, side_out)


def _mm_nt_acc(a3, b3, res, kb, name, side=None, b_is_kn=False):
    ga, t, ka = a3.shape
    gb, n, kbb = (b3.shape[0], b3.shape[2], b3.shape[1]) if b_is_kn else b3.shape
    na, nbk = ka // kb, kbb // kb
    groups = ga * na
    assert groups == gb * nbk
    tm = _tile(t, ROW_TILE)
    steps = t // tm

    def body(a_ref, b_ref, res_ref, o_ref, acc):
        u = pl.program_id(1)

        @pl.when(u == 0)
        def _():
            acc[...] = ALPHA * res_ref[...]

        if b_is_kn:
            acc[...] += jnp.dot(a_ref[...], b_ref[...], preferred_element_type=F32)
        else:
            acc[...] += lax.dot_general(a_ref[...], b_ref[...], _NT, preferred_element_type=F32)

        @pl.when(u == groups - 1)
        def _():
            o_ref[...] = acc[...]

    first = lambda: jnp.logical_and(pl.program_id(0) == 0, pl.program_id(1) == 0)
    last = lambda: jnp.logical_and(pl.program_id(0) == steps - 1, pl.program_id(1) == groups - 1)
    if b_is_kn:
        b_spec = pl.BlockSpec((None, kb, n), lambda i, u: (u // nbk, u % nbk, 0))
    else:
        b_spec = pl.BlockSpec((None, n, kb), lambda i, u: (u // nbk, 0, u % nbk))
    (out,), side_out = _call(
        body, first, last, side, grid=(steps, groups),
        in_specs=[pl.BlockSpec((None, tm, kb), lambda i, u: (u // na, i, u % na)), b_spec,
                  pl.BlockSpec((tm, n), lambda i, u: (i, 0))],
        out_specs=[pl.BlockSpec((tm, n), lambda i, u: (i, 0))],
        out_shape=[_sds((t, n), F32)], scratch_shapes=[pltpu.VMEM((tm, n), F32)], name=name, args=(a3, b3, res))
    return out if side is None else (out, side_out)


def _mm_tn(x3, dy3, out_shape, bm, bn, groups, x_idx, dy_idx, out_idx, name, side=None):
    t = x3.shape[1]
    tm = _tile(t, 2 * ROW_TILE)
    grid = (groups, x_idx[1], t // tm)

    def body(x_ref, dy_ref, o_ref):
        @pl.when(pl.program_id(2) == 0)
        def _():
            o_ref[...] = jnp.zeros_like(o_ref)

        o_ref[...] += lax.dot_general(x_ref[...], dy_ref[...], _TN, preferred_element_type=F32)

    def at(corner):
        hit = pl.program_id(0) == corner[0]
        for axis in (1, 2):
            hit = jnp.logical_and(hit, pl.program_id(axis) == corner[axis])
        return hit

    (out,), side_out = _call(
        body, lambda: at((0, 0, 0)), lambda: at(tuple(g - 1 for g in grid)), side, grid=grid,
        in_specs=[pl.BlockSpec((None, tm, bm), lambda u, mb, i: (x_idx[0](u), i, x_idx[2](u, mb))),
                  pl.BlockSpec((None, tm, bn), lambda u, mb, i: (dy_idx[0](u), i, dy_idx[1](u)))],
        out_specs=[pl.BlockSpec((None, bm, bn), lambda u, mb, i: (out_idx[0](u), out_idx[1](u, mb), out_idx[2](u)))],
        out_shape=[_sds(out_shape, F32)], scratch_shapes=[], name=name, args=(x3, dy3))
    return out if side is None else (out, side_out)


def _layer_norm_rows(r, gamma, beta):
    mu = jnp.mean(r, axis=-1, keepdims=True)
    xc = r - mu
    var = jnp.mean(xc * xc, axis=-1, keepdims=True)
    return xc * lax.rsqrt(var + LN_EPS) * gamma + beta


def _mm_res_ln(a3, w3, res, gamma, beta, name, side=None):
    g, t, kb = a3.shape
    d = w3.shape[2]
    tm = _tile(t, LN_ROW_TILE)
    steps = t // tm

    def body(a_ref, w_hbm, res_ref, g_ref, b_ref, r_ref, o_ref, ob_ref, w_vmem, sem):
        @pl.when(pl.program_id(0) == 0)
        def _():
            cp = pltpu.make_async_copy(w_hbm, w_vmem, sem)
            cp.start()
            cp.wait()

        acc = ALPHA * res_ref[...]
        for gi in range(g):
            acc = acc + jnp.dot(a_ref[gi], w_vmem[gi], preferred_element_type=F32)
        r_ref[...] = acc
        out = _layer_norm_rows(acc, g_ref[...], b_ref[...])
        o_ref[...] = out
        ob_ref[...] = out.astype(BF16)

    row = pl.BlockSpec((tm, d), lambda i: (i, 0))
    vec = pl.BlockSpec((1, d), lambda i: (0, 0))
    outs, side_out = _call(
        body, lambda: pl.program_id(0) == 0, lambda: pl.program_id(0) == steps - 1, side, grid=(steps,),
        in_specs=[pl.BlockSpec((g, tm, kb), lambda i: (0, i, 0)), pl.BlockSpec(memory_space=pl.ANY), row, vec, vec],
        out_specs=[row, row, row],
        out_shape=[_sds((t, d), F32), _sds((t, d), F32), _sds((t, d), BF16)],
        scratch_shapes=[pltpu.VMEM(w3.shape, w3.dtype), pltpu.SemaphoreType.DMA],
        name=name, args=(a3, w3, res, gamma, beta))
    return outs if side is None else (outs, side_out)


def _ln_bwd(dout, r, gamma, name):
    t, d = r.shape
    tm = _tile(t, ROW_TILE)

    def body(do_ref, r_ref, g_ref, dr_ref, drb_ref, dg_ref, db_ref):
        @pl.when(pl.program_id(0) == 0)
        def _():
            dg_ref[...] = jnp.zeros_like(dg_ref)
            db_ref[...] = jnp.zeros_like(db_ref)

        rr = r_ref[...]
        do = do_ref[...]
        mu = jnp.mean(rr, axis=-1, keepdims=True)
        xc = rr - mu
        rstd = lax.rsqrt(jnp.mean(xc * xc, axis=-1, keepdims=True) + LN_EPS)
        xhat = xc * rstd
        dxh = do * g_ref[...]
        m1 = jnp.mean(dxh, axis=-1, keepdims=True)
        m2 = jnp.mean(dxh * xhat, axis=-1, keepdims=True)
        dr = rstd * (dxh - m1 - xhat * m2)
        dr_ref[...] = dr
        drb_ref[...] = dr.astype(BF16)
        dg_ref[...] += jnp.sum(do * xhat, axis=0, keepdims=True)
        db_ref[...] += jnp.sum(do, axis=0, keepdims=True)

    row = pl.BlockSpec((tm, d), lambda i: (i, 0))
    vec = pl.BlockSpec((1, d), lambda i: (0, 0))
    return pl.pallas_call(
        body, grid=(t // tm,), in_specs=[row, row, vec], out_specs=[row, row, vec, vec],
        out_shape=[_sds((t, d), F32), _sds((t, d), BF16), _sds((1, d), F32), _sds((1, d), F32)],
        name=name, compiler_params=_cp(1))(dout, r, gamma)


def _loss_and_grad(y, target, name):
    t, d = y.shape
    tm = _tile(t, ROW_TILE)
    steps = t // tm

    def body(y_ref, t_ref, loss_ref, dy_ref, acc):
        i = pl.program_id(0)

        @pl.when(i == 0)
        def _():
            acc[...] = jnp.zeros_like(acc)

        diff = y_ref[...] - t_ref[...]
        dy_ref[...] = diff * (1.0 / d)
        acc[...] += jnp.sum(diff * diff, axis=0, keepdims=True)

        @pl.when(i == steps - 1)
        def _():
            total = jnp.sum(acc[...], axis=1, keepdims=True) * (0.5 / d)
            loss_ref[...] = jnp.broadcast_to(total, loss_ref.shape)

    row = pl.BlockSpec((tm, d), lambda i: (i, 0))
    return pl.pallas_call(
        body, grid=(steps,), in_specs=[row, row],
        out_specs=[pl.BlockSpec((1, LANES), lambda i: (0, 0)), row],
        out_shape=[_sds((1, LANES), F32), _sds((t, d), F32)],
        scratch_shapes=[pltpu.VMEM((1, d), F32)], name=name, compiler_params=_cp(1))(y, target)


def _window_sums(ext, window, forward):
    n = ext.shape[0]
    s, span = ext, 1
    while span < window:
        s = s + pltpu.roll(s, (n - span) if forward else span, 0)
        span *= 2
    return s


def _pooled_group(main, halo, gi, row0):
    window = POOL_WINDOWS[gi]
    ext = jnp.concatenate([halo, main], axis=0)
    sums = _window_sums(ext, window, forward=False)[POOL_HALO:, :]
    pos = row0 + lax.broadcasted_iota(jnp.int32, (main.shape[0], 1), 0)
    cnt = jnp.minimum(pos + 1, window).astype(F32)
    return sums / cnt - main


def _pool_specs(t, d, tm):
    per = tm // POOL_HALO
    main = pl.BlockSpec((tm, d), lambda i: (i, 0))
    before = pl.BlockSpec((POOL_HALO, d), lambda i: (jnp.maximum(i * per - 1, 0), 0))
    return main, before


def _pool_fwd(x, w, scale, gamma, beta, name):
    t, d = x.shape
    ng, cg = w.shape[0], w.shape[1]
    tm = _tile(t, LN_ROW_TILE)

    def body(x_ref, h_ref, w_ref, s_ref, g_ref, b_ref, r_ref, o_ref, ob_ref):
        i = pl.program_id(0)
        for gi in range(ng):
            cols = pl.ds(gi * cg, cg)
            main = x_ref[:, cols]
            halo = jnp.where(i > 0, h_ref[:, cols], 0.0)
            pooled = _pooled_group(main, halo, gi, i * tm)
            y = jnp.dot(pooled.astype(BF16), w_ref[gi], preferred_element_type=F32)
            r_ref[:, cols] = ALPHA * main + y * s_ref[:, cols]
        out = _layer_norm_rows(r_ref[...], g_ref[...], b_ref[...])
        o_ref[...] = out
        ob_ref[...] = out.astype(BF16)

    main, before = _pool_specs(t, d, tm)
    vec = pl.BlockSpec((1, d), lambda i: (0, 0))
    return pl.pallas_call(
        body, grid=(t // tm,),
        in_specs=[main, before, pl.BlockSpec(w.shape, lambda i: (0, 0, 0)), vec, vec, vec],
        out_specs=[main, main, main],
        out_shape=[_sds((t, d), F32), _sds((t, d), F32), _sds((t, d), BF16)],
        name=name, compiler_params=_cp(1))(x, x, w, scale, gamma, beta)


def _pool_bwd(x, dy, w, scale, name):
    t, d = x.shape
    ng, cg = w.shape[0], w.shape[1]
    tm = _tile(t, LN_ROW_TILE)

    def body(x_ref, h_ref, dy_ref, w_ref, s_ref, dp_ref, dw_ref, ds_ref):
        i = pl.program_id(0)

        @pl.when(i == 0)
        def _():
            dw_ref[...] = jnp.zeros_like(dw_ref)
            ds_ref[...] = jnp.zeros_like(ds_ref)

        for gi in range(ng):
            cols = pl.ds(gi * cg, cg)
            main = x_ref[:, cols]
            halo = jnp.where(i > 0, h_ref[:, cols], 0.0)
            pooled = _pooled_group(main, halo, gi, i * tm).astype(BF16)
            y = jnp.dot(pooled, w_ref[gi], preferred_element_type=F32)
            dyg = dy_ref[:, cols]
            ds_ref[:, cols] += jnp.sum(dyg * y, axis=0, keepdims=True)
            dyw = (dyg * s_ref[:, cols]).astype(BF16)
            dw_ref[gi] += lax.dot_general(pooled, dyw, _TN, preferred_element_type=F32)
            dp_ref[:, cols] = lax.dot_general(dyw, w_ref[gi], _NT, preferred_element_type=F32)

    main, before = _pool_specs(t, d, tm)
    vec = pl.BlockSpec((1, d), lambda i: (0, 0))
    wspec = pl.BlockSpec(w.shape, lambda i: (0, 0, 0))
    return pl.pallas_call(
        body, grid=(t // tm,), in_specs=[main, before, main, wspec, vec],
        out_specs=[main, wspec, vec],
        out_shape=[_sds((t, d), F32), _sds(w.shape, F32), _sds((1, d), F32)],
        name=name, compiler_params=_cp(1))(x, x, dy, w, scale)


def _pool_adjoint(dp, dres, n_groups, name):
    t, d = dp.shape
    cg = d // n_groups
    tm = _tile(t, ROW_TILE)
    steps = t // tm
    per = tm // POOL_HALO

    def body(dp_ref, after_ref, dres_ref, dx_ref):
        i = pl.program_id(0)
        rows = lax.broadcasted_iota(jnp.int32, (tm, 1), 0)
        rows_after = lax.broadcasted_iota(jnp.int32, (POOL_HALO, 1), 0)
        for gi in range(n_groups):
            window = POOL_WINDOWS[gi]
            cols = pl.ds(gi * cg, cg)
            main = dp_ref[:, cols]
            cnt = jnp.minimum(i * tm + rows + 1, window).astype(F32)
            cnt_after = jnp.minimum((i + 1) * tm + rows_after + 1, window).astype(F32)
            after = jnp.where(i < steps - 1, after_ref[:, cols] / cnt_after, 0.0)
            ext = jnp.concatenate([main / cnt, after], axis=0)
            sums = _window_sums(ext, window, forward=True)[:tm, :]
            dx_ref[:, cols] = ALPHA * dres_ref[:, cols] + sums - main

    main = pl.BlockSpec((tm, d), lambda i: (i, 0))
    after = pl.BlockSpec((POOL_HALO, d), lambda i: (jnp.minimum((i + 1) * per, t // POOL_HALO - 1), 0))
    return pl.pallas_call(
        body, grid=(steps,), in_specs=[main, after, main], out_specs=main,
        out_shape=_sds((t, d), F32), name=name, compiler_params=_cp(1))(dp, dp, dres)


def _split_dot(x, tri):
    hi = x.astype(BF16)
    lo = (x - hi.astype(F32)).astype(BF16)
    return jnp.dot(hi, tri, preferred_element_type=F32) + jnp.dot(lo, tri, preferred_element_type=F32)


def _att_windows(qs, k_ws, limit, carry_rests, suffix):
    heads = range(len(qs))
    zs = [lax.dot_general(qs[hh], k_ws[hh], _NT, preferred_element_type=F32) * ATT_SCALE for hh in heads]
    visible = lax.broadcasted_iota(jnp.int32, zs[0].shape, 1) < limit
    zs = [jnp.where(visible, z, -MASKED) for z in zs]
    es = [jnp.exp(-jnp.abs(z)) for z in zs]
    log_nots = [-(jnp.maximum(z, 0.0) + jnp.log(1.0 + e)) for z, e in zip(zs, es)]
    rests = [_split_dot(ln, suffix[...]) + carry for ln, carry in zip(log_nots, carry_rests)]
    weights = [jnp.exp(z + r) for z, r in zip(zs, rests)]
    return zs, es, log_nots, weights


def _tri(w, strict):
    r = lax.broadcasted_iota(jnp.int32, (w, w), 0)
    c = lax.broadcasted_iota(jnp.int32, (w, w), 1)
    return ((r > c) if strict else (r >= c)).astype(BF16)


def _heads_per_step(qkv3, n_heads, most):
    cpb = qkv3.shape[2] // HEAD_DIM
    hp = most
    while cpb % hp or n_heads % hp:
        hp //= 2
    return hp


def _att_specs(qkv3, n_heads, hp):
    t = qkv3.shape[1]
    cpb = qkv3.shape[2] // HEAD_DIM
    wd = hp * HEAD_DIM

    def slab(off):
        return pl.BlockSpec((None, t, wd), lambda g, i: ((off + g * hp) // cpb, 0, ((off + g * hp) % cpb) // hp))

    q = pl.BlockSpec((None, ATT_BLOCK, wd), lambda g, i: ((g * hp) // cpb, i, ((g * hp) % cpb) // hp))
    return q, slab(n_heads), slab(2 * n_heads)


def _head_cols(hh):
    return pl.ds(hh * HEAD_DIM, HEAD_DIM)


def _key_bounds(k_ref, kmax, hp):
    for hh in range(hp):
        kf = k_ref[:, _head_cols(hh)].astype(F32)
        kmax[hh] = jnp.sqrt(jnp.max(jnp.sum(kf * kf, axis=1, keepdims=True)))


def _score_bound(q, key_norm):
    qf = q.astype(F32)
    return ATT_SCALE * 1.001 * key_norm * jnp.sqrt(jnp.sum(qf * qf, axis=1, keepdims=True)) + 1e-3


def _any_alive(rests, bounds):
    alive = jnp.max(rests[0] + bounds[0]) > -EXP_ZERO
    for r, zb in zip(rests[1:], bounds[1:]):
        alive = jnp.logical_or(alive, jnp.max(r + zb) > -EXP_ZERO)
    return alive


def _window_rows(hi, w):
    start = jnp.maximum(hi - w, 0)
    return start, pl.ds(pl.multiple_of(start, ATT_BLOCK), w)


def _attn_fwd(qkv3, n_heads, name):
    t = qkv3.shape[1]
    b = ATT_BLOCK
    w = min(ATT_WINDOW, t)
    hp = _heads_per_step(qkv3, n_heads, ATT_FWD_HEADS)
    heads = range(hp)

    def body(q_ref, k_ref, v_ref, o_ref, kmax, suffix):
        i = pl.program_id(1)

        @pl.when(i == 0)
        def _():
            _key_bounds(k_ref, kmax, hp)
            suffix[...] = _tri(w, strict=False)

        qs = [q_ref[:, _head_cols(hh)] for hh in heads]
        bounds = [_score_bound(qs[hh], kmax[hh]) for hh in heads]
        qpos = i * b + lax.broadcasted_iota(jnp.int32, (b, 1), 0)

        def cond(c):
            return jnp.logical_and(c[0] > 0, _any_alive(c[1], bounds))

        def step(c):
            hi, rests, accs = c
            start, rows = _window_rows(hi, w)
            limit = jnp.minimum(qpos, hi) - start
            k_ws = [k_ref[rows, _head_cols(hh)] for hh in heads]
            _, _, log_nots, weights = _att_windows(qs, k_ws, limit, rests, suffix)
            new_accs = tuple(accs[hh] + jnp.dot(weights[hh].astype(BF16), v_ref[rows, _head_cols(hh)],
                                                preferred_element_type=F32) for hh in heads)
            new_rests = tuple(rests[hh] + jnp.sum(log_nots[hh], axis=1, keepdims=True) for hh in heads)
            return start, new_rests, new_accs

        init = ((i + 1) * b, tuple(jnp.zeros((b, 1), F32) for _ in heads),
                tuple(jnp.zeros((b, HEAD_DIM), F32) for _ in heads))
        _, _, accs = lax.while_loop(cond, step, init)
        for hh in heads:
            o_ref[:, _head_cols(hh)] = accs[hh].astype(o_ref.dtype)

    qs_, ks_, vs_ = _att_specs(qkv3, n_heads, hp)
    return pl.pallas_call(
        body, grid=(n_heads // hp, t // b), in_specs=[qs_, ks_, vs_],
        out_specs=pl.BlockSpec((b, hp * HEAD_DIM), lambda g, i: (i, g)),
        out_shape=_sds((t, n_heads * HEAD_DIM), BF16),
        scratch_shapes=[pltpu.SMEM((hp,), F32), pltpu.VMEM((w, w), BF16)],
        name=name, compiler_params=_cp(2))(qkv3, qkv3, qkv3)


def _attn_bwd(qkv3, do, n_heads, name):
    t = qkv3.shape[1]
    b = ATT_BLOCK
    w = min(ATT_WINDOW, t)
    nq = t // b
    hp = _heads_per_step(qkv3, n_heads, ATT_BWD_HEADS)
    heads = range(hp)
    wd = hp * HEAD_DIM

    def body(q_ref, k_ref, v_ref, do_ref, dq_ref, dk_ref, dv_ref, kmax, dk_acc, dv_acc, suffix, strict_suffix):
        i = pl.program_id(1)

        @pl.when(i == 0)
        def _():
            _key_bounds(k_ref, kmax, hp)
            dk_acc[...] = jnp.zeros_like(dk_acc)
            dv_acc[...] = jnp.zeros_like(dv_acc)
            suffix[...] = _tri(w, strict=False)
            strict_suffix[...] = _tri(w, strict=True)

        qs = [q_ref[:, _head_cols(hh)] for hh in heads]
        douts = [do_ref[:, _head_cols(hh)] for hh in heads]
        bounds = [_score_bound(qs[hh], kmax[hh]) for hh in heads]
        zero_cols = tuple(jnp.zeros((b, 1), F32) for _ in heads)
        hi0 = (i + 1) * b
        qpos = i * b + lax.broadcasted_iota(jnp.int32, (b, 1), 0)

        def cond(c):
            return jnp.logical_and(c[0] > 0, _any_alive(c[1], bounds))

        def windows(rows, limit, rests):
            k_ws = [k_ref[rows, _head_cols(hh)] for hh in heads]
            dps = [lax.dot_general(douts[hh], v_ref[rows, _head_cols(hh)], _NT, preferred_element_type=F32)
                   for hh in heads]
            zs, es, log_nots, weights = _att_windows(qs, k_ws, limit, rests, suffix)
            dlas = [a * dp for a, dp in zip(weights, dps)]
            return k_ws, zs, es, log_nots, weights, dlas

        def row_sums(carries, tiles):
            return tuple(c + jnp.sum(x, axis=1, keepdims=True) for c, x in zip(carries, tiles))

        def sweep1(c):
            hi, rests, totals = c
            start, rows = _window_rows(hi, w)
            _, _, _, log_nots, weights, dlas = windows(rows, jnp.minimum(qpos, hi) - start, rests)
            for hh in heads:
                dv_acc[rows, _head_cols(hh)] += lax.dot_general(weights[hh].astype(BF16), douts[hh], _TN,
                                                                preferred_element_type=F32)
            return start, row_sums(rests, log_nots), row_sums(totals, dlas)

        _, _, totals = lax.while_loop(cond, sweep1, (hi0, zero_cols, zero_cols))

        def sweep2(c):
            hi, rests, laters, dqs = c
            start, rows = _window_rows(hi, w)
            k_ws, zs, es, log_nots, _, dlas = windows(rows, jnp.minimum(qpos, hi) - start, rests)
            insides = [_split_dot(dla, strict_suffix[...]) for dla in dlas]
            dzs = []
            for hh in heads:
                dlog_not = totals[hh] - laters[hh] - insides[hh]
                inv = 1.0 / (1.0 + es[hh])
                sig = jnp.where(zs[hh] >= 0, inv, es[hh] * inv)
                dzs.append(((dlas[hh] - sig * dlog_not) * ATT_SCALE).astype(BF16))
            new_dqs = tuple(dqs[hh] + jnp.dot(dzs[hh], k_ws[hh], preferred_element_type=F32) for hh in heads)
            for hh in heads:
                dk_acc[rows, _head_cols(hh)] += lax.dot_general(dzs[hh], qs[hh], _TN, preferred_element_type=F32)
            return start, row_sums(rests, log_nots), row_sums(laters, dlas), new_dqs

        init = (hi0, zero_cols, zero_cols, tuple(jnp.zeros((b, HEAD_DIM), F32) for _ in heads))
        _, _, _, dqs = lax.while_loop(cond, sweep2, init)
        for hh in heads:
            dq_ref[:, _head_cols(hh)] = dqs[hh].astype(dq_ref.dtype)

        @pl.when(i == nq - 1)
        def _():
            dk_ref[...] = dk_acc[...].astype(dk_ref.dtype)
            dv_ref[...] = dv_acc[...].astype(dv_ref.dtype)

    qs_, ks_, vs_ = _att_specs(qkv3, n_heads, hp)
    blk = pl.BlockSpec((b, wd), lambda g, i: (i, g))
    slab = pl.BlockSpec((t, wd), lambda g, i: (0, g))
    d = n_heads * HEAD_DIM
    return pl.pallas_call(
        body, grid=(n_heads // hp, nq),
        in_specs=[qs_, ks_, vs_, pl.BlockSpec((None, b, wd), lambda g, i: (0, i, g))],
        out_specs=[blk, slab, slab],
        out_shape=[_sds((t, d), BF16)] * 3,
        scratch_shapes=[pltpu.SMEM((hp,), F32), pltpu.VMEM((t, wd), F32), pltpu.VMEM((t, wd), F32),
                        pltpu.VMEM((w, w), BF16), pltpu.VMEM((w, w), BF16)],
        name=name, compiler_params=_cp(2))(qkv3, qkv3, qkv3, do)


def _conv_rows(main, halo, w_ref, b_ref):
    ext = jnp.concatenate([halo, main], axis=0)
    h1 = pltpu.roll(ext, 1, 0)[CONV_HALO:, :]
    h2 = pltpu.roll(ext, 2, 0)[CONV_HALO:, :]
    hc = b_ref[...] + w_ref[0:1, :] * h2
    hc = hc + w_ref[1:2, :] * h1
    hc = hc + w_ref[2:3, :] * main
    return hc, h1, h2


def _ffn_specs(t, fp, tm, half):
    per = tm // CONV_HALO
    main = lambda off: pl.BlockSpec((None, tm, fp), lambda g, i: (g + off, i, 0))
    before = lambda off: pl.BlockSpec((None, CONV_HALO, fp), lambda g, i: (g + off, jnp.maximum(i * per - 1, 0), 0))
    cw = lambda off: pl.BlockSpec((None, 3, fp), lambda g, i: (g + off, 0, 0))
    cb = lambda off: pl.BlockSpec((None, 1, fp), lambda g, i: (g + off, 0, 0))
    return [main(0), before(0), main(half), before(half), cw(0), cw(half), cb(0), cb(half)]


def _ffn_act(h, cw, cb, name, side=None):
    n, t, fp = h.shape
    half = n // 2
    tm = _tile(t, LN_ROW_TILE)
    steps = t // tm

    def body(hg_ref, hgb_ref, hv_ref, hvb_ref, wg_ref, wv_ref, bg_ref, bv_ref, a_ref):
        first = pl.program_id(1) == 0
        gate, _, _ = _conv_rows(hg_ref[...], jnp.where(first, 0.0, hgb_ref[...]), wg_ref, bg_ref)
        val, _, _ = _conv_rows(hv_ref[...], jnp.where(first, 0.0, hvb_ref[...]), wv_ref, bv_ref)
        a_ref[...] = (gate * jax.nn.sigmoid(gate) * val).astype(a_ref.dtype)

    (a,), side_out = _call(
        body, lambda: jnp.logical_and(pl.program_id(0) == 0, pl.program_id(1) == 0),
        lambda: jnp.logical_and(pl.program_id(0) == half - 1, pl.program_id(1) == steps - 1), side,
        grid=(half, steps), in_specs=_ffn_specs(t, fp, tm, half),
        out_specs=[pl.BlockSpec((None, tm, fp), lambda g, i: (g, i, 0))],
        out_shape=[_sds((half, t, fp), BF16)], scratch_shapes=[], name=name, args=(h, h, h, h, cw, cw, cb, cb))
    return a if side is None else (a, side_out)


def _act_grads(dact, gate, val):
    sig = jax.nn.sigmoid(gate)
    return dact * val * (sig * (1.0 + gate * (1.0 - sig))), dact * (gate * sig)


def _ffn_act_bwd(h, da, cw, cb, name):
    n, t, fp = h.shape
    half = n // 2
    tm = _tile(t, LN_ROW_TILE)

    def body(hg_ref, hgb_ref, hv_ref, hvb_ref, wg_ref, wv_ref, bg_ref, bv_ref, da_ref, dhc_ref, dw_ref, db_ref):
        first = pl.program_id(1) == 0

        @pl.when(first)
        def _():
            dw_ref[...] = jnp.zeros_like(dw_ref)
            db_ref[...] = jnp.zeros_like(db_ref)

        hg, hv = hg_ref[...], hv_ref[...]
        gate, hg1, hg2 = _conv_rows(hg, jnp.where(first, 0.0, hgb_ref[...]), wg_ref, bg_ref)
        val, hv1, hv2 = _conv_rows(hv, jnp.where(first, 0.0, hvb_ref[...]), wv_ref, bv_ref)
        dgate, dval = _act_grads(da_ref[...], gate, val)
        dhc_ref[0] = dgate
        dhc_ref[1] = dval
        for s, (dd, shifted) in enumerate(((dgate, (hg2, hg1, hg)), (dval, (hv2, hv1, hv)))):
            db_ref[s] += jnp.sum(dd, axis=0, keepdims=True)
            for kk in range(3):
                dw_ref[s, kk:kk + 1, :] += jnp.sum(dd * shifted[kk], axis=0, keepdims=True)

    specs = _ffn_specs(t, fp, tm, half) + [pl.BlockSpec((None, tm, fp), lambda g, i: (g, i, 0))]
    return pl.pallas_call(
        body, grid=(half, t // tm), in_specs=specs,
        out_specs=[pl.BlockSpec((2, None, tm, fp), lambda g, i: (0, g, i, 0)),
                   pl.BlockSpec((2, None, 3, fp), lambda g, i: (0, g, 0, 0)),
                   pl.BlockSpec((2, None, 1, fp), lambda g, i: (0, g, 0, 0))],
        out_shape=[_sds((2, half, t, fp), F32), _sds((2, half, 3, fp), F32), _sds((2, half, 1, fp), F32)],
        name=name, compiler_params=_cp(2))(h, h, h, h, cw, cw, cb, cb, da)


def _conv_adjoint(dhc, cw, name):
    n, t, fp = dhc.shape
    tm = _tile(t, ROW_TILE)
    steps = t // tm
    per = tm // CONV_HALO

    def body(d_ref, after_ref, w_ref, o_ref):
        main = d_ref[...]
        after = jnp.where(pl.program_id(1) < steps - 1, after_ref[...], 0.0)
        ext = jnp.concatenate([main, after], axis=0)
        rows = ext.shape[0]
        d1 = pltpu.roll(ext, rows - 1, 0)[:tm, :]
        d2 = pltpu.roll(ext, rows - 2, 0)[:tm, :]
        o_ref[...] = (w_ref[2:3, :] * main + w_ref[1:2, :] * d1 + w_ref[0:1, :] * d2).astype(o_ref.dtype)

    main = pl.BlockSpec((None, tm, fp), lambda g, i: (g, i, 0))
    after = pl.BlockSpec((None, CONV_HALO, fp), lambda g, i: (g, jnp.minimum((i + 1) * per, t // CONV_HALO - 1), 0))
    return pl.pallas_call(
        body, grid=(n, steps), in_specs=[main, after, pl.BlockSpec((None, 3, fp), lambda g, i: (g, 0, 0))],
        out_specs=main, out_shape=_sds((n, t, fp), BF16), name=name, compiler_params=_cp(2))(dhc, dhc, cw)


def _place():
    x, y, c = lax.axis_index("x"), lax.axis_index("y"), lax.axis_index("c")
    chips = [(1 - x, y), (x, 1 - y), (1 - x, 1 - y)]
    return x, y, c, chips


def _run_sides(sides, name):
    n_in = [len(s.ins) for s in sides]
    n_out = [len(s.outs) for s in sides]
    n_sem = [len(s.sems) for s in sides]

    def body(*refs):
        ins, outs, sems = refs[:sum(n_in)], refs[sum(n_in):sum(n_in) + sum(n_out)], refs[sum(n_in) + sum(n_out):]
        oi = oo = os_ = 0
        for k, s in enumerate(sides):
            mine = (ins[oi:oi + n_in[k]], outs[oo:oo + n_out[k]], sems[os_:os_ + n_sem[k]])
            s.start(*mine)
            s.finish(*mine)
            oi, oo, os_ = oi + n_in[k], oo + n_out[k], os_ + n_sem[k]

    aliases, oi, oo = {}, 0, 0
    for k, s in enumerate(sides):
        aliases.update({oi + a: oo + b for a, b in s.alias.items()})
        oi, oo = oi + n_in[k], oo + n_out[k]
    return pl.pallas_call(
        body, in_specs=_any_specs(sum(n_in)), out_specs=_any_specs(sum(n_out)),
        out_shape=[o for s in sides for o in s.outs], input_output_aliases=aliases,
        scratch_shapes=[q for s in sides for q in s.sems], name=name)(*[a for s in sides for a in s.ins])


def _place_shard(kind, w, chip, name, rows=None, base=None):
    if kind == "pool":
        g, r, cdim = w.shape

        def body(chip_ref, w_ref, o_ref):
            del chip_ref
            o_ref[...] = w_ref[...].astype(BF16)

        return pl.pallas_call(
            body,
            grid_spec=pltpu.PrefetchScalarGridSpec(
                num_scalar_prefetch=1, grid=(1,),
                in_specs=[pl.BlockSpec((g, r, cdim), lambda i, chip_ref: (0, 0, 0))],
                out_specs=pl.BlockSpec((g, r, cdim), lambda i, chip_ref: (0, chip_ref[0], 0))),
            out_shape=_sds((g, 4 * r, cdim), BF16), name=name, compiler_params=_cp(1))(chip, w)

    r, cs = w.shape
    if kind == "lead":
        rows = rows or r
        tr = _tile(r, ROW_TILE, 16) if rows == r else rows - r
        assert r % tr == 0 and tr % 16 == 0
        n_src = r // tr

        def body(chip_ref, w_ref, o_ref):
            del chip_ref
            o_ref[...] = jnp.where(pl.program_id(0) < n_src, w_ref[...], 0.0).astype(BF16)

        return pl.pallas_call(
            body,
            grid_spec=pltpu.PrefetchScalarGridSpec(
                num_scalar_prefetch=1, grid=(rows // tr,),
                in_specs=[pl.BlockSpec((tr, cs), lambda i, chip_ref: (jnp.minimum(i, n_src - 1), 0))],
                out_specs=pl.BlockSpec((None, tr, cs), lambda i, chip_ref: (chip_ref[0], i, 0))),
            out_shape=_sds((4, rows, cs), BF16), name=name, compiler_params=_cp(1))(chip, w)

    assert kind == "down"
    tr = r // 2 if (r // 2) % 16 == 0 else r
    per = r // tr

    def body(chip_ref, w_ref, base_ref, o_ref):
        del chip_ref, base_ref
        o_ref[...] = w_ref[...].astype(BF16)

    return pl.pallas_call(
        body,
        grid_spec=pltpu.PrefetchScalarGridSpec(
            num_scalar_prefetch=1, grid=(per,),
            in_specs=[pl.BlockSpec((tr, cs), lambda i, chip_ref: (i, 0)), pl.BlockSpec(memory_space=pl.ANY)],
            out_specs=pl.BlockSpec((None, tr, cs), lambda i, chip_ref: (chip_ref[0] // 2, (chip_ref[0] % 2) * per + i, 0))),
        out_shape=_sds(base.shape, BF16), input_output_aliases={2: 0},
        name=name, compiler_params=_cp(1))(chip, w, base)


def _gather_sides(items, bufs=None):
    n = len(items)
    kinds = [it[0] for it in items]
    shard_rows = [it[2] for it in items]
    bufs = [it[1] for it in items] if bufs is None else list(bufs)

    def half_of(outs, m, chip, half):
        k = 2 * chip[0] + chip[1]
        o, r = outs[m], shard_rows[m]
        if kinds[m] == "pool":
            gh = o.shape[0] // 2
            return o.at[pl.ds(half * gh, gh), pl.ds(k * r, r)]
        r2 = r // 2
        if kinds[m] == "down":
            return o.at[k // 2, pl.ds((k % 2) * r + half * r2, r2)]
        return o.at[k, pl.ds(half * r2, r2)]

    def remote(outs, sems, m, j, chip, half, to):
        ref = half_of(outs, m, chip, half)
        return pltpu.make_async_remote_copy(src_ref=ref, dst_ref=ref, send_sem=sems[0].at[m, j],
                                            recv_sem=sems[1].at[m, j], device_id=to, device_id_type=MESH)

    def ici_copies(outs, sems, sending):
        x, y, c, chips = _place()
        if sending:
            return [remote(outs, sems, m, j, (x, y), c, (*chip, c)) for m in range(n) for j, chip in enumerate(chips)]
        return [remote(outs, sems, m, j, chip, c, (x, y, c)) for m in range(n) for j, chip in enumerate(chips)]

    def d2d_copies(outs, sems, sending):
        x, y, c, chips = _place()
        if sending:
            return [remote(outs, sems, m, j, chip, c, (x, y, 1 - c)) for m in range(n) for j, chip in enumerate(chips)]
        return [remote(outs, sems, m, j, chip, 1 - c, (x, y, c)) for m in range(n) for j, chip in enumerate(chips)]

    def phase(copies):
        def start(ins, outs, sems):
            for cp in copies(outs, sems, True):
                cp.start()

        def finish(ins, outs, sems):
            for cp in copies(outs, sems, False):
                cp.wait_recv()
            for cp in copies(outs, sems, True):
                cp.wait_send()

        return start, finish

    ici, d2d = phase(ici_copies), phase(d2d_copies)

    def both_finish(ins, outs, sems):
        ici[1](ins, outs, sems[:2])
        d2d[0](ins, outs, sems[2:])
        d2d[1](ins, outs, sems[2:])

    pair = [pltpu.SemaphoreType.DMA((n, 3)), pltpu.SemaphoreType.DMA((n, 3))]
    shapes = [_sds(b.shape, b.dtype) for b in bufs]
    alias = {m: m for m in range(n)}

    def side(which):
        if which == "both":
            return _Side(bufs, shapes, alias, pair + pair, lambda i, o, s: ici[0](i, o, s[:2]), both_finish)
        start, finish = ici if which == "ici" else d2d
        return _Side(bufs, shapes, alias, pair, start, finish)

    return side


def _sibling_side(grads):
    n = len(grads)

    def copies(ins, outs, sems):
        x, y, c, _ = _place()
        res = []
        for m in range(n):
            r2 = ins[m].shape[1] // 2
            res.append(pltpu.make_async_remote_copy(
                src_ref=ins[m].at[:, pl.ds((1 - c) * r2, r2)], dst_ref=outs[m],
                send_sem=sems[0].at[m], recv_sem=sems[1].at[m], device_id=(x, y, 1 - c), device_id_type=MESH))
        return res

    def start(ins, outs, sems):
        for cp in copies(ins, outs, sems):
            cp.start()

    def finish(ins, outs, sems):
        for cp in copies(ins, outs, sems):
            cp.wait_recv()
        for cp in copies(ins, outs, sems):
            cp.wait_send()

    return _Side(list(grads), [_sds((4, g.shape[1] // 2, g.shape[2]), g.dtype) for g in grads], {},
                 [pltpu.SemaphoreType.DMA((n,)), pltpu.SemaphoreType.DMA((n,))], start, finish)


def _owner_chips_side(parts):
    n = len(parts)

    def copies(ins, outs, sems):
        _, _, c, chips = _place()
        return [pltpu.make_async_remote_copy(
            src_ref=ins[m].at[2 * chip[0] + chip[1]], dst_ref=outs[m].at[j], send_sem=sems[0].at[m, j],
            recv_sem=sems[1].at[m, j], device_id=(*chip, c), device_id_type=MESH)
            for m in range(n) for j, chip in enumerate(chips)]

    def start(ins, outs, sems):
        for cp in copies(ins, outs, sems):
            cp.start()

    def finish(ins, outs, sems):
        for cp in copies(ins, outs, sems):
            cp.wait_recv()
        for cp in copies(ins, outs, sems):
            cp.wait_send()

    return _Side(list(parts), [_sds((3,) + p.shape[1:], p.dtype) for p in parts], {},
                 [pltpu.SemaphoreType.DMA((n, 3)), pltpu.SemaphoreType.DMA((n, 3))], start, finish)


def _exchange_finished_halves(shards, name):
    n = len(shards)

    def body(*refs):
        out = refs[n:2 * n]
        send_sems, recv_sems = refs[2 * n:]
        x, y, c, _ = _place()
        copies = []
        for m in range(n):
            r2 = out[m].shape[0] // 2
            mine = out[m].at[pl.ds(c * r2, r2)]
            copies.append(pltpu.make_async_remote_copy(
                src_ref=mine, dst_ref=mine, send_sem=send_sems.at[m], recv_sem=recv_sems.at[m],
                device_id=(x, y, 1 - c), device_id_type=MESH))
        for cp in copies:
            cp.start()
        for m in range(n):
            r2 = out[m].shape[0] // 2
            theirs = out[m].at[pl.ds((1 - c) * r2, r2)]
            pltpu.make_async_remote_copy(
                src_ref=theirs, dst_ref=theirs, send_sem=send_sems.at[m], recv_sem=recv_sems.at[m],
                device_id=(x, y, 1 - c), device_id_type=MESH).wait_recv()
        for cp in copies:
            cp.wait_send()

    return pl.pallas_call(
        body, in_specs=_any_specs(n), out_specs=_any_specs(n), out_shape=[_sds(s.shape, s.dtype) for s in shards],
        input_output_aliases={m: m for m in range(n)},
        scratch_shapes=[pltpu.SemaphoreType.DMA((n,)), pltpu.SemaphoreType.DMA((n,))], name=name)(*shards)


def _all_reduce_small(v, name):
    rows = v.shape[0]

    def body(v_ref, out_ref, buf, send_sems, recv_sems, local_sem):
        x, y, c, chips = _place()
        me, sibling = (x, y, c), (x, y, 1 - c)

        def slot(px, py, pc):
            return buf.at[4 * px + 2 * py + pc]

        def copy(k, block, to, src=None):
            return pltpu.make_async_remote_copy(
                src_ref=slot(*block) if src is None else src, dst_ref=slot(*block),
                send_sem=send_sems.at[k], recv_sem=recv_sems.at[k], device_id=to, device_id_type=MESH)

        mine = pltpu.make_async_copy(v_ref, slot(*me), local_sem)
        mine.start()
        first = [copy(0, me, sibling, src=v_ref)]
        first += [copy(1 + j, me, (*chip, c), src=v_ref) for j, chip in enumerate(chips)]
        for cp in first:
            cp.start()
        passed = [copy(4 + j, (*chip, c), sibling) for j, chip in enumerate(chips)]
        for j, chip in enumerate(chips):
            copy(1 + j, (*chip, c), me).wait_recv()
            passed[j].start()
        copy(0, sibling, me).wait_recv()
        for j, chip in enumerate(chips):
            copy(4 + j, (*chip, 1 - c), me).wait_recv()
        for cp in first + passed:
            cp.wait_send()
        mine.wait()
        total = buf[0]
        for dev in range(1, 8):
            total = total + buf[dev]
        out_ref[...] = total

    vm = pl.BlockSpec(memory_space=pltpu.VMEM)
    return pl.pallas_call(
        body, in_specs=[vm], out_specs=vm, out_shape=_sds(v.shape, F32),
        scratch_shapes=[pltpu.VMEM((8, rows, LANES), F32), pltpu.SemaphoreType.DMA((7,)),
                        pltpu.SemaphoreType.DMA((7,)), pltpu.SemaphoreType.DMA],
        name=name, compiler_params=pltpu.CompilerParams(vmem_limit_bytes=VMEM_LIMIT))(v)


def _chip_partial(grad, from_sibling, core, name):
    _, r, cdim = grad.shape
    r2 = r // 2
    tr = _tile(r2, SUM_ROW_TILE)
    per = r2 // tr

    def body(core_ref, g_ref, s_ref, o_ref, ob_ref):
        del core_ref
        total = g_ref[...] + s_ref[...]
        o_ref[...] = total
        ob_ref[...] = total.astype(BF16)

    blk = pl.BlockSpec((None, tr, cdim), lambda k, i, core_ref: (k, i, 0))
    mine = pl.BlockSpec((None, tr, cdim), lambda k, i, core_ref: (k, core_ref[0] * per + i, 0))
    return pl.pallas_call(
        body,
        grid_spec=pltpu.PrefetchScalarGridSpec(num_scalar_prefetch=1, grid=(4, per), in_specs=[mine, blk],
                                               out_specs=[blk, blk]),
        out_shape=[_sds((4, r2, cdim), F32), _sds((4, r2, cdim), BF16)],
        name=name, compiler_params=_cp(2))(core, grad, from_sibling)


def _owner_sum(partial, from_chips, place, name):
    _, r2, cdim = partial.shape
    tr = _tile(r2, SUM_ROW_TILE)
    per = r2 // tr

    def body(place_ref, p_ref, f_ref, o_ref):
        del place_ref
        total = p_ref[...]
        for j in range(3):
            total = total + f_ref[j].astype(F32)
        o_ref[...] = total

    return pl.pallas_call(
        body,
        grid_spec=pltpu.PrefetchScalarGridSpec(
            num_scalar_prefetch=1, grid=(per,),
            in_specs=[pl.BlockSpec((None, tr, cdim), lambda i, place_ref: (place_ref[0], i, 0)),
                      pl.BlockSpec((3, tr, cdim), lambda i, place_ref: (0, i, 0))],
            out_specs=pl.BlockSpec((tr, cdim), lambda i, place_ref: (place_ref[1] * per + i, 0))),
        out_shape=_sds((2 * r2, cdim), F32), name=name, compiler_params=_cp(1))(place, partial, from_chips)


def _adamw(g, w, m, v, layer, prev, name):
    _, r, cdim = w.shape
    tr = _tile(r, OPT_ROW_TILE)
    c1 = 1.0 / (1.0 - ADAM_B1 ** ADAM_STEP)
    c2 = 1.0 / (1.0 - ADAM_B2 ** ADAM_STEP)
    n_prev = 0 if prev is None else 4

    def body(g_ref, w_ref, m_ref, v_ref, *rest):
        go_ref, d_ref, mo_ref, vo_ref = rest[n_prev:]
        grad = g_ref[:, pl.ds(0, cdim)]
        m_new = ADAM_B1 * m_ref[...] + (1.0 - ADAM_B1) * grad
        v_new = ADAM_B2 * v_ref[...] + (1.0 - ADAM_B2) * (grad * grad)
        go_ref[...] = grad
        mo_ref[...] = m_new
        vo_ref[...] = v_new
        d_ref[...] = -ADAM_LR * ((m_new * c1) / (jnp.sqrt(v_new * c2) + ADAM_EPS) + ADAM_WD * w_ref[...])

    blk = pl.BlockSpec((None, tr, cdim), lambda i: (layer, i, 0))
    gblk = pl.BlockSpec((tr, g.shape[1]), lambda i: (i, 0))
    return pl.pallas_call(
        body, grid=(r // tr,), in_specs=[gblk, blk, blk, blk] + _any_specs(n_prev), out_specs=[blk] * 4,
        out_shape=[_sds(w.shape, F32)] * 4, input_output_aliases={4 + k: k for k in range(n_prev)},
        name=name, compiler_params=_cp(1))(g, w, m, v, *(prev or ()))


def _pack_rows(vectors):
    flat = [v.reshape(-1) for v in vectors]
    sizes = [f.shape[0] for f in flat]
    total = sum(sizes)
    padded = _round_up(total, 8 * LANES)
    buf = jnp.concatenate(flat + [jnp.zeros((padded - total,), F32)])
    return buf.reshape(padded // LANES, LANES), sizes


def _unpack_rows(buf, sizes, shapes):
    flat = buf.reshape(-1)
    out, off = [], 0
    for n, shp in zip(sizes, shapes):
        out.append(flat[off:off + n].reshape(shp))
        off += n
    return out


def kernel(x, pool_w, pool_scale, attn_w_qkv, attn_w_o, ffn_w_up, ffn_conv_w, ffn_conv_b, ffn_w_down, ln_mix_g, ln_mix_b, ln_ffn_g, ln_ffn_b, loss_target, m_pool_w, m_pool_scale, m_attn_w_qkv, m_attn_w_o, m_ffn_w_up, m_ffn_conv_w, m_ffn_conv_b, m_ffn_w_down, m_ln_mix_g, m_ln_mix_b, m_ln_ffn_g, m_ln_ffn_b, v_pool_w, v_pool_scale, v_attn_w_qkv, v_attn_w_o, v_ffn_w_up, v_ffn_conv_w, v_ffn_conv_b, v_ffn_w_down, v_ln_mix_g, v_ln_mix_b, v_ln_ffn_g, v_ln_ffn_b):
    t, d = x.shape[1], x.shape[2]
    n_heads = d // HEAD_DIM
    n_groups = pool_w.shape[1]
    fs = ffn_w_up.shape[2]
    fp = _round_up(fs, LANES)
    rd = ffn_w_down.shape[1]
    assert 2 * rd == fs
    xi, yi, ci = lax.axis_index("x"), lax.axis_index("y"), lax.axis_index("c")
    chip = (2 * xi + yi).astype(jnp.int32)
    chip_arr, core_arr = chip.reshape(1), ci.astype(jnp.int32).reshape(1)
    place_arr = jnp.concatenate([chip_arr, core_arr])

    x2 = x.reshape(t, d)
    target = loss_target.reshape(t, d)
    pad_cols = lambda a: jnp.pad(a, [(0, 0)] * (a.ndim - 1) + [(0, fp - fs)])
    up_t = [jnp.transpose(a, (0, 2, 1)) for a in (ffn_w_up, m_ffn_w_up, v_ffn_w_up)]

    gather_items = []
    for i in range(DEPTH):
        j = i // 2
        items = []
        if i % 2 == 0:
            items.append(("pool", _place_shard("pool", pool_w[j], chip_arr, name="place_pool"), pool_w.shape[2]))
        else:
            items.append(("lead", _place_shard("lead", attn_w_qkv[j], chip_arr, name="place_qkv"), d))
            items.append(("lead", _place_shard("lead", attn_w_o[j], chip_arr, name="place_wo"), attn_w_o.shape[1]))
        items.append(("lead", _place_shard("lead", up_t[0][i], chip_arr, name="place_up", rows=fp), fp))
        items.append(("down", _place_shard("down", ffn_w_down[i], chip_arr, name="place_down",
                                           base=jnp.zeros((2, fp, d), BF16)), rd))
        gather_items.append(items)
    weights = [None] * DEPTH
    weights[0] = _run_sides([_gather_sides(gather_items[0])("both")], name="gather_layer0")

    conv_b_all = pad_cols(ffn_conv_b.reshape(DEPTH, 4, 1, fs))
    cw_local = pad_cols(ffn_conv_w)
    slot = (jnp.arange(4, dtype=jnp.int32) == chip).astype(F32) * (1.0 - ci.astype(F32))
    cw_placed = slot[None, :, None, None] * cw_local[:, None]
    cw_buf, cw_sizes = _pack_rows([cw_placed])
    conv_w_all = _unpack_rows(_all_reduce_small(cw_buf, name="gather_conv_w"), cw_sizes, [cw_placed.shape])[0]

    gam = lambda a, i: a[i].reshape(1, d)

    saved = []
    cur, cur_b = x2, x2.astype(BF16)
    for i in range(DEPTH):
        j = i // 2
        w = weights[i]
        s = {"x_in": cur, "x_in_b": cur_b}
        if i % 2 == 0:
            w_pool, w_up, w_down = w
            s["scale"] = pool_scale[j].reshape(1, d)
            r1, x1, x1b = _pool_fwd(cur, w_pool, s["scale"], gam(ln_mix_g, i), gam(ln_mix_b, i), name="pool_fwd")
        else:
            w_qkv, w_o, w_up, w_down = w
            w_o3 = w_o.reshape(1, d, d)
            qkv = _mm_cols(cur_b, w_qkv, BF16, name="qkv_proj")
            o = _attn_fwd(qkv, n_heads, name="attn_fwd")
            s["qkv"], s["o"], s["w_o3"] = qkv, o, w_o3
            r1, x1, x1b = _mm_res_ln(o.reshape(1, t, d), w_o3, cur, gam(ln_mix_g, i), gam(ln_mix_b, i),
                                     name="attn_out_ln")
        if i + 1 < DEPTH:
            nxt = gather_items[i + 1]
            n_mix = len(nxt) - 2
            h, landed_ffn = _mm_cols(x1b, w_up, F32, name="ffn_up", transposed_b=True,
                                     side=_gather_sides(nxt[n_mix:])("ici"))
            a, landed_mix = _ffn_act(h, conv_w_all[i], conv_b_all[i], name="ffn_act",
                                     side=_gather_sides(nxt[:n_mix])("ici"))
            landed = list(landed_mix) + list(landed_ffn)
            (r2, x2n, x2b), gathered = _mm_res_ln(a, w_down, x1, gam(ln_ffn_g, i), gam(ln_ffn_b, i),
                                                  name="ffn_down_ln", side=_gather_sides(nxt, landed)("d2d"))
            weights[i + 1] = list(gathered)
        else:
            h = _mm_cols(x1b, w_up, F32, name="ffn_up", transposed_b=True)
            a = _ffn_act(h, conv_w_all[i], conv_b_all[i], name="ffn_act")
            r2, x2n, x2b = _mm_res_ln(a, w_down, x1, gam(ln_ffn_g, i), gam(ln_ffn_b, i), name="ffn_down_ln")
        s.update(r1=r1, x1b=x1b, h=h, a=a, r2=r2)
        saved.append(s)
        cur, cur_b = x2n, x2b

    loss_row, dcur = _loss_and_grad(cur, target, name="loss")
    loss = lax.psum(loss_row[0, 0], ("x", "y", "c"))

    big_grads = [None] * DEPTH
    reduced = [None] * DEPTH
    small = {}

    def finish_reduce(parts, from_chips, layer):
        halves = [_owner_sum(p[0], fc, place_arr, name="reduce_owner_sum") for p, fc in zip(parts, from_chips)]
        return _exchange_finished_halves(halves, name="reduce_halves_pool" if layer % 2 == 0 else "reduce_halves_attn")

    for i in reversed(range(DEPTH)):
        j = i // 2
        s, w = saved[i], weights[i]
        w_up, w_down = w[-2], w[-1]
        dr2, dr2b, small["ln_ffn_g", i], small["ln_ffn_b", i] = _ln_bwd(dcur, s["r2"], gam(ln_ffn_g, i), name="ln_bwd")
        pending = big_grads[i + 1] if i + 1 < DEPTH else None
        if pending is not None:
            da, from_sib = _mm_cols(dr2b, w_down, F32, name="ffn_down_bwd_act", transposed_b=True,
                                    side=_sibling_side(pending))
            parts = [_chip_partial(g, fs_, core_arr, name="reduce_chip_partial") for g, fs_ in zip(pending, from_sib)]
        else:
            da = _mm_cols(dr2b, w_down, F32, name="ffn_down_bwd_act", transposed_b=True)
        dr2b3 = dr2b.reshape(1, t, d)
        nmb = 2
        d_down = _mm_tn(s["a"], dr2b3, (2, fp, d), fp // nmb, d, 2,
                        (lambda u: u, nmb, lambda u, mb: mb), (lambda u: 0, lambda u: 0),
                        (lambda u: u, lambda u, mb: mb, lambda u: 0), name="ffn_down_bwd_w")
        dhc, dcw, dcb = _ffn_act_bwd(s["h"], da, conv_w_all[i], conv_b_all[i], name="ffn_act_bwd")
        small["conv_w", i], small["conv_b", i] = dcw, dcb
        dh = _conv_adjoint(dhc.reshape(4, t, fp), conv_w_all[i], name="ffn_conv_adjoint")
        up_w_args = (dh, s["x1b"].reshape(1, t, d), (4, fp, d), fp // 2, d, 4,
                     (lambda u: u, 2, lambda u, mb: mb), (lambda u: 0, lambda u: 0),
                     (lambda u: u, lambda u, mb: mb, lambda u: 0))
        if pending is not None:
            n_mix = len(parts) - 2
            dx1, chips_ffn = _mm_nt_acc(dh, w_up, dr2, fp, name="ffn_up_bwd_act", b_is_kn=True,
                                        side=_owner_chips_side([p[1] for p in parts[n_mix:]]))
            d_up, chips_mix = _mm_tn(*up_w_args, name="ffn_up_bwd_w",
                                     side=_owner_chips_side([p[1] for p in parts[:n_mix]]))
            reduced[i + 1] = finish_reduce(parts, list(chips_mix) + list(chips_ffn), i + 1)
        else:
            dx1 = _mm_nt_acc(dh, w_up, dr2, fp, name="ffn_up_bwd_act", b_is_kn=True)
            d_up = _mm_tn(*up_w_args, name="ffn_up_bwd_w")
        dr1, dr1b, small["ln_mix_g", i], small["ln_mix_b", i] = _ln_bwd(dx1, s["r1"], gam(ln_mix_g, i), name="ln_bwd")
        d_down4 = d_down[:, :fs].reshape(4, rd, d)
        if i % 2 == 0:
            dp, d_pool, small["pool_scale", j] = _pool_bwd(s["x_in"], dr1, w[0], s["scale"], name="pool_bwd")
            dcur = _pool_adjoint(dp, dr1, n_groups, name="pool_adjoint")
            cg = d // n_groups
            d_pool4 = d_pool.reshape(n_groups, 4, cg // 4, cg).transpose(1, 0, 2, 3).reshape(4, n_groups * (cg // 4), cg)
            big_grads[i] = [d_pool4, d_up, d_down4]
        else:
            w_qkv = w[0]
            do = _mm_cols(dr1b, s["w_o3"], BF16, name="attn_out_bwd_act", transposed_b=True)
            d_wo = _mm_tn(s["o"].reshape(1, t, d), dr1b.reshape(1, t, d), (1, d, d), d // 2, d, 1,
                          (lambda u: 0, 2, lambda u, mb: mb), (lambda u: 0, lambda u: 0),
                          (lambda u: 0, lambda u, mb: mb, lambda u: 0), name="attn_out_bwd_w")
            dq, dk, dv = _attn_bwd(s["qkv"], do, n_heads, name="attn_bwd")
            dqkv = jnp.stack([dq, dk, dv])
            cq = w_qkv.shape[2]
            kb = cq // 3
            na, nbk = d // kb, cq // kb
            dcur = _mm_nt_acc(dqkv, w_qkv, dr1, kb, name="qkv_bwd_act")
            d_qkv = _mm_tn(s["x_in_b"].reshape(1, t, d), dqkv, (4, d, cq), d // 2, kb, 3 * na,
                           (lambda u: 0, 2, lambda u, mb: mb), (lambda u: u // na, lambda u: u % na),
                           (lambda u: u // nbk, lambda u, mb: mb, lambda u: u % nbk), name="qkv_bwd_w")
            big_grads[i] = [d_qkv, d_wo.reshape(4, d // 4, d), d_up, d_down4]
    grad_x = dcur.reshape(1, t, d)

    from_sib = _run_sides([_sibling_side(big_grads[0])], name="reduce_layer0_sibling")
    parts = [_chip_partial(g, fs_, core_arr, name="reduce_chip_partial") for g, fs_ in zip(big_grads[0], from_sib)]
    from_chips = _run_sides([_owner_chips_side([p[1] for p in parts])], name="reduce_layer0_chips")
    reduced[0] = finish_reduce(parts, from_chips, 0)

    names = [("pool_scale", j) for j in range(2)]
    for nm in ("ln_mix_g", "ln_mix_b", "ln_ffn_g", "ln_ffn_b", "conv_b", "conv_w"):
        names += [(nm, i) for i in range(DEPTH)]
    vecs = [small[k] for k in names]
    sbuf, ssizes = _pack_rows(vecs)
    summed = dict(zip(names, _unpack_rows(_all_reduce_small(sbuf, name="reduce_small"), ssizes, [v.shape for v in vecs])))

    def stack_layers(nm, count):
        return jnp.stack([summed[nm, i] for i in range(count)])

    g_small = {
        "pool_scale": stack_layers("pool_scale", 2).reshape(2, d),
        "ln_mix_g": stack_layers("ln_mix_g", DEPTH).reshape(DEPTH, d),
        "ln_mix_b": stack_layers("ln_mix_b", DEPTH).reshape(DEPTH, d),
        "ln_ffn_g": stack_layers("ln_ffn_g", DEPTH).reshape(DEPTH, d),
        "ln_ffn_b": stack_layers("ln_ffn_b", DEPTH).reshape(DEPTH, d),
        "conv_b": stack_layers("conv_b", DEPTH).reshape(DEPTH, 4, fp)[:, :, :fs].reshape(DEPTH, 4 * fs),
        "conv_w": lax.dynamic_index_in_dim(stack_layers("conv_w", DEPTH).reshape(DEPTH, 4, 3, fp), chip, axis=1,
                                           keepdims=False)[:, :, :fs],
    }
    w_small = {"pool_scale": (pool_scale, m_pool_scale, v_pool_scale), "ln_mix_g": (ln_mix_g, m_ln_mix_g, v_ln_mix_g),
               "ln_mix_b": (ln_mix_b, m_ln_mix_b, v_ln_mix_b), "ln_ffn_g": (ln_ffn_g, m_ln_ffn_g, v_ln_ffn_g),
               "ln_ffn_b": (ln_ffn_b, m_ln_ffn_b, v_ln_ffn_b), "conv_b": (ffn_conv_b, m_ffn_conv_b, v_ffn_conv_b),
               "conv_w": (ffn_conv_w, m_ffn_conv_w, v_ffn_conv_w)}
    order = list(g_small)
    packs = [_pack_rows([g_small[k] for k in order])[0]]
    for idx in range(3):
        packs.append(_pack_rows([w_small[k][idx] for k in order])[0])
    small_sizes = _pack_rows([g_small[k] for k in order])[1]
    small_out = _adamw(packs[0], packs[1][None], packs[2][None], packs[3][None], 0, None, name="adamw_small")
    shapes = [g_small[k].shape for k in order]
    small_res = {k: [] for k in order}
    for arr in small_out:
        for k, val in zip(order, _unpack_rows(arr[0], small_sizes, shapes)):
            small_res[k].append(val)

    def opt_layers(per_layer_grads, w_all, m_all, v_all, name, rows=None):
        n_layers = w_all.shape[0]
        flat = [a.reshape(n_layers, rows or a.shape[1], -1) for a in (w_all, m_all, v_all)]
        res = None
        for li, g in enumerate(per_layer_grads):
            res = _adamw(g, *flat, li, res, name=name)
        return [o.reshape(w_all.shape) for o in res]

    cg = d // n_groups
    big = {
        "pool_w": opt_layers([reduced[i][0] for i in (0, 2)], pool_w, m_pool_w, v_pool_w, "adamw_pool",
                             rows=n_groups * (cg // 4)),
        "attn_w_qkv": opt_layers([reduced[i][0] for i in (1, 3)], attn_w_qkv, m_attn_w_qkv, v_attn_w_qkv, "adamw_qkv"),
        "attn_w_o": opt_layers([reduced[i][1] for i in (1, 3)], attn_w_o, m_attn_w_o, v_attn_w_o, "adamw_wo"),
        "ffn_w_up": [jnp.transpose(o, (0, 2, 1))
                     for o in opt_layers([reduced[i][-2] for i in range(DEPTH)], *up_t, "adamw_up")],
        "ffn_w_down": opt_layers([reduced[i][-1] for i in range(DEPTH)], ffn_w_down, m_ffn_w_down, v_ffn_w_down,
                                 "adamw_down"),
    }

    def leaf(k, name):
        if name in big:
            return big[name][k]
        key = {"ffn_conv_w": "conv_w", "ffn_conv_b": "conv_b"}.get(name, name)
        return small_res[key][k]

    weight_names = ["pool_w", "pool_scale", "attn_w_qkv", "attn_w_o", "ffn_w_up", "ffn_conv_w", "ffn_conv_b",
                    "ffn_w_down", "ln_mix_g", "ln_mix_b", "ln_ffn_g", "ln_ffn_b"]
    outs = [loss, grad_x]
    for k in range(4):
        outs += [leaf(k, nm) for nm in weight_names]
    return tuple(outs)
```

```python
import collections

import jax
import jax.numpy as jnp
from jax import lax
from jax.experimental import pallas as pl
from jax.experimental.pallas import tpu as pltpu

F32, BF16 = jnp.float32, jnp.bfloat16
MESH = pl.DeviceIdType.MESH

LANES = 128
HEAD_DIM = 128
ATT_BLOCK = 128
ATT_WINDOW = 3 * ATT_BLOCK
ATT_FWD_HEADS = 4
ATT_BWD_HEADS = 2
POOL_WINDOWS = (2, 4, 8, 16)
POOL_HALO = 16
CONV_HALO = 8
LN_EPS = 1e-5
DEPTH = 4
ALPHA = (2.0 * DEPTH) ** 0.25
ATT_SCALE = HEAD_DIM ** -0.5
EXP_ZERO = 115.0
MASKED = 1e30
ADAM_LR, ADAM_B1, ADAM_B2, ADAM_EPS, ADAM_WD, ADAM_STEP = 0.001, 0.9, 0.999, 1e-08, 0.01, 10

VMEM_LIMIT = 56 << 20
ROW_TILE = 512
LN_ROW_TILE = 256
OPT_ROW_TILE = 128
SUM_ROW_TILE = 512


def _cp(n_axes):
    return pltpu.CompilerParams(dimension_semantics=("arbitrary",) * n_axes, vmem_limit_bytes=VMEM_LIMIT)


def _sds(shape, dtype):
    return jax.ShapeDtypeStruct(tuple(shape), dtype)


def _round_up(n, m):
    return (n + m - 1) // m * m


def _tile(n, cap, mult=8):
    if n <= cap:
        return n
    best = None
    for d in range(mult, cap + 1, mult):
        if n % d == 0:
            best = d
    assert best is not None, (n, cap)
    return best


_NT = (((1,), (1,)), ((), ()))
_TN = (((0,), (0,)), ((), ()))

_Side = collections.namedtuple("_Side", "ins outs alias sems start finish")


def _any_specs(n):
    return [pl.BlockSpec(memory_space=pl.ANY)] * n


def _call(body, first, last, side, *, grid, in_specs, out_specs, out_shape, scratch_shapes, name, args):
    n_axes = len(grid)
    if side is None:
        res = pl.pallas_call(body, grid=grid, in_specs=in_specs, out_specs=out_specs, out_shape=out_shape,
                             scratch_shapes=scratch_shapes, name=name, compiler_params=_cp(n_axes))(*args)
        return res, ()
    n_in, n_out, n_scr = len(in_specs), len(out_shape), len(scratch_shapes)
    s_in, s_out = len(side.ins), len(side.outs)

    def carried(*refs):
        ins, refs = refs[:n_in], refs[n_in:]
        side_ins, refs = refs[:s_in], refs[s_in:]
        outs, refs = refs[:n_out], refs[n_out:]
        side_outs, refs = refs[:s_out], refs[s_out:]
        scratch, side_sems = refs[:n_scr], refs[n_scr:]

        @pl.when(first())
        def _():
            side.start(side_ins, side_outs, side_sems)

        body(*ins, *outs, *scratch)

        @pl.when(last())
        def _():
            side.finish(side_ins, side_outs, side_sems)

    res = pl.pallas_call(
        carried, grid=grid, in_specs=list(in_specs) + _any_specs(s_in), out_specs=list(out_specs) + _any_specs(s_out),
        out_shape=list(out_shape) + list(side.outs), scratch_shapes=list(scratch_shapes) + list(side.sems),
        input_output_aliases={n_in + a: n_out + b for a, b in side.alias.items()},
        name=name, compiler_params=_cp(n_axes))(*args, *side.ins)
    return res[:n_out], res[n_out:]


def _mm_cols(a, b, out_dtype, name, transposed_b=False, side=None):
    t, k = a.shape
    g = b.shape[0]
    nb = b.shape[1] if transposed_b else b.shape[2]
    tm = _tile(t, ROW_TILE)
    steps = t // tm

    def body(a_ref, b_ref, o_ref):
        if transposed_b:
            acc = lax.dot_general(a_ref[...], b_ref[...], _NT, preferred_element_type=F32)
        else:
            acc = jnp.dot(a_ref[...], b_ref[...], preferred_element_type=F32)
        o_ref[...] = acc.astype(o_ref.dtype)

    first = lambda: jnp.logical_and(pl.program_id(0) == 0, pl.program_id(1) == 0)
    last = lambda: jnp.logical_and(pl.program_id(0) == g - 1, pl.program_id(1) == steps - 1)
    (out,), side_out = _call(
        body, first, last, side, grid=(g, steps),
        in_specs=[pl.BlockSpec((tm, k), lambda gi, i: (i, 0)),
                  pl.BlockSpec((None,) + b.shape[1:], lambda gi, i: (gi, 0, 0))],
        out_specs=[pl.BlockSpec((None, tm, nb), lambda gi, i: (gi, i, 0))],
        out_shape=[_sds((g, t, nb), out_dtype)], scratch_shapes=[], name=name, args=(a, b))
    return out if side is None else (out, side_out)


def _mm_nt_acc(a3, b3, res, kb, name, side=None, b_is_kn=False):
    ga, t, ka = a3.shape
    gb, n, kbb = (b3.shape[0], b3.shape[2], b3.shape[1]) if b_is_kn else b3.shape
    na, nbk = ka // kb, kbb // kb
    groups = ga * na
    assert groups == gb * nbk
    tm = _tile(t, ROW_TILE)
    steps = t // tm

    def body(a_ref, b_ref, res_ref, o_ref, acc):
        u = pl.program_id(1)

        @pl.when(u == 0)
        def _():
            acc[...] = ALPHA * res_ref[...]

        if b_is_kn:
            acc[...] += jnp.dot(a_ref[...], b_ref[...], preferred_element_type=F32)
        else:
            acc[...] += lax.dot_general(a_ref[...], b_ref[...], _NT, preferred_element_type=F32)

        @pl.when(u == groups - 1)
        def _():
            o_ref[...] = acc[...]

    first = lambda: jnp.logical_and(pl.program_id(0) == 0, pl.program_id(1) == 0)
    last = lambda: jnp.logical_and(pl.program_id(0) == steps - 1, pl.program_id(1) == groups - 1)
    if b_is_kn:
        b_spec = pl.BlockSpec((None, kb, n), lambda i, u: (u // nbk, u % nbk, 0))
    else:
        b_spec = pl.BlockSpec((None, n, kb), lambda i, u: (u // nbk, 0, u % nbk))
    (out,), side_out = _call(
        body, first, last, side, grid=(steps, groups),
        in_specs=[pl.BlockSpec((None, tm, kb), lambda i, u: (u // na, i, u % na)), b_spec,
                  pl.BlockSpec((tm, n), lambda i, u: (i, 0))],
        out_specs=[pl.BlockSpec((tm, n), lambda i, u: (i, 0))],
        out_shape=[_sds((t, n), F32)], scratch_shapes=[pltpu.VMEM((tm, n), F32)], name=name, args=(a3, b3, res))
    return out if side is None else (out, side_out)


def _mm_tn(x3, dy3, out_shape, bm, bn, groups, x_idx, dy_idx, out_idx, name, side=None):
    t = x3.shape[1]
    tm = _tile(t, 2 * ROW_TILE)
    grid = (groups, x_idx[1], t // tm)

    def body(x_ref, dy_ref, o_ref):
        @pl.when(pl.program_id(2) == 0)
        def _():
            o_ref[...] = jnp.zeros_like(o_ref)

        o_ref[...] += lax.dot_general(x_ref[...], dy_ref[...], _TN, preferred_element_type=F32)

    def at(corner):
        hit = pl.program_id(0) == corner[0]
        for axis in (1, 2):
            hit = jnp.logical_and(hit, pl.program_id(axis) == corner[axis])
        return hit

    (out,), side_out = _call(
        body, lambda: at((0, 0, 0)), lambda: at(tuple(g - 1 for g in grid)), side, grid=grid,
        in_specs=[pl.BlockSpec((None, tm, bm), lambda u, mb, i: (x_idx[0](u), i, x_idx[2](u, mb))),
                  pl.BlockSpec((None, tm, bn), lambda u, mb, i: (dy_idx[0](u), i, dy_idx[1](u)))],
        out_specs=[pl.BlockSpec((None, bm, bn), lambda u, mb, i: (out_idx[0](u), out_idx[1](u, mb), out_idx[2](u)))],
        out_shape=[_sds(out_shape, F32)], scratch_shapes=[], name=name, args=(x3, dy3))
    return out if side is None else (out, side_out)


def _layer_norm_rows(r, gamma, beta):
    mu = jnp.mean(r, axis=-1, keepdims=True)
    xc = r - mu
    var = jnp.mean(xc * xc, axis=-1, keepdims=True)
    return xc * lax.rsqrt(var + LN_EPS) * gamma + beta


def _mm_res_ln(a3, w3, res, gamma, beta, name, side=None):
    g, t, kb = a3.shape
    d = w3.shape[2]
    tm = _tile(t, LN_ROW_TILE)
    steps = t // tm

    def body(a_ref, w_hbm, res_ref, g_ref, b_ref, r_ref, o_ref, ob_ref, w_vmem, sem):
        @pl.when(pl.program_id(0) == 0)
        def _():
            cp = pltpu.make_async_copy(w_hbm, w_vmem, sem)
            cp.start()
            cp.wait()

        acc = ALPHA * res_ref[...]
        for gi in range(g):
            acc = acc + jnp.dot(a_ref[gi], w_vmem[gi], preferred_element_type=F32)
        r_ref[...] = acc
        out = _layer_norm_rows(acc, g_ref[...], b_ref[...])
        o_ref[...] = out
        ob_ref[...] = out.astype(BF16)

    row = pl.BlockSpec((tm, d), lambda i: (i, 0))
    vec = pl.BlockSpec((1, d), lambda i: (0, 0))
    outs, side_out = _call(
        body, lambda: pl.program_id(0) == 0, lambda: pl.program_id(0) == steps - 1, side, grid=(steps,),
        in_specs=[pl.BlockSpec((g, tm, kb), lambda i: (0, i, 0)), pl.BlockSpec(memory_space=pl.ANY), row, vec, vec],
        out_specs=[row, row, row],
        out_shape=[_sds((t, d), F32), _sds((t, d), F32), _sds((t, d), BF16)],
        scratch_shapes=[pltpu.VMEM(w3.shape, w3.dtype), pltpu.SemaphoreType.DMA],
        name=name, args=(a3, w3, res, gamma, beta))
    return outs if side is None else (outs, side_out)


def _ln_bwd(dout, r, gamma, name):
    t, d = r.shape
    tm = _tile(t, ROW_TILE)

    def body(do_ref, r_ref, g_ref, dr_ref, drb_ref, dg_ref, db_ref):
        @pl.when(pl.program_id(0) == 0)
        def _():
            dg_ref[...] = jnp.zeros_like(dg_ref)
            db_ref[...] = jnp.zeros_like(db_ref)

        rr = r_ref[...]
        do = do_ref[...]
        mu = jnp.mean(rr, axis=-1, keepdims=True)
        xc = rr - mu
        rstd = lax.rsqrt(jnp.mean(xc * xc, axis=-1, keepdims=True) + LN_EPS)
        xhat = xc * rstd
        dxh = do * g_ref[...]
        m1 = jnp.mean(dxh, axis=-1, keepdims=True)
        m2 = jnp.mean(dxh * xhat, axis=-1, keepdims=True)
        dr = rstd * (dxh - m1 - xhat * m2)
        dr_ref[...] = dr
        drb_ref[...] = dr.astype(BF16)
        dg_ref[...] += jnp.sum(do * xhat, axis=0, keepdims=True)
        db_ref[...] += jnp.sum(do, axis=0, keepdims=True)

    row = pl.BlockSpec((tm, d), lambda i: (i, 0))
    vec = pl.BlockSpec((1, d), lambda i: (0, 0))
    return pl.pallas_call(
        body, grid=(t // tm,), in_specs=[row, row, vec], out_specs=[row, row, vec, vec],
        out_shape=[_sds((t, d), F32), _sds((t, d), BF16), _sds((1, d), F32), _sds((1, d), F32)],
        name=name, compiler_params=_cp(1))(dout, r, gamma)


def _loss_and_grad(y, target, name):
    t, d = y.shape
    tm = _tile(t, ROW_TILE)
    steps = t // tm

    def body(y_ref, t_ref, loss_ref, dy_ref, acc):
        i = pl.program_id(0)

        @pl.when(i == 0)
        def _():
            acc[...] = jnp.zeros_like(acc)

        diff = y_ref[...] - t_ref[...]
        dy_ref[...] = diff * (1.0 / d)
        acc[...] += jnp.sum(diff * diff, axis=0, keepdims=True)

        @pl.when(i == steps - 1)
        def _():
            total = jnp.sum(acc[...], axis=1, keepdims=True) * (0.5 / d)
            loss_ref[...] = jnp.broadcast_to(total, loss_ref.shape)

    row = pl.BlockSpec((tm, d), lambda i: (i, 0))
    return pl.pallas_call(
        body, grid=(steps,), in_specs=[row, row],
        out_specs=[pl.BlockSpec((1, LANES), lambda i: (0, 0)), row],
        out_shape=[_sds((1, LANES), F32), _sds((t, d), F32)],
        scratch_shapes=[pltpu.VMEM((1, d), F32)], name=name, compiler_params=_cp(1))(y, target)


def _window_sums(ext, window, forward):
    n = ext.shape[0]
    s, span = ext, 1
    while span < window:
        s = s + pltpu.roll(s, (n - span) if forward else span, 0)
        span *= 2
    return s


def _pooled_group(main, halo, gi, row0):
    window = POOL_WINDOWS[gi]
    ext = jnp.concatenate([halo, main], axis=0)
    sums = _window_sums(ext, window, forward=False)[POOL_HALO:, :]
    pos = row0 + lax.broadcasted_iota(jnp.int32, (main.shape[0], 1), 0)
    cnt = jnp.minimum(pos + 1, window).astype(F32)
    return sums / cnt - main


def _pool_specs(t, d, tm):
    per = tm // POOL_HALO
    main = pl.BlockSpec((tm, d), lambda i: (i, 0))
    before = pl.BlockSpec((POOL_HALO, d), lambda i: (jnp.maximum(i * per - 1, 0), 0))
    return main, before


def _pool_fwd(x, w, scale, gamma, beta, name):
    t, d = x.shape
    ng, cg = w.shape[0], w.shape[1]
    tm = _tile(t, LN_ROW_TILE)

    def body(x_ref, h_ref, w_ref, s_ref, g_ref, b_ref, r_ref, o_ref, ob_ref):
        i = pl.program_id(0)
        for gi in range(ng):
            cols = pl.ds(gi * cg, cg)
            main = x_ref[:, cols]
            halo = jnp.where(i > 0, h_ref[:, cols], 0.0)
            pooled = _pooled_group(main, halo, gi, i * tm)
            y = jnp.dot(pooled.astype(BF16), w_ref[gi], preferred_element_type=F32)
            r_ref[:, cols] = ALPHA * main + y * s_ref[:, cols]
        out = _layer_norm_rows(r_ref[...], g_ref[...], b_ref[...])
        o_ref[...] = out
        ob_ref[...] = out.astype(BF16)

    main, before = _pool_specs(t, d, tm)
    vec = pl.BlockSpec((1, d), lambda i: (0, 0))
    return pl.pallas_call(
        body, grid=(t // tm,),
        in_specs=[main, before, pl.BlockSpec(w.shape, lambda i: (0, 0, 0)), vec, vec, vec],
        out_specs=[main, main, main],
        out_shape=[_sds((t, d), F32), _sds((t, d), F32), _sds((t, d), BF16)],
        name=name, compiler_params=_cp(1))(x, x, w, scale, gamma, beta)


def _pool_bwd(x, dy, w, scale, name):
    t, d = x.shape
    ng, cg = w.shape[0], w.shape[1]
    tm = _tile(t, LN_ROW_TILE)

    def body(x_ref, h_ref, dy_ref, w_ref, s_ref, dp_ref, dw_ref, ds_ref):
        i = pl.program_id(0)

        @pl.when(i == 0)
        def _():
            dw_ref[...] = jnp.zeros_like(dw_ref)
            ds_ref[...] = jnp.zeros_like(ds_ref)

        for gi in range(ng):
            cols = pl.ds(gi * cg, cg)
            main = x_ref[:, cols]
            halo = jnp.where(i > 0, h_ref[:, cols], 0.0)
            pooled = _pooled_group(main, halo, gi, i * tm).astype(BF16)
            y = jnp.dot(pooled, w_ref[gi], preferred_element_type=F32)
            dyg = dy_ref[:, cols]
            ds_ref[:, cols] += jnp.sum(dyg * y, axis=0, keepdims=True)
            dyw = (dyg * s_ref[:, cols]).astype(BF16)
            dw_ref[gi] += lax.dot_general(pooled, dyw, _TN, preferred_element_type=F32)
            dp_ref[:, cols] = lax.dot_general(dyw, w_ref[gi], _NT, preferred_element_type=F32)

    main, before = _pool_specs(t, d, tm)
    vec = pl.BlockSpec((1, d), lambda i: (0, 0))
    wspec = pl.BlockSpec(w.shape, lambda i: (0, 0, 0))
    return pl.pallas_call(
        body, grid=(t // tm,), in_specs=[main, before, main, wspec, vec],
        out_specs=[main, wspec, vec],
        out_shape=[_sds((t, d), F32), _sds(w.shape, F32), _sds((1, d), F32)],
        name=name, compiler_params=_cp(1))(x, x, dy, w, scale)


def _pool_adjoint(dp, dres, n_groups, name):
    t, d = dp.shape
    cg = d // n_groups
    tm = _tile(t, ROW_TILE)
    steps = t // tm
    per = tm // POOL_HALO

    def body(dp_ref, after_ref, dres_ref, dx_ref):
        i = pl.program_id(0)
        rows = lax.broadcasted_iota(jnp.int32, (tm, 1), 0)
        rows_after = lax.broadcasted_iota(jnp.int32, (POOL_HALO, 1), 0)
        for gi in range(n_groups):
            window = POOL_WINDOWS[gi]
            cols = pl.ds(gi * cg, cg)
            main = dp_ref[:, cols]
            cnt = jnp.minimum(i * tm + rows + 1, window).astype(F32)
            cnt_after = jnp.minimum((i + 1) * tm + rows_after + 1, window).astype(F32)
            after = jnp.where(i < steps - 1, after_ref[:, cols] / cnt_after, 0.0)
            ext = jnp.concatenate([main / cnt, after], axis=0)
            sums = _window_sums(ext, window, forward=True)[:tm, :]
            dx_ref[:, cols] = ALPHA * dres_ref[:, cols] + sums - main

    main = pl.BlockSpec((tm, d), lambda i: (i, 0))
    after = pl.BlockSpec((POOL_HALO, d), lambda i: (jnp.minimum((i + 1) * per, t // POOL_HALO - 1), 0))
    return pl.pallas_call(
        body, grid=(steps,), in_specs=[main, after, main], out_specs=main,
        out_shape=_sds((t, d), F32), name=name, compiler_params=_cp(1))(dp, dp, dres)


def _split_dot(x, tri):
    hi = x.astype(BF16)
    lo = (x - hi.astype(F32)).astype(BF16)
    return jnp.dot(hi, tri, preferred_element_type=F32) + jnp.dot(lo, tri, preferred_element_type=F32)


def _att_windows(qs, k_ws, limit, carry_rests, suffix):
    heads = range(len(qs))
    zs = [lax.dot_general(qs[hh], k_ws[hh], _NT, preferred_element_type=F32) * ATT_SCALE for hh in heads]
    visible = lax.broadcasted_iota(jnp.int32, zs[0].shape, 1) < limit
    zs = [jnp.where(visible, z, -MASKED) for z in zs]
    es = [jnp.exp(-jnp.abs(z)) for z in zs]
    log_nots = [-(jnp.maximum(z, 0.0) + jnp.log(1.0 + e)) for z, e in zip(zs, es)]
    rests = [_split_dot(ln, suffix[...]) + carry for ln, carry in zip(log_nots, carry_rests)]
    weights = [jnp.exp(z + r) for z, r in zip(zs, rests)]
    return zs, es, log_nots, weights


def _tri(w, strict):
    r = lax.broadcasted_iota(jnp.int32, (w, w), 0)
    c = lax.broadcasted_iota(jnp.int32, (w, w), 1)
    return ((r > c) if strict else (r >= c)).astype(BF16)


def _heads_per_step(qkv3, n_heads, most):
    cpb = qkv3.shape[2] // HEAD_DIM
    hp = most
    while cpb % hp or n_heads % hp:
        hp //= 2
    return hp


def _att_specs(qkv3, n_heads, hp):
    t = qkv3.shape[1]
    cpb = qkv3.shape[2] // HEAD_DIM
    wd = hp * HEAD_DIM

    def slab(off):
        return pl.BlockSpec((None, t, wd), lambda g, i: ((off + g * hp) // cpb, 0, ((off + g * hp) % cpb) // hp))

    q = pl.BlockSpec((None, ATT_BLOCK, wd), lambda g, i: ((g * hp) // cpb, i, ((g * hp) % cpb) // hp))
    return q, slab(n_heads), slab(2 * n_heads)


def _head_cols(hh):
    return pl.ds(hh * HEAD_DIM, HEAD_DIM)


def _key_bounds(k_ref, kmax, hp):
    for hh in range(hp):
        kf = k_ref[:, _head_cols(hh)].astype(F32)
        kmax[hh] = jnp.sqrt(jnp.max(jnp.sum(kf * kf, axis=1, keepdims=True)))


def _score_bound(q, key_norm):
    qf = q.astype(F32)
    return ATT_SCALE * 1.001 * key_norm * jnp.sqrt(jnp.sum(qf * qf, axis=1, keepdims=True)) + 1e-3


def _any_alive(rests, bounds):
    alive = jnp.max(rests[0] + bounds[0]) > -EXP_ZERO
    for r, zb in zip(rests[1:], bounds[1:]):
        alive = jnp.logical_or(alive, jnp.max(r + zb) > -EXP_ZERO)
    return alive


def _window_rows(hi, w):
    start = jnp.maximum(hi - w, 0)
    return start, pl.ds(pl.multiple_of(start, ATT_BLOCK), w)


def _attn_fwd(qkv3, n_heads, name):
    t = qkv3.shape[1]
    b = ATT_BLOCK
    w = min(ATT_WINDOW, t)
    hp = _heads_per_step(qkv3, n_heads, ATT_FWD_HEADS)
    heads = range(hp)

    def body(q_ref, k_ref, v_ref, o_ref, kmax, suffix):
        i = pl.program_id(1)

        @pl.when(i == 0)
        def _():
            _key_bounds(k_ref, kmax, hp)
            suffix[...] = _tri(w, strict=False)

        qs = [q_ref[:, _head_cols(hh)] for hh in heads]
        bounds = [_score_bound(qs[hh], kmax[hh]) for hh in heads]
        qpos = i * b + lax.broadcasted_iota(jnp.int32, (b, 1), 0)

        def cond(c):
            return jnp.logical_and(c[0] > 0, _any_alive(c[1], bounds))

        def step(c):
            hi, rests, accs = c
            start, rows = _window_rows(hi, w)
            limit = jnp.minimum(qpos, hi) - start
            k_ws = [k_ref[rows, _head_cols(hh)] for hh in heads]
            _, _, log_nots, weights = _att_windows(qs, k_ws, limit, rests, suffix)
            new_accs = tuple(accs[hh] + jnp.dot(weights[hh].astype(BF16), v_ref[rows, _head_cols(hh)],
                                                preferred_element_type=F32) for hh in heads)
            new_rests = tuple(rests[hh] + jnp.sum(log_nots[hh], axis=1, keepdims=True) for hh in heads)
            return start, new_rests, new_accs

        init = ((i + 1) * b, tuple(jnp.zeros((b, 1), F32) for _ in heads),
                tuple(jnp.zeros((b, HEAD_DIM), F32) for _ in heads))
        _, _, accs = lax.while_loop(cond, step, init)
        for hh in heads:
            o_ref[:, _head_cols(hh)] = accs[hh].astype(o_ref.dtype)

    qs_, ks_, vs_ = _att_specs(qkv3, n_heads, hp)
    return pl.pallas_call(
        body, grid=(n_heads // hp, t // b), in_specs=[qs_, ks_, vs_],
        out_specs=pl.BlockSpec((b, hp * HEAD_DIM), lambda g, i: (i, g)),
        out_shape=_sds((t, n_heads * HEAD_DIM), BF16),
        scratch_shapes=[pltpu.SMEM((hp,), F32), pltpu.VMEM((w, w), BF16)],
        name=name, compiler_params=_cp(2))(qkv3, qkv3, qkv3)


def _attn_bwd(qkv3, do, n_heads, name):
    t = qkv3.shape[1]
    b = ATT_BLOCK
    w = min(ATT_WINDOW, t)
    nq = t // b
    hp = _heads_per_step(qkv3, n_heads, ATT_BWD_HEADS)
    heads = range(hp)
    wd = hp * HEAD_DIM

    def body(q_ref, k_ref, v_ref, do_ref, dq_ref, dk_ref, dv_ref, kmax, dk_acc, dv_acc, suffix, strict_suffix):
        i = pl.program_id(1)

        @pl.when(i == 0)
        def _():
            _key_bounds(k_ref, kmax, hp)
            dk_acc[...] = jnp.zeros_like(dk_acc)
            dv_acc[...] = jnp.zeros_like(dv_acc)
            suffix[...] = _tri(w, strict=False)
            strict_suffix[...] = _tri(w, strict=True)

        qs = [q_ref[:, _head_cols(hh)] for hh in heads]
        douts = [do_ref[:, _head_cols(hh)] for hh in heads]
        bounds = [_score_bound(qs[hh], kmax[hh]) for hh in heads]
        zero_cols = tuple(jnp.zeros((b, 1), F32) for _ in heads)
        hi0 = (i + 1) * b
        qpos = i * b + lax.broadcasted_iota(jnp.int32, (b, 1), 0)

        def cond(c):
            return jnp.logical_and(c[0] > 0, _any_alive(c[1], bounds))

        def tiles_of(hi, rests):
            start, rows = _window_rows(hi, w)
            k_ws = [k_ref[rows, _head_cols(hh)] for hh in heads]
            dps = [lax.dot_general(douts[hh], v_ref[rows, _head_cols(hh)], _NT, preferred_element_type=F32)
                   for hh in heads]
            zs, es, log_nots, weights = _att_windows(qs, k_ws, jnp.minimum(qpos, hi) - start, rests, suffix)
            dlas = [a * dp for a, dp in zip(weights, dps)]
            return start, rows, (k_ws, zs, es, log_nots, weights, dlas)

        def row_sums(carries, tiles):
            return tuple(c + jnp.sum(x, axis=1, keepdims=True) for c, x in zip(carries, tiles))

        def first_pass(rows, tiles, rests, totals):
            _, _, _, log_nots, weights, dlas = tiles
            for hh in heads:
                dv_acc[rows, _head_cols(hh)] += lax.dot_general(weights[hh].astype(BF16), douts[hh], _TN,
                                                                preferred_element_type=F32)
            return row_sums(rests, log_nots), row_sums(totals, dlas)

        def second_pass(rows, tiles, totals, laters, dqs):
            k_ws, zs, es, _, _, dlas = tiles
            insides = [_split_dot(dla, strict_suffix[...]) for dla in dlas]
            dzs = []
            for hh in heads:
                dlog_not = totals[hh] - laters[hh] - insides[hh]
                inv = 1.0 / (1.0 + es[hh])
                sig = jnp.where(zs[hh] >= 0, inv, es[hh] * inv)
                dzs.append(((dlas[hh] - sig * dlog_not) * ATT_SCALE).astype(BF16))
            new_dqs = tuple(dqs[hh] + jnp.dot(dzs[hh], k_ws[hh], preferred_element_type=F32) for hh in heads)
            for hh in heads:
                dk_acc[rows, _head_cols(hh)] += lax.dot_general(dzs[hh], qs[hh], _TN, preferred_element_type=F32)
            return row_sums(laters, dlas), new_dqs

        start0, rows0, tiles0 = tiles_of(hi0, zero_cols)
        rests1, totals1 = first_pass(rows0, tiles0, zero_cols, zero_cols)

        def sweep1(c):
            hi, rests, totals = c
            start, rows, tiles = tiles_of(hi, rests)
            return (start,) + first_pass(rows, tiles, rests, totals)

        _, _, totals = lax.while_loop(cond, sweep1, (start0, rests1, totals1))
        laters1, dqs1 = second_pass(rows0, tiles0, totals, zero_cols,
                                    tuple(jnp.zeros((b, HEAD_DIM), F32) for _ in heads))

        def sweep2(c):
            hi, rests, laters, dqs = c
            start, rows, tiles = tiles_of(hi, rests)
            return (start, row_sums(rests, tiles[3])) + second_pass(rows, tiles, totals, laters, dqs)

        _, _, _, dqs = lax.while_loop(cond, sweep2, (start0, rests1, laters1, dqs1))
        for hh in heads:
            dq_ref[:, _head_cols(hh)] = dqs[hh].astype(dq_ref.dtype)

        @pl.when(i == nq - 1)
        def _():
            dk_ref[...] = dk_acc[...].astype(dk_ref.dtype)
            dv_ref[...] = dv_acc[...].astype(dv_ref.dtype)

    qs_, ks_, vs_ = _att_specs(qkv3, n_heads, hp)
    blk = pl.BlockSpec((b, wd), lambda g, i: (i, g))
    slab = pl.BlockSpec((t, wd), lambda g, i: (0, g))
    d = n_heads * HEAD_DIM
    return pl.pallas_call(
        body, grid=(n_heads // hp, nq),
        in_specs=[qs_, ks_, vs_, pl.BlockSpec((None, b, wd), lambda g, i: (0, i, g))],
        out_specs=[blk, slab, slab],
        out_shape=[_sds((t, d), BF16)] * 3,
        scratch_shapes=[pltpu.SMEM((hp,), F32), pltpu.VMEM((t, wd), F32), pltpu.VMEM((t, wd), F32),
                        pltpu.VMEM((w, w), BF16), pltpu.VMEM((w, w), BF16)],
        name=name, compiler_params=_cp(2))(qkv3, qkv3, qkv3, do)


def _conv_rows(main, halo, w_ref, b_ref):
    ext = jnp.concatenate([halo, main], axis=0)
    h1 = pltpu.roll(ext, 1, 0)[CONV_HALO:, :]
    h2 = pltpu.roll(ext, 2, 0)[CONV_HALO:, :]
    hc = b_ref[...] + w_ref[0:1, :] * h2
    hc = hc + w_ref[1:2, :] * h1
    hc = hc + w_ref[2:3, :] * main
    return hc, h1, h2


def _ffn_specs(t, fp, tm, half):
    per = tm // CONV_HALO
    main = lambda off: pl.BlockSpec((None, tm, fp), lambda g, i: (g + off, i, 0))
    before = lambda off: pl.BlockSpec((None, CONV_HALO, fp), lambda g, i: (g + off, jnp.maximum(i * per - 1, 0), 0))
    cw = lambda off: pl.BlockSpec((None, 3, fp), lambda g, i: (g + off, 0, 0))
    cb = lambda off: pl.BlockSpec((None, 1, fp), lambda g, i: (g + off, 0, 0))
    return [main(0), before(0), main(half), before(half), cw(0), cw(half), cb(0), cb(half)]


def _ffn_act(h, cw, cb, name, side=None):
    n, t, fp = h.shape
    half = n // 2
    tm = _tile(t, LN_ROW_TILE)
    steps = t // tm

    def body(hg_ref, hgb_ref, hv_ref, hvb_ref, wg_ref, wv_ref, bg_ref, bv_ref, a_ref):
        first = pl.program_id(1) == 0
        gate, _, _ = _conv_rows(hg_ref[...], jnp.where(first, 0.0, hgb_ref[...]), wg_ref, bg_ref)
        val, _, _ = _conv_rows(hv_ref[...], jnp.where(first, 0.0, hvb_ref[...]), wv_ref, bv_ref)
        a_ref[...] = (gate * jax.nn.sigmoid(gate) * val).astype(a_ref.dtype)

    (a,), side_out = _call(
        body, lambda: jnp.logical_and(pl.program_id(0) == 0, pl.program_id(1) == 0),
        lambda: jnp.logical_and(pl.program_id(0) == half - 1, pl.program_id(1) == steps - 1), side,
        grid=(half, steps), in_specs=_ffn_specs(t, fp, tm, half),
        out_specs=[pl.BlockSpec((None, tm, fp), lambda g, i: (g, i, 0))],
        out_shape=[_sds((half, t, fp), BF16)], scratch_shapes=[], name=name, args=(h, h, h, h, cw, cw, cb, cb))
    return a if side is None else (a, side_out)


def _act_grads(dact, gate, val):
    sig = jax.nn.sigmoid(gate)
    return dact * val * (sig * (1.0 + gate * (1.0 - sig))), dact * (gate * sig)


def _ffn_act_bwd(h, da, cw, cb, name):
    n, t, fp = h.shape
    half = n // 2
    tm = _tile(t, LN_ROW_TILE)

    def body(hg_ref, hgb_ref, hv_ref, hvb_ref, wg_ref, wv_ref, bg_ref, bv_ref, da_ref, dhc_ref, dw_ref, db_ref):
        first = pl.program_id(1) == 0

        @pl.when(first)
        def _():
            dw_ref[...] = jnp.zeros_like(dw_ref)
            db_ref[...] = jnp.zeros_like(db_ref)

        hg, hv = hg_ref[...], hv_ref[...]
        gate, hg1, hg2 = _conv_rows(hg, jnp.where(first, 0.0, hgb_ref[...]), wg_ref, bg_ref)
        val, hv1, hv2 = _conv_rows(hv, jnp.where(first, 0.0, hvb_ref[...]), wv_ref, bv_ref)
        dgate, dval = _act_grads(da_ref[...], gate, val)
        dhc_ref[0] = dgate
        dhc_ref[1] = dval
        for s, (dd, shifted) in enumerate(((dgate, (hg2, hg1, hg)), (dval, (hv2, hv1, hv)))):
            db_ref[s] += jnp.sum(dd, axis=0, keepdims=True)
            for kk in range(3):
                dw_ref[s, kk:kk + 1, :] += jnp.sum(dd * shifted[kk], axis=0, keepdims=True)

    specs = _ffn_specs(t, fp, tm, half) + [pl.BlockSpec((None, tm, fp), lambda g, i: (g, i, 0))]
    return pl.pallas_call(
        body, grid=(half, t // tm), in_specs=specs,
        out_specs=[pl.BlockSpec((2, None, tm, fp), lambda g, i: (0, g, i, 0)),
                   pl.BlockSpec((2, None, 3, fp), lambda g, i: (0, g, 0, 0)),
                   pl.BlockSpec((2, None, 1, fp), lambda g, i: (0, g, 0, 0))],
        out_shape=[_sds((2, half, t, fp), F32), _sds((2, half, 3, fp), F32), _sds((2, half, 1, fp), F32)],
        name=name, compiler_params=_cp(2))(h, h, h, h, cw, cw, cb, cb, da)


def _conv_adjoint(dhc, cw, name):
    n, t, fp = dhc.shape
    tm = _tile(t, ROW_TILE)
    steps = t // tm
    per = tm // CONV_HALO

    def body(d_ref, after_ref, w_ref, o_ref):
        main = d_ref[...]
        after = jnp.where(pl.program_id(1) < steps - 1, after_ref[...], 0.0)
        ext = jnp.concatenate([main, after], axis=0)
        rows = ext.shape[0]
        d1 = pltpu.roll(ext, rows - 1, 0)[:tm, :]
        d2 = pltpu.roll(ext, rows - 2, 0)[:tm, :]
        o_ref[...] = (w_ref[2:3, :] * main + w_ref[1:2, :] * d1 + w_ref[0:1, :] * d2).astype(o_ref.dtype)

    main = pl.BlockSpec((None, tm, fp), lambda g, i: (g, i, 0))
    after = pl.BlockSpec((None, CONV_HALO, fp), lambda g, i: (g, jnp.minimum((i + 1) * per, t // CONV_HALO - 1), 0))
    return pl.pallas_call(
        body, grid=(n, steps), in_specs=[main, after, pl.BlockSpec((None, 3, fp), lambda g, i: (g, 0, 0))],
        out_specs=main, out_shape=_sds((n, t, fp), BF16), name=name, compiler_params=_cp(2))(dhc, dhc, cw)


def _place():
    x, y, c = lax.axis_index("x"), lax.axis_index("y"), lax.axis_index("c")
    chips = [(1 - x, y), (x, 1 - y), (1 - x, 1 - y)]
    return x, y, c, chips


def _run_sides(sides, name):
    n_in = [len(s.ins) for s in sides]
    n_out = [len(s.outs) for s in sides]
    n_sem = [len(s.sems) for s in sides]

    def body(*refs):
        ins, outs, sems = refs[:sum(n_in)], refs[sum(n_in):sum(n_in) + sum(n_out)], refs[sum(n_in) + sum(n_out):]
        oi = oo = os_ = 0
        for k, s in enumerate(sides):
            mine = (ins[oi:oi + n_in[k]], outs[oo:oo + n_out[k]], sems[os_:os_ + n_sem[k]])
            s.start(*mine)
            s.finish(*mine)
            oi, oo, os_ = oi + n_in[k], oo + n_out[k], os_ + n_sem[k]

    aliases, oi, oo = {}, 0, 0
    for k, s in enumerate(sides):
        aliases.update({oi + a: oo + b for a, b in s.alias.items()})
        oi, oo = oi + n_in[k], oo + n_out[k]
    return pl.pallas_call(
        body, in_specs=_any_specs(sum(n_in)), out_specs=_any_specs(sum(n_out)),
        out_shape=[o for s in sides for o in s.outs], input_output_aliases=aliases,
        scratch_shapes=[q for s in sides for q in s.sems], name=name)(*[a for s in sides for a in s.ins])


def _place_shard(kind, w, chip, name, rows=None, base=None):
    if kind == "pool":
        g, r, cdim = w.shape

        def body(chip_ref, w_ref, o_ref):
            del chip_ref
            o_ref[...] = w_ref[...].astype(BF16)

        return pl.pallas_call(
            body,
            grid_spec=pltpu.PrefetchScalarGridSpec(
                num_scalar_prefetch=1, grid=(1,),
                in_specs=[pl.BlockSpec((g, r, cdim), lambda i, chip_ref: (0, 0, 0))],
                out_specs=pl.BlockSpec((g, r, cdim), lambda i, chip_ref: (0, chip_ref[0], 0))),
            out_shape=_sds((g, 4 * r, cdim), BF16), name=name, compiler_params=_cp(1))(chip, w)

    r, cs = w.shape
    if kind == "lead":
        rows = rows or r
        tr = _tile(r, ROW_TILE, 16) if rows == r else rows - r
        assert r % tr == 0 and tr % 16 == 0
        n_src = r // tr

        def body(chip_ref, w_ref, o_ref):
            del chip_ref
            o_ref[...] = jnp.where(pl.program_id(0) < n_src, w_ref[...], 0.0).astype(BF16)

        return pl.pallas_call(
            body,
            grid_spec=pltpu.PrefetchScalarGridSpec(
                num_scalar_prefetch=1, grid=(rows // tr,),
                in_specs=[pl.BlockSpec((tr, cs), lambda i, chip_ref: (jnp.minimum(i, n_src - 1), 0))],
                out_specs=pl.BlockSpec((None, tr, cs), lambda i, chip_ref: (chip_ref[0], i, 0))),
            out_shape=_sds((4, rows, cs), BF16), name=name, compiler_params=_cp(1))(chip, w)

    assert kind == "down"
    tr = r // 2 if (r // 2) % 16 == 0 else r
    per = r // tr

    def body(chip_ref, w_ref, base_ref, o_ref):
        del chip_ref, base_ref
        o_ref[...] = w_ref[...].astype(BF16)

    return pl.pallas_call(
        body,
        grid_spec=pltpu.PrefetchScalarGridSpec(
            num_scalar_prefetch=1, grid=(per,),
            in_specs=[pl.BlockSpec((tr, cs), lambda i, chip_ref: (i, 0)), pl.BlockSpec(memory_space=pl.ANY)],
            out_specs=pl.BlockSpec((None, tr, cs), lambda i, chip_ref: (chip_ref[0] // 2, (chip_ref[0] % 2) * per + i, 0))),
        out_shape=_sds(base.shape, BF16), input_output_aliases={2: 0},
        name=name, compiler_params=_cp(1))(chip, w, base)


def _gather_sides(items, bufs=None):
    n = len(items)
    kinds = [it[0] for it in items]
    shard_rows = [it[2] for it in items]
    bufs = [it[1] for it in items] if bufs is None else list(bufs)

    def half_of(outs, m, chip, half):
        k = 2 * chip[0] + chip[1]
        o, r = outs[m], shard_rows[m]
        if kinds[m] == "pool":
            gh = o.shape[0] // 2
            return o.at[pl.ds(half * gh, gh), pl.ds(k * r, r)]
        r2 = r // 2
        if kinds[m] == "down":
            return o.at[k // 2, pl.ds((k % 2) * r + half * r2, r2)]
        return o.at[k, pl.ds(half * r2, r2)]

    def remote(outs, sems, m, j, chip, half, to):
        ref = half_of(outs, m, chip, half)
        return pltpu.make_async_remote_copy(src_ref=ref, dst_ref=ref, send_sem=sems[0].at[m, j],
                                            recv_sem=sems[1].at[m, j], device_id=to, device_id_type=MESH)

    def ici_copies(outs, sems, sending):
        x, y, c, chips = _place()
        if sending:
            return [remote(outs, sems, m, j, (x, y), c, (*chip, c)) for m in range(n) for j, chip in enumerate(chips)]
        return [remote(outs, sems, m, j, chip, c, (x, y, c)) for m in range(n) for j, chip in enumerate(chips)]

    def d2d_copies(outs, sems, sending):
        x, y, c, chips = _place()
        if sending:
            return [remote(outs, sems, m, j, chip, c, (x, y, 1 - c)) for m in range(n) for j, chip in enumerate(chips)]
        return [remote(outs, sems, m, j, chip, 1 - c, (x, y, c)) for m in range(n) for j, chip in enumerate(chips)]

    def phase(copies):
        def start(ins, outs, sems):
            for cp in copies(outs, sems, True):
                cp.start()

        def finish(ins, outs, sems):
            for cp in copies(outs, sems, False):
                cp.wait_recv()
            for cp in copies(outs, sems, True):
                cp.wait_send()

        return start, finish

    ici, d2d = phase(ici_copies), phase(d2d_copies)

    def both_finish(ins, outs, sems):
        ici[1](ins, outs, sems[:2])
        d2d[0](ins, outs, sems[2:])
        d2d[1](ins, outs, sems[2:])

    pair = [pltpu.SemaphoreType.DMA((n, 3)), pltpu.SemaphoreType.DMA((n, 3))]
    shapes = [_sds(b.shape, b.dtype) for b in bufs]
    alias = {m: m for m in range(n)}

    def side(which):
        if which == "both":
            return _Side(bufs, shapes, alias, pair + pair, lambda i, o, s: ici[0](i, o, s[:2]), both_finish)
        start, finish = ici if which == "ici" else d2d
        return _Side(bufs, shapes, alias, pair, start, finish)

    return side


def _sibling_side(grads):
    n = len(grads)

    def copies(ins, outs, sems):
        x, y, c, _ = _place()
        res = []
        for m in range(n):
            r2 = ins[m].shape[1] // 2
            res.append(pltpu.make_async_remote_copy(
                src_ref=ins[m].at[:, pl.ds((1 - c) * r2, r2)], dst_ref=outs[m],
                send_sem=sems[0].at[m], recv_sem=sems[1].at[m], device_id=(x, y, 1 - c), device_id_type=MESH))
        return res

    def start(ins, outs, sems):
        for cp in copies(ins, outs, sems):
            cp.start()

    def finish(ins, outs, sems):
        for cp in copies(ins, outs, sems):
            cp.wait_recv()
        for cp in copies(ins, outs, sems):
            cp.wait_send()

    return _Side(list(grads), [_sds((4, g.shape[1] // 2, g.shape[2]), g.dtype) for g in grads], {},
                 [pltpu.SemaphoreType.DMA((n,)), pltpu.SemaphoreType.DMA((n,))], start, finish)


def _owner_chips_side(parts):
    n = len(parts)

    def copies(ins, outs, sems):
        _, _, c, chips = _place()
        return [pltpu.make_async_remote_copy(
            src_ref=ins[m].at[2 * chip[0] + chip[1]], dst_ref=outs[m].at[j], send_sem=sems[0].at[m, j],
            recv_sem=sems[1].at[m, j], device_id=(*chip, c), device_id_type=MESH)
            for m in range(n) for j, chip in enumerate(chips)]

    def start(ins, outs, sems):
        for cp in copies(ins, outs, sems):
            cp.start()

    def finish(ins, outs, sems):
        for cp in copies(ins, outs, sems):
            cp.wait_recv()
        for cp in copies(ins, outs, sems):
            cp.wait_send()

    return _Side(list(parts), [_sds((3,) + p.shape[1:], p.dtype) for p in parts], {},
                 [pltpu.SemaphoreType.DMA((n, 3)), pltpu.SemaphoreType.DMA((n, 3))], start, finish)


def _exchange_finished_halves(shards, name):
    n = len(shards)

    def body(*refs):
        out = refs[n:2 * n]
        send_sems, recv_sems = refs[2 * n:]
        x, y, c, _ = _place()
        copies = []
        for m in range(n):
            r2 = out[m].shape[0] // 2
            mine = out[m].at[pl.ds(c * r2, r2)]
            copies.append(pltpu.make_async_remote_copy(
                src_ref=mine, dst_ref=mine, send_sem=send_sems.at[m], recv_sem=recv_sems.at[m],
                device_id=(x, y, 1 - c), device_id_type=MESH))
        for cp in copies:
            cp.start()
        for m in range(n):
            r2 = out[m].shape[0] // 2
            theirs = out[m].at[pl.ds((1 - c) * r2, r2)]
            pltpu.make_async_remote_copy(
                src_ref=theirs, dst_ref=theirs, send_sem=send_sems.at[m], recv_sem=recv_sems.at[m],
                device_id=(x, y, 1 - c), device_id_type=MESH).wait_recv()
        for cp in copies:
            cp.wait_send()

    return pl.pallas_call(
        body, in_specs=_any_specs(n), out_specs=_any_specs(n), out_shape=[_sds(s.shape, s.dtype) for s in shards],
        input_output_aliases={m: m for m in range(n)},
        scratch_shapes=[pltpu.SemaphoreType.DMA((n,)), pltpu.SemaphoreType.DMA((n,))], name=name)(*shards)


def _all_reduce_small(v, name):
    rows = v.shape[0]

    def body(v_ref, out_ref, buf, send_sems, recv_sems, local_sem):
        x, y, c, chips = _place()
        me, sibling = (x, y, c), (x, y, 1 - c)

        def slot(px, py, pc):
            return buf.at[4 * px + 2 * py + pc]

        def copy(k, block, to, src=None):
            return pltpu.make_async_remote_copy(
                src_ref=slot(*block) if src is None else src, dst_ref=slot(*block),
                send_sem=send_sems.at[k], recv_sem=recv_sems.at[k], device_id=to, device_id_type=MESH)

        mine = pltpu.make_async_copy(v_ref, slot(*me), local_sem)
        mine.start()
        first = [copy(0, me, sibling, src=v_ref)]
        first += [copy(1 + j, me, (*chip, c), src=v_ref) for j, chip in enumerate(chips)]
        for cp in first:
            cp.start()
        passed = [copy(4 + j, (*chip, c), sibling) for j, chip in enumerate(chips)]
        for j, chip in enumerate(chips):
            copy(1 + j, (*chip, c), me).wait_recv()
            passed[j].start()
        copy(0, sibling, me).wait_recv()
        for j, chip in enumerate(chips):
            copy(4 + j, (*chip, 1 - c), me).wait_recv()
        for cp in first + passed:
            cp.wait_send()
        mine.wait()
        total = buf[0]
        for dev in range(1, 8):
            total = total + buf[dev]
        out_ref[...] = total

    vm = pl.BlockSpec(memory_space=pltpu.VMEM)
    return pl.pallas_call(
        body, in_specs=[vm], out_specs=vm, out_shape=_sds(v.shape, F32),
        scratch_shapes=[pltpu.VMEM((8, rows, LANES), F32), pltpu.SemaphoreType.DMA((7,)),
                        pltpu.SemaphoreType.DMA((7,)), pltpu.SemaphoreType.DMA],
        name=name, compiler_params=pltpu.CompilerParams(vmem_limit_bytes=VMEM_LIMIT))(v)


def _chip_partial(grad, from_sibling, core, name):
    _, r, cdim = grad.shape
    r2 = r // 2
    tr = _tile(r2, SUM_ROW_TILE)
    per = r2 // tr

    def body(core_ref, g_ref, s_ref, o_ref, ob_ref):
        del core_ref
        total = g_ref[...] + s_ref[...]
        o_ref[...] = total
        ob_ref[...] = total.astype(BF16)

    blk = pl.BlockSpec((None, tr, cdim), lambda k, i, core_ref: (k, i, 0))
    mine = pl.BlockSpec((None, tr, cdim), lambda k, i, core_ref: (k, core_ref[0] * per + i, 0))
    return pl.pallas_call(
        body,
        grid_spec=pltpu.PrefetchScalarGridSpec(num_scalar_prefetch=1, grid=(4, per), in_specs=[mine, blk],
                                               out_specs=[blk, blk]),
        out_shape=[_sds((4, r2, cdim), F32), _sds((4, r2, cdim), BF16)],
        name=name, compiler_params=_cp(2))(core, grad, from_sibling)


def _owner_sum(partial, from_chips, place, name):
    _, r2, cdim = partial.shape
    tr = _tile(r2, SUM_ROW_TILE)
    per = r2 // tr

    def body(place_ref, p_ref, f_ref, o_ref):
        del place_ref
        total = p_ref[...]
        for j in range(3):
            total = total + f_ref[j].astype(F32)
        o_ref[...] = total

    return pl.pallas_call(
        body,
        grid_spec=pltpu.PrefetchScalarGridSpec(
            num_scalar_prefetch=1, grid=(per,),
            in_specs=[pl.BlockSpec((None, tr, cdim), lambda i, place_ref: (place_ref[0], i, 0)),
                      pl.BlockSpec((3, tr, cdim), lambda i, place_ref: (0, i, 0))],
            out_specs=pl.BlockSpec((tr, cdim), lambda i, place_ref: (place_ref[1] * per + i, 0))),
        out_shape=_sds((2 * r2, cdim), F32), name=name, compiler_params=_cp(1))(place, partial, from_chips)


def _adamw(g, w, m, v, layer, prev, name):
    _, r, cdim = w.shape
    tr = _tile(r, OPT_ROW_TILE)
    c1 = 1.0 / (1.0 - ADAM_B1 ** ADAM_STEP)
    c2 = 1.0 / (1.0 - ADAM_B2 ** ADAM_STEP)
    n_prev = 0 if prev is None else 4

    def body(g_ref, w_ref, m_ref, v_ref, *rest):
        go_ref, d_ref, mo_ref, vo_ref = rest[n_prev:]
        grad = g_ref[:, pl.ds(0, cdim)]
        m_new = ADAM_B1 * m_ref[...] + (1.0 - ADAM_B1) * grad
        v_new = ADAM_B2 * v_ref[...] + (1.0 - ADAM_B2) * (grad * grad)
        go_ref[...] = grad
        mo_ref[...] = m_new
        vo_ref[...] = v_new
        d_ref[...] = -ADAM_LR * ((m_new * c1) / (jnp.sqrt(v_new * c2) + ADAM_EPS) + ADAM_WD * w_ref[...])

    blk = pl.BlockSpec((None, tr, cdim), lambda i: (layer, i, 0))
    gblk = pl.BlockSpec((tr, g.shape[1]), lambda i: (i, 0))
    return pl.pallas_call(
        body, grid=(r // tr,), in_specs=[gblk, blk, blk, blk] + _any_specs(n_prev), out_specs=[blk] * 4,
        out_shape=[_sds(w.shape, F32)] * 4, input_output_aliases={4 + k: k for k in range(n_prev)},
        name=name, compiler_params=_cp(1))(g, w, m, v, *(prev or ()))


def _pack_rows(vectors):
    flat = [v.reshape(-1) for v in vectors]
    sizes = [f.shape[0] for f in flat]
    total = sum(sizes)
    padded = _round_up(total, 8 * LANES)
    buf = jnp.concatenate(flat + [jnp.zeros((padded - total,), F32)])
    return buf.reshape(padded // LANES, LANES), sizes


def _unpack_rows(buf, sizes, shapes):
    flat = buf.reshape(-1)
    out, off = [], 0
    for n, shp in zip(sizes, shapes):
        out.append(flat[off:off + n].reshape(shp))
        off += n
    return out


def kernel(x, pool_w, pool_scale, attn_w_qkv, attn_w_o, ffn_w_up, ffn_conv_w, ffn_conv_b, ffn_w_down, ln_mix_g, ln_mix_b, ln_ffn_g, ln_ffn_b, loss_target, m_pool_w, m_pool_scale, m_attn_w_qkv, m_attn_w_o, m_ffn_w_up, m_ffn_conv_w, m_ffn_conv_b, m_ffn_w_down, m_ln_mix_g, m_ln_mix_b, m_ln_ffn_g, m_ln_ffn_b, v_pool_w, v_pool_scale, v_attn_w_qkv, v_attn_w_o, v_ffn_w_up, v_ffn_conv_w, v_ffn_conv_b, v_ffn_w_down, v_ln_mix_g, v_ln_mix_b, v_ln_ffn_g, v_ln_ffn_b):
    t, d = x.shape[1], x.shape[2]
    n_heads = d // HEAD_DIM
    n_groups = pool_w.shape[1]
    fs = ffn_w_up.shape[2]
    fp = _round_up(fs, LANES)
    rd = ffn_w_down.shape[1]
    assert 2 * rd == fs
    xi, yi, ci = lax.axis_index("x"), lax.axis_index("y"), lax.axis_index("c")
    chip = (2 * xi + yi).astype(jnp.int32)
    chip_arr, core_arr = chip.reshape(1), ci.astype(jnp.int32).reshape(1)
    place_arr = jnp.concatenate([chip_arr, core_arr])

    x2 = x.reshape(t, d)
    target = loss_target.reshape(t, d)
    pad_cols = lambda a: jnp.pad(a, [(0, 0)] * (a.ndim - 1) + [(0, fp - fs)])
    up_t = [jnp.transpose(a, (0, 2, 1)) for a in (ffn_w_up, m_ffn_w_up, v_ffn_w_up)]

    gather_items = []
    for i in range(DEPTH):
        j = i // 2
        items = []
        if i % 2 == 0:
            items.append(("pool", _place_shard("pool", pool_w[j], chip_arr, name="place_pool"), pool_w.shape[2]))
        else:
            items.append(("lead", _place_shard("lead", attn_w_qkv[j], chip_arr, name="place_qkv"), d))
            items.append(("lead", _place_shard("lead", attn_w_o[j], chip_arr, name="place_wo"), attn_w_o.shape[1]))
        items.append(("lead", _place_shard("lead", up_t[0][i], chip_arr, name="place_up", rows=fp), fp))
        items.append(("down", _place_shard("down", ffn_w_down[i], chip_arr, name="place_down",
                                           base=jnp.zeros((2, fp, d), BF16)), rd))
        gather_items.append(items)
    weights = [None] * DEPTH
    weights[0] = _run_sides([_gather_sides(gather_items[0])("both")], name="gather_layer0")

    conv_b_all = pad_cols(ffn_conv_b.reshape(DEPTH, 4, 1, fs))
    cw_local = pad_cols(ffn_conv_w)
    slot = (jnp.arange(4, dtype=jnp.int32) == chip).astype(F32) * (1.0 - ci.astype(F32))
    cw_placed = slot[None, :, None, None] * cw_local[:, None]
    cw_buf, cw_sizes = _pack_rows([cw_placed])
    conv_w_all = _unpack_rows(_all_reduce_small(cw_buf, name="gather_conv_w"), cw_sizes, [cw_placed.shape])[0]

    gam = lambda a, i: a[i].reshape(1, d)

    saved = []
    cur, cur_b = x2, x2.astype(BF16)
    for i in range(DEPTH):
        j = i // 2
        w = weights[i]
        s = {"x_in": cur, "x_in_b": cur_b}
        if i % 2 == 0:
            w_pool, w_up, w_down = w
            s["scale"] = pool_scale[j].reshape(1, d)
            r1, x1, x1b = _pool_fwd(cur, w_pool, s["scale"], gam(ln_mix_g, i), gam(ln_mix_b, i), name="pool_fwd")
        else:
            w_qkv, w_o, w_up, w_down = w
            w_o3 = w_o.reshape(1, d, d)
            qkv = _mm_cols(cur_b, w_qkv, BF16, name="qkv_proj")
            o = _attn_fwd(qkv, n_heads, name="attn_fwd")
            s["qkv"], s["o"], s["w_o3"] = qkv, o, w_o3
            r1, x1, x1b = _mm_res_ln(o.reshape(1, t, d), w_o3, cur, gam(ln_mix_g, i), gam(ln_mix_b, i),
                                     name="attn_out_ln")
        if i + 1 < DEPTH:
            nxt = gather_items[i + 1]
            n_mix = len(nxt) - 2
            h, landed_ffn = _mm_cols(x1b, w_up, F32, name="ffn_up", transposed_b=True,
                                     side=_gather_sides(nxt[n_mix:])("ici"))
            a, landed_mix = _ffn_act(h, conv_w_all[i], conv_b_all[i], name="ffn_act",
                                     side=_gather_sides(nxt[:n_mix])("ici"))
            landed = list(landed_mix) + list(landed_ffn)
            (r2, x2n, x2b), gathered = _mm_res_ln(a, w_down, x1, gam(ln_ffn_g, i), gam(ln_ffn_b, i),
                                                  name="ffn_down_ln", side=_gather_sides(nxt, landed)("d2d"))
            weights[i + 1] = list(gathered)
        else:
            h = _mm_cols(x1b, w_up, F32, name="ffn_up", transposed_b=True)
            a = _ffn_act(h, conv_w_all[i], conv_b_all[i], name="ffn_act")
            r2, x2n, x2b = _mm_res_ln(a, w_down, x1, gam(ln_ffn_g, i), gam(ln_ffn_b, i), name="ffn_down_ln")
        s.update(r1=r1, x1b=x1b, h=h, a=a, r2=r2)
        saved.append(s)
        cur, cur_b = x2n, x2b

    loss_row, dcur = _loss_and_grad(cur, target, name="loss")
    loss = lax.psum(loss_row[0, 0], ("x", "y", "c"))

    big_grads = [None] * DEPTH
    reduced = [None] * DEPTH
    small = {}

    def finish_reduce(parts, from_chips, layer):
        halves = [_owner_sum(p[0], fc, place_arr, name="reduce_owner_sum") for p, fc in zip(parts, from_chips)]
        return _exchange_finished_halves(halves, name="reduce_halves_pool" if layer % 2 == 0 else "reduce_halves_attn")

    for i in reversed(range(DEPTH)):
        j = i // 2
        s, w = saved[i], weights[i]
        w_up, w_down = w[-2], w[-1]
        dr2, dr2b, small["ln_ffn_g", i], small["ln_ffn_b", i] = _ln_bwd(dcur, s["r2"], gam(ln_ffn_g, i), name="ln_bwd")
        pending = big_grads[i + 1] if i + 1 < DEPTH else None
        if pending is not None:
            da, from_sib = _mm_cols(dr2b, w_down, F32, name="ffn_down_bwd_act", transposed_b=True,
                                    side=_sibling_side(pending))
            parts = [_chip_partial(g, fs_, core_arr, name="reduce_chip_partial") for g, fs_ in zip(pending, from_sib)]
        else:
            da = _mm_cols(dr2b, w_down, F32, name="ffn_down_bwd_act", transposed_b=True)
        dr2b3 = dr2b.reshape(1, t, d)
        nmb = 2
        d_down = _mm_tn(s["a"], dr2b3, (2, fp, d), fp // nmb, d, 2,
                        (lambda u: u, nmb, lambda u, mb: mb), (lambda u: 0, lambda u: 0),
                        (lambda u: u, lambda u, mb: mb, lambda u: 0), name="ffn_down_bwd_w")
        dhc, dcw, dcb = _ffn_act_bwd(s["h"], da, conv_w_all[i], conv_b_all[i], name="ffn_act_bwd")
        small["conv_w", i], small["conv_b", i] = dcw, dcb
        dh = _conv_adjoint(dhc.reshape(4, t, fp), conv_w_all[i], name="ffn_conv_adjoint")
        up_w_args = (dh, s["x1b"].reshape(1, t, d), (4, fp, d), fp // 2, d, 4,
                     (lambda u: u, 2, lambda u, mb: mb), (lambda u: 0, lambda u: 0),
                     (lambda u: u, lambda u, mb: mb, lambda u: 0))
        if pending is not None:
            n_mix = len(parts) - 2
            dx1, chips_ffn = _mm_nt_acc(dh, w_up, dr2, fp, name="ffn_up_bwd_act", b_is_kn=True,
                                        side=_owner_chips_side([p[1] for p in parts[n_mix:]]))
            d_up, chips_mix = _mm_tn(*up_w_args, name="ffn_up_bwd_w",
                                     side=_owner_chips_side([p[1] for p in parts[:n_mix]]))
            reduced[i + 1] = finish_reduce(parts, list(chips_mix) + list(chips_ffn), i + 1)
        else:
            dx1 = _mm_nt_acc(dh, w_up, dr2, fp, name="ffn_up_bwd_act", b_is_kn=True)
            d_up = _mm_tn(*up_w_args, name="ffn_up_bwd_w")
        dr1, dr1b, small["ln_mix_g", i], small["ln_mix_b", i] = _ln_bwd(dx1, s["r1"], gam(ln_mix_g, i), name="ln_bwd")
        d_down4 = d_down[:, :fs].reshape(4, rd, d)
        if i % 2 == 0:
            dp, d_pool, small["pool_scale", j] = _pool_bwd(s["x_in"], dr1, w[0], s["scale"], name="pool_bwd")
            dcur = _pool_adjoint(dp, dr1, n_groups, name="pool_adjoint")
            cg = d // n_groups
            d_pool4 = d_pool.reshape(n_groups, 4, cg // 4, cg).transpose(1, 0, 2, 3).reshape(4, n_groups * (cg // 4), cg)
            big_grads[i] = [d_pool4, d_up, d_down4]
        else:
            w_qkv = w[0]
            do = _mm_cols(dr1b, s["w_o3"], BF16, name="attn_out_bwd_act", transposed_b=True)
            d_wo = _mm_tn(s["o"].reshape(1, t, d), dr1b.reshape(1, t, d), (1, d, d), d // 2, d, 1,
                          (lambda u: 0, 2, lambda u, mb: mb), (lambda u: 0, lambda u: 0),
                          (lambda u: 0, lambda u, mb: mb, lambda u: 0), name="attn_out_bwd_w")
            dq, dk, dv = _attn_bwd(s["qkv"], do, n_heads, name="attn_bwd")
            dqkv = jnp.stack([dq, dk, dv])
            cq = w_qkv.shape[2]
            kb = cq // 3
            na, nbk = d // kb, cq // kb
            dcur = _mm_nt_acc(dqkv, w_qkv, dr1, kb, name="qkv_bwd_act")
            d_qkv = _mm_tn(s["x_in_b"].reshape(1, t, d), dqkv, (4, d, cq), d // 2, kb, 3 * na,
                           (lambda u: 0, 2, lambda u, mb: mb), (lambda u: u // na, lambda u: u % na),
                           (lambda u: u // nbk, lambda u, mb: mb, lambda u: u % nbk), name="qkv_bwd_w")
            big_grads[i] = [d_qkv, d_wo.reshape(4, d // 4, d), d_up, d_down4]
    grad_x = dcur.reshape(1, t, d)

    from_sib = _run_sides([_sibling_side(big_grads[0])], name="reduce_layer0_sibling")
    parts = [_chip_partial(g, fs_, core_arr, name="reduce_chip_partial") for g, fs_ in zip(big_grads[0], from_sib)]
    from_chips = _run_sides([_owner_chips_side([p[1] for p in parts])], name="reduce_layer0_chips")
    reduced[0] = finish_reduce(parts, from_chips, 0)

    names = [("pool_scale", j) for j in range(2)]
    for nm in ("ln_mix_g", "ln_mix_b", "ln_ffn_g", "ln_ffn_b", "conv_b", "conv_w"):
        names += [(nm, i) for i in range(DEPTH)]
    vecs = [small[k] for k in names]
    sbuf, ssizes = _pack_rows(vecs)
    summed = dict(zip(names, _unpack_rows(_all_reduce_small(sbuf, name="reduce_small"), ssizes, [v.shape for v in vecs])))

    def stack_layers(nm, count):
        return jnp.stack([summed[nm, i] for i in range(count)])

    g_small = {
        "pool_scale": stack_layers("pool_scale", 2).reshape(2, d),
        "ln_mix_g": stack_layers("ln_mix_g", DEPTH).reshape(DEPTH, d),
        "ln_mix_b": stack_layers("ln_mix_b", DEPTH).reshape(DEPTH, d),
        "ln_ffn_g": stack_layers("ln_ffn_g", DEPTH).reshape(DEPTH, d),
        "ln_ffn_b": stack_layers("ln_ffn_b", DEPTH).reshape(DEPTH, d),
        "conv_b": stack_layers("conv_b", DEPTH).reshape(DEPTH, 4, fp)[:, :, :fs].reshape(DEPTH, 4 * fs),
        "conv_w": lax.dynamic_index_in_dim(stack_layers("conv_w", DEPTH).reshape(DEPTH, 4, 3, fp), chip, axis=1,
                                           keepdims=False)[:, :, :fs],
    }
    w_small = {"pool_scale": (pool_scale, m_pool_scale, v_pool_scale), "ln_mix_g": (ln_mix_g, m_ln_mix_g, v_ln_mix_g),
               "ln_mix_b": (ln_mix_b, m_ln_mix_b, v_ln_mix_b), "ln_ffn_g": (ln_ffn_g, m_ln_ffn_g, v_ln_ffn_g),
               "ln_ffn_b": (ln_ffn_b, m_ln_ffn_b, v_ln_ffn_b), "conv_b": (ffn_conv_b, m_ffn_conv_b, v_ffn_conv_b),
               "conv_w": (ffn_conv_w, m_ffn_conv_w, v_ffn_conv_w)}
    order = list(g_small)
    packs = [_pack_rows([g_small[k] for k in order])[0]]
    for idx in range(3):
        packs.append(_pack_rows([w_small[k][idx] for k in order])[0])
    small_sizes = _pack_rows([g_small[k] for k in order])[1]
    small_out = _adamw(packs[0], packs[1][None], packs[2][None], packs[3][None], 0, None, name="adamw_small")
    shapes = [g_small[k].shape for k in order]
    small_res = {k: [] for k in order}
    for arr in small_out:
        for k, val in zip(order, _unpack_rows(arr[0], small_sizes, shapes)):
            small_res[k].append(val)

    def opt_layers(per_layer_grads, w_all, m_all, v_all, name, rows=None):
        n_layers = w_all.shape[0]
        flat = [a.reshape(n_layers, rows or a.shape[1], -1) for a in (w_all, m_all, v_all)]
        res = None
        for li, g in enumerate(per_layer_grads):
            res = _adamw(g, *flat, li, res, name=name)
        return [o.reshape(w_all.shape) for o in res]

    cg = d // n_groups
    big = {
        "pool_w": opt_layers([reduced[i][0] for i in (0, 2)], pool_w, m_pool_w, v_pool_w, "adamw_pool",
                             rows=n_groups * (cg // 4)),
        "attn_w_qkv": opt_layers([reduced[i][0] for i in (1, 3)], attn_w_qkv, m_attn_w_qkv, v_attn_w_qkv, "adamw_qkv"),
        "attn_w_o": opt_layers([reduced[i][1] for i in (1, 3)], attn_w_o, m_attn_w_o, v_attn_w_o, "adamw_wo"),
        "ffn_w_up": [jnp.transpose(o, (0, 2, 1))
                     for o in opt_layers([reduced[i][-2] for i in range(DEPTH)], *up_t, "adamw_up")],
        "ffn_w_down": opt_layers([reduced[i][-1] for i in range(DEPTH)], ffn_w_down, m_ffn_w_down, v_ffn_w_down,
                                 "adamw_down"),
    }

    def leaf(k, name):
        if name in big:
            return big[name][k]
        key = {"ffn_conv_w": "conv_w", "ffn_conv_b": "conv_b"}.get(name, name)
        return small_res[key][k]

    weight_names = ["pool_w", "pool_scale", "attn_w_qkv", "attn_w_o", "ffn_w_up", "ffn_conv_w", "ffn_conv_b",
                    "ffn_w_down", "ln_mix_g", "ln_mix_b", "ln_ffn_g", "ln_ffn_b"]
    outs = [loss, grad_x]
    for k in range(4):
        outs += [leaf(k, nm) for nm in weight_names]
    return tuple(outs)
```

```python
import collections

import jax
import jax.numpy as jnp
from jax import lax
from jax.experimental import pallas as pl
from jax.experimental.pallas import tpu as pltpu

F32, BF16 = jnp.float32, jnp.bfloat16
MESH = pl.DeviceIdType.MESH

LANES = 128
HEAD_DIM = 128
ATT_BLOCK = 128
ATT_WINDOW = 3 * ATT_BLOCK
ATT_FWD_HEADS = 4
ATT_BWD_HEADS = 2
POOL_WINDOWS = (2, 4, 8, 16)
POOL_HALO = 16
CONV_HALO = 8
LN_EPS = 1e-5
DEPTH = 4
ALPHA = (2.0 * DEPTH) ** 0.25
ATT_SCALE = HEAD_DIM ** -0.5
EXP_ZERO = 115.0
MASKED = 1e30
ADAM_LR, ADAM_B1, ADAM_B2, ADAM_EPS, ADAM_WD, ADAM_STEP = 0.001, 0.9, 0.999, 1e-08, 0.01, 10

VMEM_LIMIT = 56 << 20
ROW_TILE = 512
LN_ROW_TILE = 256
OPT_ROW_TILE = 128
SUM_ROW_TILE = 512


def _cp(n_axes):
    return pltpu.CompilerParams(dimension_semantics=("arbitrary",) * n_axes, vmem_limit_bytes=VMEM_LIMIT)


def _sds(shape, dtype):
    return jax.ShapeDtypeStruct(tuple(shape), dtype)


def _round_up(n, m):
    return (n + m - 1) // m * m


def _tile(n, cap, mult=8):
    if n <= cap:
        return n
    best = None
    for d in range(mult, cap + 1, mult):
        if n % d == 0:
            best = d
    assert best is not None, (n, cap)
    return best


_NT = (((1,), (1,)), ((), ()))
_TN = (((0,), (0,)), ((), ()))

_Side = collections.namedtuple("_Side", "ins outs alias sems start finish")


def _any_specs(n):
    return [pl.BlockSpec(memory_space=pl.ANY)] * n


def _call(body, first, last, side, *, grid, in_specs, out_specs, out_shape, scratch_shapes, name, args):
    n_axes = len(grid)
    if side is None:
        res = pl.pallas_call(body, grid=grid, in_specs=in_specs, out_specs=out_specs, out_shape=out_shape,
                             scratch_shapes=scratch_shapes, name=name, compiler_params=_cp(n_axes))(*args)
        return res, ()
    n_in, n_out, n_scr = len(in_specs), len(out_shape), len(scratch_shapes)
    s_in, s_out = len(side.ins), len(side.outs)

    def carried(*refs):
        ins, refs = refs[:n_in], refs[n_in:]
        side_ins, refs = refs[:s_in], refs[s_in:]
        outs, refs = refs[:n_out], refs[n_out:]
        side_outs, refs = refs[:s_out], refs[s_out:]
        scratch, side_sems = refs[:n_scr], refs[n_scr:]

        @pl.when(first())
        def _():
            side.start(side_ins, side_outs, side_sems)

        body(*ins, *outs, *scratch)

        @pl.when(last())
        def _():
            side.finish(side_ins, side_outs, side_sems)

    res = pl.pallas_call(
        carried, grid=grid, in_specs=list(in_specs) + _any_specs(s_in), out_specs=list(out_specs) + _any_specs(s_out),
        out_shape=list(out_shape) + list(side.outs), scratch_shapes=list(scratch_shapes) + list(side.sems),
        input_output_aliases={n_in + a: n_out + b for a, b in side.alias.items()},
        name=name, compiler_params=_cp(n_axes))(*args, *side.ins)
    return res[:n_out], res[n_out:]


def _mm_cols(a, b, out_dtype, name, transposed_b=False, side=None):
    t, k = a.shape
    g = b.shape[0]
    nb = b.shape[1] if transposed_b else b.shape[2]
    tm = _tile(t, ROW_TILE)
    steps = t // tm

    def body(a_ref, b_ref, o_ref):
        if transposed_b:
            acc = lax.dot_general(a_ref[...], b_ref[...], _NT, preferred_element_type=F32)
        else:
            acc = jnp.dot(a_ref[...], b_ref[...], preferred_element_type=F32)
        o_ref[...] = acc.astype(o_ref.dtype)

    first = lambda: jnp.logical_and(pl.program_id(0) == 0, pl.program_id(1) == 0)
    last = lambda: jnp.logical_and(pl.program_id(0) == g - 1, pl.program_id(1) == steps - 1)
    (out,), side_out = _call(
        body, first, last, side, grid=(g, steps),
        in_specs=[pl.BlockSpec((tm, k), lambda gi, i: (i, 0)),
                  pl.BlockSpec((None,) + b.shape[1:], lambda gi, i: (gi, 0, 0))],
        out_specs=[pl.BlockSpec((None, tm, nb), lambda gi, i: (gi, i, 0))],
        out_shape=[_sds((g, t, nb), out_dtype)], scratch_shapes=[], name=name, args=(a, b))
    return out if side is None else (out, side_out)


def _mm_nt_acc(a3, b3, res, kb, name, side=None, b_is_kn=False):
    ga, t, ka = a3.shape
    gb, n, kbb = (b3.shape[0], b3.shape[2], b3.shape[1]) if b_is_kn else b3.shape
    na, nbk = ka // kb, kbb // kb
    groups = ga * na
    assert groups == gb * nbk
    tm = _tile(t, ROW_TILE)
    steps = t // tm

    def body(a_ref, b_ref, res_ref, o_ref, acc):
        u = pl.program_id(1)

        @pl.when(u == 0)
        def _():
            acc[...] = ALPHA * res_ref[...]

        if b_is_kn:
            acc[...] += jnp.dot(a_ref[...], b_ref[...], preferred_element_type=F32)
        else:
            acc[...] += lax.dot_general(a_ref[...], b_ref[...], _NT, preferred_element_type=F32)

        @pl.when(u == groups - 1)
        def _():
            o_ref[...] = acc[...]

    first = lambda: jnp.logical_and(pl.program_id(0) == 0, pl.program_id(1) == 0)
    last = lambda: jnp.logical_and(pl.program_id(0) == steps - 1, pl.program_id(1) == groups - 1)
    if b_is_kn:
        b_spec = pl.BlockSpec((None, kb, n), lambda i, u: (u // nbk, u % nbk, 0))
    else:
        b_spec = pl.BlockSpec((None, n, kb), lambda i, u: (u // nbk, 0, u % nbk))
    (out,), side_out = _call(
        body, first, last, side, grid=(steps, groups),
        in_specs=[pl.BlockSpec((None, tm, kb), lambda i, u: (u // na, i, u % na)), b_spec,
                  pl.BlockSpec((tm, n), lambda i, u: (i, 0))],
        out_specs=[pl.BlockSpec((tm, n), lambda i, u: (i, 0))],
        out_shape=[_sds((t, n), F32)], scratch_shapes=[pltpu.VMEM((tm, n), F32)], name=name, args=(a3, b3, res))
    return out if side is None else (out, side_out)


def _mm_tn(x3, dy3, out_shape, bm, bn, groups, x_idx, dy_idx, out_idx, name, side=None):
    t = x3.shape[1]
    tm = _tile(t, 2 * ROW_TILE)
    grid = (groups, x_idx[1], t // tm)

    def body(x_ref, dy_ref, o_ref):
        @pl.when(pl.program_id(2) == 0)
        def _():
            o_ref[...] = jnp.zeros_like(o_ref)

        o_ref[...] += lax.dot_general(x_ref[...], dy_ref[...], _TN, preferred_element_type=F32)

    def at(corner):
        hit = pl.program_id(0) == corner[0]
        for axis in (1, 2):
            hit = jnp.logical_and(hit, pl.program_id(axis) == corner[axis])
        return hit

    (out,), side_out = _call(
        body, lambda: at((0, 0, 0)), lambda: at(tuple(g - 1 for g in grid)), side, grid=grid,
        in_specs=[pl.BlockSpec((None, tm, bm), lambda u, mb, i: (x_idx[0](u), i, x_idx[2](u, mb))),
                  pl.BlockSpec((None, tm, bn), lambda u, mb, i: (dy_idx[0](u), i, dy_idx[1](u)))],
        out_specs=[pl.BlockSpec((None, bm, bn), lambda u, mb, i: (out_idx[0](u), out_idx[1](u, mb), out_idx[2](u)))],
        out_shape=[_sds(out_shape, F32)], scratch_shapes=[], name=name, args=(x3, dy3))
    return out if side is None else (out, side_out)


def _layer_norm_rows(r, gamma, beta):
    mu = jnp.mean(r, axis=-1, keepdims=True)
    xc = r - mu
    var = jnp.mean(xc * xc, axis=-1, keepdims=True)
    return xc * lax.rsqrt(var + LN_EPS) * gamma + beta


def _mm_res_ln(a3, w3, res, gamma, beta, name, side=None):
    g, t, kb = a3.shape
    d = w3.shape[2]
    tm = _tile(t, LN_ROW_TILE)
    steps = t // tm

    def body(a_ref, w_hbm, res_ref, g_ref, b_ref, r_ref, o_ref, ob_ref, w_vmem, sem):
        @pl.when(pl.program_id(0) == 0)
        def _():
            cp = pltpu.make_async_copy(w_hbm, w_vmem, sem)
            cp.start()
            cp.wait()

        acc = ALPHA * res_ref[...]
        for gi in range(g):
            acc = acc + jnp.dot(a_ref[gi], w_vmem[gi], preferred_element_type=F32)
        r_ref[...] = acc
        out = _layer_norm_rows(acc, g_ref[...], b_ref[...])
        o_ref[...] = out
        ob_ref[...] = out.astype(BF16)

    row = pl.BlockSpec((tm, d), lambda i: (i, 0))
    vec = pl.BlockSpec((1, d), lambda i: (0, 0))
    outs, side_out = _call(
        body, lambda: pl.program_id(0) == 0, lambda: pl.program_id(0) == steps - 1, side, grid=(steps,),
        in_specs=[pl.BlockSpec((g, tm, kb), lambda i: (0, i, 0)), pl.BlockSpec(memory_space=pl.ANY), row, vec, vec],
        out_specs=[row, row, row],
        out_shape=[_sds((t, d), F32), _sds((t, d), F32), _sds((t, d), BF16)],
        scratch_shapes=[pltpu.VMEM(w3.shape, w3.dtype), pltpu.SemaphoreType.DMA],
        name=name, args=(a3, w3, res, gamma, beta))
    return outs if side is None else (outs, side_out)


def _ln_bwd(dout, r, gamma, name):
    t, d = r.shape
    tm = _tile(t, ROW_TILE)

    def body(do_ref, r_ref, g_ref, dr_ref, drb_ref, dg_ref, db_ref):
        @pl.when(pl.program_id(0) == 0)
        def _():
            dg_ref[...] = jnp.zeros_like(dg_ref)
            db_ref[...] = jnp.zeros_like(db_ref)

        rr = r_ref[...]
        do = do_ref[...]
        mu = jnp.mean(rr, axis=-1, keepdims=True)
        xc = rr - mu
        rstd = lax.rsqrt(jnp.mean(xc * xc, axis=-1, keepdims=True) + LN_EPS)
        xhat = xc * rstd
        dxh = do * g_ref[...]
        m1 = jnp.mean(dxh, axis=-1, keepdims=True)
        m2 = jnp.mean(dxh * xhat, axis=-1, keepdims=True)
        dr = rstd * (dxh - m1 - xhat * m2)
        dr_ref[...] = dr
        drb_ref[...] = dr.astype(BF16)
        dg_ref[...] += jnp.sum(do * xhat, axis=0, keepdims=True)
        db_ref[...] += jnp.sum(do, axis=0, keepdims=True)

    row = pl.BlockSpec((tm, d), lambda i: (i, 0))
    vec = pl.BlockSpec((1, d), lambda i: (0, 0))
    return pl.pallas_call(
        body, grid=(t // tm,), in_specs=[row, row, vec], out_specs=[row, row, vec, vec],
        out_shape=[_sds((t, d), F32), _sds((t, d), BF16), _sds((1, d), F32), _sds((1, d), F32)],
        name=name, compiler_params=_cp(1))(dout, r, gamma)


def _loss_and_grad(y, target, name):
    t, d = y.shape
    tm = _tile(t, ROW_TILE)
    steps = t // tm

    def body(y_ref, t_ref, loss_ref, dy_ref, acc):
        i = pl.program_id(0)

        @pl.when(i == 0)
        def _():
            acc[...] = jnp.zeros_like(acc)

        diff = y_ref[...] - t_ref[...]
        dy_ref[...] = diff * (1.0 / d)
        acc[...] += jnp.sum(diff * diff, axis=0, keepdims=True)

        @pl.when(i == steps - 1)
        def _():
            total = jnp.sum(acc[...], axis=1, keepdims=True) * (0.5 / d)
            loss_ref[...] = jnp.broadcast_to(total, loss_ref.shape)

    row = pl.BlockSpec((tm, d), lambda i: (i, 0))
    return pl.pallas_call(
        body, grid=(steps,), in_specs=[row, row],
        out_specs=[pl.BlockSpec((1, LANES), lambda i: (0, 0)), row],
        out_shape=[_sds((1, LANES), F32), _sds((t, d), F32)],
        scratch_shapes=[pltpu.VMEM((1, d), F32)], name=name, compiler_params=_cp(1))(y, target)


def _window_sums(ext, window, forward):
    n = ext.shape[0]
    s, span = ext, 1
    while span < window:
        s = s + pltpu.roll(s, (n - span) if forward else span, 0)
        span *= 2
    return s


def _pooled_group(main, halo, gi, row0):
    window = POOL_WINDOWS[gi]
    ext = jnp.concatenate([halo, main], axis=0)
    sums = _window_sums(ext, window, forward=False)[POOL_HALO:, :]
    pos = row0 + lax.broadcasted_iota(jnp.int32, (main.shape[0], 1), 0)
    cnt = jnp.minimum(pos + 1, window).astype(F32)
    return sums / cnt - main


def _pool_specs(t, d, tm):
    per = tm // POOL_HALO
    main = pl.BlockSpec((tm, d), lambda i: (i, 0))
    before = pl.BlockSpec((POOL_HALO, d), lambda i: (jnp.maximum(i * per - 1, 0), 0))
    return main, before


def _pool_fwd(x, w, scale, gamma, beta, name):
    t, d = x.shape
    ng, cg = w.shape[0], w.shape[1]
    tm = _tile(t, LN_ROW_TILE)

    def body(x_ref, h_ref, w_ref, s_ref, g_ref, b_ref, r_ref, o_ref, ob_ref):
        i = pl.program_id(0)
        for gi in range(ng):
            cols = pl.ds(gi * cg, cg)
            main = x_ref[:, cols]
            halo = jnp.where(i > 0, h_ref[:, cols], 0.0)
            pooled = _pooled_group(main, halo, gi, i * tm)
            y = jnp.dot(pooled.astype(BF16), w_ref[gi], preferred_element_type=F32)
            r_ref[:, cols] = ALPHA * main + y * s_ref[:, cols]
        out = _layer_norm_rows(r_ref[...], g_ref[...], b_ref[...])
        o_ref[...] = out
        ob_ref[...] = out.astype(BF16)

    main, before = _pool_specs(t, d, tm)
    vec = pl.BlockSpec((1, d), lambda i: (0, 0))
    return pl.pallas_call(
        body, grid=(t // tm,),
        in_specs=[main, before, pl.BlockSpec(w.shape, lambda i: (0, 0, 0)), vec, vec, vec],
        out_specs=[main, main, main],
        out_shape=[_sds((t, d), F32), _sds((t, d), F32), _sds((t, d), BF16)],
        name=name, compiler_params=_cp(1))(x, x, w, scale, gamma, beta)


def _pool_bwd(x, dy, w, scale, name):
    t, d = x.shape
    ng, cg = w.shape[0], w.shape[1]
    tm = _tile(t, LN_ROW_TILE)

    def body(x_ref, h_ref, dy_ref, w_ref, s_ref, dp_ref, dw_ref, ds_ref):
        i = pl.program_id(0)

        @pl.when(i == 0)
        def _():
            dw_ref[...] = jnp.zeros_like(dw_ref)
            ds_ref[...] = jnp.zeros_like(ds_ref)

        for gi in range(ng):
            cols = pl.ds(gi * cg, cg)
            main = x_ref[:, cols]
            halo = jnp.where(i > 0, h_ref[:, cols], 0.0)
            pooled = _pooled_group(main, halo, gi, i * tm).astype(BF16)
            y = jnp.dot(pooled, w_ref[gi], preferred_element_type=F32)
            dyg = dy_ref[:, cols]
            ds_ref[:, cols] += jnp.sum(dyg * y, axis=0, keepdims=True)
            dyw = (dyg * s_ref[:, cols]).astype(BF16)
            dw_ref[gi] += lax.dot_general(pooled, dyw, _TN, preferred_element_type=F32)
            dp_ref[:, cols] = lax.dot_general(dyw, w_ref[gi], _NT, preferred_element_type=F32)

    main, before = _pool_specs(t, d, tm)
    vec = pl.BlockSpec((1, d), lambda i: (0, 0))
    wspec = pl.BlockSpec(w.shape, lambda i: (0, 0, 0))
    return pl.pallas_call(
        body, grid=(t // tm,), in_specs=[main, before, main, wspec, vec],
        out_specs=[main, wspec, vec],
        out_shape=[_sds((t, d), F32), _sds(w.shape, F32), _sds((1, d), F32)],
        name=name, compiler_params=_cp(1))(x, x, dy, w, scale)


def _pool_adjoint(dp, dres, n_groups, name):
    t, d = dp.shape
    cg = d // n_groups
    tm = _tile(t, ROW_TILE)
    steps = t // tm
    per = tm // POOL_HALO

    def body(dp_ref, after_ref, dres_ref, dx_ref):
        i = pl.program_id(0)
        rows = lax.broadcasted_iota(jnp.int32, (tm, 1), 0)
        rows_after = lax.broadcasted_iota(jnp.int32, (POOL_HALO, 1), 0)
        for gi in range(n_groups):
            window = POOL_WINDOWS[gi]
            cols = pl.ds(gi * cg, cg)
            main = dp_ref[:, cols]
            cnt = jnp.minimum(i * tm + rows + 1, window).astype(F32)
            cnt_after = jnp.minimum((i + 1) * tm + rows_after + 1, window).astype(F32)
            after = jnp.where(i < steps - 1, after_ref[:, cols] / cnt_after, 0.0)
            ext = jnp.concatenate([main / cnt, after], axis=0)
            sums = _window_sums(ext, window, forward=True)[:tm, :]
            dx_ref[:, cols] = ALPHA * dres_ref[:, cols] + sums - main

    main = pl.BlockSpec((tm, d), lambda i: (i, 0))
    after = pl.BlockSpec((POOL_HALO, d), lambda i: (jnp.minimum((i + 1) * per, t // POOL_HALO - 1), 0))
    return pl.pallas_call(
        body, grid=(steps,), in_specs=[main, after, main], out_specs=main,
        out_shape=_sds((t, d), F32), name=name, compiler_params=_cp(1))(dp, dp, dres)


def _split_dot(x, tri):
    hi = x.astype(BF16)
    lo = (x - hi.astype(F32)).astype(BF16)
    return jnp.dot(hi, tri, preferred_element_type=F32) + jnp.dot(lo, tri, preferred_element_type=F32)


def _att_windows(qs, k_ws, limit, carry_rests, suffix):
    heads = range(len(qs))
    zs = [lax.dot_general(qs[hh], k_ws[hh], _NT, preferred_element_type=F32) * ATT_SCALE for hh in heads]
    visible = lax.broadcasted_iota(jnp.int32, zs[0].shape, 1) < limit
    zs = [jnp.where(visible, z, -MASKED) for z in zs]
    es = [jnp.exp(-jnp.abs(z)) for z in zs]
    log_nots = [-(jnp.maximum(z, 0.0) + jnp.log(1.0 + e)) for z, e in zip(zs, es)]
    rests = [_split_dot(ln, suffix[...]) + carry for ln, carry in zip(log_nots, carry_rests)]
    weights = [jnp.exp(z + r) for z, r in zip(zs, rests)]
    return zs, es, log_nots, weights


def _tri(w, strict):
    r = lax.broadcasted_iota(jnp.int32, (w, w), 0)
    c = lax.broadcasted_iota(jnp.int32, (w, w), 1)
    return ((r > c) if strict else (r >= c)).astype(BF16)


def _heads_per_step(qkv3, n_heads, most):
    cpb = qkv3.shape[2] // HEAD_DIM
    hp = most
    while cpb % hp or n_heads % hp:
        hp //= 2
    return hp


def _att_specs(qkv3, n_heads, hp):
    t = qkv3.shape[1]
    cpb = qkv3.shape[2] // HEAD_DIM
    wd = hp * HEAD_DIM

    def slab(off):
        return pl.BlockSpec((None, t, wd), lambda g, i: ((off + g * hp) // cpb, 0, ((off + g * hp) % cpb) // hp))

    q = pl.BlockSpec((None, ATT_BLOCK, wd), lambda g, i: ((g * hp) // cpb, i, ((g * hp) % cpb) // hp))
    return q, slab(n_heads), slab(2 * n_heads)


def _head_cols(hh):
    return pl.ds(hh * HEAD_DIM, HEAD_DIM)


def _key_bounds(k_ref, kmax, hp):
    for hh in range(hp):
        kf = k_ref[:, _head_cols(hh)].astype(F32)
        kmax[hh] = jnp.sqrt(jnp.max(jnp.sum(kf * kf, axis=1, keepdims=True)))


def _score_bound(q, key_norm):
    qf = q.astype(F32)
    return ATT_SCALE * 1.001 * key_norm * jnp.sqrt(jnp.sum(qf * qf, axis=1, keepdims=True)) + 1e-3


def _any_alive(rests, bounds):
    alive = jnp.max(rests[0] + bounds[0]) > -EXP_ZERO
    for r, zb in zip(rests[1:], bounds[1:]):
        alive = jnp.logical_or(alive, jnp.max(r + zb) > -EXP_ZERO)
    return alive


def _window_rows(hi, w):
    start = jnp.maximum(hi - w, 0)
    return start, pl.ds(pl.multiple_of(start, ATT_BLOCK), w)


def _attn_fwd(qkv3, n_heads, name):
    t = qkv3.shape[1]
    b = ATT_BLOCK
    w = min(ATT_WINDOW, t)
    hp = _heads_per_step(qkv3, n_heads, ATT_FWD_HEADS)
    heads = range(hp)

    def body(q_ref, k_ref, v_ref, o_ref, kmax, suffix):
        i = pl.program_id(1)

        @pl.when(i == 0)
        def _():
            _key_bounds(k_ref, kmax, hp)
            suffix[...] = _tri(w, strict=False)

        qs = [q_ref[:, _head_cols(hh)] for hh in heads]
        bounds = [_score_bound(qs[hh], kmax[hh]) for hh in heads]
        qpos = i * b + lax.broadcasted_iota(jnp.int32, (b, 1), 0)

        def cond(c):
            return jnp.logical_and(c[0] > 0, _any_alive(c[1], bounds))

        def step(c):
            hi, rests, accs = c
            start, rows = _window_rows(hi, w)
            limit = jnp.minimum(qpos, hi) - start
            k_ws = [k_ref[rows, _head_cols(hh)] for hh in heads]
            _, _, log_nots, weights = _att_windows(qs, k_ws, limit, rests, suffix)
            new_accs = tuple(accs[hh] + jnp.dot(weights[hh].astype(BF16), v_ref[rows, _head_cols(hh)],
                                                preferred_element_type=F32) for hh in heads)
            new_rests = tuple(rests[hh] + jnp.sum(log_nots[hh], axis=1, keepdims=True) for hh in heads)
            return start, new_rests, new_accs

        init = ((i + 1) * b, tuple(jnp.zeros((b, 1), F32) for _ in heads),
                tuple(jnp.zeros((b, HEAD_DIM), F32) for _ in heads))
        _, _, accs = lax.while_loop(cond, step, init)
        for hh in heads:
            o_ref[:, _head_cols(hh)] = accs[hh].astype(o_ref.dtype)

    qs_, ks_, vs_ = _att_specs(qkv3, n_heads, hp)
    return pl.pallas_call(
        body, grid=(n_heads // hp, t // b), in_specs=[qs_, ks_, vs_],
        out_specs=pl.BlockSpec((b, hp * HEAD_DIM), lambda g, i: (i, g)),
        out_shape=_sds((t, n_heads * HEAD_DIM), BF16),
        scratch_shapes=[pltpu.SMEM((hp,), F32), pltpu.VMEM((w, w), BF16)],
        name=name, compiler_params=_cp(2))(qkv3, qkv3, qkv3)


def _attn_bwd(qkv3, do, n_heads, name):
    t = qkv3.shape[1]
    b = ATT_BLOCK
    w = min(ATT_WINDOW, t)
    nq = t // b
    hp = _heads_per_step(qkv3, n_heads, ATT_BWD_HEADS)
    heads = range(hp)
    wd = hp * HEAD_DIM

    def body(q_ref, k_ref, v_ref, do_ref, dq_ref, dk_ref, dv_ref, kmax, dk_acc, dv_acc, suffix, strict_suffix):
        i = pl.program_id(1)

        @pl.when(i == 0)
        def _():
            _key_bounds(k_ref, kmax, hp)
            dk_acc[...] = jnp.zeros_like(dk_acc)
            dv_acc[...] = jnp.zeros_like(dv_acc)
            suffix[...] = _tri(w, strict=False)
            strict_suffix[...] = _tri(w, strict=True)

        qs = [q_ref[:, _head_cols(hh)] for hh in heads]
        douts = [do_ref[:, _head_cols(hh)] for hh in heads]
        bounds = [_score_bound(qs[hh], kmax[hh]) for hh in heads]
        zero_cols = tuple(jnp.zeros((b, 1), F32) for _ in heads)
        hi0 = (i + 1) * b
        qpos = i * b + lax.broadcasted_iota(jnp.int32, (b, 1), 0)

        def cond(c):
            return jnp.logical_and(c[0] > 0, _any_alive(c[1], bounds))

        def tiles_of(hi, rests):
            start, rows = _window_rows(hi, w)
            k_ws = [k_ref[rows, _head_cols(hh)] for hh in heads]
            dps = [lax.dot_general(douts[hh], v_ref[rows, _head_cols(hh)], _NT, preferred_element_type=F32)
                   for hh in heads]
            zs, es, log_nots, weights = _att_windows(qs, k_ws, jnp.minimum(qpos, hi) - start, rests, suffix)
            dlas = [a * dp for a, dp in zip(weights, dps)]
            return start, rows, (k_ws, zs, es, log_nots, weights, dlas)

        def row_sums(carries, tiles):
            return tuple(c + jnp.sum(x, axis=1, keepdims=True) for c, x in zip(carries, tiles))

        def first_pass(rows, tiles, rests, totals):
            _, _, _, log_nots, weights, dlas = tiles
            for hh in heads:
                dv_acc[rows, _head_cols(hh)] += lax.dot_general(weights[hh].astype(BF16), douts[hh], _TN,
                                                                preferred_element_type=F32)
            return row_sums(rests, log_nots), row_sums(totals, dlas)

        def second_pass(rows, tiles, totals, laters, dqs):
            k_ws, zs, es, _, _, dlas = tiles
            insides = [_split_dot(dla, strict_suffix[...]) for dla in dlas]
            dzs = []
            for hh in heads:
                dlog_not = totals[hh] - laters[hh] - insides[hh]
                inv = 1.0 / (1.0 + es[hh])
                sig = jnp.where(zs[hh] >= 0, inv, es[hh] * inv)
                dzs.append(((dlas[hh] - sig * dlog_not) * ATT_SCALE).astype(BF16))
            new_dqs = tuple(dqs[hh] + jnp.dot(dzs[hh], k_ws[hh], preferred_element_type=F32) for hh in heads)
            for hh in heads:
                dk_acc[rows, _head_cols(hh)] += lax.dot_general(dzs[hh], qs[hh], _TN, preferred_element_type=F32)
            return row_sums(laters, dlas), new_dqs

        start0, rows0, tiles0 = tiles_of(hi0, zero_cols)
        rests1, totals1 = first_pass(rows0, tiles0, zero_cols, zero_cols)

        def sweep1(c):
            hi, rests, totals = c
            start, rows, tiles = tiles_of(hi, rests)
            return (start,) + first_pass(rows, tiles, rests, totals)

        _, _, totals = lax.while_loop(cond, sweep1, (start0, rests1, totals1))
        laters1, dqs1 = second_pass(rows0, tiles0, totals, zero_cols,
                                    tuple(jnp.zeros((b, HEAD_DIM), F32) for _ in heads))

        def sweep2(c):
            hi, rests, laters, dqs = c
            start, rows, tiles = tiles_of(hi, rests)
            return (start, row_sums(rests, tiles[3])) + second_pass(rows, tiles, totals, laters, dqs)

        _, _, _, dqs = lax.while_loop(cond, sweep2, (start0, rests1, laters1, dqs1))
        for hh in heads:
            dq_ref[:, _head_cols(hh)] = dqs[hh].astype(dq_ref.dtype)

        @pl.when(i == nq - 1)
        def _():
            dk_ref[...] = dk_acc[...].astype(dk_ref.dtype)
            dv_ref[...] = dv_acc[...].astype(dv_ref.dtype)

    qs_, ks_, vs_ = _att_specs(qkv3, n_heads, hp)
    blk = pl.BlockSpec((b, wd), lambda g, i: (i, g))
    slab = pl.BlockSpec((t, wd), lambda g, i: (0, g))
    d = n_heads * HEAD_DIM
    return pl.pallas_call(
        body, grid=(n_heads // hp, nq),
        in_specs=[qs_, ks_, vs_, pl.BlockSpec((None, b, wd), lambda g, i: (0, i, g))],
        out_specs=[blk, slab, slab],
        out_shape=[_sds((t, d), BF16)] * 3,
        scratch_shapes=[pltpu.SMEM((hp,), F32), pltpu.VMEM((t, wd), F32), pltpu.VMEM((t, wd), F32),
                        pltpu.VMEM((w, w), BF16), pltpu.VMEM((w, w), BF16)],
        name=name, compiler_params=_cp(2))(qkv3, qkv3, qkv3, do)


def _conv_rows(main, halo, w_ref, b_ref):
    ext = jnp.concatenate([halo, main], axis=0)
    h1 = pltpu.roll(ext, 1, 0)[CONV_HALO:, :]
    h2 = pltpu.roll(ext, 2, 0)[CONV_HALO:, :]
    hc = b_ref[...] + w_ref[0:1, :] * h2
    hc = hc + w_ref[1:2, :] * h1
    hc = hc + w_ref[2:3, :] * main
    return hc, h1, h2


def _ffn_specs(t, fp, tm, half):
    per = tm // CONV_HALO
    main = lambda off: pl.BlockSpec((None, tm, fp), lambda g, i: (g + off, i, 0))
    before = lambda off: pl.BlockSpec((None, CONV_HALO, fp), lambda g, i: (g + off, jnp.maximum(i * per - 1, 0), 0))
    cw = lambda off: pl.BlockSpec((None, 3, fp), lambda g, i: (g + off, 0, 0))
    cb = lambda off: pl.BlockSpec((None, 1, fp), lambda g, i: (g + off, 0, 0))
    return [main(0), before(0), main(half), before(half), cw(0), cw(half), cb(0), cb(half)]


def _ffn_act(h, cw, cb, name, side=None):
    n, t, fp = h.shape
    half = n // 2
    tm = _tile(t, LN_ROW_TILE)
    steps = t // tm

    def body(hg_ref, hgb_ref, hv_ref, hvb_ref, wg_ref, wv_ref, bg_ref, bv_ref, a_ref):
        first = pl.program_id(1) == 0
        gate, _, _ = _conv_rows(hg_ref[...], jnp.where(first, 0.0, hgb_ref[...]), wg_ref, bg_ref)
        val, _, _ = _conv_rows(hv_ref[...], jnp.where(first, 0.0, hvb_ref[...]), wv_ref, bv_ref)
        a_ref[...] = (gate * jax.nn.sigmoid(gate) * val).astype(a_ref.dtype)

    (a,), side_out = _call(
        body, lambda: jnp.logical_and(pl.program_id(0) == 0, pl.program_id(1) == 0),
        lambda: jnp.logical_and(pl.program_id(0) == half - 1, pl.program_id(1) == steps - 1), side,
        grid=(half, steps), in_specs=_ffn_specs(t, fp, tm, half),
        out_specs=[pl.BlockSpec((None, tm, fp), lambda g, i: (g, i, 0))],
        out_shape=[_sds((half, t, fp), BF16)], scratch_shapes=[], name=name, args=(h, h, h, h, cw, cw, cb, cb))
    return a if side is None else (a, side_out)


def _act_grads(dact, gate, val):
    sig = jax.nn.sigmoid(gate)
    return dact * val * (sig * (1.0 + gate * (1.0 - sig))), dact * (gate * sig)


def _ffn_act_bwd(h, da, cw, cb, name):
    n, t, fp = h.shape
    half = n // 2
    tm = _tile(t, LN_ROW_TILE)

    def body(hg_ref, hgb_ref, hv_ref, hvb_ref, wg_ref, wv_ref, bg_ref, bv_ref, da_ref, dhc_ref, dw_ref, db_ref):
        first = pl.program_id(1) == 0

        @pl.when(first)
        def _():
            dw_ref[...] = jnp.zeros_like(dw_ref)
            db_ref[...] = jnp.zeros_like(db_ref)

        hg, hv = hg_ref[...], hv_ref[...]
        gate, hg1, hg2 = _conv_rows(hg, jnp.where(first, 0.0, hgb_ref[...]), wg_ref, bg_ref)
        val, hv1, hv2 = _conv_rows(hv, jnp.where(first, 0.0, hvb_ref[...]), wv_ref, bv_ref)
        dgate, dval = _act_grads(da_ref[...], gate, val)
        dhc_ref[0] = dgate
        dhc_ref[1] = dval
        for s, (dd, shifted) in enumerate(((dgate, (hg2, hg1, hg)), (dval, (hv2, hv1, hv)))):
            db_ref[s] += jnp.sum(dd, axis=0, keepdims=True)
            for kk in range(3):
                dw_ref[s, kk:kk + 1, :] += jnp.sum(dd * shifted[kk], axis=0, keepdims=True)

    specs = _ffn_specs(t, fp, tm, half) + [pl.BlockSpec((None, tm, fp), lambda g, i: (g, i, 0))]
    return pl.pallas_call(
        body, grid=(half, t // tm), in_specs=specs,
        out_specs=[pl.BlockSpec((2, None, tm, fp), lambda g, i: (0, g, i, 0)),
                   pl.BlockSpec((2, None, 3, fp), lambda g, i: (0, g, 0, 0)),
                   pl.BlockSpec((2, None, 1, fp), lambda g, i: (0, g, 0, 0))],
        out_shape=[_sds((2, half, t, fp), F32), _sds((2, half, 3, fp), F32), _sds((2, half, 1, fp), F32)],
        name=name, compiler_params=_cp(2))(h, h, h, h, cw, cw, cb, cb, da)


def _conv_adjoint(dhc, cw, name):
    n, t, fp = dhc.shape
    tm = _tile(t, ROW_TILE)
    steps = t // tm
    per = tm // CONV_HALO

    def body(d_ref, after_ref, w_ref, o_ref):
        main = d_ref[...]
        after = jnp.where(pl.program_id(1) < steps - 1, after_ref[...], 0.0)
        ext = jnp.concatenate([main, after], axis=0)
        rows = ext.shape[0]
        d1 = pltpu.roll(ext, rows - 1, 0)[:tm, :]
        d2 = pltpu.roll(ext, rows - 2, 0)[:tm, :]
        o_ref[...] = (w_ref[2:3, :] * main + w_ref[1:2, :] * d1 + w_ref[0:1, :] * d2).astype(o_ref.dtype)

    main = pl.BlockSpec((None, tm, fp), lambda g, i: (g, i, 0))
    after = pl.BlockSpec((None, CONV_HALO, fp), lambda g, i: (g, jnp.minimum((i + 1) * per, t // CONV_HALO - 1), 0))
    return pl.pallas_call(
        body, grid=(n, steps), in_specs=[main, after, pl.BlockSpec((None, 3, fp), lambda g, i: (g, 0, 0))],
        out_specs=main, out_shape=_sds((n, t, fp), BF16), name=name, compiler_params=_cp(2))(dhc, dhc, cw)


def _place():
    x, y, c = lax.axis_index("x"), lax.axis_index("y"), lax.axis_index("c")
    chips = [(1 - x, y), (x, 1 - y), (1 - x, 1 - y)]
    return x, y, c, chips


def _run_sides(sides, name):
    n_in = [len(s.ins) for s in sides]
    n_out = [len(s.outs) for s in sides]
    n_sem = [len(s.sems) for s in sides]

    def body(*refs):
        ins, outs, sems = refs[:sum(n_in)], refs[sum(n_in):sum(n_in) + sum(n_out)], refs[sum(n_in) + sum(n_out):]
        oi = oo = os_ = 0
        for k, s in enumerate(sides):
            mine = (ins[oi:oi + n_in[k]], outs[oo:oo + n_out[k]], sems[os_:os_ + n_sem[k]])
            s.start(*mine)
            s.finish(*mine)
            oi, oo, os_ = oi + n_in[k], oo + n_out[k], os_ + n_sem[k]

    aliases, oi, oo = {}, 0, 0
    for k, s in enumerate(sides):
        aliases.update({oi + a: oo + b for a, b in s.alias.items()})
        oi, oo = oi + n_in[k], oo + n_out[k]
    return pl.pallas_call(
        body, in_specs=_any_specs(sum(n_in)), out_specs=_any_specs(sum(n_out)),
        out_shape=[o for s in sides for o in s.outs], input_output_aliases=aliases,
        scratch_shapes=[q for s in sides for q in s.sems], name=name)(*[a for s in sides for a in s.ins])


def _place_shard(kind, w, chip, name, rows=None, base=None):
    if kind == "pool":
        g, r, cdim = w.shape

        def body(chip_ref, w_ref, o_ref):
            del chip_ref
            o_ref[...] = w_ref[...].astype(BF16)

        return pl.pallas_call(
            body,
            grid_spec=pltpu.PrefetchScalarGridSpec(
                num_scalar_prefetch=1, grid=(1,),
                in_specs=[pl.BlockSpec((g, r, cdim), lambda i, chip_ref: (0, 0, 0))],
                out_specs=pl.BlockSpec((g, r, cdim), lambda i, chip_ref: (0, chip_ref[0], 0))),
            out_shape=_sds((g, 4 * r, cdim), BF16), name=name, compiler_params=_cp(1))(chip, w)

    r, cs = w.shape
    if kind == "lead":
        rows = rows or r
        tr = _tile(r, ROW_TILE, 16) if rows == r else rows - r
        assert r % tr == 0 and tr % 16 == 0
        n_src = r // tr

        def body(chip_ref, w_ref, o_ref):
            del chip_ref
            o_ref[...] = jnp.where(pl.program_id(0) < n_src, w_ref[...], 0.0).astype(BF16)

        return pl.pallas_call(
            body,
            grid_spec=pltpu.PrefetchScalarGridSpec(
                num_scalar_prefetch=1, grid=(rows // tr,),
                in_specs=[pl.BlockSpec((tr, cs), lambda i, chip_ref: (jnp.minimum(i, n_src - 1), 0))],
                out_specs=pl.BlockSpec((None, tr, cs), lambda i, chip_ref: (chip_ref[0], i, 0))),
            out_shape=_sds((4, rows, cs), BF16), name=name, compiler_params=_cp(1))(chip, w)

    assert kind == "down"
    tr = r // 2 if (r // 2) % 16 == 0 else r
    per = r // tr

    def body(chip_ref, w_ref, base_ref, o_ref):
        del chip_ref, base_ref
        o_ref[...] = w_ref[...].astype(BF16)

    return pl.pallas_call(
        body,
        grid_spec=pltpu.PrefetchScalarGridSpec(
            num_scalar_prefetch=1, grid=(per,),
            in_specs=[pl.BlockSpec((tr, cs), lambda i, chip_ref: (i, 0)), pl.BlockSpec(memory_space=pl.ANY)],
            out_specs=pl.BlockSpec((None, tr, cs), lambda i, chip_ref: (chip_ref[0] // 2, (chip_ref[0] % 2) * per + i, 0))),
        out_shape=_sds(base.shape, BF16), input_output_aliases={2: 0},
        name=name, compiler_params=_cp(1))(chip, w, base)


def _gather_sides(items, bufs=None):
    n = len(items)
    kinds = [it[0] for it in items]
    shard_rows = [it[2] for it in items]
    bufs = [it[1] for it in items] if bufs is None else list(bufs)

    def half_of(outs, m, chip, half):
        k = 2 * chip[0] + chip[1]
        o, r = outs[m], shard_rows[m]
        if kinds[m] == "pool":
            gh = o.shape[0] // 2
            return o.at[pl.ds(half * gh, gh), pl.ds(k * r, r)]
        r2 = r // 2
        if kinds[m] == "down":
            return o.at[k // 2, pl.ds((k % 2) * r + half * r2, r2)]
        return o.at[k, pl.ds(half * r2, r2)]

    def remote(outs, sems, m, j, chip, half, to):
        ref = half_of(outs, m, chip, half)
        return pltpu.make_async_remote_copy(src_ref=ref, dst_ref=ref, send_sem=sems[0].at[m, j],
                                            recv_sem=sems[1].at[m, j], device_id=to, device_id_type=MESH)

    def ici_copies(outs, sems, sending):
        x, y, c, chips = _place()
        if sending:
            return [remote(outs, sems, m, j, (x, y), c, (*chip, c)) for m in range(n) for j, chip in enumerate(chips)]
        return [remote(outs, sems, m, j, chip, c, (x, y, c)) for m in range(n) for j, chip in enumerate(chips)]

    def d2d_copies(outs, sems, sending):
        x, y, c, chips = _place()
        if sending:
            return [remote(outs, sems, m, j, chip, c, (x, y, 1 - c)) for m in range(n) for j, chip in enumerate(chips)]
        return [remote(outs, sems, m, j, chip, 1 - c, (x, y, c)) for m in range(n) for j, chip in enumerate(chips)]

    def phase(copies):
        def start(ins, outs, sems):
            for cp in copies(outs, sems, True):
                cp.start()

        def finish(ins, outs, sems):
            for cp in copies(outs, sems, False):
                cp.wait_recv()
            for cp in copies(outs, sems, True):
                cp.wait_send()

        return start, finish

    ici, d2d = phase(ici_copies), phase(d2d_copies)

    def both_finish(ins, outs, sems):
        ici[1](ins, outs, sems[:2])
        d2d[0](ins, outs, sems[2:])
        d2d[1](ins, outs, sems[2:])

    pair = [pltpu.SemaphoreType.DMA((n, 3)), pltpu.SemaphoreType.DMA((n, 3))]
    shapes = [_sds(b.shape, b.dtype) for b in bufs]
    alias = {m: m for m in range(n)}

    def side(which):
        if which == "both":
            return _Side(bufs, shapes, alias, pair + pair, lambda i, o, s: ici[0](i, o, s[:2]), both_finish)
        start, finish = ici if which == "ici" else d2d
        return _Side(bufs, shapes, alias, pair, start, finish)

    return side


def _sibling_side(grads):
    n = len(grads)

    def copies(ins, outs, sems):
        x, y, c, _ = _place()
        res = []
        for m in range(n):
            r2 = ins[m].shape[1] // 2
            res.append(pltpu.make_async_remote_copy(
                src_ref=ins[m].at[:, pl.ds((1 - c) * r2, r2)], dst_ref=outs[m],
                send_sem=sems[0].at[m], recv_sem=sems[1].at[m], device_id=(x, y, 1 - c), device_id_type=MESH))
        return res

    def start(ins, outs, sems):
        for cp in copies(ins, outs, sems):
            cp.start()

    def finish(ins, outs, sems):
        for cp in copies(ins, outs, sems):
            cp.wait_recv()
        for cp in copies(ins, outs, sems):
            cp.wait_send()

    return _Side(list(grads), [_sds((4, g.shape[1] // 2, g.shape[2]), g.dtype) for g in grads], {},
                 [pltpu.SemaphoreType.DMA((n,)), pltpu.SemaphoreType.DMA((n,))], start, finish)


def _owner_chips_side(parts):
    n = len(parts)

    def copies(ins, outs, sems):
        _, _, c, chips = _place()
        return [pltpu.make_async_remote_copy(
            src_ref=ins[m].at[2 * chip[0] + chip[1]], dst_ref=outs[m].at[j], send_sem=sems[0].at[m, j],
            recv_sem=sems[1].at[m, j], device_id=(*chip, c), device_id_type=MESH)
            for m in range(n) for j, chip in enumerate(chips)]

    def start(ins, outs, sems):
        for cp in copies(ins, outs, sems):
            cp.start()

    def finish(ins, outs, sems):
        for cp in copies(ins, outs, sems):
            cp.wait_recv()
        for cp in copies(ins, outs, sems):
            cp.wait_send()

    return _Side(list(parts), [_sds((3,) + p.shape[1:], p.dtype) for p in parts], {},
                 [pltpu.SemaphoreType.DMA((n, 3)), pltpu.SemaphoreType.DMA((n, 3))], start, finish)


def _exchange_finished_halves(shards, name):
    n = len(shards)

    def body(*refs):
        out = refs[n:2 * n]
        send_sems, recv_sems = refs[2 * n:]
        x, y, c, _ = _place()
        copies = []
        for m in range(n):
            r2 = out[m].shape[0] // 2
            mine = out[m].at[pl.ds(c * r2, r2)]
            copies.append(pltpu.make_async_remote_copy(
                src_ref=mine, dst_ref=mine, send_sem=send_sems.at[m], recv_sem=recv_sems.at[m],
                device_id=(x, y, 1 - c), device_id_type=MESH))
        for cp in copies:
            cp.start()
        for m in range(n):
            r2 = out[m].shape[0] // 2
            theirs = out[m].at[pl.ds((1 - c) * r2, r2)]
            pltpu.make_async_remote_copy(
                src_ref=theirs, dst_ref=theirs, send_sem=send_sems.at[m], recv_sem=recv_sems.at[m],
                device_id=(x, y, 1 - c), device_id_type=MESH).wait_recv()
        for cp in copies:
            cp.wait_send()

    return pl.pallas_call(
        body, in_specs=_any_specs(n), out_specs=_any_specs(n), out_shape=[_sds(s.shape, s.dtype) for s in shards],
        input_output_aliases={m: m for m in range(n)},
        scratch_shapes=[pltpu.SemaphoreType.DMA((n,)), pltpu.SemaphoreType.DMA((n,))], name=name)(*shards)


def _all_reduce_small(v, name):
    rows = v.shape[0]

    def body(v_ref, out_ref, buf, send_sems, recv_sems, local_sem):
        x, y, c, chips = _place()
        me, sibling = (x, y, c), (x, y, 1 - c)

        def slot(px, py, pc):
            return buf.at[4 * px + 2 * py + pc]

        def copy(k, block, to, src=None):
            return pltpu.make_async_remote_copy(
                src_ref=slot(*block) if src is None else src, dst_ref=slot(*block),
                send_sem=send_sems.at[k], recv_sem=recv_sems.at[k], device_id=to, device_id_type=MESH)

        mine = pltpu.make_async_copy(v_ref, slot(*me), local_sem)
        mine.start()
        first = [copy(0, me, sibling, src=v_ref)]
        first += [copy(1 + j, me, (*chip, c), src=v_ref) for j, chip in enumerate(chips)]
        for cp in first:
            cp.start()
        passed = [copy(4 + j, (*chip, c), sibling) for j, chip in enumerate(chips)]
        for j, chip in enumerate(chips):
            copy(1 + j, (*chip, c), me).wait_recv()
            passed[j].start()
        copy(0, sibling, me).wait_recv()
        for j, chip in enumerate(chips):
            copy(4 + j, (*chip, 1 - c), me).wait_recv()
        for cp in first + passed:
            cp.wait_send()
        mine.wait()
        total = buf[0]
        for dev in range(1, 8):
            total = total + buf[dev]
        out_ref[...] = total

    vm = pl.BlockSpec(memory_space=pltpu.VMEM)
    return pl.pallas_call(
        body, in_specs=[vm], out_specs=vm, out_shape=_sds(v.shape, F32),
        scratch_shapes=[pltpu.VMEM((8, rows, LANES), F32), pltpu.SemaphoreType.DMA((7,)),
                        pltpu.SemaphoreType.DMA((7,)), pltpu.SemaphoreType.DMA],
        name=name, compiler_params=pltpu.CompilerParams(vmem_limit_bytes=VMEM_LIMIT))(v)


def _chip_partial(grad, from_sibling, core, name):
    _, r, cdim = grad.shape
    r2 = r // 2
    tr = _tile(r2, SUM_ROW_TILE)
    per = r2 // tr

    def body(core_ref, g_ref, s_ref, o_ref, ob_ref):
        del core_ref
        total = g_ref[...] + s_ref[...]
        o_ref[...] = total
        ob_ref[...] = total.astype(BF16)

    blk = pl.BlockSpec((None, tr, cdim), lambda k, i, core_ref: (k, i, 0))
    mine = pl.BlockSpec((None, tr, cdim), lambda k, i, core_ref: (k, core_ref[0] * per + i, 0))
    return pl.pallas_call(
        body,
        grid_spec=pltpu.PrefetchScalarGridSpec(num_scalar_prefetch=1, grid=(4, per), in_specs=[mine, blk],
                                               out_specs=[blk, blk]),
        out_shape=[_sds((4, r2, cdim), F32), _sds((4, r2, cdim), BF16)],
        name=name, compiler_params=_cp(2))(core, grad, from_sibling)


def _owner_sum(partial, from_chips, place, name):
    _, r2, cdim = partial.shape
    tr = _tile(r2, SUM_ROW_TILE)
    per = r2 // tr

    def body(place_ref, p_ref, f_ref, o_ref):
        del place_ref
        total = p_ref[...]
        for j in range(3):
            total = total + f_ref[j].astype(F32)
        o_ref[...] = total

    return pl.pallas_call(
        body,
        grid_spec=pltpu.PrefetchScalarGridSpec(
            num_scalar_prefetch=1, grid=(per,),
            in_specs=[pl.BlockSpec((None, tr, cdim), lambda i, place_ref: (place_ref[0], i, 0)),
                      pl.BlockSpec((3, tr, cdim), lambda i, place_ref: (0, i, 0))],
            out_specs=pl.BlockSpec((tr, cdim), lambda i, place_ref: (place_ref[1] * per + i, 0))),
        out_shape=_sds((2 * r2, cdim), F32), name=name, compiler_params=_cp(1))(place, partial, from_chips)


def _adamw(g, w, m, v, layer, prev, name):
    _, r, cdim = w.shape
    tr = _tile(r, OPT_ROW_TILE)
    c1 = 1.0 / (1.0 - ADAM_B1 ** ADAM_STEP)
    c2 = 1.0 / (1.0 - ADAM_B2 ** ADAM_STEP)
    n_prev = 0 if prev is None else 4

    def body(g_ref, w_ref, m_ref, v_ref, *rest):
        go_ref, d_ref, mo_ref, vo_ref = rest[n_prev:]
        grad = g_ref[:, pl.ds(0, cdim)]
        m_new = ADAM_B1 * m_ref[...] + (1.0 - ADAM_B1) * grad
        v_new = ADAM_B2 * v_ref[...] + (1.0 - ADAM_B2) * (grad * grad)
        go_ref[...] = grad
        mo_ref[...] = m_new
        vo_ref[...] = v_new
        d_ref[...] = -ADAM_LR * ((m_new * c1) / (jnp.sqrt(v_new * c2) + ADAM_EPS) + ADAM_WD * w_ref[...])

    blk = pl.BlockSpec((None, tr, cdim), lambda i: (layer, i, 0))
    gblk = pl.BlockSpec((tr, g.shape[1]), lambda i: (i, 0))
    return pl.pallas_call(
        body, grid=(r // tr,), in_specs=[gblk, blk, blk, blk] + _any_specs(n_prev), out_specs=[blk] * 4,
        out_shape=[_sds(w.shape, F32)] * 4, input_output_aliases={4 + k: k for k in range(n_prev)},
        name=name, compiler_params=_cp(1))(g, w, m, v, *(prev or ()))


def _pack_rows(vectors):
    flat = [v.reshape(-1) for v in vectors]
    sizes = [f.shape[0] for f in flat]
    total = sum(sizes)
    padded = _round_up(total, 8 * LANES)
    buf = jnp.concatenate(flat + [jnp.zeros((padded - total,), F32)])
    return buf.reshape(padded // LANES, LANES), sizes


def _unpack_rows(buf, sizes, shapes):
    flat = buf.reshape(-1)
    out, off = [], 0
    for n, shp in zip(sizes, shapes):
        out.append(flat[off:off + n].reshape(shp))
        off += n
    return out


def kernel(x, pool_w, pool_scale, attn_w_qkv, attn_w_o, ffn_w_up, ffn_conv_w, ffn_conv_b, ffn_w_down, ln_mix_g, ln_mix_b, ln_ffn_g, ln_ffn_b, loss_target, m_pool_w, m_pool_scale, m_attn_w_qkv, m_attn_w_o, m_ffn_w_up, m_ffn_conv_w, m_ffn_conv_b, m_ffn_w_down, m_ln_mix_g, m_ln_mix_b, m_ln_ffn_g, m_ln_ffn_b, v_pool_w, v_pool_scale, v_attn_w_qkv, v_attn_w_o, v_ffn_w_up, v_ffn_conv_w, v_ffn_conv_b, v_ffn_w_down, v_ln_mix_g, v_ln_mix_b, v_ln_ffn_g, v_ln_ffn_b):
    t, d = x.shape[1], x.shape[2]
    n_heads = d // HEAD_DIM
    n_groups = pool_w.shape[1]
    fs = ffn_w_up.shape[2]
    fp = _round_up(fs, LANES)
    rd = ffn_w_down.shape[1]
    assert 2 * rd == fs
    xi, yi, ci = lax.axis_index("x"), lax.axis_index("y"), lax.axis_index("c")
    chip = (2 * xi + yi).astype(jnp.int32)
    chip_arr, core_arr = chip.reshape(1), ci.astype(jnp.int32).reshape(1)
    place_arr = jnp.concatenate([chip_arr, core_arr])

    x2 = x.reshape(t, d)
    target = loss_target.reshape(t, d)
    pad_cols = lambda a: jnp.pad(a, [(0, 0)] * (a.ndim - 1) + [(0, fp - fs)])
    up_t = [jnp.transpose(a, (0, 2, 1)) for a in (ffn_w_up, m_ffn_w_up, v_ffn_w_up)]

    gather_items = []
    for i in range(DEPTH):
        j = i // 2
        items = []
        if i % 2 == 0:
            items.append(("pool", _place_shard("pool", pool_w[j], chip_arr, name="place_pool"), pool_w.shape[2]))
        else:
            items.append(("lead", _place_shard("lead", attn_w_qkv[j], chip_arr, name="place_qkv"), d))
            items.append(("lead", _place_shard("lead", attn_w_o[j], chip_arr, name="place_wo"), attn_w_o.shape[1]))
        items.append(("lead", _place_shard("lead", up_t[0][i], chip_arr, name="place_up", rows=fp), fp))
        items.append(("down", _place_shard("down", ffn_w_down[i], chip_arr, name="place_down",
                                           base=jnp.zeros((2, fp, d), BF16)), rd))
        gather_items.append(items)
    weights = [None] * DEPTH
    weights[0] = _run_sides([_gather_sides(gather_items[0])("both")], name="gather_layer0")

    conv_b_all = pad_cols(ffn_conv_b.reshape(DEPTH, 4, 1, fs))
    cw_local = pad_cols(ffn_conv_w)
    slot = (jnp.arange(4, dtype=jnp.int32) == chip).astype(F32) * (1.0 - ci.astype(F32))
    cw_placed = slot[None, :, None, None] * cw_local[:, None]
    cw_buf, cw_sizes = _pack_rows([cw_placed])
    conv_w_all = _unpack_rows(_all_reduce_small(cw_buf, name="gather_conv_w"), cw_sizes, [cw_placed.shape])[0]

    gam = lambda a, i: a[i].reshape(1, d)

    saved = []
    cur, cur_b = x2, x2.astype(BF16)
    for i in range(DEPTH):
        j = i // 2
        w = weights[i]
        s = {"x_in": cur, "x_in_b": cur_b}
        if i % 2 == 0:
            w_pool, w_up, w_down = w
            s["scale"] = pool_scale[j].reshape(1, d)
            r1, x1, x1b = _pool_fwd(cur, w_pool, s["scale"], gam(ln_mix_g, i), gam(ln_mix_b, i), name="pool_fwd")
        else:
            w_qkv, w_o, w_up, w_down = w
            w_o3 = w_o.reshape(1, d, d)
            qkv = _mm_cols(cur_b, w_qkv, BF16, name="qkv_proj")
            o = _attn_fwd(qkv, n_heads, name="attn_fwd")
            s["qkv"], s["o"], s["w_o3"] = qkv, o, w_o3
            r1, x1, x1b = _mm_res_ln(o.reshape(1, t, d), w_o3, cur, gam(ln_mix_g, i), gam(ln_mix_b, i),
                                     name="attn_out_ln")
        if i + 1 < DEPTH:
            nxt = gather_items[i + 1]
            n_mix = len(nxt) - 2
            h, landed_ffn = _mm_cols(x1b, w_up, F32, name="ffn_up", transposed_b=True,
                                     side=_gather_sides(nxt[n_mix:])("ici"))
            a, landed_mix = _ffn_act(h, conv_w_all[i], conv_b_all[i], name="ffn_act",
                                     side=_gather_sides(nxt[:n_mix])("ici"))
            landed = list(landed_mix) + list(landed_ffn)
            (r2, x2n, x2b), gathered = _mm_res_ln(a, w_down, x1, gam(ln_ffn_g, i), gam(ln_ffn_b, i),
                                                  name="ffn_down_ln", side=_gather_sides(nxt, landed)("d2d"))
            weights[i + 1] = list(gathered)
        else:
            h = _mm_cols(x1b, w_up, F32, name="ffn_up", transposed_b=True)
            a = _ffn_act(h, conv_w_all[i], conv_b_all[i], name="ffn_act")
            r2, x2n, x2b = _mm_res_ln(a, w_down, x1, gam(ln_ffn_g, i), gam(ln_ffn_b, i), name="ffn_down_ln")
        s.update(r1=r1, x1b=x1b, h=h, a=a, r2=r2)
        saved.append(s)
        cur, cur_b = x2n, x2b

    loss_row, dcur = _loss_and_grad(cur, target, name="loss")
    loss = lax.psum(loss_row[0, 0], ("x", "y", "c"))

    big_grads = [None] * DEPTH
    reduced = [None] * DEPTH
    small = {}

    def finish_reduce(parts, from_chips, layer):
        halves = [_owner_sum(p[0], fc, place_arr, name="reduce_owner_sum") for p, fc in zip(parts, from_chips)]
        return _exchange_finished_halves(halves, name="reduce_halves_pool" if layer % 2 == 0 else "reduce_halves_attn")

    for i in reversed(range(DEPTH)):
        j = i // 2
        s, w = saved[i], weights[i]
        w_up, w_down = w[-2], w[-1]
        dr2, dr2b, small["ln_ffn_g", i], small["ln_ffn_b", i] = _ln_bwd(dcur, s["r2"], gam(ln_ffn_g, i), name="ln_bwd")
        pending = big_grads[i + 1] if i + 1 < DEPTH else None
        if pending is not None:
            da, from_sib = _mm_cols(dr2b, w_down, F32, name="ffn_down_bwd_act", transposed_b=True,
                                    side=_sibling_side(pending))
            parts = [_chip_partial(g, fs_, core_arr, name="reduce_chip_partial") for g, fs_ in zip(pending, from_sib)]
        else:
            da = _mm_cols(dr2b, w_down, F32, name="ffn_down_bwd_act", transposed_b=True)
        dr2b3 = dr2b.reshape(1, t, d)
        nmb = 2
        d_down = _mm_tn(s["a"], dr2b3, (2, fp, d), fp // nmb, d, 2,
                        (lambda u: u, nmb, lambda u, mb: mb), (lambda u: 0, lambda u: 0),
                        (lambda u: u, lambda u, mb: mb, lambda u: 0), name="ffn_down_bwd_w")
        dhc, dcw, dcb = _ffn_act_bwd(s["h"], da, conv_w_all[i], conv_b_all[i], name="ffn_act_bwd")
        small["conv_w", i], small["conv_b", i] = dcw, dcb
        dh = _conv_adjoint(dhc.reshape(4, t, fp), conv_w_all[i], name="ffn_conv_adjoint")
        up_w_args = (dh, s["x1b"].reshape(1, t, d), (4, fp, d), fp // 2, d, 4,
                     (lambda u: u, 2, lambda u, mb: mb), (lambda u: 0, lambda u: 0),
                     (lambda u: u, lambda u, mb: mb, lambda u: 0))
        if pending is not None:
            n_mix = len(parts) - 2
            dx1, chips_ffn = _mm_nt_acc(dh, w_up, dr2, fp, name="ffn_up_bwd_act", b_is_kn=True,
                                        side=_owner_chips_side([p[1] for p in parts[n_mix:]]))
            d_up, chips_mix = _mm_tn(*up_w_args, name="ffn_up_bwd_w",
                                     side=_owner_chips_side([p[1] for p in parts[:n_mix]]))
            reduced[i + 1] = finish_reduce(parts, list(chips_mix) + list(chips_ffn), i + 1)
        else:
            dx1 = _mm_nt_acc(dh, w_up, dr2, fp, name="ffn_up_bwd_act", b_is_kn=True)
            d_up = _mm_tn(*up_w_args, name="ffn_up_bwd_w")
        dr1, dr1b, small["ln_mix_g", i], small["ln_mix_b", i] = _ln_bwd(dx1, s["r1"], gam(ln_mix_g, i), name="ln_bwd")
        d_down4 = d_down[:, :fs].reshape(4, rd, d)
        if i % 2 == 0:
            dp, d_pool, small["pool_scale", j] = _pool_bwd(s["x_in"], dr1, w[0], s["scale"], name="pool_bwd")
            dcur = _pool_adjoint(dp, dr1, n_groups, name="pool_adjoint")
            cg = d // n_groups
            d_pool4 = d_pool.reshape(n_groups, 4, cg // 4, cg).transpose(1, 0, 2, 3).reshape(4, n_groups * (cg // 4), cg)
            big_grads[i] = [d_pool4, d_up, d_down4]
        else:
            w_qkv = w[0]
            do = _mm_cols(dr1b, s["w_o3"], BF16, name="attn_out_bwd_act", transposed_b=True)
            d_wo = _mm_tn(s["o"].reshape(1, t, d), dr1b.reshape(1, t, d), (1, d, d), d // 2, d, 1,
                          (lambda u: 0, 2, lambda u, mb: mb), (lambda u: 0, lambda u: 0),
                          (lambda u: 0, lambda u, mb: mb, lambda u: 0), name="attn_out_bwd_w")
            dq, dk, dv = _attn_bwd(s["qkv"], do, n_heads, name="attn_bwd")
            cq = w_qkv.shape[2]
            dqkv = jnp.concatenate([dq, dk, dv], axis=1).reshape(t, 4, cq).transpose(1, 0, 2)
            dcur = _mm_nt_acc(dqkv, w_qkv, dr1, cq, name="qkv_bwd_act")
            d_qkv = _mm_tn(s["x_in_b"].reshape(1, t, d), dqkv, (4, d, cq), d // 2, cq, 4,
                           (lambda u: 0, 2, lambda u, mb: mb), (lambda u: u, lambda u: 0),
                           (lambda u: u, lambda u, mb: mb, lambda u: 0), name="qkv_bwd_w")
            big_grads[i] = [d_qkv, d_wo.reshape(4, d // 4, d), d_up, d_down4]
    grad_x = dcur.reshape(1, t, d)

    from_sib = _run_sides([_sibling_side(big_grads[0])], name="reduce_layer0_sibling")
    parts = [_chip_partial(g, fs_, core_arr, name="reduce_chip_partial") for g, fs_ in zip(big_grads[0], from_sib)]
    from_chips = _run_sides([_owner_chips_side([p[1] for p in parts])], name="reduce_layer0_chips")
    reduced[0] = finish_reduce(parts, from_chips, 0)

    names = [("pool_scale", j) for j in range(2)]
    for nm in ("ln_mix_g", "ln_mix_b", "ln_ffn_g", "ln_ffn_b", "conv_b", "conv_w"):
        names += [(nm, i) for i in range(DEPTH)]
    vecs = [small[k] for k in names]
    sbuf, ssizes = _pack_rows(vecs)
    summed = dict(zip(names, _unpack_rows(_all_reduce_small(sbuf, name="reduce_small"), ssizes, [v.shape for v in vecs])))

    def stack_layers(nm, count):
        return jnp.stack([summed[nm, i] for i in range(count)])

    g_small = {
        "pool_scale": stack_layers("pool_scale", 2).reshape(2, d),
        "ln_mix_g": stack_layers("ln_mix_g", DEPTH).reshape(DEPTH, d),
        "ln_mix_b": stack_layers("ln_mix_b", DEPTH).reshape(DEPTH, d),
        "ln_ffn_g": stack_layers("ln_ffn_g", DEPTH).reshape(DEPTH, d),
        "ln_ffn_b": stack_layers("ln_ffn_b", DEPTH).reshape(DEPTH, d),
        "conv_b": stack_layers("conv_b", DEPTH).reshape(DEPTH, 4, fp)[:, :, :fs].reshape(DEPTH, 4 * fs),
        "conv_w": lax.dynamic_index_in_dim(stack_layers("conv_w", DEPTH).reshape(DEPTH, 4, 3, fp), chip, axis=1,
                                           keepdims=False)[:, :, :fs],
    }
    w_small = {"pool_scale": (pool_scale, m_pool_scale, v_pool_scale), "ln_mix_g": (ln_mix_g, m_ln_mix_g, v_ln_mix_g),
               "ln_mix_b": (ln_mix_b, m_ln_mix_b, v_ln_mix_b), "ln_ffn_g": (ln_ffn_g, m_ln_ffn_g, v_ln_ffn_g),
               "ln_ffn_b": (ln_ffn_b, m_ln_ffn_b, v_ln_ffn_b), "conv_b": (ffn_conv_b, m_ffn_conv_b, v_ffn_conv_b),
               "conv_w": (ffn_conv_w, m_ffn_conv_w, v_ffn_conv_w)}
    order = list(g_small)
    packs = [_pack_rows([g_small[k] for k in order])[0]]
    for idx in range(3):
        packs.append(_pack_rows([w_small[k][idx] for k in order])[0])
    small_sizes = _pack_rows([g_small[k] for k in order])[1]
    small_out = _adamw(packs[0], packs[1][None], packs[2][None], packs[3][None], 0, None, name="adamw_small")
    shapes = [g_small[k].shape for k in order]
    small_res = {k: [] for k in order}
    for arr in small_out:
        for k, val in zip(order, _unpack_rows(arr[0], small_sizes, shapes)):
            small_res[k].append(val)

    def opt_layers(per_layer_grads, w_all, m_all, v_all, name, rows=None):
        n_layers = w_all.shape[0]
        flat = [a.reshape(n_layers, rows or a.shape[1], -1) for a in (w_all, m_all, v_all)]
        res = None
        for li, g in enumerate(per_layer_grads):
            res = _adamw(g, *flat, li, res, name=name)
        return [o.reshape(w_all.shape) for o in res]

    cg = d // n_groups
    big = {
        "pool_w": opt_layers([reduced[i][0] for i in (0, 2)], pool_w, m_pool_w, v_pool_w, "adamw_pool",
                             rows=n_groups * (cg // 4)),
        "attn_w_qkv": opt_layers([reduced[i][0] for i in (1, 3)], attn_w_qkv, m_attn_w_qkv, v_attn_w_qkv, "adamw_qkv"),
        "attn_w_o": opt_layers([reduced[i][1] for i in (1, 3)], attn_w_o, m_attn_w_o, v_attn_w_o, "adamw_wo"),
        "ffn_w_up": [jnp.transpose(o, (0, 2, 1))
                     for o in opt_layers([reduced[i][-2] for i in range(DEPTH)], *up_t, "adamw_up")],
        "ffn_w_down": opt_layers([reduced[i][-1] for i in range(DEPTH)], ffn_w_down, m_ffn_w_down, v_ffn_w_down,
                                 "adamw_down"),
    }

    def leaf(k, name):
        if name in big:
            return big[name][k]
        key = {"ffn_conv_w": "conv_w", "ffn_conv_b": "conv_b"}.get(name, name)
        return small_res[key][k]

    weight_names = ["pool_w", "pool_scale", "attn_w_qkv", "attn_w_o", "ffn_w_up", "ffn_conv_w", "ffn_conv_b",
                    "ffn_w_down", "ln_mix_g", "ln_mix_b", "ln_ffn_g", "ln_ffn_b"]
    outs = [loss, grad_x]
    for k in range(4):
        outs += [leaf(k, nm) for nm in weight_names]
    return tuple(outs)
```

```python
import collections

import jax
import jax.numpy as jnp
from jax import lax
from jax.experimental import pallas as pl
from jax.experimental.pallas import tpu as pltpu

F32, BF16 = jnp.float32, jnp.bfloat16
MESH = pl.DeviceIdType.MESH

LANES = 128
HEAD_DIM = 128
ATT_BLOCK = 128
ATT_WINDOW = 3 * ATT_BLOCK
ATT_FWD_HEADS = 4
ATT_BWD_HEADS = 2
POOL_WINDOWS = (2, 4, 8, 16)
POOL_HALO = 16
CONV_HALO = 8
LN_EPS = 1e-5
DEPTH = 4
ALPHA = (2.0 * DEPTH) ** 0.25
ATT_SCALE = HEAD_DIM ** -0.5
EXP_ZERO = 115.0
MASKED = 1e30
ADAM_LR, ADAM_B1, ADAM_B2, ADAM_EPS, ADAM_WD, ADAM_STEP = 0.001, 0.9, 0.999, 1e-08, 0.01, 10

VMEM_LIMIT = 56 << 20
ROW_TILE = 512
LN_ROW_TILE = 256
OPT_ROW_TILE = 128
SUM_ROW_TILE = 512


def _cp(n_axes):
    return pltpu.CompilerParams(dimension_semantics=("arbitrary",) * n_axes, vmem_limit_bytes=VMEM_LIMIT)


def _sds(shape, dtype):
    return jax.ShapeDtypeStruct(tuple(shape), dtype)


def _round_up(n, m):
    return (n + m - 1) // m * m


def _tile(n, cap, mult=8):
    if n <= cap:
        return n
    best = None
    for d in range(mult, cap + 1, mult):
        if n % d == 0:
            best = d
    assert best is not None, (n, cap)
    return best


_NT = (((1,), (1,)), ((), ()))
_TN = (((0,), (0,)), ((), ()))

_Side = collections.namedtuple("_Side", "ins outs alias sems start finish")


def _any_specs(n):
    return [pl.BlockSpec(memory_space=pl.ANY)] * n


def _call(body, first, last, side, *, grid, in_specs, out_specs, out_shape, scratch_shapes, name, args):
    n_axes = len(grid)
    if side is None:
        res = pl.pallas_call(body, grid=grid, in_specs=in_specs, out_specs=out_specs, out_shape=out_shape,
                             scratch_shapes=scratch_shapes, name=name, compiler_params=_cp(n_axes))(*args)
        return res, ()
    n_in, n_out, n_scr = len(in_specs), len(out_shape), len(scratch_shapes)
    s_in, s_out = len(side.ins), len(side.outs)

    def carried(*refs):
        ins, refs = refs[:n_in], refs[n_in:]
        side_ins, refs = refs[:s_in], refs[s_in:]
        outs, refs = refs[:n_out], refs[n_out:]
        side_outs, refs = refs[:s_out], refs[s_out:]
        scratch, side_sems = refs[:n_scr], refs[n_scr:]

        @pl.when(first())
        def _():
            side.start(side_ins, side_outs, side_sems)

        body(*ins, *outs, *scratch)

        @pl.when(last())
        def _():
            side.finish(side_ins, side_outs, side_sems)

    res = pl.pallas_call(
        carried, grid=grid, in_specs=list(in_specs) + _any_specs(s_in), out_specs=list(out_specs) + _any_specs(s_out),
        out_shape=list(out_shape) + list(side.outs), scratch_shapes=list(scratch_shapes) + list(side.sems),
        input_output_aliases={n_in + a: n_out + b for a, b in side.alias.items()},
        name=name, compiler_params=_cp(n_axes))(*args, *side.ins)
    return res[:n_out], res[n_out:]


def _mm_cols(a, b, out_dtype, name, transposed_b=False, side=None):
    t, k = a.shape
    g = b.shape[0]
    nb = b.shape[1] if transposed_b else b.shape[2]
    tm = _tile(t, ROW_TILE)
    steps = t // tm

    def body(a_ref, b_ref, o_ref):
        if transposed_b:
            acc = lax.dot_general(a_ref[...], b_ref[...], _NT, preferred_element_type=F32)
        else:
            acc = jnp.dot(a_ref[...], b_ref[...], preferred_element_type=F32)
        o_ref[...] = acc.astype(o_ref.dtype)

    first = lambda: jnp.logical_and(pl.program_id(0) == 0, pl.program_id(1) == 0)
    last = lambda: jnp.logical_and(pl.program_id(0) == g - 1, pl.program_id(1) == steps - 1)
    (out,), side_out = _call(
        body, first, last, side, grid=(g, steps),
        in_specs=[pl.BlockSpec((tm, k), lambda gi, i: (i, 0)),
                  pl.BlockSpec((None,) + b.shape[1:], lambda gi, i: (gi, 0, 0))],
        out_specs=[pl.BlockSpec((None, tm, nb), lambda gi, i: (gi, i, 0))],
        out_shape=[_sds((g, t, nb), out_dtype)], scratch_shapes=[], name=name, args=(a, b))
    return out if side is None else (out, side_out)


def _mm_nt_acc(a3, b3, res, kb, name, side=None, b_is_kn=False):
    ga, t, ka = a3.shape
    gb, n, kbb = (b3.shape[0], b3.shape[2], b3.shape[1]) if b_is_kn else b3.shape
    na, nbk = ka // kb, kbb // kb
    groups = ga * na
    assert groups == gb * nbk
    tm = _tile(t, ROW_TILE)
    steps = t // tm

    def body(a_ref, b_ref, res_ref, o_ref, acc):
        u = pl.program_id(1)

        @pl.when(u == 0)
        def _():
            acc[...] = ALPHA * res_ref[...]

        if b_is_kn:
            acc[...] += jnp.dot(a_ref[...], b_ref[...], preferred_element_type=F32)
        else:
            acc[...] += lax.dot_general(a_ref[...], b_ref[...], _NT, preferred_element_type=F32)

        @pl.when(u == groups - 1)
        def _():
            o_ref[...] = acc[...]

    first = lambda: jnp.logical_and(pl.program_id(0) == 0, pl.program_id(1) == 0)
    last = lambda: jnp.logical_and(pl.program_id(0) == steps - 1, pl.program_id(1) == groups - 1)
    if b_is_kn:
        b_spec = pl.BlockSpec((None, kb, n), lambda i, u: (u // nbk, u % nbk, 0))
    else:
        b_spec = pl.BlockSpec((None, n, kb), lambda i, u: (u // nbk, 0, u % nbk))
    (out,), side_out = _call(
        body, first, last, side, grid=(steps, groups),
        in_specs=[pl.BlockSpec((None, tm, kb), lambda i, u: (u // na, i, u % na)), b_spec,
                  pl.BlockSpec((tm, n), lambda i, u: (i, 0))],
        out_specs=[pl.BlockSpec((tm, n), lambda i, u: (i, 0))],
        out_shape=[_sds((t, n), F32)], scratch_shapes=[pltpu.VMEM((tm, n), F32)], name=name, args=(a3, b3, res))
    return out if side is None else (out, side_out)


def _mm_tn(x3, dy3, out_shape, bm, bn, groups, x_idx, dy_idx, out_idx, name, side=None):
    t = x3.shape[1]
    tm = _tile(t, 2 * ROW_TILE)
    grid = (groups, x_idx[1], t // tm)

    def body(x_ref, dy_ref, o_ref):
        @pl.when(pl.program_id(2) == 0)
        def _():
            o_ref[...] = jnp.zeros_like(o_ref)

        o_ref[...] += lax.dot_general(x_ref[...], dy_ref[...], _TN, preferred_element_type=F32)

    def at(corner):
        hit = pl.program_id(0) == corner[0]
        for axis in (1, 2):
            hit = jnp.logical_and(hit, pl.program_id(axis) == corner[axis])
        return hit

    (out,), side_out = _call(
        body, lambda: at((0, 0, 0)), lambda: at(tuple(g - 1 for g in grid)), side, grid=grid,
        in_specs=[pl.BlockSpec((None, tm, bm), lambda u, mb, i: (x_idx[0](u), i, x_idx[2](u, mb))),
                  pl.BlockSpec((None, tm, bn), lambda u, mb, i: (dy_idx[0](u), i, dy_idx[1](u)))],
        out_specs=[pl.BlockSpec((None, bm, bn), lambda u, mb, i: (out_idx[0](u), out_idx[1](u, mb), out_idx[2](u)))],
        out_shape=[_sds(out_shape, F32)], scratch_shapes=[], name=name, args=(x3, dy3))
    return out if side is None else (out, side_out)


def _layer_norm_rows(r, gamma, beta):
    mu = jnp.mean(r, axis=-1, keepdims=True)
    xc = r - mu
    var = jnp.mean(xc * xc, axis=-1, keepdims=True)
    return xc * lax.rsqrt(var + LN_EPS) * gamma + beta


def _mm_res_ln(a3, w3, res, gamma, beta, name, side=None):
    g, t, kb = a3.shape
    d = w3.shape[2]
    tm = _tile(t, LN_ROW_TILE)
    steps = t // tm

    def body(a_ref, w_hbm, res_ref, g_ref, b_ref, r_ref, o_ref, ob_ref, w_vmem, sem):
        @pl.when(pl.program_id(0) == 0)
        def _():
            cp = pltpu.make_async_copy(w_hbm, w_vmem, sem)
            cp.start()
            cp.wait()

        acc = ALPHA * res_ref[...]
        for gi in range(g):
            acc = acc + jnp.dot(a_ref[gi], w_vmem[gi], preferred_element_type=F32)
        r_ref[...] = acc
        out = _layer_norm_rows(acc, g_ref[...], b_ref[...])
        o_ref[...] = out
        ob_ref[...] = out.astype(BF16)

    row = pl.BlockSpec((tm, d), lambda i: (i, 0))
    vec = pl.BlockSpec((1, d), lambda i: (0, 0))
    outs, side_out = _call(
        body, lambda: pl.program_id(0) == 0, lambda: pl.program_id(0) == steps - 1, side, grid=(steps,),
        in_specs=[pl.BlockSpec((g, tm, kb), lambda i: (0, i, 0)), pl.BlockSpec(memory_space=pl.ANY), row, vec, vec],
        out_specs=[row, row, row],
        out_shape=[_sds((t, d), F32), _sds((t, d), F32), _sds((t, d), BF16)],
        scratch_shapes=[pltpu.VMEM(w3.shape, w3.dtype), pltpu.SemaphoreType.DMA],
        name=name, args=(a3, w3, res, gamma, beta))
    return outs if side is None else (outs, side_out)


def _ln_bwd(dout, r, gamma, name):
    t, d = r.shape
    tm = _tile(t, ROW_TILE)

    def body(do_ref, r_ref, g_ref, dr_ref, drb_ref, dg_ref, db_ref):
        @pl.when(pl.program_id(0) == 0)
        def _():
            dg_ref[...] = jnp.zeros_like(dg_ref)
            db_ref[...] = jnp.zeros_like(db_ref)

        rr = r_ref[...]
        do = do_ref[...]
        mu = jnp.mean(rr, axis=-1, keepdims=True)
        xc = rr - mu
        rstd = lax.rsqrt(jnp.mean(xc * xc, axis=-1, keepdims=True) + LN_EPS)
        xhat = xc * rstd
        dxh = do * g_ref[...]
        m1 = jnp.mean(dxh, axis=-1, keepdims=True)
        m2 = jnp.mean(dxh * xhat, axis=-1, keepdims=True)
        dr = rstd * (dxh - m1 - xhat * m2)
        dr_ref[...] = dr
        drb_ref[...] = dr.astype(BF16)
        dg_ref[...] += jnp.sum(do * xhat, axis=0, keepdims=True)
        db_ref[...] += jnp.sum(do, axis=0, keepdims=True)

    row = pl.BlockSpec((tm, d), lambda i: (i, 0))
    vec = pl.BlockSpec((1, d), lambda i: (0, 0))
    return pl.pallas_call(
        body, grid=(t // tm,), in_specs=[row, row, vec], out_specs=[row, row, vec, vec],
        out_shape=[_sds((t, d), F32), _sds((t, d), BF16), _sds((1, d), F32), _sds((1, d), F32)],
        name=name, compiler_params=_cp(1))(dout, r, gamma)


def _loss_and_grad(y, target, name):
    t, d = y.shape
    tm = _tile(t, ROW_TILE)
    steps = t // tm

    def body(y_ref, t_ref, loss_ref, dy_ref, acc):
        i = pl.program_id(0)

        @pl.when(i == 0)
        def _():
            acc[...] = jnp.zeros_like(acc)

        diff = y_ref[...] - t_ref[...]
        dy_ref[...] = diff * (1.0 / d)
        acc[...] += jnp.sum(diff * diff, axis=0, keepdims=True)

        @pl.when(i == steps - 1)
        def _():
            total = jnp.sum(acc[...], axis=1, keepdims=True) * (0.5 / d)
            loss_ref[...] = jnp.broadcast_to(total, loss_ref.shape)

    row = pl.BlockSpec((tm, d), lambda i: (i, 0))
    return pl.pallas_call(
        body, grid=(steps,), in_specs=[row, row],
        out_specs=[pl.BlockSpec((1, LANES), lambda i: (0, 0)), row],
        out_shape=[_sds((1, LANES), F32), _sds((t, d), F32)],
        scratch_shapes=[pltpu.VMEM((1, d), F32)], name=name, compiler_params=_cp(1))(y, target)


def _window_sums(ext, window, forward):
    n = ext.shape[0]
    s, span = ext, 1
    while span < window:
        s = s + pltpu.roll(s, (n - span) if forward else span, 0)
        span *= 2
    return s


def _pooled_group(main, halo, gi, row0):
    window = POOL_WINDOWS[gi]
    ext = jnp.concatenate([halo, main], axis=0)
    sums = _window_sums(ext, window, forward=False)[POOL_HALO:, :]
    pos = row0 + lax.broadcasted_iota(jnp.int32, (main.shape[0], 1), 0)
    cnt = jnp.minimum(pos + 1, window).astype(F32)
    return sums / cnt - main


def _pool_specs(t, d, tm):
    per = tm // POOL_HALO
    main = pl.BlockSpec((tm, d), lambda i: (i, 0))
    before = pl.BlockSpec((POOL_HALO, d), lambda i: (jnp.maximum(i * per - 1, 0), 0))
    return main, before


def _pool_fwd(x, w, scale, gamma, beta, name):
    t, d = x.shape
    ng, cg = w.shape[0], w.shape[1]
    tm = _tile(t, LN_ROW_TILE)

    def body(x_ref, h_ref, w_ref, s_ref, g_ref, b_ref, r_ref, o_ref, ob_ref):
        i = pl.program_id(0)
        for gi in range(ng):
            cols = pl.ds(gi * cg, cg)
            main = x_ref[:, cols]
            halo = jnp.where(i > 0, h_ref[:, cols], 0.0)
            pooled = _pooled_group(main, halo, gi, i * tm)
            y = jnp.dot(pooled.astype(BF16), w_ref[gi], preferred_element_type=F32)
            r_ref[:, cols] = ALPHA * main + y * s_ref[:, cols]
        out = _layer_norm_rows(r_ref[...], g_ref[...], b_ref[...])
        o_ref[...] = out
        ob_ref[...] = out.astype(BF16)

    main, before = _pool_specs(t, d, tm)
    vec = pl.BlockSpec((1, d), lambda i: (0, 0))
    return pl.pallas_call(
        body, grid=(t // tm,),
        in_specs=[main, before, pl.BlockSpec(w.shape, lambda i: (0, 0, 0)), vec, vec, vec],
        out_specs=[main, main, main],
        out_shape=[_sds((t, d), F32), _sds((t, d), F32), _sds((t, d), BF16)],
        name=name, compiler_params=_cp(1))(x, x, w, scale, gamma, beta)


def _pool_bwd(x, dy, w, scale, name):
    t, d = x.shape
    ng, cg = w.shape[0], w.shape[1]
    tm = _tile(t, LN_ROW_TILE)

    def body(x_ref, h_ref, dy_ref, w_ref, s_ref, dp_ref, dw_ref, ds_ref):
        i = pl.program_id(0)

        @pl.when(i == 0)
        def _():
            dw_ref[...] = jnp.zeros_like(dw_ref)
            ds_ref[...] = jnp.zeros_like(ds_ref)

        for gi in range(ng):
            cols = pl.ds(gi * cg, cg)
            main = x_ref[:, cols]
            halo = jnp.where(i > 0, h_ref[:, cols], 0.0)
            pooled = _pooled_group(main, halo, gi, i * tm).astype(BF16)
            y = jnp.dot(pooled, w_ref[gi], preferred_element_type=F32)
            dyg = dy_ref[:, cols]
            ds_ref[:, cols] += jnp.sum(dyg * y, axis=0, keepdims=True)
            dyw = (dyg * s_ref[:, cols]).astype(BF16)
            dw_ref[gi] += lax.dot_general(pooled, dyw, _TN, preferred_element_type=F32)
            dp_ref[:, cols] = lax.dot_general(dyw, w_ref[gi], _NT, preferred_element_type=F32)

    main, before = _pool_specs(t, d, tm)
    vec = pl.BlockSpec((1, d), lambda i: (0, 0))
    wspec = pl.BlockSpec(w.shape, lambda i: (0, 0, 0))
    return pl.pallas_call(
        body, grid=(t // tm,), in_specs=[main, before, main, wspec, vec],
        out_specs=[main, wspec, vec],
        out_shape=[_sds((t, d), F32), _sds(w.shape, F32), _sds((1, d), F32)],
        name=name, compiler_params=_cp(1))(x, x, dy, w, scale)


def _pool_adjoint(dp, dres, n_groups, name):
    t, d = dp.shape
    cg = d // n_groups
    tm = _tile(t, ROW_TILE)
    steps = t // tm
    per = tm // POOL_HALO

    def body(dp_ref, after_ref, dres_ref, dx_ref):
        i = pl.program_id(0)
        rows = lax.broadcasted_iota(jnp.int32, (tm, 1), 0)
        rows_after = lax.broadcasted_iota(jnp.int32, (POOL_HALO, 1), 0)
        for gi in range(n_groups):
            window = POOL_WINDOWS[gi]
            cols = pl.ds(gi * cg, cg)
            main = dp_ref[:, cols]
            cnt = jnp.minimum(i * tm + rows + 1, window).astype(F32)
            cnt_after = jnp.minimum((i + 1) * tm + rows_after + 1, window).astype(F32)
            after = jnp.where(i < steps - 1, after_ref[:, cols] / cnt_after, 0.0)
            ext = jnp.concatenate([main / cnt, after], axis=0)
            sums = _window_sums(ext, window, forward=True)[:tm, :]
            dx_ref[:, cols] = ALPHA * dres_ref[:, cols] + sums - main

    main = pl.BlockSpec((tm, d), lambda i: (i, 0))
    after = pl.BlockSpec((POOL_HALO, d), lambda i: (jnp.minimum((i + 1) * per, t // POOL_HALO - 1), 0))
    return pl.pallas_call(
        body, grid=(steps,), in_specs=[main, after, main], out_specs=main,
        out_shape=_sds((t, d), F32), name=name, compiler_params=_cp(1))(dp, dp, dres)


def _split_dot(x, tri):
    hi = x.astype(BF16)
    lo = (x - hi.astype(F32)).astype(BF16)
    return jnp.dot(hi, tri, preferred_element_type=F32) + jnp.dot(lo, tri, preferred_element_type=F32)


def _att_windows(qs, k_ws, limit, carry_rests, suffix):
    heads = range(len(qs))
    zs = [lax.dot_general(qs[hh], k_ws[hh], _NT, preferred_element_type=F32) * ATT_SCALE for hh in heads]
    visible = lax.broadcasted_iota(jnp.int32, zs[0].shape, 1) < limit
    zs = [jnp.where(visible, z, -MASKED) for z in zs]
    es = [jnp.exp(-jnp.abs(z)) for z in zs]
    log_nots = [-(jnp.maximum(z, 0.0) + jnp.log(1.0 + e)) for z, e in zip(zs, es)]
    rests = [_split_dot(ln, suffix[...]) + carry for ln, carry in zip(log_nots, carry_rests)]
    weights = [jnp.exp(z + r) for z, r in zip(zs, rests)]
    return zs, es, log_nots, weights


def _tri(w, strict):
    r = lax.broadcasted_iota(jnp.int32, (w, w), 0)
    c = lax.broadcasted_iota(jnp.int32, (w, w), 1)
    return ((r > c) if strict else (r >= c)).astype(BF16)


def _heads_per_step(qkv3, n_heads, most):
    cpb = qkv3.shape[2] // HEAD_DIM
    hp = most
    while cpb % hp or n_heads % hp:
        hp //= 2
    return hp


def _att_specs(qkv3, n_heads, hp):
    t = qkv3.shape[1]
    cpb = qkv3.shape[2] // HEAD_DIM
    wd = hp * HEAD_DIM

    def slab(off):
        return pl.BlockSpec((None, t, wd), lambda g, i: ((off + g * hp) // cpb, 0, ((off + g * hp) % cpb) // hp))

    q = pl.BlockSpec((None, ATT_BLOCK, wd), lambda g, i: ((g * hp) // cpb, i, ((g * hp) % cpb) // hp))
    return q, slab(n_heads), slab(2 * n_heads)


def _head_cols(hh):
    return pl.ds(hh * HEAD_DIM, HEAD_DIM)


def _key_bounds(k_ref, kmax, hp):
    for hh in range(hp):
        kf = k_ref[:, _head_cols(hh)].astype(F32)
        kmax[hh] = jnp.sqrt(jnp.max(jnp.sum(kf * kf, axis=1, keepdims=True)))


def _score_bound(q, key_norm):
    qf = q.astype(F32)
    return ATT_SCALE * 1.001 * key_norm * jnp.sqrt(jnp.sum(qf * qf, axis=1, keepdims=True)) + 1e-3


def _any_alive(rests, bounds):
    alive = jnp.max(rests[0] + bounds[0]) > -EXP_ZERO
    for r, zb in zip(rests[1:], bounds[1:]):
        alive = jnp.logical_or(alive, jnp.max(r + zb) > -EXP_ZERO)
    return alive


def _window_rows(hi, w):
    start = jnp.maximum(hi - w, 0)
    return start, pl.ds(pl.multiple_of(start, ATT_BLOCK), w)


def _attn_fwd(qkv3, n_heads, name):
    t = qkv3.shape[1]
    b = ATT_BLOCK
    w = min(ATT_WINDOW, t)
    hp = _heads_per_step(qkv3, n_heads, ATT_FWD_HEADS)
    heads = range(hp)

    def body(q_ref, k_ref, v_ref, o_ref, kmax, suffix):
        i = pl.program_id(1)

        @pl.when(i == 0)
        def _():
            _key_bounds(k_ref, kmax, hp)
            suffix[...] = _tri(w, strict=False)

        qs = [q_ref[:, _head_cols(hh)] for hh in heads]
        bounds = [_score_bound(qs[hh], kmax[hh]) for hh in heads]
        qpos = i * b + lax.broadcasted_iota(jnp.int32, (b, 1), 0)

        def cond(c):
            return jnp.logical_and(c[0] > 0, _any_alive(c[1], bounds))

        def step(c):
            hi, rests, accs = c
            start, rows = _window_rows(hi, w)
            limit = jnp.minimum(qpos, hi) - start
            k_ws = [k_ref[rows, _head_cols(hh)] for hh in heads]
            _, _, log_nots, weights = _att_windows(qs, k_ws, limit, rests, suffix)
            new_accs = tuple(accs[hh] + jnp.dot(weights[hh].astype(BF16), v_ref[rows, _head_cols(hh)],
                                                preferred_element_type=F32) for hh in heads)
            new_rests = tuple(rests[hh] + jnp.sum(log_nots[hh], axis=1, keepdims=True) for hh in heads)
            return start, new_rests, new_accs

        init = ((i + 1) * b, tuple(jnp.zeros((b, 1), F32) for _ in heads),
                tuple(jnp.zeros((b, HEAD_DIM), F32) for _ in heads))
        _, _, accs = lax.while_loop(cond, step, step(init))
        for hh in heads:
            o_ref[:, _head_cols(hh)] = accs[hh].astype(o_ref.dtype)

    qs_, ks_, vs_ = _att_specs(qkv3, n_heads, hp)
    return pl.pallas_call(
        body, grid=(n_heads // hp, t // b), in_specs=[qs_, ks_, vs_],
        out_specs=pl.BlockSpec((b, hp * HEAD_DIM), lambda g, i: (i, g)),
        out_shape=_sds((t, n_heads * HEAD_DIM), BF16),
        scratch_shapes=[pltpu.SMEM((hp,), F32), pltpu.VMEM((w, w), BF16)],
        name=name, compiler_params=_cp(2))(qkv3, qkv3, qkv3)


def _attn_bwd(qkv3, do, n_heads, name):
    t = qkv3.shape[1]
    b = ATT_BLOCK
    w = min(ATT_WINDOW, t)
    nq = t // b
    hp = _heads_per_step(qkv3, n_heads, ATT_BWD_HEADS)
    heads = range(hp)
    wd = hp * HEAD_DIM

    def body(q_ref, k_ref, v_ref, do_ref, dq_ref, dk_ref, dv_ref, kmax, dk_acc, dv_acc, suffix, strict_suffix):
        i = pl.program_id(1)

        @pl.when(i == 0)
        def _():
            _key_bounds(k_ref, kmax, hp)
            dk_acc[...] = jnp.zeros_like(dk_acc)
            dv_acc[...] = jnp.zeros_like(dv_acc)
            suffix[...] = _tri(w, strict=False)
            strict_suffix[...] = _tri(w, strict=True)

        qs = [q_ref[:, _head_cols(hh)] for hh in heads]
        douts = [do_ref[:, _head_cols(hh)] for hh in heads]
        bounds = [_score_bound(qs[hh], kmax[hh]) for hh in heads]
        zero_cols = tuple(jnp.zeros((b, 1), F32) for _ in heads)
        hi0 = (i + 1) * b
        qpos = i * b + lax.broadcasted_iota(jnp.int32, (b, 1), 0)

        def cond(c):
            return jnp.logical_and(c[0] > 0, _any_alive(c[1], bounds))

        def tiles_of(hi, rests):
            start, rows = _window_rows(hi, w)
            k_ws = [k_ref[rows, _head_cols(hh)] for hh in heads]
            dps = [lax.dot_general(douts[hh], v_ref[rows, _head_cols(hh)], _NT, preferred_element_type=F32)
                   for hh in heads]
            zs, es, log_nots, weights = _att_windows(qs, k_ws, jnp.minimum(qpos, hi) - start, rests, suffix)
            dlas = [a * dp for a, dp in zip(weights, dps)]
            return start, rows, (k_ws, zs, es, log_nots, weights, dlas)

        def row_sums(carries, tiles):
            return tuple(c + jnp.sum(x, axis=1, keepdims=True) for c, x in zip(carries, tiles))

        def first_pass(rows, tiles, rests, totals):
            _, _, _, log_nots, weights, dlas = tiles
            for hh in heads:
                dv_acc[rows, _head_cols(hh)] += lax.dot_general(weights[hh].astype(BF16), douts[hh], _TN,
                                                                preferred_element_type=F32)
            return row_sums(rests, log_nots), row_sums(totals, dlas)

        def second_pass(rows, tiles, totals, laters, dqs):
            k_ws, zs, es, _, _, dlas = tiles
            insides = [_split_dot(dla, strict_suffix[...]) for dla in dlas]
            dzs = []
            for hh in heads:
                dlog_not = totals[hh] - laters[hh] - insides[hh]
                inv = 1.0 / (1.0 + es[hh])
                sig = jnp.where(zs[hh] >= 0, inv, es[hh] * inv)
                dzs.append(((dlas[hh] - sig * dlog_not) * ATT_SCALE).astype(BF16))
            new_dqs = tuple(dqs[hh] + jnp.dot(dzs[hh], k_ws[hh], preferred_element_type=F32) for hh in heads)
            for hh in heads:
                dk_acc[rows, _head_cols(hh)] += lax.dot_general(dzs[hh], qs[hh], _TN, preferred_element_type=F32)
            return row_sums(laters, dlas), new_dqs

        start0, rows0, tiles0 = tiles_of(hi0, zero_cols)
        rests1, totals1 = first_pass(rows0, tiles0, zero_cols, zero_cols)

        def sweep1(c):
            hi, rests, totals = c
            start, rows, tiles = tiles_of(hi, rests)
            return (start,) + first_pass(rows, tiles, rests, totals)

        _, _, totals = lax.while_loop(cond, sweep1, (start0, rests1, totals1))
        laters1, dqs1 = second_pass(rows0, tiles0, totals, zero_cols,
                                    tuple(jnp.zeros((b, HEAD_DIM), F32) for _ in heads))

        def sweep2(c):
            hi, rests, laters, dqs = c
            start, rows, tiles = tiles_of(hi, rests)
            return (start, row_sums(rests, tiles[3])) + second_pass(rows, tiles, totals, laters, dqs)

        _, _, _, dqs = lax.while_loop(cond, sweep2, (start0, rests1, laters1, dqs1))
        for hh in heads:
            dq_ref[:, _head_cols(hh)] = dqs[hh].astype(dq_ref.dtype)

        @pl.when(i == nq - 1)
        def _():
            dk_ref[...] = dk_acc[...].astype(dk_ref.dtype)
            dv_ref[...] = dv_acc[...].astype(dv_ref.dtype)

    qs_, ks_, vs_ = _att_specs(qkv3, n_heads, hp)
    blk = pl.BlockSpec((b, wd), lambda g, i: (i, g))
    slab = pl.BlockSpec((t, wd), lambda g, i: (0, g))
    d = n_heads * HEAD_DIM
    return pl.pallas_call(
        body, grid=(n_heads // hp, nq),
        in_specs=[qs_, ks_, vs_, pl.BlockSpec((None, b, wd), lambda g, i: (0, i, g))],
        out_specs=[blk, slab, slab],
        out_shape=[_sds((t, d), BF16)] * 3,
        scratch_shapes=[pltpu.SMEM((hp,), F32), pltpu.VMEM((t, wd), F32), pltpu.VMEM((t, wd), F32),
                        pltpu.VMEM((w, w), BF16), pltpu.VMEM((w, w), BF16)],
        name=name, compiler_params=_cp(2))(qkv3, qkv3, qkv3, do)


def _conv_rows(main, halo, w_ref, b_ref):
    ext = jnp.concatenate([halo, main], axis=0)
    h1 = pltpu.roll(ext, 1, 0)[CONV_HALO:, :]
    h2 = pltpu.roll(ext, 2, 0)[CONV_HALO:, :]
    hc = b_ref[...] + w_ref[0:1, :] * h2
    hc = hc + w_ref[1:2, :] * h1
    hc = hc + w_ref[2:3, :] * main
    return hc, h1, h2


def _ffn_specs(t, fp, tm, half):
    per = tm // CONV_HALO
    main = lambda off: pl.BlockSpec((None, tm, fp), lambda g, i: (g + off, i, 0))
    before = lambda off: pl.BlockSpec((None, CONV_HALO, fp), lambda g, i: (g + off, jnp.maximum(i * per - 1, 0), 0))
    cw = lambda off: pl.BlockSpec((None, 3, fp), lambda g, i: (g + off, 0, 0))
    cb = lambda off: pl.BlockSpec((None, 1, fp), lambda g, i: (g + off, 0, 0))
    return [main(0), before(0), main(half), before(half), cw(0), cw(half), cb(0), cb(half)]


def _ffn_act(h, cw, cb, name, side=None):
    n, t, fp = h.shape
    half = n // 2
    tm = _tile(t, LN_ROW_TILE)
    steps = t // tm

    def body(hg_ref, hgb_ref, hv_ref, hvb_ref, wg_ref, wv_ref, bg_ref, bv_ref, a_ref):
        first = pl.program_id(1) == 0
        gate, _, _ = _conv_rows(hg_ref[...], jnp.where(first, 0.0, hgb_ref[...]), wg_ref, bg_ref)
        val, _, _ = _conv_rows(hv_ref[...], jnp.where(first, 0.0, hvb_ref[...]), wv_ref, bv_ref)
        a_ref[...] = (gate * jax.nn.sigmoid(gate) * val).astype(a_ref.dtype)

    (a,), side_out = _call(
        body, lambda: jnp.logical_and(pl.program_id(0) == 0, pl.program_id(1) == 0),
        lambda: jnp.logical_and(pl.program_id(0) == half - 1, pl.program_id(1) == steps - 1), side,
        grid=(half, steps), in_specs=_ffn_specs(t, fp, tm, half),
        out_specs=[pl.BlockSpec((None, tm, fp), lambda g, i: (g, i, 0))],
        out_shape=[_sds((half, t, fp), BF16)], scratch_shapes=[], name=name, args=(h, h, h, h, cw, cw, cb, cb))
    return a if side is None else (a, side_out)


def _act_grads(dact, gate, val):
    sig = jax.nn.sigmoid(gate)
    return dact * val * (sig * (1.0 + gate * (1.0 - sig))), dact * (gate * sig)


def _ffn_act_bwd(h, da, cw, cb, name):
    n, t, fp = h.shape
    half = n // 2
    tm = _tile(t, LN_ROW_TILE)

    def body(hg_ref, hgb_ref, hv_ref, hvb_ref, wg_ref, wv_ref, bg_ref, bv_ref, da_ref, dhc_ref, dw_ref, db_ref):
        first = pl.program_id(1) == 0

        @pl.when(first)
        def _():
            dw_ref[...] = jnp.zeros_like(dw_ref)
            db_ref[...] = jnp.zeros_like(db_ref)

        hg, hv = hg_ref[...], hv_ref[...]
        gate, hg1, hg2 = _conv_rows(hg, jnp.where(first, 0.0, hgb_ref[...]), wg_ref, bg_ref)
        val, hv1, hv2 = _conv_rows(hv, jnp.where(first, 0.0, hvb_ref[...]), wv_ref, bv_ref)
        dgate, dval = _act_grads(da_ref[...], gate, val)
        dhc_ref[0] = dgate
        dhc_ref[1] = dval
        for s, (dd, shifted) in enumerate(((dgate, (hg2, hg1, hg)), (dval, (hv2, hv1, hv)))):
            db_ref[s] += jnp.sum(dd, axis=0, keepdims=True)
            for kk in range(3):
                dw_ref[s, kk:kk + 1, :] += jnp.sum(dd * shifted[kk], axis=0, keepdims=True)

    specs = _ffn_specs(t, fp, tm, half) + [pl.BlockSpec((None, tm, fp), lambda g, i: (g, i, 0))]
    return pl.pallas_call(
        body, grid=(half, t // tm), in_specs=specs,
        out_specs=[pl.BlockSpec((2, None, tm, fp), lambda g, i: (0, g, i, 0)),
                   pl.BlockSpec((2, None, 3, fp), lambda g, i: (0, g, 0, 0)),
                   pl.BlockSpec((2, None, 1, fp), lambda g, i: (0, g, 0, 0))],
        out_shape=[_sds((2, half, t, fp), F32), _sds((2, half, 3, fp), F32), _sds((2, half, 1, fp), F32)],
        name=name, compiler_params=_cp(2))(h, h, h, h, cw, cw, cb, cb, da)


def _conv_adjoint(dhc, cw, name):
    n, t, fp = dhc.shape
    tm = _tile(t, ROW_TILE)
    steps = t // tm
    per = tm // CONV_HALO

    def body(d_ref, after_ref, w_ref, o_ref):
        main = d_ref[...]
        after = jnp.where(pl.program_id(1) < steps - 1, after_ref[...], 0.0)
        ext = jnp.concatenate([main, after], axis=0)
        rows = ext.shape[0]
        d1 = pltpu.roll(ext, rows - 1, 0)[:tm, :]
        d2 = pltpu.roll(ext, rows - 2, 0)[:tm, :]
        o_ref[...] = (w_ref[2:3, :] * main + w_ref[1:2, :] * d1 + w_ref[0:1, :] * d2).astype(o_ref.dtype)

    main = pl.BlockSpec((None, tm, fp), lambda g, i: (g, i, 0))
    after = pl.BlockSpec((None, CONV_HALO, fp), lambda g, i: (g, jnp.minimum((i + 1) * per, t // CONV_HALO - 1), 0))
    return pl.pallas_call(
        body, grid=(n, steps), in_specs=[main, after, pl.BlockSpec((None, 3, fp), lambda g, i: (g, 0, 0))],
        out_specs=main, out_shape=_sds((n, t, fp), BF16), name=name, compiler_params=_cp(2))(dhc, dhc, cw)


def _place():
    x, y, c = lax.axis_index("x"), lax.axis_index("y"), lax.axis_index("c")
    chips = [(1 - x, y), (x, 1 - y), (1 - x, 1 - y)]
    return x, y, c, chips


def _run_sides(sides, name):
    n_in = [len(s.ins) for s in sides]
    n_out = [len(s.outs) for s in sides]
    n_sem = [len(s.sems) for s in sides]

    def body(*refs):
        ins, outs, sems = refs[:sum(n_in)], refs[sum(n_in):sum(n_in) + sum(n_out)], refs[sum(n_in) + sum(n_out):]
        oi = oo = os_ = 0
        for k, s in enumerate(sides):
            mine = (ins[oi:oi + n_in[k]], outs[oo:oo + n_out[k]], sems[os_:os_ + n_sem[k]])
            s.start(*mine)
            s.finish(*mine)
            oi, oo, os_ = oi + n_in[k], oo + n_out[k], os_ + n_sem[k]

    aliases, oi, oo = {}, 0, 0
    for k, s in enumerate(sides):
        aliases.update({oi + a: oo + b for a, b in s.alias.items()})
        oi, oo = oi + n_in[k], oo + n_out[k]
    return pl.pallas_call(
        body, in_specs=_any_specs(sum(n_in)), out_specs=_any_specs(sum(n_out)),
        out_shape=[o for s in sides for o in s.outs], input_output_aliases=aliases,
        scratch_shapes=[q for s in sides for q in s.sems], name=name)(*[a for s in sides for a in s.ins])


def _place_shard(kind, w, chip, name, rows=None, base=None, layer=None):
    if kind == "pool":
        g, r, cdim = w.shape

        def body(chip_ref, w_ref, o_ref):
            del chip_ref
            o_ref[...] = w_ref[...].astype(BF16)

        return pl.pallas_call(
            body,
            grid_spec=pltpu.PrefetchScalarGridSpec(
                num_scalar_prefetch=1, grid=(1,),
                in_specs=[pl.BlockSpec((g, r, cdim), lambda i, chip_ref: (0, 0, 0))],
                out_specs=pl.BlockSpec((g, r, cdim), lambda i, chip_ref: (0, chip_ref[0], 0))),
            out_shape=_sds((g, 4 * r, cdim), BF16), name=name, compiler_params=_cp(1))(chip, w)

    if kind == "lead" and w.ndim == 3:
        r, cs = w.shape[1:]
        src = lambda n_src: pl.BlockSpec((None, tr, cs), lambda i, chip_ref: (layer, jnp.minimum(i, n_src - 1), 0))
    else:
        r, cs = w.shape
        src = lambda n_src: pl.BlockSpec((tr, cs), lambda i, chip_ref: (jnp.minimum(i, n_src - 1), 0))
    if kind == "lead":
        rows = rows or r
        tr = _tile(r, ROW_TILE, 16) if rows == r else rows - r
        assert r % tr == 0 and tr % 16 == 0
        n_src = r // tr

        def body(chip_ref, w_ref, o_ref):
            del chip_ref
            o_ref[...] = jnp.where(pl.program_id(0) < n_src, w_ref[...], 0.0).astype(BF16)

        return pl.pallas_call(
            body,
            grid_spec=pltpu.PrefetchScalarGridSpec(
                num_scalar_prefetch=1, grid=(rows // tr,), in_specs=[src(n_src)],
                out_specs=pl.BlockSpec((None, tr, cs), lambda i, chip_ref: (chip_ref[0], i, 0))),
            out_shape=_sds((4, rows, cs), BF16), name=name, compiler_params=_cp(1))(chip, w)

    assert kind == "down"
    tr = r // 2 if (r // 2) % 16 == 0 else r
    per = r // tr

    def body(chip_ref, w_ref, base_ref, o_ref):
        del chip_ref, base_ref
        o_ref[...] = w_ref[...].astype(BF16)

    return pl.pallas_call(
        body,
        grid_spec=pltpu.PrefetchScalarGridSpec(
            num_scalar_prefetch=1, grid=(per,),
            in_specs=[pl.BlockSpec((tr, cs), lambda i, chip_ref: (i, 0)), pl.BlockSpec(memory_space=pl.ANY)],
            out_specs=pl.BlockSpec((None, tr, cs), lambda i, chip_ref: (chip_ref[0] // 2, (chip_ref[0] % 2) * per + i, 0))),
        out_shape=_sds(base.shape, BF16), input_output_aliases={2: 0},
        name=name, compiler_params=_cp(1))(chip, w, base)


def _gather_sides(items, bufs=None):
    n = len(items)
    kinds = [it[0] for it in items]
    shard_rows = [it[2] for it in items]
    bufs = [it[1] for it in items] if bufs is None else list(bufs)

    def half_of(outs, m, chip, half):
        k = 2 * chip[0] + chip[1]
        o, r = outs[m], shard_rows[m]
        if kinds[m] == "pool":
            gh = o.shape[0] // 2
            return o.at[pl.ds(half * gh, gh), pl.ds(k * r, r)]
        r2 = r // 2
        if kinds[m] == "down":
            return o.at[k // 2, pl.ds((k % 2) * r + half * r2, r2)]
        return o.at[k, pl.ds(half * r2, r2)]

    def remote(outs, sems, m, j, chip, half, to):
        ref = half_of(outs, m, chip, half)
        return pltpu.make_async_remote_copy(src_ref=ref, dst_ref=ref, send_sem=sems[0].at[m, j],
                                            recv_sem=sems[1].at[m, j], device_id=to, device_id_type=MESH)

    def ici_copies(outs, sems, sending):
        x, y, c, chips = _place()
        if sending:
            return [remote(outs, sems, m, j, (x, y), c, (*chip, c)) for m in range(n) for j, chip in enumerate(chips)]
        return [remote(outs, sems, m, j, chip, c, (x, y, c)) for m in range(n) for j, chip in enumerate(chips)]

    def d2d_copies(outs, sems, sending):
        x, y, c, chips = _place()
        if sending:
            return [remote(outs, sems, m, j, chip, c, (x, y, 1 - c)) for m in range(n) for j, chip in enumerate(chips)]
        return [remote(outs, sems, m, j, chip, 1 - c, (x, y, c)) for m in range(n) for j, chip in enumerate(chips)]

    def phase(copies):
        def start(ins, outs, sems):
            for cp in copies(outs, sems, True):
                cp.start()

        def finish(ins, outs, sems):
            for cp in copies(outs, sems, False):
                cp.wait_recv()
            for cp in copies(outs, sems, True):
                cp.wait_send()

        return start, finish

    ici, d2d = phase(ici_copies), phase(d2d_copies)

    def both_finish(ins, outs, sems):
        ici[1](ins, outs, sems[:2])
        d2d[0](ins, outs, sems[2:])
        d2d[1](ins, outs, sems[2:])

    pair = [pltpu.SemaphoreType.DMA((n, 3)), pltpu.SemaphoreType.DMA((n, 3))]
    shapes = [_sds(b.shape, b.dtype) for b in bufs]
    alias = {m: m for m in range(n)}

    def side(which):
        if which == "both":
            return _Side(bufs, shapes, alias, pair + pair, lambda i, o, s: ici[0](i, o, s[:2]), both_finish)
        start, finish = ici if which == "ici" else d2d
        return _Side(bufs, shapes, alias, pair, start, finish)

    return side


def _sibling_side(grads):
    n = len(grads)

    def copies(ins, outs, sems):
        x, y, c, _ = _place()
        res = []
        for m in range(n):
            r2 = ins[m].shape[1] // 2
            res.append(pltpu.make_async_remote_copy(
                src_ref=ins[m].at[:, pl.ds((1 - c) * r2, r2)], dst_ref=outs[m],
                send_sem=sems[0].at[m], recv_sem=sems[1].at[m], device_id=(x, y, 1 - c), device_id_type=MESH))
        return res

    def start(ins, outs, sems):
        for cp in copies(ins, outs, sems):
            cp.start()

    def finish(ins, outs, sems):
        for cp in copies(ins, outs, sems):
            cp.wait_recv()
        for cp in copies(ins, outs, sems):
            cp.wait_send()

    return _Side(list(grads), [_sds((4, g.shape[1] // 2, g.shape[2]), g.dtype) for g in grads], {},
                 [pltpu.SemaphoreType.DMA((n,)), pltpu.SemaphoreType.DMA((n,))], start, finish)


def _owner_chips_side(parts):
    n = len(parts)

    def copies(ins, outs, sems):
        _, _, c, chips = _place()
        return [pltpu.make_async_remote_copy(
            src_ref=ins[m].at[2 * chip[0] + chip[1]], dst_ref=outs[m].at[j], send_sem=sems[0].at[m, j],
            recv_sem=sems[1].at[m, j], device_id=(*chip, c), device_id_type=MESH)
            for m in range(n) for j, chip in enumerate(chips)]

    def start(ins, outs, sems):
        for cp in copies(ins, outs, sems):
            cp.start()

    def finish(ins, outs, sems):
        for cp in copies(ins, outs, sems):
            cp.wait_recv()
        for cp in copies(ins, outs, sems):
            cp.wait_send()

    return _Side(list(parts), [_sds((3,) + p.shape[1:], p.dtype) for p in parts], {},
                 [pltpu.SemaphoreType.DMA((n, 3)), pltpu.SemaphoreType.DMA((n, 3))], start, finish)


def _exchange_finished_halves(shards, name):
    n = len(shards)

    def body(*refs):
        out = refs[n:2 * n]
        send_sems, recv_sems = refs[2 * n:]
        x, y, c, _ = _place()
        copies = []
        for m in range(n):
            r2 = out[m].shape[0] // 2
            mine = out[m].at[pl.ds(c * r2, r2)]
            copies.append(pltpu.make_async_remote_copy(
                src_ref=mine, dst_ref=mine, send_sem=send_sems.at[m], recv_sem=recv_sems.at[m],
                device_id=(x, y, 1 - c), device_id_type=MESH))
        for cp in copies:
            cp.start()
        for m in range(n):
            r2 = out[m].shape[0] // 2
            theirs = out[m].at[pl.ds((1 - c) * r2, r2)]
            pltpu.make_async_remote_copy(
                src_ref=theirs, dst_ref=theirs, send_sem=send_sems.at[m], recv_sem=recv_sems.at[m],
                device_id=(x, y, 1 - c), device_id_type=MESH).wait_recv()
        for cp in copies:
            cp.wait_send()

    return pl.pallas_call(
        body, in_specs=_any_specs(n), out_specs=_any_specs(n), out_shape=[_sds(s.shape, s.dtype) for s in shards],
        input_output_aliases={m: m for m in range(n)},
        scratch_shapes=[pltpu.SemaphoreType.DMA((n,)), pltpu.SemaphoreType.DMA((n,))], name=name)(*shards)


def _all_reduce_small(v, name):
    rows = v.shape[0]

    def body(v_ref, out_ref, buf, send_sems, recv_sems, local_sem):
        x, y, c, chips = _place()
        me, sibling = (x, y, c), (x, y, 1 - c)

        def slot(px, py, pc):
            return buf.at[4 * px + 2 * py + pc]

        def copy(k, block, to, src=None):
            return pltpu.make_async_remote_copy(
                src_ref=slot(*block) if src is None else src, dst_ref=slot(*block),
                send_sem=send_sems.at[k], recv_sem=recv_sems.at[k], device_id=to, device_id_type=MESH)

        mine = pltpu.make_async_copy(v_ref, slot(*me), local_sem)
        mine.start()
        first = [copy(0, me, sibling, src=v_ref)]
        first += [copy(1 + j, me, (*chip, c), src=v_ref) for j, chip in enumerate(chips)]
        for cp in first:
            cp.start()
        passed = [copy(4 + j, (*chip, c), sibling) for j, chip in enumerate(chips)]
        for j, chip in enumerate(chips):
            copy(1 + j, (*chip, c), me).wait_recv()
            passed[j].start()
        copy(0, sibling, me).wait_recv()
        for j, chip in enumerate(chips):
            copy(4 + j, (*chip, 1 - c), me).wait_recv()
        for cp in first + passed:
            cp.wait_send()
        mine.wait()
        total = buf[0]
        for dev in range(1, 8):
            total = total + buf[dev]
        out_ref[...] = total

    vm = pl.BlockSpec(memory_space=pltpu.VMEM)
    return pl.pallas_call(
        body, in_specs=[vm], out_specs=vm, out_shape=_sds(v.shape, F32),
        scratch_shapes=[pltpu.VMEM((8, rows, LANES), F32), pltpu.SemaphoreType.DMA((7,)),
                        pltpu.SemaphoreType.DMA((7,)), pltpu.SemaphoreType.DMA],
        name=name, compiler_params=pltpu.CompilerParams(vmem_limit_bytes=VMEM_LIMIT))(v)


def _chip_partial(grad, from_sibling, core, name):
    _, r, cdim = grad.shape
    r2 = r // 2
    tr = _tile(r2, SUM_ROW_TILE)
    per = r2 // tr

    def body(core_ref, g_ref, s_ref, o_ref, ob_ref):
        del core_ref
        total = g_ref[...] + s_ref[...]
        o_ref[...] = total
        ob_ref[...] = total.astype(BF16)

    blk = pl.BlockSpec((None, tr, cdim), lambda k, i, core_ref: (k, i, 0))
    mine = pl.BlockSpec((None, tr, cdim), lambda k, i, core_ref: (k, core_ref[0] * per + i, 0))
    return pl.pallas_call(
        body,
        grid_spec=pltpu.PrefetchScalarGridSpec(num_scalar_prefetch=1, grid=(4, per), in_specs=[mine, blk],
                                               out_specs=[blk, blk]),
        out_shape=[_sds((4, r2, cdim), F32), _sds((4, r2, cdim), BF16)],
        name=name, compiler_params=_cp(2))(core, grad, from_sibling)


def _owner_sum(partial, from_chips, place, name):
    _, r2, cdim = partial.shape
    tr = _tile(r2, SUM_ROW_TILE)
    per = r2 // tr

    def body(place_ref, p_ref, f_ref, o_ref):
        del place_ref
        total = p_ref[...]
        for j in range(3):
            total = total + f_ref[j].astype(F32)
        o_ref[...] = total

    return pl.pallas_call(
        body,
        grid_spec=pltpu.PrefetchScalarGridSpec(
            num_scalar_prefetch=1, grid=(per,),
            in_specs=[pl.BlockSpec((None, tr, cdim), lambda i, place_ref: (place_ref[0], i, 0)),
                      pl.BlockSpec((3, tr, cdim), lambda i, place_ref: (0, i, 0))],
            out_specs=pl.BlockSpec((tr, cdim), lambda i, place_ref: (place_ref[1] * per + i, 0))),
        out_shape=_sds((2 * r2, cdim), F32), name=name, compiler_params=_cp(1))(place, partial, from_chips)


def _adamw(g, w, m, v, layer, prev, name):
    _, r, cdim = w.shape
    tr = _tile(r, OPT_ROW_TILE)
    c1 = 1.0 / (1.0 - ADAM_B1 ** ADAM_STEP)
    c2 = 1.0 / (1.0 - ADAM_B2 ** ADAM_STEP)
    n_prev = 0 if prev is None else 4

    def body(g_ref, w_ref, m_ref, v_ref, *rest):
        go_ref, d_ref, mo_ref, vo_ref = rest[n_prev:]
        grad = g_ref[:, pl.ds(0, cdim)]
        m_new = ADAM_B1 * m_ref[...] + (1.0 - ADAM_B1) * grad
        v_new = ADAM_B2 * v_ref[...] + (1.0 - ADAM_B2) * (grad * grad)
        go_ref[...] = grad
        mo_ref[...] = m_new
        vo_ref[...] = v_new
        d_ref[...] = -ADAM_LR * ((m_new * c1) / (jnp.sqrt(v_new * c2) + ADAM_EPS) + ADAM_WD * w_ref[...])

    blk = pl.BlockSpec((None, tr, cdim), lambda i: (layer, i, 0))
    gblk = pl.BlockSpec((tr, g.shape[1]), lambda i: (i, 0))
    return pl.pallas_call(
        body, grid=(r // tr,), in_specs=[gblk, blk, blk, blk] + _any_specs(n_prev), out_specs=[blk] * 4,
        out_shape=[_sds(w.shape, F32)] * 4, input_output_aliases={4 + k: k for k in range(n_prev)},
        name=name, compiler_params=_cp(1))(g, w, m, v, *(prev or ()))


def _pack_rows(vectors):
    flat = [v.reshape(-1) for v in vectors]
    sizes = [f.shape[0] for f in flat]
    total = sum(sizes)
    padded = _round_up(total, 8 * LANES)
    buf = jnp.concatenate(flat + [jnp.zeros((padded - total,), F32)])
    return buf.reshape(padded // LANES, LANES), sizes


def _unpack_rows(buf, sizes, shapes):
    flat = buf.reshape(-1)
    out, off = [], 0
    for n, shp in zip(sizes, shapes):
        out.append(flat[off:off + n].reshape(shp))
        off += n
    return out


def kernel(x, pool_w, pool_scale, attn_w_qkv, attn_w_o, ffn_w_up, ffn_conv_w, ffn_conv_b, ffn_w_down, ln_mix_g, ln_mix_b, ln_ffn_g, ln_ffn_b, loss_target, m_pool_w, m_pool_scale, m_attn_w_qkv, m_attn_w_o, m_ffn_w_up, m_ffn_conv_w, m_ffn_conv_b, m_ffn_w_down, m_ln_mix_g, m_ln_mix_b, m_ln_ffn_g, m_ln_ffn_b, v_pool_w, v_pool_scale, v_attn_w_qkv, v_attn_w_o, v_ffn_w_up, v_ffn_conv_w, v_ffn_conv_b, v_ffn_w_down, v_ln_mix_g, v_ln_mix_b, v_ln_ffn_g, v_ln_ffn_b):
    t, d = x.shape[1], x.shape[2]
    n_heads = d // HEAD_DIM
    n_groups = pool_w.shape[1]
    fs = ffn_w_up.shape[2]
    fp = _round_up(fs, LANES)
    rd = ffn_w_down.shape[1]
    assert 2 * rd == fs
    xi, yi, ci = lax.axis_index("x"), lax.axis_index("y"), lax.axis_index("c")
    chip = (2 * xi + yi).astype(jnp.int32)
    chip_arr, core_arr = chip.reshape(1), ci.astype(jnp.int32).reshape(1)
    place_arr = jnp.concatenate([chip_arr, core_arr])

    x2 = x.reshape(t, d)
    target = loss_target.reshape(t, d)
    pad_cols = lambda a: jnp.pad(a, [(0, 0)] * (a.ndim - 1) + [(0, fp - fs)])
    up_t = [jnp.transpose(a, (0, 2, 1)) for a in (ffn_w_up, m_ffn_w_up, v_ffn_w_up)]

    gather_items = []
    for i in range(DEPTH):
        j = i // 2
        items = []
        if i % 2 == 0:
            items.append(("pool", _place_shard("pool", pool_w[j], chip_arr, name="place_pool"), pool_w.shape[2]))
        else:
            items.append(("lead", _place_shard("lead", attn_w_qkv[j], chip_arr, name="place_qkv"), d))
            items.append(("lead", _place_shard("lead", attn_w_o[j], chip_arr, name="place_wo"), attn_w_o.shape[1]))
        items.append(("lead", _place_shard("lead", up_t[0], chip_arr, name="place_up", rows=fp, layer=i), fp))
        items.append(("down", _place_shard("down", ffn_w_down[i], chip_arr, name="place_down",
                                           base=jnp.zeros((2, fp, d), BF16)), rd))
        gather_items.append(items)
    weights = [None] * DEPTH
    weights[0] = _run_sides([_gather_sides(gather_items[0])("both")], name="gather_layer0")

    conv_b_all = pad_cols(ffn_conv_b.reshape(DEPTH, 4, 1, fs))
    cw_local = pad_cols(ffn_conv_w)
    slot = (jnp.arange(4, dtype=jnp.int32) == chip).astype(F32) * (1.0 - ci.astype(F32))
    cw_placed = slot[None, :, None, None] * cw_local[:, None]
    cw_buf, cw_sizes = _pack_rows([cw_placed])
    conv_w_all = _unpack_rows(_all_reduce_small(cw_buf, name="gather_conv_w"), cw_sizes, [cw_placed.shape])[0]

    gam = lambda a, i: a[i].reshape(1, d)

    saved = []
    cur, cur_b = x2, x2.astype(BF16)
    for i in range(DEPTH):
        j = i // 2
        w = weights[i]
        s = {"x_in": cur, "x_in_b": cur_b}
        if i % 2 == 0:
            w_pool, w_up, w_down = w
            s["scale"] = pool_scale[j].reshape(1, d)
            r1, x1, x1b = _pool_fwd(cur, w_pool, s["scale"], gam(ln_mix_g, i), gam(ln_mix_b, i), name="pool_fwd")
        else:
            w_qkv, w_o, w_up, w_down = w
            w_o3 = w_o.reshape(1, d, d)
            qkv = _mm_cols(cur_b, w_qkv, BF16, name="qkv_proj")
            o = _attn_fwd(qkv, n_heads, name="attn_fwd")
            s["qkv"], s["o"], s["w_o3"] = qkv, o, w_o3
            r1, x1, x1b = _mm_res_ln(o.reshape(1, t, d), w_o3, cur, gam(ln_mix_g, i), gam(ln_mix_b, i),
                                     name="attn_out_ln")
        if i + 1 < DEPTH:
            nxt = gather_items[i + 1]
            n_mix = len(nxt) - 2
            h, landed_ffn = _mm_cols(x1b, w_up, F32, name="ffn_up", transposed_b=True,
                                     side=_gather_sides(nxt[n_mix:])("ici"))
            a, landed_mix = _ffn_act(h, conv_w_all[i], conv_b_all[i], name="ffn_act",
                                     side=_gather_sides(nxt[:n_mix])("ici"))
            landed = list(landed_mix) + list(landed_ffn)
            (r2, x2n, x2b), gathered = _mm_res_ln(a, w_down, x1, gam(ln_ffn_g, i), gam(ln_ffn_b, i),
                                                  name="ffn_down_ln", side=_gather_sides(nxt, landed)("d2d"))
            weights[i + 1] = list(gathered)
        else:
            h = _mm_cols(x1b, w_up, F32, name="ffn_up", transposed_b=True)
            a = _ffn_act(h, conv_w_all[i], conv_b_all[i], name="ffn_act")
            r2, x2n, x2b = _mm_res_ln(a, w_down, x1, gam(ln_ffn_g, i), gam(ln_ffn_b, i), name="ffn_down_ln")
        s.update(r1=r1, x1b=x1b, h=h, a=a, r2=r2)
        saved.append(s)
        cur, cur_b = x2n, x2b

    loss_row, dcur = _loss_and_grad(cur, target, name="loss")
    loss = lax.psum(loss_row[0, 0], ("x", "y", "c"))

    big_grads = [None] * DEPTH
    reduced = [None] * DEPTH
    small = {}

    def finish_reduce(parts, from_chips, layer):
        halves = [_owner_sum(p[0], fc, place_arr, name="reduce_owner_sum") for p, fc in zip(parts, from_chips)]
        return _exchange_finished_halves(halves, name="reduce_halves_pool" if layer % 2 == 0 else "reduce_halves_attn")

    for i in reversed(range(DEPTH)):
        j = i // 2
        s, w = saved[i], weights[i]
        w_up, w_down = w[-2], w[-1]
        dr2, dr2b, small["ln_ffn_g", i], small["ln_ffn_b", i] = _ln_bwd(dcur, s["r2"], gam(ln_ffn_g, i), name="ln_bwd")
        pending = big_grads[i + 1] if i + 1 < DEPTH else None
        if pending is not None:
            da, from_sib = _mm_cols(dr2b, w_down, F32, name="ffn_down_bwd_act", transposed_b=True,
                                    side=_sibling_side(pending))
            parts = [_chip_partial(g, fs_, core_arr, name="reduce_chip_partial") for g, fs_ in zip(pending, from_sib)]
        else:
            da = _mm_cols(dr2b, w_down, F32, name="ffn_down_bwd_act", transposed_b=True)
        dr2b3 = dr2b.reshape(1, t, d)
        nmb = 2
        d_down = _mm_tn(s["a"], dr2b3, (2, fp, d), fp // nmb, d, 2,
                        (lambda u: u, nmb, lambda u, mb: mb), (lambda u: 0, lambda u: 0),
                        (lambda u: u, lambda u, mb: mb, lambda u: 0), name="ffn_down_bwd_w")
        dhc, dcw, dcb = _ffn_act_bwd(s["h"], da, conv_w_all[i], conv_b_all[i], name="ffn_act_bwd")
        small["conv_w", i], small["conv_b", i] = dcw, dcb
        dh = _conv_adjoint(dhc.reshape(4, t, fp), conv_w_all[i], name="ffn_conv_adjoint")
        up_w_args = (dh, s["x1b"].reshape(1, t, d), (4, fp, d), fp // 2, d, 4,
                     (lambda u: u, 2, lambda u, mb: mb), (lambda u: 0, lambda u: 0),
                     (lambda u: u, lambda u, mb: mb, lambda u: 0))
        if pending is not None:
            n_mix = len(parts) - 2
            dx1, chips_ffn = _mm_nt_acc(dh, w_up, dr2, fp, name="ffn_up_bwd_act", b_is_kn=True,
                                        side=_owner_chips_side([p[1] for p in parts[n_mix:]]))
            d_up, chips_mix = _mm_tn(*up_w_args, name="ffn_up_bwd_w",
                                     side=_owner_chips_side([p[1] for p in parts[:n_mix]]))
            reduced[i + 1] = finish_reduce(parts, list(chips_mix) + list(chips_ffn), i + 1)
        else:
            dx1 = _mm_nt_acc(dh, w_up, dr2, fp, name="ffn_up_bwd_act", b_is_kn=True)
            d_up = _mm_tn(*up_w_args, name="ffn_up_bwd_w")
        dr1, dr1b, small["ln_mix_g", i], small["ln_mix_b", i] = _ln_bwd(dx1, s["r1"], gam(ln_mix_g, i), name="ln_bwd")
        d_down4 = d_down[:, :fs].reshape(4, rd, d)
        if i % 2 == 0:
            dp, d_pool, small["pool_scale", j] = _pool_bwd(s["x_in"], dr1, w[0], s["scale"], name="pool_bwd")
            dcur = _pool_adjoint(dp, dr1, n_groups, name="pool_adjoint")
            cg = d // n_groups
            d_pool4 = d_pool.reshape(n_groups, 4, cg // 4, cg).transpose(1, 0, 2, 3).reshape(4, n_groups * (cg // 4), cg)
            big_grads[i] = [d_pool4, d_up, d_down4]
        else:
            w_qkv = w[0]
            do = _mm_cols(dr1b, s["w_o3"], BF16, name="attn_out_bwd_act", transposed_b=True)
            d_wo = _mm_tn(s["o"].reshape(1, t, d), dr1b.reshape(1, t, d), (1, d, d), d // 2, d, 1,
                          (lambda u: 0, 2, lambda u, mb: mb), (lambda u: 0, lambda u: 0),
                          (lambda u: 0, lambda u, mb: mb, lambda u: 0), name="attn_out_bwd_w")
            dq, dk, dv = _attn_bwd(s["qkv"], do, n_heads, name="attn_bwd")
            cq = w_qkv.shape[2]
            dqkv = jnp.concatenate([dq, dk, dv], axis=1).reshape(t, 4, cq).transpose(1, 0, 2)
            dcur = _mm_nt_acc(dqkv, w_qkv, dr1, cq, name="qkv_bwd_act")
            d_qkv = _mm_tn(s["x_in_b"].reshape(1, t, d), dqkv, (4, d, cq), d // 2, cq, 4,
                           (lambda u: 0, 2, lambda u, mb: mb), (lambda u: u, lambda u: 0),
                           (lambda u: u, lambda u, mb: mb, lambda u: 0), name="qkv_bwd_w")
            big_grads[i] = [d_qkv, d_wo.reshape(4, d // 4, d), d_up, d_down4]
    grad_x = dcur.reshape(1, t, d)

    from_sib = _run_sides([_sibling_side(big_grads[0])], name="reduce_layer0_sibling")
    parts = [_chip_partial(g, fs_, core_arr, name="reduce_chip_partial") for g, fs_ in zip(big_grads[0], from_sib)]
    from_chips = _run_sides([_owner_chips_side([p[1] for p in parts])], name="reduce_layer0_chips")
    reduced[0] = finish_reduce(parts, from_chips, 0)

    names = [("pool_scale", j) for j in range(2)]
    for nm in ("ln_mix_g", "ln_mix_b", "ln_ffn_g", "ln_ffn_b", "conv_b", "conv_w"):
        names += [(nm, i) for i in range(DEPTH)]
    vecs = [small[k] for k in names]
    sbuf, ssizes = _pack_rows(vecs)
    summed = dict(zip(names, _unpack_rows(_all_reduce_small(sbuf, name="reduce_small"), ssizes, [v.shape for v in vecs])))

    def stack_layers(nm, count):
        return jnp.stack([summed[nm, i] for i in range(count)])

    g_small = {
        "pool_scale": stack_layers("pool_scale", 2).reshape(2, d),
        "ln_mix_g": stack_layers("ln_mix_g", DEPTH).reshape(DEPTH, d),
        "ln_mix_b": stack_layers("ln_mix_b", DEPTH).reshape(DEPTH, d),
        "ln_ffn_g": stack_layers("ln_ffn_g", DEPTH).reshape(DEPTH, d),
        "ln_ffn_b": stack_layers("ln_ffn_b", DEPTH).reshape(DEPTH, d),
        "conv_b": stack_layers("conv_b", DEPTH).reshape(DEPTH, 4, fp)[:, :, :fs].reshape(DEPTH, 4 * fs),
        "conv_w": lax.dynamic_index_in_dim(stack_layers("conv_w", DEPTH).reshape(DEPTH, 4, 3, fp), chip, axis=1,
                                           keepdims=False)[:, :, :fs],
    }
    w_small = {"pool_scale": (pool_scale, m_pool_scale, v_pool_scale), "ln_mix_g": (ln_mix_g, m_ln_mix_g, v_ln_mix_g),
               "ln_mix_b": (ln_mix_b, m_ln_mix_b, v_ln_mix_b), "ln_ffn_g": (ln_ffn_g, m_ln_ffn_g, v_ln_ffn_g),
               "ln_ffn_b": (ln_ffn_b, m_ln_ffn_b, v_ln_ffn_b), "conv_b": (ffn_conv_b, m_ffn_conv_b, v_ffn_conv_b),
               "conv_w": (ffn_conv_w, m_ffn_conv_w, v_ffn_conv_w)}
    order = list(g_small)
    packs = [_pack_rows([g_small[k] for k in order])[0]]
    for idx in range(3):
        packs.append(_pack_rows([w_small[k][idx] for k in order])[0])
    small_sizes = _pack_rows([g_small[k] for k in order])[1]
    small_out = _adamw(packs[0], packs[1][None], packs[2][None], packs[3][None], 0, None, name="adamw_small")
    shapes = [g_small[k].shape for k in order]
    small_res = {k: [] for k in order}
    for arr in small_out:
        for k, val in zip(order, _unpack_rows(arr[0], small_sizes, shapes)):
            small_res[k].append(val)

    def opt_layers(per_layer_grads, w_all, m_all, v_all, name, rows=None):
        n_layers = w_all.shape[0]
        flat = [a.reshape(n_layers, rows or a.shape[1], -1) for a in (w_all, m_all, v_all)]
        res = None
        for li, g in enumerate(per_layer_grads):
            res = _adamw(g, *flat, li, res, name=name)
        return [o.reshape(w_all.shape) for o in res]

    cg = d // n_groups
    big = {
        "pool_w": opt_layers([reduced[i][0] for i in (0, 2)], pool_w, m_pool_w, v_pool_w, "adamw_pool",
                             rows=n_groups * (cg // 4)),
        "attn_w_qkv": opt_layers([reduced[i][0] for i in (1, 3)], attn_w_qkv, m_attn_w_qkv, v_attn_w_qkv, "adamw_qkv"),
        "attn_w_o": opt_layers([reduced[i][1] for i in (1, 3)], attn_w_o, m_attn_w_o, v_attn_w_o, "adamw_wo"),
        "ffn_w_up": [jnp.transpose(o, (0, 2, 1))
                     for o in opt_layers([reduced[i][-2] for i in range(DEPTH)], *up_t, "adamw_up")],
        "ffn_w_down": opt_layers([reduced[i][-1] for i in range(DEPTH)], ffn_w_down, m_ffn_w_down, v_ffn_w_down,
                                 "adamw_down"),
    }

    def leaf(k, name):
        if name in big:
            return big[name][k]
        key = {"ffn_conv_w": "conv_w", "ffn_conv_b": "conv_b"}.get(name, name)
        return small_res[key][k]

    weight_names = ["pool_w", "pool_scale", "attn_w_qkv", "attn_w_o", "ffn_w_up", "ffn_conv_w", "ffn_conv_b",
                    "ffn_w_down", "ln_mix_g", "ln_mix_b", "ln_ffn_g", "ln_ffn_b"]
    outs = [loss, grad_x]
    for k in range(4):
        outs += [leaf(k, nm) for nm in weight_names]
    return tuple(outs)
```

```python
import collections

import jax
import jax.numpy as jnp
from jax import lax
from jax.experimental import pallas as pl
from jax.experimental.pallas import tpu as pltpu

F32, BF16 = jnp.float32, jnp.bfloat16
MESH = pl.DeviceIdType.MESH

LANES = 128
HEAD_DIM = 128
ATT_BLOCK = 128
ATT_WINDOW = 3 * ATT_BLOCK
ATT_FWD_HEADS = 4
ATT_BWD_HEADS = 2
POOL_WINDOWS = (2, 4, 8, 16)
POOL_HALO = 16
CONV_HALO = 8
LN_EPS = 1e-5
DEPTH = 4
ALPHA = (2.0 * DEPTH) ** 0.25
ATT_SCALE = HEAD_DIM ** -0.5
EXP_ZERO = 115.0
MASKED = 1e30
ADAM_LR, ADAM_B1, ADAM_B2, ADAM_EPS, ADAM_WD, ADAM_STEP = 0.001, 0.9, 0.999, 1e-08, 0.01, 10

VMEM_LIMIT = 56 << 20
ROW_TILE = 512
LN_ROW_TILE = 256
OPT_ROW_TILE = 128
SUM_ROW_TILE = 512


def _cp(n_axes):
    return pltpu.CompilerParams(dimension_semantics=("arbitrary",) * n_axes, vmem_limit_bytes=VMEM_LIMIT)


def _sds(shape, dtype):
    return jax.ShapeDtypeStruct(tuple(shape), dtype)


def _round_up(n, m):
    return (n + m - 1) // m * m


def _tile(n, cap, mult=8):
    if n <= cap:
        return n
    best = None
    for d in range(mult, cap + 1, mult):
        if n % d == 0:
            best = d
    assert best is not None, (n, cap)
    return best


_NT = (((1,), (1,)), ((), ()))
_TN = (((0,), (0,)), ((), ()))

_Side = collections.namedtuple("_Side", "ins outs alias sems start finish")


def _any_specs(n):
    return [pl.BlockSpec(memory_space=pl.ANY)] * n


def _call(body, first, last, side, *, grid, in_specs, out_specs, out_shape, scratch_shapes, name, args):
    n_axes = len(grid)
    if side is None:
        res = pl.pallas_call(body, grid=grid, in_specs=in_specs, out_specs=out_specs, out_shape=out_shape,
                             scratch_shapes=scratch_shapes, name=name, compiler_params=_cp(n_axes))(*args)
        return res, ()
    n_in, n_out, n_scr = len(in_specs), len(out_shape), len(scratch_shapes)
    s_in, s_out = len(side.ins), len(side.outs)

    def carried(*refs):
        ins, refs = refs[:n_in], refs[n_in:]
        side_ins, refs = refs[:s_in], refs[s_in:]
        outs, refs = refs[:n_out], refs[n_out:]
        side_outs, refs = refs[:s_out], refs[s_out:]
        scratch, side_sems = refs[:n_scr], refs[n_scr:]

        @pl.when(first())
        def _():
            side.start(side_ins, side_outs, side_sems)

        body(*ins, *outs, *scratch)

        @pl.when(last())
        def _():
            side.finish(side_ins, side_outs, side_sems)

    res = pl.pallas_call(
        carried, grid=grid, in_specs=list(in_specs) + _any_specs(s_in), out_specs=list(out_specs) + _any_specs(s_out),
        out_shape=list(out_shape) + list(side.outs), scratch_shapes=list(scratch_shapes) + list(side.sems),
        input_output_aliases={n_in + a: n_out + b for a, b in side.alias.items()},
        name=name, compiler_params=_cp(n_axes))(*args, *side.ins)
    return res[:n_out], res[n_out:]


def _mm_cols(a, b, out_dtype, name, transposed_b=False, side=None):
    t, k = a.shape
    g = b.shape[0]
    nb = b.shape[1] if transposed_b else b.shape[2]
    tm = _tile(t, ROW_TILE)
    steps = t // tm

    def body(a_ref, b_ref, o_ref):
        if transposed_b:
            acc = lax.dot_general(a_ref[...], b_ref[...], _NT, preferred_element_type=F32)
        else:
            acc = jnp.dot(a_ref[...], b_ref[...], preferred_element_type=F32)
        o_ref[...] = acc.astype(o_ref.dtype)

    first = lambda: jnp.logical_and(pl.program_id(0) == 0, pl.program_id(1) == 0)
    last = lambda: jnp.logical_and(pl.program_id(0) == g - 1, pl.program_id(1) == steps - 1)
    (out,), side_out = _call(
        body, first, last, side, grid=(g, steps),
        in_specs=[pl.BlockSpec((tm, k), lambda gi, i: (i, 0)),
                  pl.BlockSpec((None,) + b.shape[1:], lambda gi, i: (gi, 0, 0))],
        out_specs=[pl.BlockSpec((None, tm, nb), lambda gi, i: (gi, i, 0))],
        out_shape=[_sds((g, t, nb), out_dtype)], scratch_shapes=[], name=name, args=(a, b))
    return out if side is None else (out, side_out)


def _mm_nt_acc(a3, b3, res, kb, name, side=None, b_is_kn=False):
    ga, t, ka = a3.shape
    gb, n, kbb = (b3.shape[0], b3.shape[2], b3.shape[1]) if b_is_kn else b3.shape
    na, nbk = ka // kb, kbb // kb
    groups = ga * na
    assert groups == gb * nbk
    tm = _tile(t, ROW_TILE)
    steps = t // tm

    def body(a_ref, b_ref, res_ref, o_ref, acc):
        u = pl.program_id(1)

        @pl.when(u == 0)
        def _():
            acc[...] = ALPHA * res_ref[...]

        if b_is_kn:
            acc[...] += jnp.dot(a_ref[...], b_ref[...], preferred_element_type=F32)
        else:
            acc[...] += lax.dot_general(a_ref[...], b_ref[...], _NT, preferred_element_type=F32)

        @pl.when(u == groups - 1)
        def _():
            o_ref[...] = acc[...]

    first = lambda: jnp.logical_and(pl.program_id(0) == 0, pl.program_id(1) == 0)
    last = lambda: jnp.logical_and(pl.program_id(0) == steps - 1, pl.program_id(1) == groups - 1)
    if b_is_kn:
        b_spec = pl.BlockSpec((None, kb, n), lambda i, u: (u // nbk, u % nbk, 0))
    else:
        b_spec = pl.BlockSpec((None, n, kb), lambda i, u: (u // nbk, 0, u % nbk))
    (out,), side_out = _call(
        body, first, last, side, grid=(steps, groups),
        in_specs=[pl.BlockSpec((None, tm, kb), lambda i, u: (u // na, i, u % na)), b_spec,
                  pl.BlockSpec((tm, n), lambda i, u: (i, 0))],
        out_specs=[pl.BlockSpec((tm, n), lambda i, u: (i, 0))],
        out_shape=[_sds((t, n), F32)], scratch_shapes=[pltpu.VMEM((tm, n), F32)], name=name, args=(a3, b3, res))
    return out if side is None else (out, side_out)


def _mm_tn(x3, dy3, out_shape, bm, bn, groups, x_idx, dy_idx, out_idx, name, side=None):
    t = x3.shape[1]
    tm = _tile(t, 2 * ROW_TILE)
    grid = (groups, x_idx[1], t // tm)

    def body(x_ref, dy_ref, o_ref):
        @pl.when(pl.program_id(2) == 0)
        def _():
            o_ref[...] = jnp.zeros_like(o_ref)

        o_ref[...] += lax.dot_general(x_ref[...], dy_ref[...], _TN, preferred_element_type=F32)

    def at(corner):
        hit = pl.program_id(0) == corner[0]
        for axis in (1, 2):
            hit = jnp.logical_and(hit, pl.program_id(axis) == corner[axis])
        return hit

    (out,), side_out = _call(
        body, lambda: at((0, 0, 0)), lambda: at(tuple(g - 1 for g in grid)), side, grid=grid,
        in_specs=[pl.BlockSpec((None, tm, bm), lambda u, mb, i: (x_idx[0](u), i, x_idx[2](u, mb))),
                  pl.BlockSpec((None, tm, bn), lambda u, mb, i: (dy_idx[0](u), i, dy_idx[1](u)))],
        out_specs=[pl.BlockSpec((None, bm, bn), lambda u, mb, i: (out_idx[0](u), out_idx[1](u, mb), out_idx[2](u)))],
        out_shape=[_sds(out_shape, F32)], scratch_shapes=[], name=name, args=(x3, dy3))
    return out if side is None else (out, side_out)


def _layer_norm_rows(r, gamma, beta):
    mu = jnp.mean(r, axis=-1, keepdims=True)
    xc = r - mu
    var = jnp.mean(xc * xc, axis=-1, keepdims=True)
    return xc * lax.rsqrt(var + LN_EPS) * gamma + beta


def _mm_res_ln(a3, w3, res, gamma, beta, name, side=None):
    g, t, kb = a3.shape
    d = w3.shape[2]
    tm = _tile(t, LN_ROW_TILE)
    steps = t // tm

    def body(a_ref, w_hbm, res_ref, g_ref, b_ref, r_ref, o_ref, ob_ref, w_vmem, sem):
        @pl.when(pl.program_id(0) == 0)
        def _():
            cp = pltpu.make_async_copy(w_hbm, w_vmem, sem)
            cp.start()
            cp.wait()

        acc = ALPHA * res_ref[...]
        for gi in range(g):
            acc = acc + jnp.dot(a_ref[gi], w_vmem[gi], preferred_element_type=F32)
        r_ref[...] = acc
        out = _layer_norm_rows(acc, g_ref[...], b_ref[...])
        o_ref[...] = out
        ob_ref[...] = out.astype(BF16)

    row = pl.BlockSpec((tm, d), lambda i: (i, 0))
    vec = pl.BlockSpec((1, d), lambda i: (0, 0))
    outs, side_out = _call(
        body, lambda: pl.program_id(0) == 0, lambda: pl.program_id(0) == steps - 1, side, grid=(steps,),
        in_specs=[pl.BlockSpec((g, tm, kb), lambda i: (0, i, 0)), pl.BlockSpec(memory_space=pl.ANY), row, vec, vec],
        out_specs=[row, row, row],
        out_shape=[_sds((t, d), F32), _sds((t, d), F32), _sds((t, d), BF16)],
        scratch_shapes=[pltpu.VMEM(w3.shape, w3.dtype), pltpu.SemaphoreType.DMA],
        name=name, args=(a3, w3, res, gamma, beta))
    return outs if side is None else (outs, side_out)


def _ln_bwd(dout, r, gamma, name, side=None):
    t, d = r.shape
    tm = _tile(t, ROW_TILE)

    def body(do_ref, r_ref, g_ref, dr_ref, drb_ref, dg_ref, db_ref):
        @pl.when(pl.program_id(0) == 0)
        def _():
            dg_ref[...] = jnp.zeros_like(dg_ref)
            db_ref[...] = jnp.zeros_like(db_ref)

        rr = r_ref[...]
        do = do_ref[...]
        mu = jnp.mean(rr, axis=-1, keepdims=True)
        xc = rr - mu
        rstd = lax.rsqrt(jnp.mean(xc * xc, axis=-1, keepdims=True) + LN_EPS)
        xhat = xc * rstd
        dxh = do * g_ref[...]
        m1 = jnp.mean(dxh, axis=-1, keepdims=True)
        m2 = jnp.mean(dxh * xhat, axis=-1, keepdims=True)
        dr = rstd * (dxh - m1 - xhat * m2)
        dr_ref[...] = dr
        drb_ref[...] = dr.astype(BF16)
        dg_ref[...] += jnp.sum(do * xhat, axis=0, keepdims=True)
        db_ref[...] += jnp.sum(do, axis=0, keepdims=True)

    row = pl.BlockSpec((tm, d), lambda i: (i, 0))
    vec = pl.BlockSpec((1, d), lambda i: (0, 0))
    steps = t // tm
    outs, side_out = _call(
        body, lambda: pl.program_id(0) == 0, lambda: pl.program_id(0) == steps - 1, side, grid=(steps,),
        in_specs=[row, row, vec], out_specs=[row, row, vec, vec],
        out_shape=[_sds((t, d), F32), _sds((t, d), BF16), _sds((1, d), F32), _sds((1, d), F32)],
        scratch_shapes=[], name=name, args=(dout, r, gamma))
    return outs if side is None else (outs, side_out)


def _loss_and_grad(y, target, name):
    t, d = y.shape
    tm = _tile(t, ROW_TILE)
    steps = t // tm

    def body(y_ref, t_ref, loss_ref, dy_ref, acc):
        i = pl.program_id(0)

        @pl.when(i == 0)
        def _():
            acc[...] = jnp.zeros_like(acc)

        diff = y_ref[...] - t_ref[...]
        dy_ref[...] = diff * (1.0 / d)
        acc[...] += jnp.sum(diff * diff, axis=0, keepdims=True)

        @pl.when(i == steps - 1)
        def _():
            total = jnp.sum(acc[...], axis=1, keepdims=True) * (0.5 / d)
            loss_ref[...] = jnp.broadcast_to(total, loss_ref.shape)

    row = pl.BlockSpec((tm, d), lambda i: (i, 0))
    return pl.pallas_call(
        body, grid=(steps,), in_specs=[row, row],
        out_specs=[pl.BlockSpec((1, LANES), lambda i: (0, 0)), row],
        out_shape=[_sds((1, LANES), F32), _sds((t, d), F32)],
        scratch_shapes=[pltpu.VMEM((1, d), F32)], name=name, compiler_params=_cp(1))(y, target)


def _window_sums(ext, window, forward):
    n = ext.shape[0]
    s, span = ext, 1
    while span < window:
        s = s + pltpu.roll(s, (n - span) if forward else span, 0)
        span *= 2
    return s


def _pooled_group(main, halo, gi, row0):
    window = POOL_WINDOWS[gi]
    ext = jnp.concatenate([halo, main], axis=0)
    sums = _window_sums(ext, window, forward=False)[POOL_HALO:, :]
    pos = row0 + lax.broadcasted_iota(jnp.int32, (main.shape[0], 1), 0)
    cnt = jnp.minimum(pos + 1, window).astype(F32)
    return sums / cnt - main


def _pool_specs(t, d, tm):
    per = tm // POOL_HALO
    main = pl.BlockSpec((tm, d), lambda i: (i, 0))
    before = pl.BlockSpec((POOL_HALO, d), lambda i: (jnp.maximum(i * per - 1, 0), 0))
    return main, before


def _pool_fwd(x, w, scale, gamma, beta, name):
    t, d = x.shape
    ng, cg = w.shape[0], w.shape[1]
    tm = _tile(t, LN_ROW_TILE)

    def body(x_ref, h_ref, w_ref, s_ref, g_ref, b_ref, r_ref, o_ref, ob_ref):
        i = pl.program_id(0)
        for gi in range(ng):
            cols = pl.ds(gi * cg, cg)
            main = x_ref[:, cols]
            halo = jnp.where(i > 0, h_ref[:, cols], 0.0)
            pooled = _pooled_group(main, halo, gi, i * tm)
            y = jnp.dot(pooled.astype(BF16), w_ref[gi], preferred_element_type=F32)
            r_ref[:, cols] = ALPHA * main + y * s_ref[:, cols]
        out = _layer_norm_rows(r_ref[...], g_ref[...], b_ref[...])
        o_ref[...] = out
        ob_ref[...] = out.astype(BF16)

    main, before = _pool_specs(t, d, tm)
    vec = pl.BlockSpec((1, d), lambda i: (0, 0))
    return pl.pallas_call(
        body, grid=(t // tm,),
        in_specs=[main, before, pl.BlockSpec(w.shape, lambda i: (0, 0, 0)), vec, vec, vec],
        out_specs=[main, main, main],
        out_shape=[_sds((t, d), F32), _sds((t, d), F32), _sds((t, d), BF16)],
        name=name, compiler_params=_cp(1))(x, x, w, scale, gamma, beta)


def _pool_bwd(x, dy, w, scale, name, side=None):
    t, d = x.shape
    ng, cg = w.shape[0], w.shape[1]
    tm = _tile(t, LN_ROW_TILE)

    def body(x_ref, h_ref, dy_ref, w_ref, s_ref, dp_ref, dw_ref, ds_ref):
        i = pl.program_id(0)

        @pl.when(i == 0)
        def _():
            dw_ref[...] = jnp.zeros_like(dw_ref)
            ds_ref[...] = jnp.zeros_like(ds_ref)

        for gi in range(ng):
            cols = pl.ds(gi * cg, cg)
            main = x_ref[:, cols]
            halo = jnp.where(i > 0, h_ref[:, cols], 0.0)
            pooled = _pooled_group(main, halo, gi, i * tm).astype(BF16)
            y = jnp.dot(pooled, w_ref[gi], preferred_element_type=F32)
            dyg = dy_ref[:, cols]
            ds_ref[:, cols] += jnp.sum(dyg * y, axis=0, keepdims=True)
            dyw = (dyg * s_ref[:, cols]).astype(BF16)
            dw_ref[gi] += lax.dot_general(pooled, dyw, _TN, preferred_element_type=F32)
            dp_ref[:, cols] = lax.dot_general(dyw, w_ref[gi], _NT, preferred_element_type=F32)

    main, before = _pool_specs(t, d, tm)
    vec = pl.BlockSpec((1, d), lambda i: (0, 0))
    wspec = pl.BlockSpec(w.shape, lambda i: (0, 0, 0))
    steps = t // tm
    outs, side_out = _call(
        body, lambda: pl.program_id(0) == 0, lambda: pl.program_id(0) == steps - 1, side, grid=(steps,),
        in_specs=[main, before, main, wspec, vec], out_specs=[main, wspec, vec],
        out_shape=[_sds((t, d), F32), _sds(w.shape, F32), _sds((1, d), F32)],
        scratch_shapes=[], name=name, args=(x, x, dy, w, scale))
    return outs if side is None else (outs, side_out)


def _pool_adjoint(dp, dres, n_groups, name):
    t, d = dp.shape
    cg = d // n_groups
    tm = _tile(t, ROW_TILE)
    steps = t // tm
    per = tm // POOL_HALO

    def body(dp_ref, after_ref, dres_ref, dx_ref):
        i = pl.program_id(0)
        rows = lax.broadcasted_iota(jnp.int32, (tm, 1), 0)
        rows_after = lax.broadcasted_iota(jnp.int32, (POOL_HALO, 1), 0)
        for gi in range(n_groups):
            window = POOL_WINDOWS[gi]
            cols = pl.ds(gi * cg, cg)
            main = dp_ref[:, cols]
            cnt = jnp.minimum(i * tm + rows + 1, window).astype(F32)
            cnt_after = jnp.minimum((i + 1) * tm + rows_after + 1, window).astype(F32)
            after = jnp.where(i < steps - 1, after_ref[:, cols] / cnt_after, 0.0)
            ext = jnp.concatenate([main / cnt, after], axis=0)
            sums = _window_sums(ext, window, forward=True)[:tm, :]
            dx_ref[:, cols] = ALPHA * dres_ref[:, cols] + sums - main

    main = pl.BlockSpec((tm, d), lambda i: (i, 0))
    after = pl.BlockSpec((POOL_HALO, d), lambda i: (jnp.minimum((i + 1) * per, t // POOL_HALO - 1), 0))
    return pl.pallas_call(
        body, grid=(steps,), in_specs=[main, after, main], out_specs=main,
        out_shape=_sds((t, d), F32), name=name, compiler_params=_cp(1))(dp, dp, dres)


def _split_dot(x, tri):
    hi = x.astype(BF16)
    lo = (x - hi.astype(F32)).astype(BF16)
    return jnp.dot(hi, tri, preferred_element_type=F32) + jnp.dot(lo, tri, preferred_element_type=F32)


def _att_windows(qs, k_ws, limit, carry_rests, suffix):
    heads = range(len(qs))
    zs = [lax.dot_general(qs[hh], k_ws[hh], _NT, preferred_element_type=F32) * ATT_SCALE for hh in heads]
    visible = lax.broadcasted_iota(jnp.int32, zs[0].shape, 1) < limit
    zs = [jnp.where(visible, z, -MASKED) for z in zs]
    es = [jnp.exp(-jnp.abs(z)) for z in zs]
    log_nots = [-(jnp.maximum(z, 0.0) + jnp.log(1.0 + e)) for z, e in zip(zs, es)]
    rests = [_split_dot(ln, suffix[...]) + carry for ln, carry in zip(log_nots, carry_rests)]
    weights = [jnp.exp(z + r) for z, r in zip(zs, rests)]
    return zs, es, log_nots, weights


def _tri(w, strict):
    r = lax.broadcasted_iota(jnp.int32, (w, w), 0)
    c = lax.broadcasted_iota(jnp.int32, (w, w), 1)
    return ((r > c) if strict else (r >= c)).astype(BF16)


def _heads_per_step(qkv3, n_heads, most):
    cpb = qkv3.shape[2] // HEAD_DIM
    hp = most
    while cpb % hp or n_heads % hp:
        hp //= 2
    return hp


def _att_specs(qkv3, n_heads, hp):
    t = qkv3.shape[1]
    cpb = qkv3.shape[2] // HEAD_DIM
    wd = hp * HEAD_DIM

    def slab(off):
        return pl.BlockSpec((None, t, wd), lambda g, i: ((off + g * hp) // cpb, 0, ((off + g * hp) % cpb) // hp))

    q = pl.BlockSpec((None, ATT_BLOCK, wd), lambda g, i: ((g * hp) // cpb, i, ((g * hp) % cpb) // hp))
    return q, slab(n_heads), slab(2 * n_heads)


def _head_cols(hh):
    return pl.ds(hh * HEAD_DIM, HEAD_DIM)


def _key_bounds(k_ref, kmax, hp):
    for hh in range(hp):
        kf = k_ref[:, _head_cols(hh)].astype(F32)
        kmax[hh] = jnp.sqrt(jnp.max(jnp.sum(kf * kf, axis=1, keepdims=True)))


def _score_bound(q, key_norm):
    qf = q.astype(F32)
    return ATT_SCALE * 1.001 * key_norm * jnp.sqrt(jnp.sum(qf * qf, axis=1, keepdims=True)) + 1e-3


def _any_alive(rests, bounds):
    alive = jnp.max(rests[0] + bounds[0]) > -EXP_ZERO
    for r, zb in zip(rests[1:], bounds[1:]):
        alive = jnp.logical_or(alive, jnp.max(r + zb) > -EXP_ZERO)
    return alive


def _window_rows(hi, w):
    start = jnp.maximum(hi - w, 0)
    return start, pl.ds(pl.multiple_of(start, ATT_BLOCK), w)


def _attn_fwd(qkv3, n_heads, name):
    t = qkv3.shape[1]
    b = ATT_BLOCK
    w = min(ATT_WINDOW, t)
    hp = _heads_per_step(qkv3, n_heads, ATT_FWD_HEADS)
    heads = range(hp)

    def body(q_ref, k_ref, v_ref, o_ref, kmax, suffix):
        i = pl.program_id(1)

        @pl.when(i == 0)
        def _():
            _key_bounds(k_ref, kmax, hp)
            suffix[...] = _tri(w, strict=False)

        qs = [q_ref[:, _head_cols(hh)] for hh in heads]
        bounds = [_score_bound(qs[hh], kmax[hh]) for hh in heads]
        qpos = i * b + lax.broadcasted_iota(jnp.int32, (b, 1), 0)

        def cond(c):
            return jnp.logical_and(c[0] > 0, _any_alive(c[1], bounds))

        def step(c):
            hi, rests, accs = c
            start, rows = _window_rows(hi, w)
            limit = jnp.minimum(qpos, hi) - start
            k_ws = [k_ref[rows, _head_cols(hh)] for hh in heads]
            _, _, log_nots, weights = _att_windows(qs, k_ws, limit, rests, suffix)
            new_accs = tuple(accs[hh] + jnp.dot(weights[hh].astype(BF16), v_ref[rows, _head_cols(hh)],
                                                preferred_element_type=F32) for hh in heads)
            new_rests = tuple(rests[hh] + jnp.sum(log_nots[hh], axis=1, keepdims=True) for hh in heads)
            return start, new_rests, new_accs

        init = ((i + 1) * b, tuple(jnp.zeros((b, 1), F32) for _ in heads),
                tuple(jnp.zeros((b, HEAD_DIM), F32) for _ in heads))
        _, _, accs = lax.while_loop(cond, step, step(init))
        for hh in heads:
            o_ref[:, _head_cols(hh)] = accs[hh].astype(o_ref.dtype)

    qs_, ks_, vs_ = _att_specs(qkv3, n_heads, hp)
    return pl.pallas_call(
        body, grid=(n_heads // hp, t // b), in_specs=[qs_, ks_, vs_],
        out_specs=pl.BlockSpec((b, hp * HEAD_DIM), lambda g, i: (i, g)),
        out_shape=_sds((t, n_heads * HEAD_DIM), BF16),
        scratch_shapes=[pltpu.SMEM((hp,), F32), pltpu.VMEM((w, w), BF16)],
        name=name, compiler_params=_cp(2))(qkv3, qkv3, qkv3)


def _attn_bwd(qkv3, do, n_heads, name):
    t = qkv3.shape[1]
    b = ATT_BLOCK
    w = min(ATT_WINDOW, t)
    nq = t // b
    hp = _heads_per_step(qkv3, n_heads, ATT_BWD_HEADS)
    heads = range(hp)
    wd = hp * HEAD_DIM

    def body(q_ref, k_ref, v_ref, do_ref, dq_ref, dk_ref, dv_ref, kmax, dk_acc, dv_acc, suffix, strict_suffix):
        i = pl.program_id(1)

        @pl.when(i == 0)
        def _():
            _key_bounds(k_ref, kmax, hp)
            dk_acc[...] = jnp.zeros_like(dk_acc)
            dv_acc[...] = jnp.zeros_like(dv_acc)
            suffix[...] = _tri(w, strict=False)
            strict_suffix[...] = _tri(w, strict=True)

        qs = [q_ref[:, _head_cols(hh)] for hh in heads]
        douts = [do_ref[:, _head_cols(hh)] for hh in heads]
        bounds = [_score_bound(qs[hh], kmax[hh]) for hh in heads]
        zero_cols = tuple(jnp.zeros((b, 1), F32) for _ in heads)
        hi0 = (i + 1) * b
        qpos = i * b + lax.broadcasted_iota(jnp.int32, (b, 1), 0)

        def cond(c):
            return jnp.logical_and(c[0] > 0, _any_alive(c[1], bounds))

        def tiles_of(hi, rests):
            start, rows = _window_rows(hi, w)
            k_ws = [k_ref[rows, _head_cols(hh)] for hh in heads]
            dps = [lax.dot_general(douts[hh], v_ref[rows, _head_cols(hh)], _NT, preferred_element_type=F32)
                   for hh in heads]
            zs, es, log_nots, weights = _att_windows(qs, k_ws, jnp.minimum(qpos, hi) - start, rests, suffix)
            dlas = [a * dp for a, dp in zip(weights, dps)]
            return start, rows, (k_ws, zs, es, log_nots, weights, dlas)

        def row_sums(carries, tiles):
            return tuple(c + jnp.sum(x, axis=1, keepdims=True) for c, x in zip(carries, tiles))

        def first_pass(rows, tiles, rests, totals):
            _, _, _, log_nots, weights, dlas = tiles
            for hh in heads:
                dv_acc[rows, _head_cols(hh)] += lax.dot_general(weights[hh].astype(BF16), douts[hh], _TN,
                                                                preferred_element_type=F32)
            return row_sums(rests, log_nots), row_sums(totals, dlas)

        def second_pass(rows, tiles, totals, laters, dqs):
            k_ws, zs, es, _, _, dlas = tiles
            insides = [_split_dot(dla, strict_suffix[...]) for dla in dlas]
            dzs = []
            for hh in heads:
                dlog_not = totals[hh] - laters[hh] - insides[hh]
                inv = 1.0 / (1.0 + es[hh])
                sig = jnp.where(zs[hh] >= 0, inv, es[hh] * inv)
                dzs.append(((dlas[hh] - sig * dlog_not) * ATT_SCALE).astype(BF16))
            new_dqs = tuple(dqs[hh] + jnp.dot(dzs[hh], k_ws[hh], preferred_element_type=F32) for hh in heads)
            for hh in heads:
                dk_acc[rows, _head_cols(hh)] += lax.dot_general(dzs[hh], qs[hh], _TN, preferred_element_type=F32)
            return row_sums(laters, dlas), new_dqs

        start0, rows0, tiles0 = tiles_of(hi0, zero_cols)
        rests1, totals1 = first_pass(rows0, tiles0, zero_cols, zero_cols)

        def sweep1(c):
            hi, rests, totals = c
            start, rows, tiles = tiles_of(hi, rests)
            return (start,) + first_pass(rows, tiles, rests, totals)

        _, _, totals = lax.while_loop(cond, sweep1, (start0, rests1, totals1))
        laters1, dqs1 = second_pass(rows0, tiles0, totals, zero_cols,
                                    tuple(jnp.zeros((b, HEAD_DIM), F32) for _ in heads))

        def sweep2(c):
            hi, rests, laters, dqs = c
            start, rows, tiles = tiles_of(hi, rests)
            return (start, row_sums(rests, tiles[3])) + second_pass(rows, tiles, totals, laters, dqs)

        _, _, _, dqs = lax.while_loop(cond, sweep2, (start0, rests1, laters1, dqs1))
        for hh in heads:
            dq_ref[:, _head_cols(hh)] = dqs[hh].astype(dq_ref.dtype)

        @pl.when(i == nq - 1)
        def _():
            dk_ref[...] = dk_acc[...].astype(dk_ref.dtype)
            dv_ref[...] = dv_acc[...].astype(dv_ref.dtype)

    qs_, ks_, vs_ = _att_specs(qkv3, n_heads, hp)
    blk = pl.BlockSpec((b, wd), lambda g, i: (i, g))
    slab = pl.BlockSpec((t, wd), lambda g, i: (0, g))
    d = n_heads * HEAD_DIM
    return pl.pallas_call(
        body, grid=(n_heads // hp, nq),
        in_specs=[qs_, ks_, vs_, pl.BlockSpec((None, b, wd), lambda g, i: (0, i, g))],
        out_specs=[blk, slab, slab],
        out_shape=[_sds((t, d), BF16)] * 3,
        scratch_shapes=[pltpu.SMEM((hp,), F32), pltpu.VMEM((t, wd), F32), pltpu.VMEM((t, wd), F32),
                        pltpu.VMEM((w, w), BF16), pltpu.VMEM((w, w), BF16)],
        name=name, compiler_params=_cp(2))(qkv3, qkv3, qkv3, do)


def _conv_rows(main, halo, w_ref, b_ref):
    ext = jnp.concatenate([halo, main], axis=0)
    h1 = pltpu.roll(ext, 1, 0)[CONV_HALO:, :]
    h2 = pltpu.roll(ext, 2, 0)[CONV_HALO:, :]
    hc = b_ref[...] + w_ref[0:1, :] * h2
    hc = hc + w_ref[1:2, :] * h1
    hc = hc + w_ref[2:3, :] * main
    return hc, h1, h2


def _ffn_specs(t, fp, tm, half):
    per = tm // CONV_HALO
    main = lambda off: pl.BlockSpec((None, tm, fp), lambda g, i: (g + off, i, 0))
    before = lambda off: pl.BlockSpec((None, CONV_HALO, fp), lambda g, i: (g + off, jnp.maximum(i * per - 1, 0), 0))
    cw = lambda off: pl.BlockSpec((None, 3, fp), lambda g, i: (g + off, 0, 0))
    cb = lambda off: pl.BlockSpec((None, 1, fp), lambda g, i: (g + off, 0, 0))
    return [main(0), before(0), main(half), before(half), cw(0), cw(half), cb(0), cb(half)]


def _ffn_act(h, cw, cb, name, side=None):
    n, t, fp = h.shape
    half = n // 2
    tm = _tile(t, LN_ROW_TILE)
    steps = t // tm

    def body(hg_ref, hgb_ref, hv_ref, hvb_ref, wg_ref, wv_ref, bg_ref, bv_ref, a_ref):
        first = pl.program_id(1) == 0
        gate, _, _ = _conv_rows(hg_ref[...], jnp.where(first, 0.0, hgb_ref[...]), wg_ref, bg_ref)
        val, _, _ = _conv_rows(hv_ref[...], jnp.where(first, 0.0, hvb_ref[...]), wv_ref, bv_ref)
        a_ref[...] = (gate * jax.nn.sigmoid(gate) * val).astype(a_ref.dtype)

    (a,), side_out = _call(
        body, lambda: jnp.logical_and(pl.program_id(0) == 0, pl.program_id(1) == 0),
        lambda: jnp.logical_and(pl.program_id(0) == half - 1, pl.program_id(1) == steps - 1), side,
        grid=(half, steps), in_specs=_ffn_specs(t, fp, tm, half),
        out_specs=[pl.BlockSpec((None, tm, fp), lambda g, i: (g, i, 0))],
        out_shape=[_sds((half, t, fp), BF16)], scratch_shapes=[], name=name, args=(h, h, h, h, cw, cw, cb, cb))
    return a if side is None else (a, side_out)


def _act_grads(dact, gate, val):
    sig = jax.nn.sigmoid(gate)
    return dact * val * (sig * (1.0 + gate * (1.0 - sig))), dact * (gate * sig)


def _ffn_act_bwd(h, da, cw, cb, name):
    n, t, fp = h.shape
    half = n // 2
    tm = _tile(t, LN_ROW_TILE)

    def body(hg_ref, hgb_ref, hv_ref, hvb_ref, wg_ref, wv_ref, bg_ref, bv_ref, da_ref, dhc_ref, dw_ref, db_ref):
        first = pl.program_id(1) == 0

        @pl.when(first)
        def _():
            dw_ref[...] = jnp.zeros_like(dw_ref)
            db_ref[...] = jnp.zeros_like(db_ref)

        hg, hv = hg_ref[...], hv_ref[...]
        gate, hg1, hg2 = _conv_rows(hg, jnp.where(first, 0.0, hgb_ref[...]), wg_ref, bg_ref)
        val, hv1, hv2 = _conv_rows(hv, jnp.where(first, 0.0, hvb_ref[...]), wv_ref, bv_ref)
        dgate, dval = _act_grads(da_ref[...], gate, val)
        dhc_ref[0] = dgate
        dhc_ref[1] = dval
        for s, (dd, shifted) in enumerate(((dgate, (hg2, hg1, hg)), (dval, (hv2, hv1, hv)))):
            db_ref[s] += jnp.sum(dd, axis=0, keepdims=True)
            for kk in range(3):
                dw_ref[s, kk:kk + 1, :] += jnp.sum(dd * shifted[kk], axis=0, keepdims=True)

    specs = _ffn_specs(t, fp, tm, half) + [pl.BlockSpec((None, tm, fp), lambda g, i: (g, i, 0))]
    return pl.pallas_call(
        body, grid=(half, t // tm), in_specs=specs,
        out_specs=[pl.BlockSpec((2, None, tm, fp), lambda g, i: (0, g, i, 0)),
                   pl.BlockSpec((2, None, 3, fp), lambda g, i: (0, g, 0, 0)),
                   pl.BlockSpec((2, None, 1, fp), lambda g, i: (0, g, 0, 0))],
        out_shape=[_sds((2, half, t, fp), F32), _sds((2, half, 3, fp), F32), _sds((2, half, 1, fp), F32)],
        name=name, compiler_params=_cp(2))(h, h, h, h, cw, cw, cb, cb, da)


def _conv_adjoint(dhc, cw, name):
    n, t, fp = dhc.shape
    tm = _tile(t, ROW_TILE)
    steps = t // tm
    per = tm // CONV_HALO

    def body(d_ref, after_ref, w_ref, o_ref):
        main = d_ref[...]
        after = jnp.where(pl.program_id(1) < steps - 1, after_ref[...], 0.0)
        ext = jnp.concatenate([main, after], axis=0)
        rows = ext.shape[0]
        d1 = pltpu.roll(ext, rows - 1, 0)[:tm, :]
        d2 = pltpu.roll(ext, rows - 2, 0)[:tm, :]
        o_ref[...] = (w_ref[2:3, :] * main + w_ref[1:2, :] * d1 + w_ref[0:1, :] * d2).astype(o_ref.dtype)

    main = pl.BlockSpec((None, tm, fp), lambda g, i: (g, i, 0))
    after = pl.BlockSpec((None, CONV_HALO, fp), lambda g, i: (g, jnp.minimum((i + 1) * per, t // CONV_HALO - 1), 0))
    return pl.pallas_call(
        body, grid=(n, steps), in_specs=[main, after, pl.BlockSpec((None, 3, fp), lambda g, i: (g, 0, 0))],
        out_specs=main, out_shape=_sds((n, t, fp), BF16), name=name, compiler_params=_cp(2))(dhc, dhc, cw)


def _place():
    x, y, c = lax.axis_index("x"), lax.axis_index("y"), lax.axis_index("c")
    chips = [(1 - x, y), (x, 1 - y), (1 - x, 1 - y)]
    return x, y, c, chips


def _run_sides(sides, name):
    n_in = [len(s.ins) for s in sides]
    n_out = [len(s.outs) for s in sides]
    n_sem = [len(s.sems) for s in sides]

    def body(*refs):
        ins, outs, sems = refs[:sum(n_in)], refs[sum(n_in):sum(n_in) + sum(n_out)], refs[sum(n_in) + sum(n_out):]
        oi = oo = os_ = 0
        for k, s in enumerate(sides):
            mine = (ins[oi:oi + n_in[k]], outs[oo:oo + n_out[k]], sems[os_:os_ + n_sem[k]])
            s.start(*mine)
            s.finish(*mine)
            oi, oo, os_ = oi + n_in[k], oo + n_out[k], os_ + n_sem[k]

    aliases, oi, oo = {}, 0, 0
    for k, s in enumerate(sides):
        aliases.update({oi + a: oo + b for a, b in s.alias.items()})
        oi, oo = oi + n_in[k], oo + n_out[k]
    return pl.pallas_call(
        body, in_specs=_any_specs(sum(n_in)), out_specs=_any_specs(sum(n_out)),
        out_shape=[o for s in sides for o in s.outs], input_output_aliases=aliases,
        scratch_shapes=[q for s in sides for q in s.sems], name=name)(*[a for s in sides for a in s.ins])


def _place_shard(kind, w, chip, name, rows=None, base=None, layer=None):
    if kind == "pool":
        g, r, cdim = w.shape

        def body(chip_ref, w_ref, o_ref):
            del chip_ref
            o_ref[...] = w_ref[...].astype(BF16)

        return pl.pallas_call(
            body,
            grid_spec=pltpu.PrefetchScalarGridSpec(
                num_scalar_prefetch=1, grid=(1,),
                in_specs=[pl.BlockSpec((g, r, cdim), lambda i, chip_ref: (0, 0, 0))],
                out_specs=pl.BlockSpec((g, r, cdim), lambda i, chip_ref: (0, chip_ref[0], 0))),
            out_shape=_sds((g, 4 * r, cdim), BF16), name=name, compiler_params=_cp(1))(chip, w)

    if kind == "lead" and w.ndim == 3:
        r, cs = w.shape[1:]
        src = lambda n_src: pl.BlockSpec((None, tr, cs), lambda i, chip_ref: (layer, jnp.minimum(i, n_src - 1), 0))
    else:
        r, cs = w.shape
        src = lambda n_src: pl.BlockSpec((tr, cs), lambda i, chip_ref: (jnp.minimum(i, n_src - 1), 0))
    if kind == "lead":
        rows = rows or r
        tr = _tile(r, ROW_TILE, 16) if rows == r else rows - r
        assert r % tr == 0 and tr % 16 == 0
        n_src = r // tr

        def body(chip_ref, w_ref, o_ref):
            del chip_ref
            o_ref[...] = jnp.where(pl.program_id(0) < n_src, w_ref[...], 0.0).astype(BF16)

        return pl.pallas_call(
            body,
            grid_spec=pltpu.PrefetchScalarGridSpec(
                num_scalar_prefetch=1, grid=(rows // tr,), in_specs=[src(n_src)],
                out_specs=pl.BlockSpec((None, tr, cs), lambda i, chip_ref: (chip_ref[0], i, 0))),
            out_shape=_sds((4, rows, cs), BF16), name=name, compiler_params=_cp(1))(chip, w)

    assert kind == "down"
    tr = r // 2 if (r // 2) % 16 == 0 else r
    per = r // tr

    def body(chip_ref, w_ref, base_ref, o_ref):
        del chip_ref, base_ref
        o_ref[...] = w_ref[...].astype(BF16)

    return pl.pallas_call(
        body,
        grid_spec=pltpu.PrefetchScalarGridSpec(
            num_scalar_prefetch=1, grid=(per,),
            in_specs=[pl.BlockSpec((tr, cs), lambda i, chip_ref: (i, 0)), pl.BlockSpec(memory_space=pl.ANY)],
            out_specs=pl.BlockSpec((None, tr, cs), lambda i, chip_ref: (chip_ref[0] // 2, (chip_ref[0] % 2) * per + i, 0))),
        out_shape=_sds(base.shape, BF16), input_output_aliases={2: 0},
        name=name, compiler_params=_cp(1))(chip, w, base)


def _gather_sides(items, bufs=None):
    n = len(items)
    kinds = [it[0] for it in items]
    shard_rows = [it[2] for it in items]
    bufs = [it[1] for it in items] if bufs is None else list(bufs)

    def half_of(outs, m, chip, half):
        k = 2 * chip[0] + chip[1]
        o, r = outs[m], shard_rows[m]
        if kinds[m] == "pool":
            gh = o.shape[0] // 2
            return o.at[pl.ds(half * gh, gh), pl.ds(k * r, r)]
        r2 = r // 2
        if kinds[m] == "down":
            return o.at[k // 2, pl.ds((k % 2) * r + half * r2, r2)]
        return o.at[k, pl.ds(half * r2, r2)]

    def remote(outs, sems, m, j, chip, half, to):
        ref = half_of(outs, m, chip, half)
        return pltpu.make_async_remote_copy(src_ref=ref, dst_ref=ref, send_sem=sems[0].at[m, j],
                                            recv_sem=sems[1].at[m, j], device_id=to, device_id_type=MESH)

    def ici_copies(outs, sems, sending):
        x, y, c, chips = _place()
        if sending:
            return [remote(outs, sems, m, j, (x, y), c, (*chip, c)) for m in range(n) for j, chip in enumerate(chips)]
        return [remote(outs, sems, m, j, chip, c, (x, y, c)) for m in range(n) for j, chip in enumerate(chips)]

    def d2d_copies(outs, sems, sending):
        x, y, c, chips = _place()
        if sending:
            return [remote(outs, sems, m, j, chip, c, (x, y, 1 - c)) for m in range(n) for j, chip in enumerate(chips)]
        return [remote(outs, sems, m, j, chip, 1 - c, (x, y, c)) for m in range(n) for j, chip in enumerate(chips)]

    def phase(copies):
        def start(ins, outs, sems):
            for cp in copies(outs, sems, True):
                cp.start()

        def finish(ins, outs, sems):
            for cp in copies(outs, sems, False):
                cp.wait_recv()
            for cp in copies(outs, sems, True):
                cp.wait_send()

        return start, finish

    ici, d2d = phase(ici_copies), phase(d2d_copies)

    def both_finish(ins, outs, sems):
        ici[1](ins, outs, sems[:2])
        d2d[0](ins, outs, sems[2:])
        d2d[1](ins, outs, sems[2:])

    pair = [pltpu.SemaphoreType.DMA((n, 3)), pltpu.SemaphoreType.DMA((n, 3))]
    shapes = [_sds(b.shape, b.dtype) for b in bufs]
    alias = {m: m for m in range(n)}

    def side(which):
        if which == "both":
            return _Side(bufs, shapes, alias, pair + pair, lambda i, o, s: ici[0](i, o, s[:2]), both_finish)
        start, finish = ici if which == "ici" else d2d
        return _Side(bufs, shapes, alias, pair, start, finish)

    return side


def _sibling_side(grads):
    n = len(grads)

    def copies(ins, outs, sems):
        x, y, c, _ = _place()
        res = []
        for m in range(n):
            r2 = ins[m].shape[1] // 2
            res.append(pltpu.make_async_remote_copy(
                src_ref=ins[m].at[:, pl.ds((1 - c) * r2, r2)], dst_ref=outs[m],
                send_sem=sems[0].at[m], recv_sem=sems[1].at[m], device_id=(x, y, 1 - c), device_id_type=MESH))
        return res

    def start(ins, outs, sems):
        for cp in copies(ins, outs, sems):
            cp.start()

    def finish(ins, outs, sems):
        for cp in copies(ins, outs, sems):
            cp.wait_recv()
        for cp in copies(ins, outs, sems):
            cp.wait_send()

    return _Side(list(grads), [_sds((4, g.shape[1] // 2, g.shape[2]), g.dtype) for g in grads], {},
                 [pltpu.SemaphoreType.DMA((n,)), pltpu.SemaphoreType.DMA((n,))], start, finish)


def _owner_chips_side(parts):
    n = len(parts)

    def copies(ins, outs, sems):
        _, _, c, chips = _place()
        return [pltpu.make_async_remote_copy(
            src_ref=ins[m].at[2 * chip[0] + chip[1]], dst_ref=outs[m].at[j], send_sem=sems[0].at[m, j],
            recv_sem=sems[1].at[m, j], device_id=(*chip, c), device_id_type=MESH)
            for m in range(n) for j, chip in enumerate(chips)]

    def start(ins, outs, sems):
        for cp in copies(ins, outs, sems):
            cp.start()

    def finish(ins, outs, sems):
        for cp in copies(ins, outs, sems):
            cp.wait_recv()
        for cp in copies(ins, outs, sems):
            cp.wait_send()

    return _Side(list(parts), [_sds((3,) + p.shape[1:], p.dtype) for p in parts], {},
                 [pltpu.SemaphoreType.DMA((n, 3)), pltpu.SemaphoreType.DMA((n, 3))], start, finish)


def _exchange_finished_halves(shards, name):
    n = len(shards)

    def body(*refs):
        out = refs[n:2 * n]
        send_sems, recv_sems = refs[2 * n:]
        x, y, c, _ = _place()
        copies = []
        for m in range(n):
            r2 = out[m].shape[0] // 2
            mine = out[m].at[pl.ds(c * r2, r2)]
            copies.append(pltpu.make_async_remote_copy(
                src_ref=mine, dst_ref=mine, send_sem=send_sems.at[m], recv_sem=recv_sems.at[m],
                device_id=(x, y, 1 - c), device_id_type=MESH))
        for cp in copies:
            cp.start()
        for m in range(n):
            r2 = out[m].shape[0] // 2
            theirs = out[m].at[pl.ds((1 - c) * r2, r2)]
            pltpu.make_async_remote_copy(
                src_ref=theirs, dst_ref=theirs, send_sem=send_sems.at[m], recv_sem=recv_sems.at[m],
                device_id=(x, y, 1 - c), device_id_type=MESH).wait_recv()
        for cp in copies:
            cp.wait_send()

    return pl.pallas_call(
        body, in_specs=_any_specs(n), out_specs=_any_specs(n), out_shape=[_sds(s.shape, s.dtype) for s in shards],
        input_output_aliases={m: m for m in range(n)},
        scratch_shapes=[pltpu.SemaphoreType.DMA((n,)), pltpu.SemaphoreType.DMA((n,))], name=name)(*shards)


def _all_reduce_small(v, name):
    rows = v.shape[0]

    def body(v_ref, out_ref, buf, send_sems, recv_sems, local_sem):
        x, y, c, chips = _place()
        me, sibling = (x, y, c), (x, y, 1 - c)

        def slot(px, py, pc):
            return buf.at[4 * px + 2 * py + pc]

        def copy(k, block, to, src=None):
            return pltpu.make_async_remote_copy(
                src_ref=slot(*block) if src is None else src, dst_ref=slot(*block),
                send_sem=send_sems.at[k], recv_sem=recv_sems.at[k], device_id=to, device_id_type=MESH)

        mine = pltpu.make_async_copy(v_ref, slot(*me), local_sem)
        mine.start()
        first = [copy(0, me, sibling, src=v_ref)]
        first += [copy(1 + j, me, (*chip, c), src=v_ref) for j, chip in enumerate(chips)]
        for cp in first:
            cp.start()
        passed = [copy(4 + j, (*chip, c), sibling) for j, chip in enumerate(chips)]
        for j, chip in enumerate(chips):
            copy(1 + j, (*chip, c), me).wait_recv()
            passed[j].start()
        copy(0, sibling, me).wait_recv()
        for j, chip in enumerate(chips):
            copy(4 + j, (*chip, 1 - c), me).wait_recv()
        for cp in first + passed:
            cp.wait_send()
        mine.wait()
        total = buf[0]
        for dev in range(1, 8):
            total = total + buf[dev]
        out_ref[...] = total

    vm = pl.BlockSpec(memory_space=pltpu.VMEM)
    return pl.pallas_call(
        body, in_specs=[vm], out_specs=vm, out_shape=_sds(v.shape, F32),
        scratch_shapes=[pltpu.VMEM((8, rows, LANES), F32), pltpu.SemaphoreType.DMA((7,)),
                        pltpu.SemaphoreType.DMA((7,)), pltpu.SemaphoreType.DMA],
        name=name, compiler_params=pltpu.CompilerParams(vmem_limit_bytes=VMEM_LIMIT))(v)


def _chip_partial(grad, from_sibling, core, name):
    _, r, cdim = grad.shape
    r2 = r // 2
    tr = _tile(r2, SUM_ROW_TILE)
    per = r2 // tr

    def body(core_ref, g_ref, s_ref, o_ref, ob_ref):
        del core_ref
        total = g_ref[...] + s_ref[...]
        o_ref[...] = total
        ob_ref[...] = total.astype(BF16)

    blk = pl.BlockSpec((None, tr, cdim), lambda k, i, core_ref: (k, i, 0))
    mine = pl.BlockSpec((None, tr, cdim), lambda k, i, core_ref: (k, core_ref[0] * per + i, 0))
    return pl.pallas_call(
        body,
        grid_spec=pltpu.PrefetchScalarGridSpec(num_scalar_prefetch=1, grid=(4, per), in_specs=[mine, blk],
                                               out_specs=[blk, blk]),
        out_shape=[_sds((4, r2, cdim), F32), _sds((4, r2, cdim), BF16)],
        name=name, compiler_params=_cp(2))(core, grad, from_sibling)


def _owner_sum(partial, from_chips, place, name):
    _, r2, cdim = partial.shape
    tr = _tile(r2, SUM_ROW_TILE)
    per = r2 // tr

    def body(place_ref, p_ref, f_ref, o_ref):
        del place_ref
        total = p_ref[...]
        for j in range(3):
            total = total + f_ref[j].astype(F32)
        o_ref[...] = total

    return pl.pallas_call(
        body,
        grid_spec=pltpu.PrefetchScalarGridSpec(
            num_scalar_prefetch=1, grid=(per,),
            in_specs=[pl.BlockSpec((None, tr, cdim), lambda i, place_ref: (place_ref[0], i, 0)),
                      pl.BlockSpec((3, tr, cdim), lambda i, place_ref: (0, i, 0))],
            out_specs=pl.BlockSpec((tr, cdim), lambda i, place_ref: (place_ref[1] * per + i, 0))),
        out_shape=_sds((2 * r2, cdim), F32), name=name, compiler_params=_cp(1))(place, partial, from_chips)


def _adamw(g, w, m, v, layer, prev, name):
    _, r, cdim = w.shape
    tr = _tile(r, OPT_ROW_TILE)
    c1 = 1.0 / (1.0 - ADAM_B1 ** ADAM_STEP)
    c2 = 1.0 / (1.0 - ADAM_B2 ** ADAM_STEP)
    n_prev = 0 if prev is None else 4

    def body(g_ref, w_ref, m_ref, v_ref, *rest):
        go_ref, d_ref, mo_ref, vo_ref = rest[n_prev:]
        grad = g_ref[:, pl.ds(0, cdim)]
        m_new = ADAM_B1 * m_ref[...] + (1.0 - ADAM_B1) * grad
        v_new = ADAM_B2 * v_ref[...] + (1.0 - ADAM_B2) * (grad * grad)
        go_ref[...] = grad
        mo_ref[...] = m_new
        vo_ref[...] = v_new
        d_ref[...] = -ADAM_LR * ((m_new * c1) / (jnp.sqrt(v_new * c2) + ADAM_EPS) + ADAM_WD * w_ref[...])

    blk = pl.BlockSpec((None, tr, cdim), lambda i: (layer, i, 0))
    gblk = pl.BlockSpec((tr, g.shape[1]), lambda i: (i, 0))
    return pl.pallas_call(
        body, grid=(r // tr,), in_specs=[gblk, blk, blk, blk] + _any_specs(n_prev), out_specs=[blk] * 4,
        out_shape=[_sds(w.shape, F32)] * 4, input_output_aliases={4 + k: k for k in range(n_prev)},
        name=name, compiler_params=_cp(1))(g, w, m, v, *(prev or ()))


def _pack_rows(vectors):
    flat = [v.reshape(-1) for v in vectors]
    sizes = [f.shape[0] for f in flat]
    total = sum(sizes)
    padded = _round_up(total, 8 * LANES)
    buf = jnp.concatenate(flat + [jnp.zeros((padded - total,), F32)])
    return buf.reshape(padded // LANES, LANES), sizes


def _unpack_rows(buf, sizes, shapes):
    flat = buf.reshape(-1)
    out, off = [], 0
    for n, shp in zip(sizes, shapes):
        out.append(flat[off:off + n].reshape(shp))
        off += n
    return out


def kernel(x, pool_w, pool_scale, attn_w_qkv, attn_w_o, ffn_w_up, ffn_conv_w, ffn_conv_b, ffn_w_down, ln_mix_g, ln_mix_b, ln_ffn_g, ln_ffn_b, loss_target, m_pool_w, m_pool_scale, m_attn_w_qkv, m_attn_w_o, m_ffn_w_up, m_ffn_conv_w, m_ffn_conv_b, m_ffn_w_down, m_ln_mix_g, m_ln_mix_b, m_ln_ffn_g, m_ln_ffn_b, v_pool_w, v_pool_scale, v_attn_w_qkv, v_attn_w_o, v_ffn_w_up, v_ffn_conv_w, v_ffn_conv_b, v_ffn_w_down, v_ln_mix_g, v_ln_mix_b, v_ln_ffn_g, v_ln_ffn_b):
    t, d = x.shape[1], x.shape[2]
    n_heads = d // HEAD_DIM
    n_groups = pool_w.shape[1]
    fs = ffn_w_up.shape[2]
    fp = _round_up(fs, LANES)
    rd = ffn_w_down.shape[1]
    assert 2 * rd == fs
    xi, yi, ci = lax.axis_index("x"), lax.axis_index("y"), lax.axis_index("c")
    chip = (2 * xi + yi).astype(jnp.int32)
    chip_arr, core_arr = chip.reshape(1), ci.astype(jnp.int32).reshape(1)
    place_arr = jnp.concatenate([chip_arr, core_arr])

    x2 = x.reshape(t, d)
    target = loss_target.reshape(t, d)
    pad_cols = lambda a: jnp.pad(a, [(0, 0)] * (a.ndim - 1) + [(0, fp - fs)])
    up_t = [jnp.transpose(a, (0, 2, 1)) for a in (ffn_w_up, m_ffn_w_up, v_ffn_w_up)]

    gather_items = []
    for i in range(DEPTH):
        j = i // 2
        items = []
        if i % 2 == 0:
            items.append(("pool", _place_shard("pool", pool_w[j], chip_arr, name="place_pool"), pool_w.shape[2]))
        else:
            items.append(("lead", _place_shard("lead", attn_w_qkv[j], chip_arr, name="place_qkv"), d))
            items.append(("lead", _place_shard("lead", attn_w_o[j], chip_arr, name="place_wo"), attn_w_o.shape[1]))
        items.append(("lead", _place_shard("lead", up_t[0], chip_arr, name="place_up", rows=fp, layer=i), fp))
        items.append(("down", _place_shard("down", ffn_w_down[i], chip_arr, name="place_down",
                                           base=jnp.zeros((2, fp, d), BF16)), rd))
        gather_items.append(items)
    weights = [None] * DEPTH
    weights[0] = _run_sides([_gather_sides(gather_items[0])("both")], name="gather_layer0")

    conv_b_all = pad_cols(ffn_conv_b.reshape(DEPTH, 4, 1, fs))
    cw_local = pad_cols(ffn_conv_w)
    slot = (jnp.arange(4, dtype=jnp.int32) == chip).astype(F32) * (1.0 - ci.astype(F32))
    cw_placed = slot[None, :, None, None] * cw_local[:, None]
    cw_buf, cw_sizes = _pack_rows([cw_placed])
    conv_w_all = _unpack_rows(_all_reduce_small(cw_buf, name="gather_conv_w"), cw_sizes, [cw_placed.shape])[0]

    gam = lambda a, i: a[i].reshape(1, d)

    saved = []
    cur, cur_b = x2, x2.astype(BF16)
    for i in range(DEPTH):
        j = i // 2
        w = weights[i]
        s = {"x_in": cur, "x_in_b": cur_b}
        if i % 2 == 0:
            w_pool, w_up, w_down = w
            s["scale"] = pool_scale[j].reshape(1, d)
            r1, x1, x1b = _pool_fwd(cur, w_pool, s["scale"], gam(ln_mix_g, i), gam(ln_mix_b, i), name="pool_fwd")
        else:
            w_qkv, w_o, w_up, w_down = w
            w_o3 = w_o.reshape(1, d, d)
            qkv = _mm_cols(cur_b, w_qkv, BF16, name="qkv_proj")
            o = _attn_fwd(qkv, n_heads, name="attn_fwd")
            s["qkv"], s["o"], s["w_o3"] = qkv, o, w_o3
            r1, x1, x1b = _mm_res_ln(o.reshape(1, t, d), w_o3, cur, gam(ln_mix_g, i), gam(ln_mix_b, i),
                                     name="attn_out_ln")
        if i + 1 < DEPTH:
            nxt = gather_items[i + 1]
            n_mix = len(nxt) - 2
            h, landed_ffn = _mm_cols(x1b, w_up, F32, name="ffn_up", transposed_b=True,
                                     side=_gather_sides(nxt[n_mix:])("ici"))
            a, landed_mix = _ffn_act(h, conv_w_all[i], conv_b_all[i], name="ffn_act",
                                     side=_gather_sides(nxt[:n_mix])("ici"))
            landed = list(landed_mix) + list(landed_ffn)
            (r2, x2n, x2b), gathered = _mm_res_ln(a, w_down, x1, gam(ln_ffn_g, i), gam(ln_ffn_b, i),
                                                  name="ffn_down_ln", side=_gather_sides(nxt, landed)("d2d"))
            weights[i + 1] = list(gathered)
        else:
            h = _mm_cols(x1b, w_up, F32, name="ffn_up", transposed_b=True)
            a = _ffn_act(h, conv_w_all[i], conv_b_all[i], name="ffn_act")
            r2, x2n, x2b = _mm_res_ln(a, w_down, x1, gam(ln_ffn_g, i), gam(ln_ffn_b, i), name="ffn_down_ln")
        s.update(r1=r1, x1b=x1b, h=h, a=a, r2=r2)
        saved.append(s)
        cur, cur_b = x2n, x2b

    loss_row, dcur = _loss_and_grad(cur, target, name="loss")
    loss = lax.psum(loss_row[0, 0], ("x", "y", "c"))

    big_grads = [None] * DEPTH
    reduced = [None] * DEPTH
    small = {}

    def finish_reduce(parts, from_chips, layer):
        halves = [_owner_sum(p[0], fc, place_arr, name="reduce_owner_sum") for p, fc in zip(parts, from_chips)]
        return _exchange_finished_halves(halves, name="reduce_halves_pool" if layer % 2 == 0 else "reduce_halves_attn")

    for i in reversed(range(DEPTH)):
        j = i // 2
        s, w = saved[i], weights[i]
        w_up, w_down = w[-2], w[-1]
        dr2, dr2b, small["ln_ffn_g", i], small["ln_ffn_b", i] = _ln_bwd(dcur, s["r2"], gam(ln_ffn_g, i), name="ln_bwd")
        pending = big_grads[i + 1] if i + 1 < DEPTH else None
        if pending is not None:
            da, from_sib = _mm_cols(dr2b, w_down, F32, name="ffn_down_bwd_act", transposed_b=True,
                                    side=_sibling_side(pending))
            parts = [_chip_partial(g, fs_, core_arr, name="reduce_chip_partial") for g, fs_ in zip(pending, from_sib)]
        else:
            da = _mm_cols(dr2b, w_down, F32, name="ffn_down_bwd_act", transposed_b=True)
        dr2b3 = dr2b.reshape(1, t, d)
        nmb = 2
        d_down = _mm_tn(s["a"], dr2b3, (2, fp, d), fp // nmb, d, 2,
                        (lambda u: u, nmb, lambda u, mb: mb), (lambda u: 0, lambda u: 0),
                        (lambda u: u, lambda u, mb: mb, lambda u: 0), name="ffn_down_bwd_w")
        dhc, dcw, dcb = _ffn_act_bwd(s["h"], da, conv_w_all[i], conv_b_all[i], name="ffn_act_bwd")
        small["conv_w", i], small["conv_b", i] = dcw, dcb
        dh = _conv_adjoint(dhc.reshape(4, t, fp), conv_w_all[i], name="ffn_conv_adjoint")
        up_w_args = (dh, s["x1b"].reshape(1, t, d), (4, fp, d), fp // 2, d, 4,
                     (lambda u: u, 2, lambda u, mb: mb), (lambda u: 0, lambda u: 0),
                     (lambda u: u, lambda u, mb: mb, lambda u: 0))
        if pending is not None:
            n_mix = len(parts) - 2
            dx1, chips_ffn = _mm_nt_acc(dh, w_up, dr2, fp, name="ffn_up_bwd_act", b_is_kn=True,
                                        side=_owner_chips_side([p[1] for p in parts[n_mix:]]))
            d_up, chips_mix = _mm_tn(*up_w_args, name="ffn_up_bwd_w",
                                     side=_owner_chips_side([p[1] for p in parts[:n_mix]]))
            reduced[i + 1] = finish_reduce(parts, list(chips_mix) + list(chips_ffn), i + 1)
        else:
            dx1 = _mm_nt_acc(dh, w_up, dr2, fp, name="ffn_up_bwd_act", b_is_kn=True)
            d_up = _mm_tn(*up_w_args, name="ffn_up_bwd_w")
        d_down4 = d_down[:, :fs].reshape(4, rd, d)
        if i == 0:
            (dr1, dr1b, small["ln_mix_g", i], small["ln_mix_b", i]), sib0 = _ln_bwd(
                dx1, s["r1"], gam(ln_mix_g, i), name="ln_bwd", side=_sibling_side([d_up, d_down4]))
            parts0 = [_chip_partial(g, fs_, core_arr, name="reduce_chip_partial")
                      for g, fs_ in zip([d_up, d_down4], sib0)]
            (dp, d_pool, small["pool_scale", j]), chips0 = _pool_bwd(
                s["x_in"], dr1, w[0], s["scale"], name="pool_bwd", side=_owner_chips_side([p[1] for p in parts0]))
        else:
            dr1, dr1b, small["ln_mix_g", i], small["ln_mix_b", i] = _ln_bwd(dx1, s["r1"], gam(ln_mix_g, i),
                                                                          name="ln_bwd")
            if i % 2 == 0:
                dp, d_pool, small["pool_scale", j] = _pool_bwd(s["x_in"], dr1, w[0], s["scale"], name="pool_bwd")
        if i % 2 == 0:
            dcur = _pool_adjoint(dp, dr1, n_groups, name="pool_adjoint")
            cg = d // n_groups
            d_pool4 = d_pool.reshape(n_groups, 4, cg // 4, cg).transpose(1, 0, 2, 3).reshape(4, n_groups * (cg // 4), cg)
            big_grads[i] = [d_pool4, d_up, d_down4]
        else:
            w_qkv = w[0]
            do = _mm_cols(dr1b, s["w_o3"], BF16, name="attn_out_bwd_act", transposed_b=True)
            d_wo = _mm_tn(s["o"].reshape(1, t, d), dr1b.reshape(1, t, d), (1, d, d), d // 2, d, 1,
                          (lambda u: 0, 2, lambda u, mb: mb), (lambda u: 0, lambda u: 0),
                          (lambda u: 0, lambda u, mb: mb, lambda u: 0), name="attn_out_bwd_w")
            dq, dk, dv = _attn_bwd(s["qkv"], do, n_heads, name="attn_bwd")
            cq = w_qkv.shape[2]
            dqkv = jnp.concatenate([dq, dk, dv], axis=1).reshape(t, 4, cq).transpose(1, 0, 2)
            dcur = _mm_nt_acc(dqkv, w_qkv, dr1, cq, name="qkv_bwd_act")
            d_qkv = _mm_tn(s["x_in_b"].reshape(1, t, d), dqkv, (4, d, cq), d // 2, cq, 4,
                           (lambda u: 0, 2, lambda u, mb: mb), (lambda u: u, lambda u: 0),
                           (lambda u: u, lambda u, mb: mb, lambda u: 0), name="qkv_bwd_w")
            big_grads[i] = [d_qkv, d_wo.reshape(4, d // 4, d), d_up, d_down4]
    grad_x = dcur.reshape(1, t, d)

    mixer0 = big_grads[0][:1]
    from_sib = _run_sides([_sibling_side(mixer0)], name="reduce_layer0_sibling")
    parts = [_chip_partial(g, fs_, core_arr, name="reduce_chip_partial") for g, fs_ in zip(mixer0, from_sib)]
    from_chips = _run_sides([_owner_chips_side([p[1] for p in parts])], name="reduce_layer0_chips")
    reduced[0] = finish_reduce(parts + parts0, list(from_chips) + list(chips0), 0)

    names = [("pool_scale", j) for j in range(2)]
    for nm in ("ln_mix_g", "ln_mix_b", "ln_ffn_g", "ln_ffn_b", "conv_b", "conv_w"):
        names += [(nm, i) for i in range(DEPTH)]
    vecs = [small[k] for k in names]
    sbuf, ssizes = _pack_rows(vecs)
    summed = dict(zip(names, _unpack_rows(_all_reduce_small(sbuf, name="reduce_small"), ssizes, [v.shape for v in vecs])))

    def stack_layers(nm, count):
        return jnp.stack([summed[nm, i] for i in range(count)])

    g_small = {
        "pool_scale": stack_layers("pool_scale", 2).reshape(2, d),
        "ln_mix_g": stack_layers("ln_mix_g", DEPTH).reshape(DEPTH, d),
        "ln_mix_b": stack_layers("ln_mix_b", DEPTH).reshape(DEPTH, d),
        "ln_ffn_g": stack_layers("ln_ffn_g", DEPTH).reshape(DEPTH, d),
        "ln_ffn_b": stack_layers("ln_ffn_b", DEPTH).reshape(DEPTH, d),
        "conv_b": stack_layers("conv_b", DEPTH).reshape(DEPTH, 4, fp)[:, :, :fs].reshape(DEPTH, 4 * fs),
        "conv_w": lax.dynamic_index_in_dim(stack_layers("conv_w", DEPTH).reshape(DEPTH, 4, 3, fp), chip, axis=1,
                                           keepdims=False)[:, :, :fs],
    }
    w_small = {"pool_scale": (pool_scale, m_pool_scale, v_pool_scale), "ln_mix_g": (ln_mix_g, m_ln_mix_g, v_ln_mix_g),
               "ln_mix_b": (ln_mix_b, m_ln_mix_b, v_ln_mix_b), "ln_ffn_g": (ln_ffn_g, m_ln_ffn_g, v_ln_ffn_g),
               "ln_ffn_b": (ln_ffn_b, m_ln_ffn_b, v_ln_ffn_b), "conv_b": (ffn_conv_b, m_ffn_conv_b, v_ffn_conv_b),
               "conv_w": (ffn_conv_w, m_ffn_conv_w, v_ffn_conv_w)}
    order = list(g_small)
    packs = [_pack_rows([g_small[k] for k in order])[0]]
    for idx in range(3):
        packs.append(_pack_rows([w_small[k][idx] for k in order])[0])
    small_sizes = _pack_rows([g_small[k] for k in order])[1]
    small_out = _adamw(packs[0], packs[1][None], packs[2][None], packs[3][None], 0, None, name="adamw_small")
    shapes = [g_small[k].shape for k in order]
    small_res = {k: [] for k in order}
    for arr in small_out:
        for k, val in zip(order, _unpack_rows(arr[0], small_sizes, shapes)):
            small_res[k].append(val)

    def opt_layers(per_layer_grads, w_all, m_all, v_all, name, rows=None):
        n_layers = w_all.shape[0]
        flat = [a.reshape(n_layers, rows or a.shape[1], -1) for a in (w_all, m_all, v_all)]
        res = None
        for li, g in enumerate(per_layer_grads):
            res = _adamw(g, *flat, li, res, name=name)
        return [o.reshape(w_all.shape) for o in res]

    cg = d // n_groups
    big = {
        "pool_w": opt_layers([reduced[i][0] for i in (0, 2)], pool_w, m_pool_w, v_pool_w, "adamw_pool",
                             rows=n_groups * (cg // 4)),
        "attn_w_qkv": opt_layers([reduced[i][0] for i in (1, 3)], attn_w_qkv, m_attn_w_qkv, v_attn_w_qkv, "adamw_qkv"),
        "attn_w_o": opt_layers([reduced[i][1] for i in (1, 3)], attn_w_o, m_attn_w_o, v_attn_w_o, "adamw_wo"),
        "ffn_w_up": [jnp.transpose(o, (0, 2, 1))
                     for o in opt_layers([reduced[i][-2] for i in range(DEPTH)], *up_t, "adamw_up")],
        "ffn_w_down": opt_layers([reduced[i][-1] for i in range(DEPTH)], ffn_w_down, m_ffn_w_down, v_ffn_w_down,
                                 "adamw_down"),
    }

    def leaf(k, name):
        if name in big:
            return big[name][k]
        key = {"ffn_conv_w": "conv_w", "ffn_conv_b": "conv_b"}.get(name, name)
        return small_res[key][k]

    weight_names = ["pool_w", "pool_scale", "attn_w_qkv", "attn_w_o", "ffn_w_up", "ffn_conv_w", "ffn_conv_b",
                    "ffn_w_down", "ln_mix_g", "ln_mix_b", "ln_ffn_g", "ln_ffn_b"]
    outs = [loss, grad_x]
    for k in range(4):
        outs += [leaf(k, nm) for nm in weight_names]
    return tuple(outs)
```

```python
import collections

import jax
import jax.numpy as jnp
from jax import lax
from jax.experimental import pallas as pl
from jax.experimental.pallas import tpu as pltpu

F32, BF16 = jnp.float32, jnp.bfloat16
MESH = pl.DeviceIdType.MESH

LANES = 128
HEAD_DIM = 128
ATT_BLOCK = 128
ATT_WINDOW = 3 * ATT_BLOCK
ATT_FWD_HEADS = 4
ATT_BWD_HEADS = 2
POOL_WINDOWS = (2, 4, 8, 16)
POOL_HALO = 16
CONV_HALO = 8
LN_EPS = 1e-5
DEPTH = 4
ALPHA = (2.0 * DEPTH) ** 0.25
ATT_SCALE = HEAD_DIM ** -0.5
EXP_ZERO = 115.0
MASKED = 1e30
ADAM_LR, ADAM_B1, ADAM_B2, ADAM_EPS, ADAM_WD, ADAM_STEP = 0.001, 0.9, 0.999, 1e-08, 0.01, 10

VMEM_LIMIT = 56 << 20
ROW_TILE = 512
LN_ROW_TILE = 256
FFN_ROW_TILE = 128
OPT_ROW_TILE = 128
SUM_ROW_TILE = 512


def _cp(n_axes):
    return pltpu.CompilerParams(dimension_semantics=("arbitrary",) * n_axes, vmem_limit_bytes=VMEM_LIMIT)


def _sds(shape, dtype):
    return jax.ShapeDtypeStruct(tuple(shape), dtype)


def _round_up(n, m):
    return (n + m - 1) // m * m


def _tile(n, cap, mult=8):
    if n <= cap:
        return n
    best = None
    for d in range(mult, cap + 1, mult):
        if n % d == 0:
            best = d
    assert best is not None, (n, cap)
    return best


_NT = (((1,), (1,)), ((), ()))
_TN = (((0,), (0,)), ((), ()))

_Side = collections.namedtuple("_Side", "ins outs alias sems start finish")


def _any_specs(n):
    return [pl.BlockSpec(memory_space=pl.ANY)] * n


def _call(body, first, last, side, *, grid, in_specs, out_specs, out_shape, scratch_shapes, name, args):
    n_axes = len(grid)
    if side is None:
        res = pl.pallas_call(body, grid=grid, in_specs=in_specs, out_specs=out_specs, out_shape=out_shape,
                             scratch_shapes=scratch_shapes, name=name, compiler_params=_cp(n_axes))(*args)
        return res, ()
    n_in, n_out, n_scr = len(in_specs), len(out_shape), len(scratch_shapes)
    s_in, s_out = len(side.ins), len(side.outs)

    def carried(*refs):
        ins, refs = refs[:n_in], refs[n_in:]
        side_ins, refs = refs[:s_in], refs[s_in:]
        outs, refs = refs[:n_out], refs[n_out:]
        side_outs, refs = refs[:s_out], refs[s_out:]
        scratch, side_sems = refs[:n_scr], refs[n_scr:]

        @pl.when(first())
        def _():
            side.start(side_ins, side_outs, side_sems)

        body(*ins, *outs, *scratch)

        @pl.when(last())
        def _():
            side.finish(side_ins, side_outs, side_sems)

    res = pl.pallas_call(
        carried, grid=grid, in_specs=list(in_specs) + _any_specs(s_in), out_specs=list(out_specs) + _any_specs(s_out),
        out_shape=list(out_shape) + list(side.outs), scratch_shapes=list(scratch_shapes) + list(side.sems),
        input_output_aliases={n_in + a: n_out + b for a, b in side.alias.items()},
        name=name, compiler_params=_cp(n_axes))(*args, *side.ins)
    return res[:n_out], res[n_out:]


def _mm_cols(a, b, out_dtype, name, transposed_b=False, side=None):
    t, k = a.shape
    g = b.shape[0]
    nb = b.shape[1] if transposed_b else b.shape[2]
    tm = _tile(t, ROW_TILE)
    steps = t // tm

    def body(a_ref, b_ref, o_ref):
        if transposed_b:
            acc = lax.dot_general(a_ref[...], b_ref[...], _NT, preferred_element_type=F32)
        else:
            acc = jnp.dot(a_ref[...], b_ref[...], preferred_element_type=F32)
        o_ref[...] = acc.astype(o_ref.dtype)

    first = lambda: jnp.logical_and(pl.program_id(0) == 0, pl.program_id(1) == 0)
    last = lambda: jnp.logical_and(pl.program_id(0) == g - 1, pl.program_id(1) == steps - 1)
    (out,), side_out = _call(
        body, first, last, side, grid=(g, steps),
        in_specs=[pl.BlockSpec((tm, k), lambda gi, i: (i, 0)),
                  pl.BlockSpec((None,) + b.shape[1:], lambda gi, i: (gi, 0, 0))],
        out_specs=[pl.BlockSpec((None, tm, nb), lambda gi, i: (gi, i, 0))],
        out_shape=[_sds((g, t, nb), out_dtype)], scratch_shapes=[], name=name, args=(a, b))
    return out if side is None else (out, side_out)


def _mm_nt_acc(a3, b3, res, kb, name, side=None, b_is_kn=False):
    ga, t, ka = a3.shape
    gb, n, kbb = (b3.shape[0], b3.shape[2], b3.shape[1]) if b_is_kn else b3.shape
    na, nbk = ka // kb, kbb // kb
    groups = ga * na
    assert groups == gb * nbk
    tm = _tile(t, ROW_TILE)
    steps = t // tm

    def body(a_ref, b_ref, res_ref, o_ref, acc):
        u = pl.program_id(1)

        @pl.when(u == 0)
        def _():
            acc[...] = ALPHA * res_ref[...]

        if b_is_kn:
            acc[...] += jnp.dot(a_ref[...], b_ref[...], preferred_element_type=F32)
        else:
            acc[...] += lax.dot_general(a_ref[...], b_ref[...], _NT, preferred_element_type=F32)

        @pl.when(u == groups - 1)
        def _():
            o_ref[...] = acc[...]

    first = lambda: jnp.logical_and(pl.program_id(0) == 0, pl.program_id(1) == 0)
    last = lambda: jnp.logical_and(pl.program_id(0) == steps - 1, pl.program_id(1) == groups - 1)
    if b_is_kn:
        b_spec = pl.BlockSpec((None, kb, n), lambda i, u: (u // nbk, u % nbk, 0))
    else:
        b_spec = pl.BlockSpec((None, n, kb), lambda i, u: (u // nbk, 0, u % nbk))
    (out,), side_out = _call(
        body, first, last, side, grid=(steps, groups),
        in_specs=[pl.BlockSpec((None, tm, kb), lambda i, u: (u // na, i, u % na)), b_spec,
                  pl.BlockSpec((tm, n), lambda i, u: (i, 0))],
        out_specs=[pl.BlockSpec((tm, n), lambda i, u: (i, 0))],
        out_shape=[_sds((t, n), F32)], scratch_shapes=[pltpu.VMEM((tm, n), F32)], name=name, args=(a3, b3, res))
    return out if side is None else (out, side_out)


def _mm_tn(x3, dy3, out_shape, bm, bn, groups, x_idx, dy_idx, out_idx, name, side=None):
    t = x3.shape[1]
    tm = _tile(t, 2 * ROW_TILE)
    grid = (groups, x_idx[1], t // tm)

    def body(x_ref, dy_ref, o_ref):
        @pl.when(pl.program_id(2) == 0)
        def _():
            o_ref[...] = jnp.zeros_like(o_ref)

        o_ref[...] += lax.dot_general(x_ref[...], dy_ref[...], _TN, preferred_element_type=F32)

    def at(corner):
        hit = pl.program_id(0) == corner[0]
        for axis in (1, 2):
            hit = jnp.logical_and(hit, pl.program_id(axis) == corner[axis])
        return hit

    (out,), side_out = _call(
        body, lambda: at((0, 0, 0)), lambda: at(tuple(g - 1 for g in grid)), side, grid=grid,
        in_specs=[pl.BlockSpec((None, tm, bm), lambda u, mb, i: (x_idx[0](u), i, x_idx[2](u, mb))),
                  pl.BlockSpec((None, tm, bn), lambda u, mb, i: (dy_idx[0](u), i, dy_idx[1](u)))],
        out_specs=[pl.BlockSpec((None, bm, bn), lambda u, mb, i: (out_idx[0](u), out_idx[1](u, mb), out_idx[2](u)))],
        out_shape=[_sds(out_shape, F32)], scratch_shapes=[], name=name, args=(x3, dy3))
    return out if side is None else (out, side_out)


def _layer_norm_rows(r, gamma, beta):
    mu = jnp.mean(r, axis=-1, keepdims=True)
    xc = r - mu
    var = jnp.mean(xc * xc, axis=-1, keepdims=True)
    return xc * lax.rsqrt(var + LN_EPS) * gamma + beta


def _mm_res_ln(a3, w3, res, gamma, beta, name, side=None):
    g, t, kb = a3.shape
    d = w3.shape[2]
    tm = _tile(t, LN_ROW_TILE)
    steps = t // tm

    def body(a_ref, w_hbm, res_ref, g_ref, b_ref, r_ref, o_ref, ob_ref, w_vmem, sem):
        @pl.when(pl.program_id(0) == 0)
        def _():
            cp = pltpu.make_async_copy(w_hbm, w_vmem, sem)
            cp.start()
            cp.wait()

        acc = ALPHA * res_ref[...]
        for gi in range(g):
            acc = acc + jnp.dot(a_ref[gi], w_vmem[gi], preferred_element_type=F32)
        r_ref[...] = acc
        out = _layer_norm_rows(acc, g_ref[...], b_ref[...])
        o_ref[...] = out
        ob_ref[...] = out.astype(BF16)

    row = pl.BlockSpec((tm, d), lambda i: (i, 0))
    vec = pl.BlockSpec((1, d), lambda i: (0, 0))
    outs, side_out = _call(
        body, lambda: pl.program_id(0) == 0, lambda: pl.program_id(0) == steps - 1, side, grid=(steps,),
        in_specs=[pl.BlockSpec((g, tm, kb), lambda i: (0, i, 0)), pl.BlockSpec(memory_space=pl.ANY), row, vec, vec],
        out_specs=[row, row, row],
        out_shape=[_sds((t, d), F32), _sds((t, d), F32), _sds((t, d), BF16)],
        scratch_shapes=[pltpu.VMEM(w3.shape, w3.dtype), pltpu.SemaphoreType.DMA],
        name=name, args=(a3, w3, res, gamma, beta))
    return outs if side is None else (outs, side_out)


def _ln_bwd(dout, r, gamma, name, side=None):
    t, d = r.shape
    tm = _tile(t, ROW_TILE)

    def body(do_ref, r_ref, g_ref, dr_ref, drb_ref, dg_ref, db_ref):
        @pl.when(pl.program_id(0) == 0)
        def _():
            dg_ref[...] = jnp.zeros_like(dg_ref)
            db_ref[...] = jnp.zeros_like(db_ref)

        rr = r_ref[...]
        do = do_ref[...]
        mu = jnp.mean(rr, axis=-1, keepdims=True)
        xc = rr - mu
        rstd = lax.rsqrt(jnp.mean(xc * xc, axis=-1, keepdims=True) + LN_EPS)
        xhat = xc * rstd
        dxh = do * g_ref[...]
        m1 = jnp.mean(dxh, axis=-1, keepdims=True)
        m2 = jnp.mean(dxh * xhat, axis=-1, keepdims=True)
        dr = rstd * (dxh - m1 - xhat * m2)
        dr_ref[...] = dr
        drb_ref[...] = dr.astype(BF16)
        dg_ref[...] += jnp.sum(do * xhat, axis=0, keepdims=True)
        db_ref[...] += jnp.sum(do, axis=0, keepdims=True)

    row = pl.BlockSpec((tm, d), lambda i: (i, 0))
    vec = pl.BlockSpec((1, d), lambda i: (0, 0))
    steps = t // tm
    outs, side_out = _call(
        body, lambda: pl.program_id(0) == 0, lambda: pl.program_id(0) == steps - 1, side, grid=(steps,),
        in_specs=[row, row, vec], out_specs=[row, row, vec, vec],
        out_shape=[_sds((t, d), F32), _sds((t, d), BF16), _sds((1, d), F32), _sds((1, d), F32)],
        scratch_shapes=[], name=name, args=(dout, r, gamma))
    return outs if side is None else (outs, side_out)


def _loss_and_grad(y, target, name):
    t, d = y.shape
    tm = _tile(t, ROW_TILE)
    steps = t // tm

    def body(y_ref, t_ref, loss_ref, dy_ref, acc):
        i = pl.program_id(0)

        @pl.when(i == 0)
        def _():
            acc[...] = jnp.zeros_like(acc)

        diff = y_ref[...] - t_ref[...]
        dy_ref[...] = diff * (1.0 / d)
        acc[...] += jnp.sum(diff * diff, axis=0, keepdims=True)

        @pl.when(i == steps - 1)
        def _():
            total = jnp.sum(acc[...], axis=1, keepdims=True) * (0.5 / d)
            loss_ref[...] = jnp.broadcast_to(total, loss_ref.shape)

    row = pl.BlockSpec((tm, d), lambda i: (i, 0))
    return pl.pallas_call(
        body, grid=(steps,), in_specs=[row, row],
        out_specs=[pl.BlockSpec((1, LANES), lambda i: (0, 0)), row],
        out_shape=[_sds((1, LANES), F32), _sds((t, d), F32)],
        scratch_shapes=[pltpu.VMEM((1, d), F32)], name=name, compiler_params=_cp(1))(y, target)


def _window_sums(ext, window, forward):
    n = ext.shape[0]
    s, span = ext, 1
    while span < window:
        s = s + pltpu.roll(s, (n - span) if forward else span, 0)
        span *= 2
    return s


def _pooled_group(main, halo, gi, row0):
    window = POOL_WINDOWS[gi]
    ext = jnp.concatenate([halo, main], axis=0)
    sums = _window_sums(ext, window, forward=False)[POOL_HALO:, :]
    pos = row0 + lax.broadcasted_iota(jnp.int32, (main.shape[0], 1), 0)
    cnt = jnp.minimum(pos + 1, window).astype(F32)
    return sums / cnt - main


def _pool_specs(t, d, tm):
    per = tm // POOL_HALO
    main = pl.BlockSpec((tm, d), lambda i: (i, 0))
    before = pl.BlockSpec((POOL_HALO, d), lambda i: (jnp.maximum(i * per - 1, 0), 0))
    return main, before


def _pool_fwd(x, w, scale, gamma, beta, name):
    t, d = x.shape
    ng, cg = w.shape[0], w.shape[1]
    tm = _tile(t, LN_ROW_TILE)

    def body(x_ref, h_ref, w_ref, s_ref, g_ref, b_ref, r_ref, o_ref, ob_ref):
        i = pl.program_id(0)
        for gi in range(ng):
            cols = pl.ds(gi * cg, cg)
            main = x_ref[:, cols]
            halo = jnp.where(i > 0, h_ref[:, cols], 0.0)
            pooled = _pooled_group(main, halo, gi, i * tm)
            y = jnp.dot(pooled.astype(BF16), w_ref[gi], preferred_element_type=F32)
            r_ref[:, cols] = ALPHA * main + y * s_ref[:, cols]
        out = _layer_norm_rows(r_ref[...], g_ref[...], b_ref[...])
        o_ref[...] = out
        ob_ref[...] = out.astype(BF16)

    main, before = _pool_specs(t, d, tm)
    vec = pl.BlockSpec((1, d), lambda i: (0, 0))
    return pl.pallas_call(
        body, grid=(t // tm,),
        in_specs=[main, before, pl.BlockSpec(w.shape, lambda i: (0, 0, 0)), vec, vec, vec],
        out_specs=[main, main, main],
        out_shape=[_sds((t, d), F32), _sds((t, d), F32), _sds((t, d), BF16)],
        name=name, compiler_params=_cp(1))(x, x, w, scale, gamma, beta)


def _pool_bwd(x, dy, w, scale, name, side=None):
    t, d = x.shape
    ng, cg = w.shape[0], w.shape[1]
    tm = _tile(t, LN_ROW_TILE)

    def body(x_ref, h_ref, dy_ref, w_ref, s_ref, dp_ref, dw_ref, ds_ref):
        i = pl.program_id(0)

        @pl.when(i == 0)
        def _():
            dw_ref[...] = jnp.zeros_like(dw_ref)
            ds_ref[...] = jnp.zeros_like(ds_ref)

        for gi in range(ng):
            cols = pl.ds(gi * cg, cg)
            main = x_ref[:, cols]
            halo = jnp.where(i > 0, h_ref[:, cols], 0.0)
            pooled = _pooled_group(main, halo, gi, i * tm).astype(BF16)
            y = jnp.dot(pooled, w_ref[gi], preferred_element_type=F32)
            dyg = dy_ref[:, cols]
            ds_ref[:, cols] += jnp.sum(dyg * y, axis=0, keepdims=True)
            dyw = (dyg * s_ref[:, cols]).astype(BF16)
            dw_ref[gi] += lax.dot_general(pooled, dyw, _TN, preferred_element_type=F32)
            dp_ref[:, cols] = lax.dot_general(dyw, w_ref[gi], _NT, preferred_element_type=F32)

    main, before = _pool_specs(t, d, tm)
    vec = pl.BlockSpec((1, d), lambda i: (0, 0))
    wspec = pl.BlockSpec(w.shape, lambda i: (0, 0, 0))
    steps = t // tm
    outs, side_out = _call(
        body, lambda: pl.program_id(0) == 0, lambda: pl.program_id(0) == steps - 1, side, grid=(steps,),
        in_specs=[main, before, main, wspec, vec], out_specs=[main, wspec, vec],
        out_shape=[_sds((t, d), F32), _sds(w.shape, F32), _sds((1, d), F32)],
        scratch_shapes=[], name=name, args=(x, x, dy, w, scale))
    return outs if side is None else (outs, side_out)


def _pool_adjoint(dp, dres, n_groups, name):
    t, d = dp.shape
    cg = d // n_groups
    tm = _tile(t, ROW_TILE)
    steps = t // tm
    per = tm // POOL_HALO

    def body(dp_ref, after_ref, dres_ref, dx_ref):
        i = pl.program_id(0)
        rows = lax.broadcasted_iota(jnp.int32, (tm, 1), 0)
        rows_after = lax.broadcasted_iota(jnp.int32, (POOL_HALO, 1), 0)
        for gi in range(n_groups):
            window = POOL_WINDOWS[gi]
            cols = pl.ds(gi * cg, cg)
            main = dp_ref[:, cols]
            cnt = jnp.minimum(i * tm + rows + 1, window).astype(F32)
            cnt_after = jnp.minimum((i + 1) * tm + rows_after + 1, window).astype(F32)
            after = jnp.where(i < steps - 1, after_ref[:, cols] / cnt_after, 0.0)
            ext = jnp.concatenate([main / cnt, after], axis=0)
            sums = _window_sums(ext, window, forward=True)[:tm, :]
            dx_ref[:, cols] = ALPHA * dres_ref[:, cols] + sums - main

    main = pl.BlockSpec((tm, d), lambda i: (i, 0))
    after = pl.BlockSpec((POOL_HALO, d), lambda i: (jnp.minimum((i + 1) * per, t // POOL_HALO - 1), 0))
    return pl.pallas_call(
        body, grid=(steps,), in_specs=[main, after, main], out_specs=main,
        out_shape=_sds((t, d), F32), name=name, compiler_params=_cp(1))(dp, dp, dres)


def _split_dot(x, tri):
    hi = x.astype(BF16)
    lo = (x - hi.astype(F32)).astype(BF16)
    return jnp.dot(hi, tri, preferred_element_type=F32) + jnp.dot(lo, tri, preferred_element_type=F32)


def _att_windows(qs, k_ws, limit, carry_rests, suffix):
    heads = range(len(qs))
    zs = [lax.dot_general(qs[hh], k_ws[hh], _NT, preferred_element_type=F32) * ATT_SCALE for hh in heads]
    visible = lax.broadcasted_iota(jnp.int32, zs[0].shape, 1) < limit
    zs = [jnp.where(visible, z, -MASKED) for z in zs]
    es = [jnp.exp(-jnp.abs(z)) for z in zs]
    log_nots = [-(jnp.maximum(z, 0.0) + jnp.log(1.0 + e)) for z, e in zip(zs, es)]
    rests = [_split_dot(ln, suffix[...]) + carry for ln, carry in zip(log_nots, carry_rests)]
    weights = [jnp.exp(z + r) for z, r in zip(zs, rests)]
    return zs, es, log_nots, weights


def _tri(w, strict):
    r = lax.broadcasted_iota(jnp.int32, (w, w), 0)
    c = lax.broadcasted_iota(jnp.int32, (w, w), 1)
    return ((r > c) if strict else (r >= c)).astype(BF16)


def _heads_per_step(qkv3, n_heads, most):
    cpb = qkv3.shape[2] // HEAD_DIM
    hp = most
    while cpb % hp or n_heads % hp:
        hp //= 2
    return hp


def _att_specs(qkv3, n_heads, hp):
    t = qkv3.shape[1]
    cpb = qkv3.shape[2] // HEAD_DIM
    wd = hp * HEAD_DIM

    def slab(off):
        return pl.BlockSpec((None, t, wd), lambda g, i: ((off + g * hp) // cpb, 0, ((off + g * hp) % cpb) // hp))

    q = pl.BlockSpec((None, ATT_BLOCK, wd), lambda g, i: ((g * hp) // cpb, i, ((g * hp) % cpb) // hp))
    return q, slab(n_heads), slab(2 * n_heads)


def _head_cols(hh):
    return pl.ds(hh * HEAD_DIM, HEAD_DIM)


def _key_bounds(k_ref, kmax, hp):
    for hh in range(hp):
        kf = k_ref[:, _head_cols(hh)].astype(F32)
        kmax[hh] = jnp.sqrt(jnp.max(jnp.sum(kf * kf, axis=1, keepdims=True)))


def _score_bound(q, key_norm):
    qf = q.astype(F32)
    return ATT_SCALE * 1.001 * key_norm * jnp.sqrt(jnp.sum(qf * qf, axis=1, keepdims=True)) + 1e-3


def _any_alive(rests, bounds):
    alive = jnp.max(rests[0] + bounds[0]) > -EXP_ZERO
    for r, zb in zip(rests[1:], bounds[1:]):
        alive = jnp.logical_or(alive, jnp.max(r + zb) > -EXP_ZERO)
    return alive


def _window_rows(hi, w):
    start = jnp.maximum(hi - w, 0)
    return start, pl.ds(pl.multiple_of(start, ATT_BLOCK), w)


def _attn_fwd(qkv3, n_heads, name):
    t = qkv3.shape[1]
    b = ATT_BLOCK
    w = min(ATT_WINDOW, t)
    hp = _heads_per_step(qkv3, n_heads, ATT_FWD_HEADS)
    heads = range(hp)

    def body(q_ref, k_ref, v_ref, o_ref, kmax, suffix):
        i = pl.program_id(1)

        @pl.when(i == 0)
        def _():
            _key_bounds(k_ref, kmax, hp)
            suffix[...] = _tri(w, strict=False)

        qs = [q_ref[:, _head_cols(hh)] for hh in heads]
        bounds = [_score_bound(qs[hh], kmax[hh]) for hh in heads]
        qpos = i * b + lax.broadcasted_iota(jnp.int32, (b, 1), 0)

        def cond(c):
            return jnp.logical_and(c[0] > 0, _any_alive(c[1], bounds))

        def step(c):
            hi, rests, accs = c
            start, rows = _window_rows(hi, w)
            limit = jnp.minimum(qpos, hi) - start
            k_ws = [k_ref[rows, _head_cols(hh)] for hh in heads]
            _, _, log_nots, weights = _att_windows(qs, k_ws, limit, rests, suffix)
            new_accs = tuple(accs[hh] + jnp.dot(weights[hh].astype(BF16), v_ref[rows, _head_cols(hh)],
                                                preferred_element_type=F32) for hh in heads)
            new_rests = tuple(rests[hh] + jnp.sum(log_nots[hh], axis=1, keepdims=True) for hh in heads)
            return start, new_rests, new_accs

        init = ((i + 1) * b, tuple(jnp.zeros((b, 1), F32) for _ in heads),
                tuple(jnp.zeros((b, HEAD_DIM), F32) for _ in heads))
        _, _, accs = lax.while_loop(cond, step, step(init))
        for hh in heads:
            o_ref[:, _head_cols(hh)] = accs[hh].astype(o_ref.dtype)

    qs_, ks_, vs_ = _att_specs(qkv3, n_heads, hp)
    return pl.pallas_call(
        body, grid=(n_heads // hp, t // b), in_specs=[qs_, ks_, vs_],
        out_specs=pl.BlockSpec((b, hp * HEAD_DIM), lambda g, i: (i, g)),
        out_shape=_sds((t, n_heads * HEAD_DIM), BF16),
        scratch_shapes=[pltpu.SMEM((hp,), F32), pltpu.VMEM((w, w), BF16)],
        name=name, compiler_params=_cp(2))(qkv3, qkv3, qkv3)


def _attn_bwd(qkv3, do, n_heads, name):
    t = qkv3.shape[1]
    b = ATT_BLOCK
    w = min(ATT_WINDOW, t)
    nq = t // b
    hp = _heads_per_step(qkv3, n_heads, ATT_BWD_HEADS)
    heads = range(hp)
    wd = hp * HEAD_DIM

    def body(q_ref, k_ref, v_ref, do_ref, dq_ref, dk_ref, dv_ref, kmax, dk_acc, dv_acc, suffix, strict_suffix):
        i = pl.program_id(1)

        @pl.when(i == 0)
        def _():
            _key_bounds(k_ref, kmax, hp)
            dk_acc[...] = jnp.zeros_like(dk_acc)
            dv_acc[...] = jnp.zeros_like(dv_acc)
            suffix[...] = _tri(w, strict=False)
            strict_suffix[...] = _tri(w, strict=True)

        qs = [q_ref[:, _head_cols(hh)] for hh in heads]
        douts = [do_ref[:, _head_cols(hh)] for hh in heads]
        bounds = [_score_bound(qs[hh], kmax[hh]) for hh in heads]
        zero_cols = tuple(jnp.zeros((b, 1), F32) for _ in heads)
        hi0 = (i + 1) * b
        qpos = i * b + lax.broadcasted_iota(jnp.int32, (b, 1), 0)

        def cond(c):
            return jnp.logical_and(c[0] > 0, _any_alive(c[1], bounds))

        def tiles_of(hi, rests):
            start, rows = _window_rows(hi, w)
            k_ws = [k_ref[rows, _head_cols(hh)] for hh in heads]
            dps = [lax.dot_general(douts[hh], v_ref[rows, _head_cols(hh)], _NT, preferred_element_type=F32)
                   for hh in heads]
            zs, es, log_nots, weights = _att_windows(qs, k_ws, jnp.minimum(qpos, hi) - start, rests, suffix)
            dlas = [a * dp for a, dp in zip(weights, dps)]
            return start, rows, (k_ws, zs, es, log_nots, weights, dlas)

        def row_sums(carries, tiles):
            return tuple(c + jnp.sum(x, axis=1, keepdims=True) for c, x in zip(carries, tiles))

        def first_pass(rows, tiles, rests, totals):
            _, _, _, log_nots, weights, dlas = tiles
            for hh in heads:
                dv_acc[rows, _head_cols(hh)] += lax.dot_general(weights[hh].astype(BF16), douts[hh], _TN,
                                                                preferred_element_type=F32)
            return row_sums(rests, log_nots), row_sums(totals, dlas)

        def second_pass(rows, tiles, totals, laters, dqs):
            k_ws, zs, es, _, _, dlas = tiles
            insides = [_split_dot(dla, strict_suffix[...]) for dla in dlas]
            dzs = []
            for hh in heads:
                dlog_not = totals[hh] - laters[hh] - insides[hh]
                inv = 1.0 / (1.0 + es[hh])
                sig = jnp.where(zs[hh] >= 0, inv, es[hh] * inv)
                dzs.append(((dlas[hh] - sig * dlog_not) * ATT_SCALE).astype(BF16))
            new_dqs = tuple(dqs[hh] + jnp.dot(dzs[hh], k_ws[hh], preferred_element_type=F32) for hh in heads)
            for hh in heads:
                dk_acc[rows, _head_cols(hh)] += lax.dot_general(dzs[hh], qs[hh], _TN, preferred_element_type=F32)
            return row_sums(laters, dlas), new_dqs

        start0, rows0, tiles0 = tiles_of(hi0, zero_cols)
        rests1, totals1 = first_pass(rows0, tiles0, zero_cols, zero_cols)

        def sweep1(c):
            hi, rests, totals = c
            start, rows, tiles = tiles_of(hi, rests)
            return (start,) + first_pass(rows, tiles, rests, totals)

        _, _, totals = lax.while_loop(cond, sweep1, (start0, rests1, totals1))
        laters1, dqs1 = second_pass(rows0, tiles0, totals, zero_cols,
                                    tuple(jnp.zeros((b, HEAD_DIM), F32) for _ in heads))

        def sweep2(c):
            hi, rests, laters, dqs = c
            start, rows, tiles = tiles_of(hi, rests)
            return (start, row_sums(rests, tiles[3])) + second_pass(rows, tiles, totals, laters, dqs)

        _, _, _, dqs = lax.while_loop(cond, sweep2, (start0, rests1, laters1, dqs1))
        for hh in heads:
            dq_ref[:, _head_cols(hh)] = dqs[hh].astype(dq_ref.dtype)

        @pl.when(i == nq - 1)
        def _():
            dk_ref[...] = dk_acc[...].astype(dk_ref.dtype)
            dv_ref[...] = dv_acc[...].astype(dv_ref.dtype)

    qs_, ks_, vs_ = _att_specs(qkv3, n_heads, hp)
    blk = pl.BlockSpec((b, wd), lambda g, i: (i, g))
    slab = pl.BlockSpec((t, wd), lambda g, i: (0, g))
    d = n_heads * HEAD_DIM
    return pl.pallas_call(
        body, grid=(n_heads // hp, nq),
        in_specs=[qs_, ks_, vs_, pl.BlockSpec((None, b, wd), lambda g, i: (0, i, g))],
        out_specs=[blk, slab, slab],
        out_shape=[_sds((t, d), BF16)] * 3,
        scratch_shapes=[pltpu.SMEM((hp,), F32), pltpu.VMEM((t, wd), F32), pltpu.VMEM((t, wd), F32),
                        pltpu.VMEM((w, w), BF16), pltpu.VMEM((w, w), BF16)],
        name=name, compiler_params=_cp(2))(qkv3, qkv3, qkv3, do)


def _conv_rows(main, halo, w_ref, b_ref):
    ext = jnp.concatenate([halo, main], axis=0)
    h1 = pltpu.roll(ext, 1, 0)[CONV_HALO:, :]
    h2 = pltpu.roll(ext, 2, 0)[CONV_HALO:, :]
    hc = b_ref[...] + w_ref[0:1, :] * h2
    hc = hc + w_ref[1:2, :] * h1
    hc = hc + w_ref[2:3, :] * main
    return hc, h1, h2


def _ffn_specs(t, fp, tm, half):
    per = tm // CONV_HALO
    main = lambda off: pl.BlockSpec((None, tm, fp), lambda g, i: (g + off, i, 0))
    before = lambda off: pl.BlockSpec((None, CONV_HALO, fp), lambda g, i: (g + off, jnp.maximum(i * per - 1, 0), 0))
    cw = lambda off: pl.BlockSpec((None, 3, fp), lambda g, i: (g + off, 0, 0))
    cb = lambda off: pl.BlockSpec((None, 1, fp), lambda g, i: (g + off, 0, 0))
    return [main(0), before(0), main(half), before(half), cw(0), cw(half), cb(0), cb(half)]


def _ffn_act(h, cw, cb, name, side=None):
    n, t, fp = h.shape
    half = n // 2
    tm = _tile(t, FFN_ROW_TILE)
    steps = t // tm

    def body(hg_ref, hgb_ref, hv_ref, hvb_ref, wg_ref, wv_ref, bg_ref, bv_ref, a_ref):
        first = pl.program_id(1) == 0
        gate, _, _ = _conv_rows(hg_ref[...], jnp.where(first, 0.0, hgb_ref[...]), wg_ref, bg_ref)
        val, _, _ = _conv_rows(hv_ref[...], jnp.where(first, 0.0, hvb_ref[...]), wv_ref, bv_ref)
        a_ref[...] = (gate * jax.nn.sigmoid(gate) * val).astype(a_ref.dtype)

    (a,), side_out = _call(
        body, lambda: jnp.logical_and(pl.program_id(0) == 0, pl.program_id(1) == 0),
        lambda: jnp.logical_and(pl.program_id(0) == half - 1, pl.program_id(1) == steps - 1), side,
        grid=(half, steps), in_specs=_ffn_specs(t, fp, tm, half),
        out_specs=[pl.BlockSpec((None, tm, fp), lambda g, i: (g, i, 0))],
        out_shape=[_sds((half, t, fp), BF16)], scratch_shapes=[], name=name, args=(h, h, h, h, cw, cw, cb, cb))
    return a if side is None else (a, side_out)


def _act_grads(dact, gate, val):
    sig = jax.nn.sigmoid(gate)
    return dact * val * (sig * (1.0 + gate * (1.0 - sig))), dact * (gate * sig)


def _ffn_act_bwd(h, da, cw, cb, name):
    n, t, fp = h.shape
    half = n // 2
    tm = _tile(t, FFN_ROW_TILE)

    def body(hg_ref, hgb_ref, hv_ref, hvb_ref, wg_ref, wv_ref, bg_ref, bv_ref, da_ref, dhc_ref, dw_ref, db_ref):
        first = pl.program_id(1) == 0

        @pl.when(first)
        def _():
            dw_ref[...] = jnp.zeros_like(dw_ref)
            db_ref[...] = jnp.zeros_like(db_ref)

        hg, hv = hg_ref[...], hv_ref[...]
        gate, hg1, hg2 = _conv_rows(hg, jnp.where(first, 0.0, hgb_ref[...]), wg_ref, bg_ref)
        val, hv1, hv2 = _conv_rows(hv, jnp.where(first, 0.0, hvb_ref[...]), wv_ref, bv_ref)
        dgate, dval = _act_grads(da_ref[...], gate, val)
        dhc_ref[0] = dgate
        dhc_ref[1] = dval
        for s, (dd, shifted) in enumerate(((dgate, (hg2, hg1, hg)), (dval, (hv2, hv1, hv)))):
            db_ref[s] += jnp.sum(dd, axis=0, keepdims=True)
            for kk in range(3):
                dw_ref[s, kk:kk + 1, :] += jnp.sum(dd * shifted[kk], axis=0, keepdims=True)

    specs = _ffn_specs(t, fp, tm, half) + [pl.BlockSpec((None, tm, fp), lambda g, i: (g, i, 0))]
    return pl.pallas_call(
        body, grid=(half, t // tm), in_specs=specs,
        out_specs=[pl.BlockSpec((2, None, tm, fp), lambda g, i: (0, g, i, 0)),
                   pl.BlockSpec((2, None, 3, fp), lambda g, i: (0, g, 0, 0)),
                   pl.BlockSpec((2, None, 1, fp), lambda g, i: (0, g, 0, 0))],
        out_shape=[_sds((2, half, t, fp), F32), _sds((2, half, 3, fp), F32), _sds((2, half, 1, fp), F32)],
        name=name, compiler_params=_cp(2))(h, h, h, h, cw, cw, cb, cb, da)


def _conv_adjoint(dhc, cw, name):
    n, t, fp = dhc.shape
    tm = _tile(t, FFN_ROW_TILE)
    steps = t // tm
    per = tm // CONV_HALO

    def body(d_ref, after_ref, w_ref, o_ref):
        main = d_ref[...]
        after = jnp.where(pl.program_id(1) < steps - 1, after_ref[...], 0.0)
        ext = jnp.concatenate([main, after], axis=0)
        rows = ext.shape[0]
        d1 = pltpu.roll(ext, rows - 1, 0)[:tm, :]
        d2 = pltpu.roll(ext, rows - 2, 0)[:tm, :]
        o_ref[...] = (w_ref[2:3, :] * main + w_ref[1:2, :] * d1 + w_ref[0:1, :] * d2).astype(o_ref.dtype)

    main = pl.BlockSpec((None, tm, fp), lambda g, i: (g, i, 0))
    after = pl.BlockSpec((None, CONV_HALO, fp), lambda g, i: (g, jnp.minimum((i + 1) * per, t // CONV_HALO - 1), 0))
    return pl.pallas_call(
        body, grid=(n, steps), in_specs=[main, after, pl.BlockSpec((None, 3, fp), lambda g, i: (g, 0, 0))],
        out_specs=main, out_shape=_sds((n, t, fp), BF16), name=name, compiler_params=_cp(2))(dhc, dhc, cw)


def _place():
    x, y, c = lax.axis_index("x"), lax.axis_index("y"), lax.axis_index("c")
    chips = [(1 - x, y), (x, 1 - y), (1 - x, 1 - y)]
    return x, y, c, chips


def _run_sides(sides, name):
    n_in = [len(s.ins) for s in sides]
    n_out = [len(s.outs) for s in sides]
    n_sem = [len(s.sems) for s in sides]

    def body(*refs):
        ins, outs, sems = refs[:sum(n_in)], refs[sum(n_in):sum(n_in) + sum(n_out)], refs[sum(n_in) + sum(n_out):]
        oi = oo = os_ = 0
        for k, s in enumerate(sides):
            mine = (ins[oi:oi + n_in[k]], outs[oo:oo + n_out[k]], sems[os_:os_ + n_sem[k]])
            s.start(*mine)
            s.finish(*mine)
            oi, oo, os_ = oi + n_in[k], oo + n_out[k], os_ + n_sem[k]

    aliases, oi, oo = {}, 0, 0
    for k, s in enumerate(sides):
        aliases.update({oi + a: oo + b for a, b in s.alias.items()})
        oi, oo = oi + n_in[k], oo + n_out[k]
    return pl.pallas_call(
        body, in_specs=_any_specs(sum(n_in)), out_specs=_any_specs(sum(n_out)),
        out_shape=[o for s in sides for o in s.outs], input_output_aliases=aliases,
        scratch_shapes=[q for s in sides for q in s.sems], name=name)(*[a for s in sides for a in s.ins])


def _place_shard(kind, w, chip, name, rows=None, base=None, layer=None):
    if kind == "pool":
        g, r, cdim = w.shape

        def body(chip_ref, w_ref, o_ref):
            del chip_ref
            o_ref[...] = w_ref[...].astype(BF16)

        return pl.pallas_call(
            body,
            grid_spec=pltpu.PrefetchScalarGridSpec(
                num_scalar_prefetch=1, grid=(1,),
                in_specs=[pl.BlockSpec((g, r, cdim), lambda i, chip_ref: (0, 0, 0))],
                out_specs=pl.BlockSpec((g, r, cdim), lambda i, chip_ref: (0, chip_ref[0], 0))),
            out_shape=_sds((g, 4 * r, cdim), BF16), name=name, compiler_params=_cp(1))(chip, w)

    if kind == "lead" and w.ndim == 3:
        r, cs = w.shape[1:]
        src = lambda n_src: pl.BlockSpec((None, tr, cs), lambda i, chip_ref: (layer, jnp.minimum(i, n_src - 1), 0))
    else:
        r, cs = w.shape
        src = lambda n_src: pl.BlockSpec((tr, cs), lambda i, chip_ref: (jnp.minimum(i, n_src - 1), 0))
    if kind == "lead":
        rows = rows or r
        tr = _tile(r, ROW_TILE, 16) if rows == r else rows - r
        assert r % tr == 0 and tr % 16 == 0
        n_src = r // tr

        def body(chip_ref, w_ref, o_ref):
            del chip_ref
            o_ref[...] = jnp.where(pl.program_id(0) < n_src, w_ref[...], 0.0).astype(BF16)

        return pl.pallas_call(
            body,
            grid_spec=pltpu.PrefetchScalarGridSpec(
                num_scalar_prefetch=1, grid=(rows // tr,), in_specs=[src(n_src)],
                out_specs=pl.BlockSpec((None, tr, cs), lambda i, chip_ref: (chip_ref[0], i, 0))),
            out_shape=_sds((4, rows, cs), BF16), name=name, compiler_params=_cp(1))(chip, w)

    assert kind == "down"
    tr = r // 2 if (r // 2) % 16 == 0 else r
    per = r // tr

    def body(chip_ref, w_ref, base_ref, o_ref):
        del chip_ref, base_ref
        o_ref[...] = w_ref[...].astype(BF16)

    return pl.pallas_call(
        body,
        grid_spec=pltpu.PrefetchScalarGridSpec(
            num_scalar_prefetch=1, grid=(per,),
            in_specs=[pl.BlockSpec((tr, cs), lambda i, chip_ref: (i, 0)), pl.BlockSpec(memory_space=pl.ANY)],
            out_specs=pl.BlockSpec((None, tr, cs), lambda i, chip_ref: (chip_ref[0] // 2, (chip_ref[0] % 2) * per + i, 0))),
        out_shape=_sds(base.shape, BF16), input_output_aliases={2: 0},
        name=name, compiler_params=_cp(1))(chip, w, base)


def _gather_sides(items, bufs=None):
    n = len(items)
    kinds = [it[0] for it in items]
    shard_rows = [it[2] for it in items]
    bufs = [it[1] for it in items] if bufs is None else list(bufs)

    def half_of(outs, m, chip, half):
        k = 2 * chip[0] + chip[1]
        o, r = outs[m], shard_rows[m]
        if kinds[m] == "pool":
            gh = o.shape[0] // 2
            return o.at[pl.ds(half * gh, gh), pl.ds(k * r, r)]
        r2 = r // 2
        if kinds[m] == "down":
            return o.at[k // 2, pl.ds((k % 2) * r + half * r2, r2)]
        return o.at[k, pl.ds(half * r2, r2)]

    def remote(outs, sems, m, j, chip, half, to):
        ref = half_of(outs, m, chip, half)
        return pltpu.make_async_remote_copy(src_ref=ref, dst_ref=ref, send_sem=sems[0].at[m, j],
                                            recv_sem=sems[1].at[m, j], device_id=to, device_id_type=MESH)

    def ici_copies(outs, sems, sending):
        x, y, c, chips = _place()
        if sending:
            return [remote(outs, sems, m, j, (x, y), c, (*chip, c)) for m in range(n) for j, chip in enumerate(chips)]
        return [remote(outs, sems, m, j, chip, c, (x, y, c)) for m in range(n) for j, chip in enumerate(chips)]

    def d2d_copies(outs, sems, sending):
        x, y, c, chips = _place()
        if sending:
            return [remote(outs, sems, m, j, chip, c, (x, y, 1 - c)) for m in range(n) for j, chip in enumerate(chips)]
        return [remote(outs, sems, m, j, chip, 1 - c, (x, y, c)) for m in range(n) for j, chip in enumerate(chips)]

    def phase(copies):
        def start(ins, outs, sems):
            for cp in copies(outs, sems, True):
                cp.start()

        def finish(ins, outs, sems):
            for cp in copies(outs, sems, False):
                cp.wait_recv()
            for cp in copies(outs, sems, True):
                cp.wait_send()

        return start, finish

    ici, d2d = phase(ici_copies), phase(d2d_copies)

    def both_finish(ins, outs, sems):
        ici[1](ins, outs, sems[:2])
        d2d[0](ins, outs, sems[2:])
        d2d[1](ins, outs, sems[2:])

    pair = [pltpu.SemaphoreType.DMA((n, 3)), pltpu.SemaphoreType.DMA((n, 3))]
    shapes = [_sds(b.shape, b.dtype) for b in bufs]
    alias = {m: m for m in range(n)}

    def side(which):
        if which == "both":
            return _Side(bufs, shapes, alias, pair + pair, lambda i, o, s: ici[0](i, o, s[:2]), both_finish)
        start, finish = ici if which == "ici" else d2d
        return _Side(bufs, shapes, alias, pair, start, finish)

    return side


def _sibling_side(grads):
    n = len(grads)

    def copies(ins, outs, sems):
        x, y, c, _ = _place()
        res = []
        for m in range(n):
            r2 = ins[m].shape[1] // 2
            res.append(pltpu.make_async_remote_copy(
                src_ref=ins[m].at[:, pl.ds((1 - c) * r2, r2)], dst_ref=outs[m],
                send_sem=sems[0].at[m], recv_sem=sems[1].at[m], device_id=(x, y, 1 - c), device_id_type=MESH))
        return res

    def start(ins, outs, sems):
        for cp in copies(ins, outs, sems):
            cp.start()

    def finish(ins, outs, sems):
        for cp in copies(ins, outs, sems):
            cp.wait_recv()
        for cp in copies(ins, outs, sems):
            cp.wait_send()

    return _Side(list(grads), [_sds((4, g.shape[1] // 2, g.shape[2]), g.dtype) for g in grads], {},
                 [pltpu.SemaphoreType.DMA((n,)), pltpu.SemaphoreType.DMA((n,))], start, finish)


def _owner_chips_side(parts):
    n = len(parts)

    def copies(ins, outs, sems):
        _, _, c, chips = _place()
        return [pltpu.make_async_remote_copy(
            src_ref=ins[m].at[2 * chip[0] + chip[1]], dst_ref=outs[m].at[j], send_sem=sems[0].at[m, j],
            recv_sem=sems[1].at[m, j], device_id=(*chip, c), device_id_type=MESH)
            for m in range(n) for j, chip in enumerate(chips)]

    def start(ins, outs, sems):
        for cp in copies(ins, outs, sems):
            cp.start()

    def finish(ins, outs, sems):
        for cp in copies(ins, outs, sems):
            cp.wait_recv()
        for cp in copies(ins, outs, sems):
            cp.wait_send()

    return _Side(list(parts), [_sds((3,) + p.shape[1:], p.dtype) for p in parts], {},
                 [pltpu.SemaphoreType.DMA((n, 3)), pltpu.SemaphoreType.DMA((n, 3))], start, finish)


def _exchange_finished_halves(shards, name):
    n = len(shards)

    def body(*refs):
        out = refs[n:2 * n]
        send_sems, recv_sems = refs[2 * n:]
        x, y, c, _ = _place()
        copies = []
        for m in range(n):
            r2 = out[m].shape[0] // 2
            mine = out[m].at[pl.ds(c * r2, r2)]
            copies.append(pltpu.make_async_remote_copy(
                src_ref=mine, dst_ref=mine, send_sem=send_sems.at[m], recv_sem=recv_sems.at[m],
                device_id=(x, y, 1 - c), device_id_type=MESH))
        for cp in copies:
            cp.start()
        for m in range(n):
            r2 = out[m].shape[0] // 2
            theirs = out[m].at[pl.ds((1 - c) * r2, r2)]
            pltpu.make_async_remote_copy(
                src_ref=theirs, dst_ref=theirs, send_sem=send_sems.at[m], recv_sem=recv_sems.at[m],
                device_id=(x, y, 1 - c), device_id_type=MESH).wait_recv()
        for cp in copies:
            cp.wait_send()

    return pl.pallas_call(
        body, in_specs=_any_specs(n), out_specs=_any_specs(n), out_shape=[_sds(s.shape, s.dtype) for s in shards],
        input_output_aliases={m: m for m in range(n)},
        scratch_shapes=[pltpu.SemaphoreType.DMA((n,)), pltpu.SemaphoreType.DMA((n,))], name=name)(*shards)


def _all_reduce_small(v, name):
    rows = v.shape[0]

    def body(v_ref, out_ref, buf, send_sems, recv_sems, local_sem):
        x, y, c, chips = _place()
        me, sibling = (x, y, c), (x, y, 1 - c)

        def slot(px, py, pc):
            return buf.at[4 * px + 2 * py + pc]

        def copy(k, block, to, src=None):
            return pltpu.make_async_remote_copy(
                src_ref=slot(*block) if src is None else src, dst_ref=slot(*block),
                send_sem=send_sems.at[k], recv_sem=recv_sems.at[k], device_id=to, device_id_type=MESH)

        mine = pltpu.make_async_copy(v_ref, slot(*me), local_sem)
        mine.start()
        first = [copy(0, me, sibling, src=v_ref)]
        first += [copy(1 + j, me, (*chip, c), src=v_ref) for j, chip in enumerate(chips)]
        for cp in first:
            cp.start()
        passed = [copy(4 + j, (*chip, c), sibling) for j, chip in enumerate(chips)]
        for j, chip in enumerate(chips):
            copy(1 + j, (*chip, c), me).wait_recv()
            passed[j].start()
        copy(0, sibling, me).wait_recv()
        for j, chip in enumerate(chips):
            copy(4 + j, (*chip, 1 - c), me).wait_recv()
        for cp in first + passed:
            cp.wait_send()
        mine.wait()
        total = buf[0]
        for dev in range(1, 8):
            total = total + buf[dev]
        out_ref[...] = total

    vm = pl.BlockSpec(memory_space=pltpu.VMEM)
    return pl.pallas_call(
        body, in_specs=[vm], out_specs=vm, out_shape=_sds(v.shape, F32),
        scratch_shapes=[pltpu.VMEM((8, rows, LANES), F32), pltpu.SemaphoreType.DMA((7,)),
                        pltpu.SemaphoreType.DMA((7,)), pltpu.SemaphoreType.DMA],
        name=name, compiler_params=pltpu.CompilerParams(vmem_limit_bytes=VMEM_LIMIT))(v)


def _chip_partial(grad, from_sibling, core, name):
    _, r, cdim = grad.shape
    r2 = r // 2
    tr = _tile(r2, SUM_ROW_TILE)
    per = r2 // tr

    def body(core_ref, g_ref, s_ref, o_ref, ob_ref):
        del core_ref
        total = g_ref[...] + s_ref[...]
        o_ref[...] = total
        ob_ref[...] = total.astype(BF16)

    blk = pl.BlockSpec((None, tr, cdim), lambda k, i, core_ref: (k, i, 0))
    mine = pl.BlockSpec((None, tr, cdim), lambda k, i, core_ref: (k, core_ref[0] * per + i, 0))
    return pl.pallas_call(
        body,
        grid_spec=pltpu.PrefetchScalarGridSpec(num_scalar_prefetch=1, grid=(4, per), in_specs=[mine, blk],
                                               out_specs=[blk, blk]),
        out_shape=[_sds((4, r2, cdim), F32), _sds((4, r2, cdim), BF16)],
        name=name, compiler_params=_cp(2))(core, grad, from_sibling)


def _owner_sum(partial, from_chips, place, name):
    _, r2, cdim = partial.shape
    tr = _tile(r2, SUM_ROW_TILE)
    per = r2 // tr

    def body(place_ref, p_ref, f_ref, o_ref):
        del place_ref
        total = p_ref[...]
        for j in range(3):
            total = total + f_ref[j].astype(F32)
        o_ref[...] = total

    return pl.pallas_call(
        body,
        grid_spec=pltpu.PrefetchScalarGridSpec(
            num_scalar_prefetch=1, grid=(per,),
            in_specs=[pl.BlockSpec((None, tr, cdim), lambda i, place_ref: (place_ref[0], i, 0)),
                      pl.BlockSpec((3, tr, cdim), lambda i, place_ref: (0, i, 0))],
            out_specs=pl.BlockSpec((tr, cdim), lambda i, place_ref: (place_ref[1] * per + i, 0))),
        out_shape=_sds((2 * r2, cdim), F32), name=name, compiler_params=_cp(1))(place, partial, from_chips)


def _adamw(g, w, m, v, layer, prev, name):
    _, r, cdim = w.shape
    tr = _tile(r, OPT_ROW_TILE)
    c1 = 1.0 / (1.0 - ADAM_B1 ** ADAM_STEP)
    c2 = 1.0 / (1.0 - ADAM_B2 ** ADAM_STEP)
    n_prev = 0 if prev is None else 4

    def body(g_ref, w_ref, m_ref, v_ref, *rest):
        go_ref, d_ref, mo_ref, vo_ref = rest[n_prev:]
        grad = g_ref[:, pl.ds(0, cdim)]
        m_new = ADAM_B1 * m_ref[...] + (1.0 - ADAM_B1) * grad
        v_new = ADAM_B2 * v_ref[...] + (1.0 - ADAM_B2) * (grad * grad)
        go_ref[...] = grad
        mo_ref[...] = m_new
        vo_ref[...] = v_new
        d_ref[...] = -ADAM_LR * ((m_new * c1) / (jnp.sqrt(v_new * c2) + ADAM_EPS) + ADAM_WD * w_ref[...])

    blk = pl.BlockSpec((None, tr, cdim), lambda i: (layer, i, 0))
    gblk = pl.BlockSpec((tr, g.shape[1]), lambda i: (i, 0))
    return pl.pallas_call(
        body, grid=(r // tr,), in_specs=[gblk, blk, blk, blk] + _any_specs(n_prev), out_specs=[blk] * 4,
        out_shape=[_sds(w.shape, F32)] * 4, input_output_aliases={4 + k: k for k in range(n_prev)},
        name=name, compiler_params=_cp(1))(g, w, m, v, *(prev or ()))


def _pack_rows(vectors):
    flat = [v.reshape(-1) for v in vectors]
    sizes = [f.shape[0] for f in flat]
    total = sum(sizes)
    padded = _round_up(total, 8 * LANES)
    buf = jnp.concatenate(flat + [jnp.zeros((padded - total,), F32)])
    return buf.reshape(padded // LANES, LANES), sizes


def _unpack_rows(buf, sizes, shapes):
    flat = buf.reshape(-1)
    out, off = [], 0
    for n, shp in zip(sizes, shapes):
        out.append(flat[off:off + n].reshape(shp))
        off += n
    return out


def kernel(x, pool_w, pool_scale, attn_w_qkv, attn_w_o, ffn_w_up, ffn_conv_w, ffn_conv_b, ffn_w_down, ln_mix_g, ln_mix_b, ln_ffn_g, ln_ffn_b, loss_target, m_pool_w, m_pool_scale, m_attn_w_qkv, m_attn_w_o, m_ffn_w_up, m_ffn_conv_w, m_ffn_conv_b, m_ffn_w_down, m_ln_mix_g, m_ln_mix_b, m_ln_ffn_g, m_ln_ffn_b, v_pool_w, v_pool_scale, v_attn_w_qkv, v_attn_w_o, v_ffn_w_up, v_ffn_conv_w, v_ffn_conv_b, v_ffn_w_down, v_ln_mix_g, v_ln_mix_b, v_ln_ffn_g, v_ln_ffn_b):
    t, d = x.shape[1], x.shape[2]
    n_heads = d // HEAD_DIM
    n_groups = pool_w.shape[1]
    fs = ffn_w_up.shape[2]
    fp = _round_up(fs, LANES)
    rd = ffn_w_down.shape[1]
    assert 2 * rd == fs
    xi, yi, ci = lax.axis_index("x"), lax.axis_index("y"), lax.axis_index("c")
    chip = (2 * xi + yi).astype(jnp.int32)
    chip_arr, core_arr = chip.reshape(1), ci.astype(jnp.int32).reshape(1)
    place_arr = jnp.concatenate([chip_arr, core_arr])

    x2 = x.reshape(t, d)
    target = loss_target.reshape(t, d)
    pad_cols = lambda a: jnp.pad(a, [(0, 0)] * (a.ndim - 1) + [(0, fp - fs)])
    up_t = [jnp.transpose(a, (0, 2, 1)) for a in (ffn_w_up, m_ffn_w_up, v_ffn_w_up)]

    gather_items = []
    for i in range(DEPTH):
        j = i // 2
        items = []
        if i % 2 == 0:
            items.append(("pool", _place_shard("pool", pool_w[j], chip_arr, name="place_pool"), pool_w.shape[2]))
        else:
            items.append(("lead", _place_shard("lead", attn_w_qkv[j], chip_arr, name="place_qkv"), d))
            items.append(("lead", _place_shard("lead", attn_w_o[j], chip_arr, name="place_wo"), attn_w_o.shape[1]))
        items.append(("lead", _place_shard("lead", up_t[0], chip_arr, name="place_up", rows=fp, layer=i), fp))
        items.append(("down", _place_shard("down", ffn_w_down[i], chip_arr, name="place_down",
                                           base=jnp.zeros((2, fp, d), BF16)), rd))
        gather_items.append(items)
    weights = [None] * DEPTH
    weights[0] = _run_sides([_gather_sides(gather_items[0])("both")], name="gather_layer0")

    conv_b_all = pad_cols(ffn_conv_b.reshape(DEPTH, 4, 1, fs))
    cw_local = pad_cols(ffn_conv_w)
    slot = (jnp.arange(4, dtype=jnp.int32) == chip).astype(F32) * (1.0 - ci.astype(F32))
    cw_placed = slot[None, :, None, None] * cw_local[:, None]
    cw_buf, cw_sizes = _pack_rows([cw_placed])
    conv_w_all = _unpack_rows(_all_reduce_small(cw_buf, name="gather_conv_w"), cw_sizes, [cw_placed.shape])[0]

    gam = lambda a, i: a[i].reshape(1, d)

    saved = []
    cur, cur_b = x2, x2.astype(BF16)
    for i in range(DEPTH):
        j = i // 2
        w = weights[i]
        s = {"x_in": cur, "x_in_b": cur_b}
        if i % 2 == 0:
            w_pool, w_up, w_down = w
            s["scale"] = pool_scale[j].reshape(1, d)
            r1, x1, x1b = _pool_fwd(cur, w_pool, s["scale"], gam(ln_mix_g, i), gam(ln_mix_b, i), name="pool_fwd")
        else:
            w_qkv, w_o, w_up, w_down = w
            w_o3 = w_o.reshape(1, d, d)
            qkv = _mm_cols(cur_b, w_qkv, BF16, name="qkv_proj")
            o = _attn_fwd(qkv, n_heads, name="attn_fwd")
            s["qkv"], s["o"], s["w_o3"] = qkv, o, w_o3
            r1, x1, x1b = _mm_res_ln(o.reshape(1, t, d), w_o3, cur, gam(ln_mix_g, i), gam(ln_mix_b, i),
                                     name="attn_out_ln")
        if i + 1 < DEPTH:
            nxt = gather_items[i + 1]
            n_mix = len(nxt) - 2
            h, landed_ffn = _mm_cols(x1b, w_up, F32, name="ffn_up", transposed_b=True,
                                     side=_gather_sides(nxt[n_mix:])("ici"))
            a, landed_mix = _ffn_act(h, conv_w_all[i], conv_b_all[i], name="ffn_act",
                                     side=_gather_sides(nxt[:n_mix])("ici"))
            landed = list(landed_mix) + list(landed_ffn)
            (r2, x2n, x2b), gathered = _mm_res_ln(a, w_down, x1, gam(ln_ffn_g, i), gam(ln_ffn_b, i),
                                                  name="ffn_down_ln", side=_gather_sides(nxt, landed)("d2d"))
            weights[i + 1] = list(gathered)
        else:
            h = _mm_cols(x1b, w_up, F32, name="ffn_up", transposed_b=True)
            a = _ffn_act(h, conv_w_all[i], conv_b_all[i], name="ffn_act")
            r2, x2n, x2b = _mm_res_ln(a, w_down, x1, gam(ln_ffn_g, i), gam(ln_ffn_b, i), name="ffn_down_ln")
        s.update(r1=r1, x1b=x1b, h=h, a=a, r2=r2)
        saved.append(s)
        cur, cur_b = x2n, x2b

    loss_row, dcur = _loss_and_grad(cur, target, name="loss")
    loss = lax.psum(loss_row[0, 0], ("x", "y", "c"))

    big_grads = [None] * DEPTH
    reduced = [None] * DEPTH
    small = {}

    def finish_reduce(parts, from_chips, layer):
        halves = [_owner_sum(p[0], fc, place_arr, name="reduce_owner_sum") for p, fc in zip(parts, from_chips)]
        return _exchange_finished_halves(halves, name="reduce_halves_pool" if layer % 2 == 0 else "reduce_halves_attn")

    for i in reversed(range(DEPTH)):
        j = i // 2
        s, w = saved[i], weights[i]
        w_up, w_down = w[-2], w[-1]
        dr2, dr2b, small["ln_ffn_g", i], small["ln_ffn_b", i] = _ln_bwd(dcur, s["r2"], gam(ln_ffn_g, i), name="ln_bwd")
        pending = big_grads[i + 1] if i + 1 < DEPTH else None
        if pending is not None:
            da, from_sib = _mm_cols(dr2b, w_down, F32, name="ffn_down_bwd_act", transposed_b=True,
                                    side=_sibling_side(pending))
            parts = [_chip_partial(g, fs_, core_arr, name="reduce_chip_partial") for g, fs_ in zip(pending, from_sib)]
        else:
            da = _mm_cols(dr2b, w_down, F32, name="ffn_down_bwd_act", transposed_b=True)
        dr2b3 = dr2b.reshape(1, t, d)
        nmb = 2
        d_down = _mm_tn(s["a"], dr2b3, (2, fp, d), fp // nmb, d, 2,
                        (lambda u: u, nmb, lambda u, mb: mb), (lambda u: 0, lambda u: 0),
                        (lambda u: u, lambda u, mb: mb, lambda u: 0), name="ffn_down_bwd_w")
        dhc, dcw, dcb = _ffn_act_bwd(s["h"], da, conv_w_all[i], conv_b_all[i], name="ffn_act_bwd")
        small["conv_w", i], small["conv_b", i] = dcw, dcb
        dh = _conv_adjoint(dhc.reshape(4, t, fp), conv_w_all[i], name="ffn_conv_adjoint")
        up_w_args = (dh, s["x1b"].reshape(1, t, d), (4, fp, d), fp // 2, d, 4,
                     (lambda u: u, 2, lambda u, mb: mb), (lambda u: 0, lambda u: 0),
                     (lambda u: u, lambda u, mb: mb, lambda u: 0))
        if pending is not None:
            n_mix = len(parts) - 2
            dx1, chips_ffn = _mm_nt_acc(dh, w_up, dr2, fp, name="ffn_up_bwd_act", b_is_kn=True,
                                        side=_owner_chips_side([p[1] for p in parts[n_mix:]]))
            d_up, chips_mix = _mm_tn(*up_w_args, name="ffn_up_bwd_w",
                                     side=_owner_chips_side([p[1] for p in parts[:n_mix]]))
            reduced[i + 1] = finish_reduce(parts, list(chips_mix) + list(chips_ffn), i + 1)
        else:
            dx1 = _mm_nt_acc(dh, w_up, dr2, fp, name="ffn_up_bwd_act", b_is_kn=True)
            d_up = _mm_tn(*up_w_args, name="ffn_up_bwd_w")
        d_down4 = d_down[:, :fs].reshape(4, rd, d)
        if i == 0:
            (dr1, dr1b, small["ln_mix_g", i], small["ln_mix_b", i]), sib0 = _ln_bwd(
                dx1, s["r1"], gam(ln_mix_g, i), name="ln_bwd", side=_sibling_side([d_up, d_down4]))
            parts0 = [_chip_partial(g, fs_, core_arr, name="reduce_chip_partial")
                      for g, fs_ in zip([d_up, d_down4], sib0)]
            (dp, d_pool, small["pool_scale", j]), chips0 = _pool_bwd(
                s["x_in"], dr1, w[0], s["scale"], name="pool_bwd", side=_owner_chips_side([p[1] for p in parts0]))
        else:
            dr1, dr1b, small["ln_mix_g", i], small["ln_mix_b", i] = _ln_bwd(dx1, s["r1"], gam(ln_mix_g, i),
                                                                          name="ln_bwd")
            if i % 2 == 0:
                dp, d_pool, small["pool_scale", j] = _pool_bwd(s["x_in"], dr1, w[0], s["scale"], name="pool_bwd")
        if i % 2 == 0:
            dcur = _pool_adjoint(dp, dr1, n_groups, name="pool_adjoint")
            cg = d // n_groups
            d_pool4 = d_pool.reshape(n_groups, 4, cg // 4, cg).transpose(1, 0, 2, 3).reshape(4, n_groups * (cg // 4), cg)
            big_grads[i] = [d_pool4, d_up, d_down4]
        else:
            w_qkv = w[0]
            do = _mm_cols(dr1b, s["w_o3"], BF16, name="attn_out_bwd_act", transposed_b=True)
            d_wo = _mm_tn(s["o"].reshape(1, t, d), dr1b.reshape(1, t, d), (1, d, d), d // 2, d, 1,
                          (lambda u: 0, 2, lambda u, mb: mb), (lambda u: 0, lambda u: 0),
                          (lambda u: 0, lambda u, mb: mb, lambda u: 0), name="attn_out_bwd_w")
            dq, dk, dv = _attn_bwd(s["qkv"], do, n_heads, name="attn_bwd")
            cq = w_qkv.shape[2]
            dqkv = jnp.concatenate([dq, dk, dv], axis=1).reshape(t, 4, cq).transpose(1, 0, 2)
            dcur = _mm_nt_acc(dqkv, w_qkv, dr1, cq, name="qkv_bwd_act")
            d_qkv = _mm_tn(s["x_in_b"].reshape(1, t, d), dqkv, (4, d, cq), d // 2, cq, 4,
                           (lambda u: 0, 2, lambda u, mb: mb), (lambda u: u, lambda u: 0),
                           (lambda u: u, lambda u, mb: mb, lambda u: 0), name="qkv_bwd_w")
            big_grads[i] = [d_qkv, d_wo.reshape(4, d // 4, d), d_up, d_down4]
    grad_x = dcur.reshape(1, t, d)

    mixer0 = big_grads[0][:1]
    from_sib = _run_sides([_sibling_side(mixer0)], name="reduce_layer0_sibling")
    parts = [_chip_partial(g, fs_, core_arr, name="reduce_chip_partial") for g, fs_ in zip(mixer0, from_sib)]
    from_chips = _run_sides([_owner_chips_side([p[1] for p in parts])], name="reduce_layer0_chips")
    reduced[0] = finish_reduce(parts + parts0, list(from_chips) + list(chips0), 0)

    names = [("pool_scale", j) for j in range(2)]
    for nm in ("ln_mix_g", "ln_mix_b", "ln_ffn_g", "ln_ffn_b", "conv_b", "conv_w"):
        names += [(nm, i) for i in range(DEPTH)]
    vecs = [small[k] for k in names]
    sbuf, ssizes = _pack_rows(vecs)
    summed = dict(zip(names, _unpack_rows(_all_reduce_small(sbuf, name="reduce_small"), ssizes, [v.shape for v in vecs])))

    def stack_layers(nm, count):
        return jnp.stack([summed[nm, i] for i in range(count)])

    g_small = {
        "pool_scale": stack_layers("pool_scale", 2).reshape(2, d),
        "ln_mix_g": stack_layers("ln_mix_g", DEPTH).reshape(DEPTH, d),
        "ln_mix_b": stack_layers("ln_mix_b", DEPTH).reshape(DEPTH, d),
        "ln_ffn_g": stack_layers("ln_ffn_g", DEPTH).reshape(DEPTH, d),
        "ln_ffn_b": stack_layers("ln_ffn_b", DEPTH).reshape(DEPTH, d),
        "conv_b": stack_layers("conv_b", DEPTH).reshape(DEPTH, 4, fp)[:, :, :fs].reshape(DEPTH, 4 * fs),
        "conv_w": lax.dynamic_index_in_dim(stack_layers("conv_w", DEPTH).reshape(DEPTH, 4, 3, fp), chip, axis=1,
                                           keepdims=False)[:, :, :fs],
    }
    w_small = {"pool_scale": (pool_scale, m_pool_scale, v_pool_scale), "ln_mix_g": (ln_mix_g, m_ln_mix_g, v_ln_mix_g),
               "ln_mix_b": (ln_mix_b, m_ln_mix_b, v_ln_mix_b), "ln_ffn_g": (ln_ffn_g, m_ln_ffn_g, v_ln_ffn_g),
               "ln_ffn_b": (ln_ffn_b, m_ln_ffn_b, v_ln_ffn_b), "conv_b": (ffn_conv_b, m_ffn_conv_b, v_ffn_conv_b),
               "conv_w": (ffn_conv_w, m_ffn_conv_w, v_ffn_conv_w)}
    order = list(g_small)
    packs = [_pack_rows([g_small[k] for k in order])[0]]
    for idx in range(3):
        packs.append(_pack_rows([w_small[k][idx] for k in order])[0])
    small_sizes = _pack_rows([g_small[k] for k in order])[1]
    small_out = _adamw(packs[0], packs[1][None], packs[2][None], packs[3][None], 0, None, name="adamw_small")
    shapes = [g_small[k].shape for k in order]
    small_res = {k: [] for k in order}
    for arr in small_out:
        for k, val in zip(order, _unpack_rows(arr[0], small_sizes, shapes)):
            small_res[k].append(val)

    def opt_layers(per_layer_grads, w_all, m_all, v_all, name, rows=None):
        n_layers = w_all.shape[0]
        flat = [a.reshape(n_layers, rows or a.shape[1], -1) for a in (w_all, m_all, v_all)]
        res = None
        for li, g in enumerate(per_layer_grads):
            res = _adamw(g, *flat, li, res, name=name)
        return [o.reshape(w_all.shape) for o in res]

    cg = d // n_groups
    big = {
        "pool_w": opt_layers([reduced[i][0] for i in (0, 2)], pool_w, m_pool_w, v_pool_w, "adamw_pool",
                             rows=n_groups * (cg // 4)),
        "attn_w_qkv": opt_layers([reduced[i][0] for i in (1, 3)], attn_w_qkv, m_attn_w_qkv, v_attn_w_qkv, "adamw_qkv"),
        "attn_w_o": opt_layers([reduced[i][1] for i in (1, 3)], attn_w_o, m_attn_w_o, v_attn_w_o, "adamw_wo"),
        "ffn_w_up": [jnp.transpose(o, (0, 2, 1))
                     for o in opt_layers([reduced[i][-2] for i in range(DEPTH)], *up_t, "adamw_up")],
        "ffn_w_down": opt_layers([reduced[i][-1] for i in range(DEPTH)], ffn_w_down, m_ffn_w_down, v_ffn_w_down,
                                 "adamw_down"),
    }

    def leaf(k, name):
        if name in big:
            return big[name][k]
        key = {"ffn_conv_w": "conv_w", "ffn_conv_b": "conv_b"}.get(name, name)
        return small_res[key][k]

    weight_names = ["pool_w", "pool_scale", "attn_w_qkv", "attn_w_o", "ffn_w_up", "ffn_conv_w", "ffn_conv_b",
                    "ffn_w_down", "ln_mix_g", "ln_mix_b", "ln_ffn_g", "ln_ffn_b"]
    outs = [loss, grad_x]
    for k in range(4):
        outs += [leaf(k, nm) for nm in weight_names]
    return tuple(outs)
```
